```python
import jax, jax.numpy as jnp
from jax import lax
import numpy as np

D_MODEL = 2048
BATCH = 4
SEQ = 4096
DEPTH = 1

CHUNK = 64
Q_BLOCK = 128
EPS = 1e-6

DN_HEADS = 8
DN_DK = 128
DN_DV = 128
CONV_K = 4
FOX_HEADS = 8
FOX_DH = 128

DN_QK = DN_HEADS * DN_DK
DN_V = DN_HEADS * DN_DV
FOX_W = FOX_HEADS * FOX_DH
MIX_WIDTH = DN_V + FOX_W
CONV_CH = 2 * DN_QK + DN_V
IN_SIZES = (DN_QK, DN_QK, DN_V, DN_V, DN_HEADS, DN_HEADS, FOX_W, FOX_W, FOX_W, FOX_HEADS)
IN_WIDTH = 2 * DN_QK + 2 * DN_V + 2 * DN_HEADS + 3 * FOX_W + FOX_HEADS

N_GROUPS = 4
EXPERTS_PER_GROUP = 8
TOP_K = 2
D_EXPERT = 512

kernel_name = 'hybrid_deltanet_fox_hmoe_block'


def rmsnorm(x, g):
    x32 = x.astype(jnp.float32)
    r = lax.rsqrt(jnp.mean(x32 * x32, axis=-1, keepdims=True) + EPS)
    return (x32 * r).astype(x.dtype) * g


def l2norm(x):
    x32 = x.astype(jnp.float32)
    return x32 * lax.rsqrt(jnp.sum(x32 * x32, axis=-1, keepdims=True) + EPS)


def split_cols(u, sizes):
    points, acc = [], 0
    for s in sizes[:-1]:
        acc += s
        points.append(acc)
    return jnp.split(u, points, axis=-1)


def causal_conv_silu(u, w):
    t = u.shape[1]
    up = jnp.pad(u, ((0, 0), (CONV_K - 1, 0), (0, 0)))
    y = up[:, 0:t] * w[0]
    for j in range(1, CONV_K):
        y = y + up[:, j:j + t] * w[j]
    return jax.nn.silu(y)


def gated_delta_rule(q, k, v, g, beta):
    b, t, h, dk = q.shape
    dv = v.shape[-1]
    nc = t // CHUNK
    f32 = jnp.float32

    def blk(u):
        return u.reshape(b, nc, CHUNK, h, -1).transpose(0, 3, 1, 2, 4).astype(f32)

    qc = blk(q) * (dk ** -0.5)
    kc = blk(k)
    vc = blk(v)
    gch = g.astype(f32).reshape(b, nc, CHUNK, h).transpose(0, 3, 1, 2)
    bch = beta.astype(f32).reshape(b, nc, CHUNK, h).transpose(0, 3, 1, 2)
    gc = jnp.cumsum(gch, axis=-1)
    idx = jnp.arange(CHUNK)
    incl = idx[:, None] >= idx[None, :]
    strict = idx[:, None] > idx[None, :]
    decay = jnp.exp(jnp.where(incl, gc[..., :, None] - gc[..., None, :], -jnp.inf))
    kb = kc * bch[..., None]
    a = jnp.where(strict, jnp.einsum('bhnid,bhnjd->bhnij', kb, kc) * decay, 0.0)
    eye = jnp.broadcast_to(jnp.eye(CHUNK, dtype=f32), a.shape)
    t_inv = lax.linalg.triangular_solve(eye + a, eye, left_side=True, lower=True, unit_diagonal=True)
    u_val = jnp.einsum('bhnij,bhnjd->bhnid', t_inv, vc * bch[..., None])
    w_key = jnp.einsum('bhnij,bhnjd->bhnid', t_inv, kb * jnp.exp(gc)[..., None])
    qk = jnp.where(incl, jnp.einsum('bhnid,bhnjd->bhnij', qc, kc) * decay, 0.0)
    g_last = gc[..., -1:]
    q_dec = qc * jnp.exp(gc)[..., None]
    k_dec = kc * jnp.exp(g_last - gc)[..., None]
    last = jnp.exp(g_last[..., 0])
    xs = tuple(jnp.moveaxis(z, 2, 0) for z in (q_dec, qk, u_val, w_key, k_dec, last))

    def step(s, inp):
        qd, a_in, u_in, w_in, kd, lst = inp
        v_new = u_in - jnp.einsum('bhcd,bhde->bhce', w_in, s)
        o = jnp.einsum('bhcd,bhde->bhce', qd, s) + jnp.einsum('bhij,bhje->bhie', a_in, v_new)
        s = s * lst[..., None, None] + jnp.einsum('bhcd,bhce->bhde', kd, v_new)
        return s, o

    s0 = jnp.zeros((b, h, dk, dv), f32)
    _, o = lax.scan(step, s0, xs)
    return o.transpose(1, 0, 3, 2, 4).reshape(b, t, h, dv)


def forgetting_attention(q, k, v, f_logit):
    b, t, h, d = q.shape
    nb = t // Q_BLOCK
    scale = d ** -0.5
    qh = q.transpose(0, 2, 1, 3)
    kh = k.transpose(0, 2, 1, 3)
    vh = v.transpose(0, 2, 1, 3)
    cum_f = jnp.cumsum(jax.nn.log_sigmoid(f_logit.astype(jnp.float32)), axis=1).transpose(0, 2, 1)
    qb = qh.reshape(b, h, nb, Q_BLOCK, d).transpose(2, 0, 1, 3, 4)
    fb = cum_f.reshape(b, h, nb, Q_BLOCK).transpose(2, 0, 1, 3)
    starts = jnp.arange(nb) * Q_BLOCK
    kpos = jnp.arange(t)

    def one_block(args):
        qi, fi, s0 = args
        logits = jnp.einsum('bhqd,bhkd->bhqk', qi, kh).astype(jnp.float32) * scale
        logits = logits + (fi[..., :, None] - cum_f[..., None, :])
        qpos = s0 + jnp.arange(Q_BLOCK)
        logits = jnp.where(kpos[None, :] <= qpos[:, None], logits, -jnp.inf)
        p = jax.nn.softmax(logits, axis=-1).astype(vh.dtype)
        return jnp.einsum('bhqk,bhkd->bhqd', p, vh)

    o = lax.map(one_block, (qb, fb, starts))
    return o.transpose(1, 0, 3, 2, 4).reshape(b, t, h * d)


def hierarchical_moe(h, w_rg, b_rg, w_re, b_re, w1, w3, w2):
    b, t, d = h.shape
    hf = h.reshape(-1, d)
    n = hf.shape[0]
    gl = (hf @ w_rg + b_rg).astype(jnp.float32)
    gp = jax.nn.softmax(gl, axis=-1)
    _, gidx = lax.top_k(gl, 1)
    pg = jnp.take_along_axis(gp, gidx, axis=1)
    el = (hf @ w_re + b_re).astype(jnp.float32).reshape(n, N_GROUPS, EXPERTS_PER_GROUP)
    el_sel = jnp.take_along_axis(el, gidx[:, :, None], axis=1)[:, 0]
    tv, ti = lax.top_k(el_sel, TOP_K)
    tw = jax.nn.softmax(tv, axis=-1) * pg
    within = jnp.sum(jax.nn.one_hot(ti, EXPERTS_PER_GROUP, dtype=jnp.float32) * tw[..., None], axis=1)
    combine = (jax.nn.one_hot(gidx[:, 0], N_GROUPS, dtype=jnp.float32)[:, :, None] * within[:, None, :]).astype(h.dtype)
    y = jnp.zeros_like(hf)
    for gi in range(N_GROUPS):
        hid = jax.nn.silu(jnp.einsum('nd,edf->nef', hf, w1[gi])) * jnp.einsum('nd,edf->nef', hf, w3[gi])
        y = y + jnp.einsum('nef,efd->nd', hid * combine[:, gi, :, None], w2[gi])
    return y.reshape(b, t, d)


def setup_inputs(seed: int = 0) -> dict:
    key = jax.random.key(seed)
    ks = jax.random.split(key, 24)
    f32 = jnp.float32
    L, D = DEPTH, D_MODEL
    nrm = lambda k_, shape, fan: jax.random.normal(k_, shape, f32) * (fan ** -0.5)
    dt = jax.random.uniform(ks[8], (L, DN_HEADS), f32, 0.001, 0.1)
    return {
        'x': jax.random.normal(ks[0], (BATCH, SEQ, D), f32),
        'c': jax.random.normal(ks[1], (BATCH, D), f32),
        'w_ada': nrm(ks[2], (L, D, 6 * D), D),
        'b_ada': 0.01 * jax.random.normal(ks[3], (L, 6 * D), f32),
        'norm1_g': 1.0 + 0.01 * jax.random.normal(ks[4], (L, D), f32),
        'w_in': nrm(ks[5], (L, D, IN_WIDTH), D),
        'conv_w': nrm(ks[6], (L, CONV_K, CONV_CH), CONV_K),
        'a_log': jnp.log(jax.random.uniform(ks[7], (L, DN_HEADS), f32, 1.0, 16.0)),
        'dt_bias': dt + jnp.log(-jnp.expm1(-dt)),
        'dn_onorm_g': 1.0 + 0.01 * jax.random.normal(ks[9], (L, DN_DV), f32),
        'fox_f_bias': jax.random.uniform(ks[10], (L, FOX_HEADS), f32, 1.0, 3.0),
        'w_out': nrm(ks[11], (L, MIX_WIDTH, D), MIX_WIDTH),
        'norm2_g': 1.0 + 0.01 * jax.random.normal(ks[12], (L, D), f32),
        'w_router_group': nrm(ks[13], (L, D, N_GROUPS), D),
        'b_router_group': 0.01 * jax.random.normal(ks[14], (L, N_GROUPS), f32),
        'w_router_expert': nrm(ks[15], (L, D, N_GROUPS * EXPERTS_PER_GROUP), D),
        'b_router_expert': 0.01 * jax.random.normal(ks[16], (L, N_GROUPS * EXPERTS_PER_GROUP), f32),
        'w1': nrm(ks[17], (L, N_GROUPS, EXPERTS_PER_GROUP, D, D_EXPERT), D),
        'w3': nrm(ks[18], (L, N_GROUPS, EXPERTS_PER_GROUP, D, D_EXPERT), D),
        'w2': nrm(ks[19], (L, N_GROUPS, EXPERTS_PER_GROUP, D_EXPERT, D), D_EXPERT),
        'final_g': 1.0 + 0.01 * jax.random.normal(ks[20], (D,), f32),
    }


def reference(x, c, w_ada, b_ada, norm1_g, w_in, conv_w, a_log, dt_bias, dn_onorm_g, fox_f_bias,
              w_out, norm2_g, w_router_group, b_router_group, w_router_expert, b_router_expert,
              w1, w3, w2, final_g):
    b, t, _ = x.shape
    for l in range(DEPTH):
        mod = jax.nn.silu(c) @ w_ada[l] + b_ada[l]
        sh1, sc1, g1, sh2, sc2, g2 = jnp.split(mod[:, None, :], 6, axis=-1)

        h = rmsnorm(x, norm1_g[l]) * (1.0 + sc1) + sh1
        proj = h @ w_in[l]
        q_a, k_a, v_a, z_a, a_a, b_a, q_b, k_b, v_b, f_b = split_cols(proj, IN_SIZES)

        qkv = causal_conv_silu(jnp.concatenate([q_a, k_a, v_a], axis=-1), conv_w[l])
        q_a, k_a, v_a = split_cols(qkv, (DN_QK, DN_QK, DN_V))
        q_a = l2norm(q_a.reshape(b, t, DN_HEADS, DN_DK))
        k_a = l2norm(k_a.reshape(b, t, DN_HEADS, DN_DK))
        v_a = v_a.reshape(b, t, DN_HEADS, DN_DV)
        g_dec = -jnp.exp(a_log[l].astype(jnp.float32)) * jax.nn.softplus(a_a.astype(jnp.float32) + dt_bias[l])
        beta = jax.nn.sigmoid(b_a.astype(jnp.float32))
        o_a = gated_delta_rule(q_a, k_a, v_a, g_dec, beta).astype(x.dtype)
        o_a = rmsnorm(o_a, dn_onorm_g[l]) * jax.nn.silu(z_a.reshape(b, t, DN_HEADS, DN_DV))
        o_a = o_a.reshape(b, t, DN_V)

        o_b = forgetting_attention(q_b.reshape(b, t, FOX_HEADS, FOX_DH),
                                   k_b.reshape(b, t, FOX_HEADS, FOX_DH),
                                   v_b.reshape(b, t, FOX_HEADS, FOX_DH),
                                   f_b + fox_f_bias[l])

        mix = jnp.concatenate([o_a, o_b.astype(x.dtype)], axis=-1) @ w_out[l]
        x = x + g1 * mix

        h2 = rmsnorm(x, norm2_g[l]) * (1.0 + sc2) + sh2
        x = x + g2 * hierarchical_moe(h2, w_router_group[l], b_router_group[l],
                                      w_router_expert[l], b_router_expert[l],
                                      w1[l], w3[l], w2[l])
    return rmsnorm(x, final_g)
```

```python
import functools

import jax
import jax.numpy as jnp
from jax import lax
from jax.experimental import pallas as pl
from jax.experimental.pallas import tpu as pltpu

F32 = jnp.float32
BF16 = jnp.bfloat16

D_MODEL = 2048
EPS = 1e-6
CHUNK = 64
HEADS = 8
HEAD_DIM = 128
HEAD_W = HEADS * HEAD_DIM
CONV_K = 4
N_GROUPS = 4
EXPERTS_PER_GROUP = 8
N_EXPERTS = N_GROUPS * EXPERTS_PER_GROUP
D_EXPERT = 512
LANES = 128
SUBLANES = 8
MAIN_W = 7 * HEAD_W
VMEM_LIMIT = 56 * 1024 * 1024

L_GC, L_BETA, L_F, L_EGC, L_EK, L_ELAST = 0, 8, 16, 24, 32, 40


def _cparams(sem):
    return pltpu.CompilerParams(dimension_semantics=sem, vmem_limit_bytes=VMEM_LIMIT)


def _split_bf16(a):
    hi = a.astype(BF16)
    lo = (a - hi.astype(F32)).astype(BF16)
    return hi, lo


def _dot(a, b):
    return jnp.dot(a, b, preferred_element_type=F32)


def _dot_nt(a, b):
    return lax.dot_general(a, b, (((1,), (1,)), ((), ())), preferred_element_type=F32)


def _dot3(a, b):
    ah, al = _split_bf16(a)
    bh, bl = _split_bf16(b)
    return _dot(ah, bh) + (_dot(al, bh) + _dot(ah, bl))


def _dot3_pre(a, bh, bl):
    ah, al = _split_bf16(a)
    return _dot(ah, bh) + (_dot(al, bh) + _dot(ah, bl))


def _softplus(x):
    return jnp.maximum(x, 0.0) + jnp.log1p(jnp.exp(-jnp.abs(x)))


def _silu(x):
    return x * jax.nn.sigmoid(x)


def _adaln_kernel(c_ref, w_ref, b_ref, o_ref):
    c = c_ref[...]
    o_ref[...] = _dot(_silu(c).astype(BF16), w_ref[...].astype(BF16)) + b_ref[...]


def _adaln(c, w, b):
    bsz = c.shape[0]
    n = w.shape[1]
    tn = 1024
    cp = jnp.zeros((SUBLANES, D_MODEL), F32).at[:bsz].set(c)
    out = pl.pallas_call(
        _adaln_kernel,
        grid=(n // tn,),
        in_specs=[
            pl.BlockSpec((SUBLANES, D_MODEL), lambda j: (0, 0)),
            pl.BlockSpec((D_MODEL, tn), lambda j: (0, j)),
            pl.BlockSpec((1, tn), lambda j: (0, j)),
        ],
        out_specs=pl.BlockSpec((SUBLANES, tn), lambda j: (0, j)),
        out_shape=jax.ShapeDtypeStruct((SUBLANES, n), F32),
        compiler_params=_cparams(("parallel",)),
        name="adaln",
    )(cp, w, b.reshape(1, n))
    return out[:bsz]


INPROJ_TM = 1024
INPROJ_TN = 512
ROW_STEP = 128


def _modulated_norm(x, g, sc, sh):
    r = lax.rsqrt(jnp.mean(x * x, axis=-1, keepdims=True) + EPS)
    return ((x * r) * g) * (1.0 + sc) + sh


def _inproj_kernel(x_ref, sc_ref, sh_ref, g_ref, w_ref, wsh_ref, wsl_ref, o_ref, og_ref, h_ref):
    @pl.when(pl.program_id(1) == 0)
    def _():
        def body(i, _):
            rows = pl.ds(pl.multiple_of(i * ROW_STEP, ROW_STEP), ROW_STEP)
            h = _modulated_norm(x_ref[rows, :], g_ref[...], sc_ref[...], sh_ref[...])
            h_ref[rows, :] = h.astype(BF16)
            og_ref[rows, :] = _dot3_pre(h, wsh_ref[...], wsl_ref[...])
            return 0

        lax.fori_loop(0, INPROJ_TM // ROW_STEP, body, 0)

    o_ref[...] = _dot(h_ref[...], w_ref[...]).astype(BF16)


def _inproj(x2, sc1, sh1, g, w_main, ws_hi, ws_lo, seq):
    n = x2.shape[0]
    tm, tn = INPROJ_TM, INPROJ_TN
    per_b = seq // tm
    return pl.pallas_call(
        _inproj_kernel,
        grid=(n // tm, MAIN_W // tn),
        in_specs=[
            pl.BlockSpec((tm, D_MODEL), lambda i, j: (i, 0)),
            pl.BlockSpec((None, 1, D_MODEL), lambda i, j: (i // per_b, 0, 0)),
            pl.BlockSpec((None, 1, D_MODEL), lambda i, j: (i // per_b, 0, 0)),
            pl.BlockSpec((1, D_MODEL), lambda i, j: (0, 0)),
            pl.BlockSpec((D_MODEL, tn), lambda i, j: (0, j)),
            pl.BlockSpec((D_MODEL, LANES), lambda i, j: (0, 0)),
            pl.BlockSpec((D_MODEL, LANES), lambda i, j: (0, 0)),
        ],
        out_specs=[
            pl.BlockSpec((tm, tn), lambda i, j: (i, j)),
            pl.BlockSpec((tm, LANES), lambda i, j: (i, 0)),
        ],
        out_shape=[
            jax.ShapeDtypeStruct((n, MAIN_W), BF16),
            jax.ShapeDtypeStruct((n, LANES), F32),
        ],
        scratch_shapes=[pltpu.VMEM((tm, D_MODEL), BF16)],
        compiler_params=_cparams(("parallel", "arbitrary")),
        name="inproj",
    )(x2, sc1, sh1, g, w_main, ws_hi, ws_lo)


GATES_TB = 256


def _split3(a):
    p0 = a.astype(BF16)
    r1 = a - p0.astype(F32)
    p1 = r1.astype(BF16)
    p2 = (r1 - p1.astype(F32)).astype(BF16)
    return p0, p1, p2


def _dot_ones(m, a):
    p0, p1, p2 = _split3(a)
    return _dot(m, p0) + (_dot(m, p1) + _dot(m, p2))


def _gates_kernel(x_ref, alog_ref, dt_ref, fb_ref, o_ref, carry_ref):
    tb = GATES_TB

    @pl.when(pl.program_id(1) == 0)
    def _():
        carry_ref[...] = jnp.zeros_like(carry_ref)

    x = x_ref[...]
    lane = lax.broadcasted_iota(jnp.int32, (tb, LANES), 1)
    g = -jnp.exp(alog_ref[...]) * _softplus(x + dt_ref[...])
    beta = jax.nn.sigmoid(x)
    lf = -_softplus(-(x + fb_ref[...]))

    ri = lax.broadcasted_iota(jnp.int32, (tb, tb), 0)
    ci = lax.broadcasted_iota(jnp.int32, (tb, tb), 1)
    same_chunk = (ri // CHUNK) == (ci // CHUNK)
    tri = (ri >= ci)
    m_all = jnp.where(tri, 1.0, 0.0).astype(BF16)
    m_chunk = jnp.where(tri & same_chunk, 1.0, 0.0).astype(BF16)
    m_tot = jnp.where(same_chunk, 1.0, 0.0).astype(BF16)

    gc = _dot_ones(m_chunk, g)
    glast = _dot_ones(m_tot, g)
    fcum = _dot_ones(m_all, lf) + carry_ref[...]
    carry_ref[...] = fcum[tb - 1:tb, :]

    in_a = lane < HEADS
    egc = jnp.where(in_a, jnp.exp(gc), 0.0)
    ek = jnp.where(in_a, jnp.exp(glast - gc), 0.0)
    elast = jnp.where(in_a, jnp.exp(glast), 0.0)
    out = jnp.where(in_a, gc, jnp.where(lane < 2 * HEADS, beta, jnp.where(lane < 3 * HEADS, fcum, 0.0)))
    out = out + pltpu.roll(egc, L_EGC, 1) + pltpu.roll(ek, L_EK, 1) + pltpu.roll(elast, L_ELAST, 1)
    o_ref[...] = out


def _gates(gates, alog_row, dt_row, fb_row):
    bsz, seq, _ = gates.shape
    tb = GATES_TB
    row = pl.BlockSpec((1, LANES), lambda b, t: (0, 0))
    return pl.pallas_call(
        _gates_kernel,
        grid=(bsz, seq // tb),
        in_specs=[pl.BlockSpec((None, tb, LANES), lambda b, t: (b, t, 0)), row, row, row],
        out_specs=pl.BlockSpec((None, tb, LANES), lambda b, t: (b, t, 0)),
        out_shape=jax.ShapeDtypeStruct((bsz, seq, LANES), F32),
        scratch_shapes=[pltpu.VMEM((1, LANES), F32)],
        compiler_params=_cparams(("parallel", "arbitrary")),
        name="gates",
    )(gates, alog_row, dt_row, fb_row)


DN_TB = 256
HALO = SUBLANES


def _inv_unit_lower(a, eye, blk16, blk32):
    n = jnp.where(blk16, -a, 0.0)
    e1 = jnp.where(blk32 & jnp.logical_not(blk16), a, 0.0)
    e2 = jnp.where(blk32, 0.0, a)
    t = eye + n
    p = _dot3(n, n)
    t = t + _dot3(t, p)
    p = _dot3(p, p)
    t = t + _dot3(t, p)
    p = _dot3(p, p)
    t = t + _dot3(t, p)
    t = t - _dot3(_dot3(t, e1), t)
    t = t - _dot3(_dot3(t, e2), t)
    return t


def _deltanet_kernel(q_ref, k_ref, v_ref, z_ref, wq_ref, wk_ref, wv_ref, slab_ref, gct_ref, og_ref,
                     o_ref, ext_ref, qn_ref, kn_ref, vv_ref, s_ref):
    tb = DN_TB

    @pl.when(pl.program_id(1) == 0)
    def _():
        ext_ref[:, 0:HALO, :] = jnp.zeros((3, HALO, HEAD_W), F32)
        s_ref[...] = jnp.zeros_like(s_ref)

    for idx, (u_ref, w_ref) in enumerate(((q_ref, wq_ref), (k_ref, wk_ref), (v_ref, wv_ref))):
        for h in range(HEADS):
            cols = slice(h * HEAD_DIM, (h + 1) * HEAD_DIM)
            ext_ref[idx, HALO:HALO + tb, cols] = u_ref[:, cols].astype(F32)
            y = None
            for j in range(CONV_K):
                start = HALO - (CONV_K - 1) + j
                term = ext_ref[idx, start:start + tb, cols] * w_ref[j:j + 1, cols]
                y = term if y is None else y + term
            y = _silu(y)
            if idx == 2:
                vv_ref[:, cols] = y
            else:
                yn = y * lax.rsqrt(jnp.sum(y * y, axis=-1, keepdims=True) + EPS)
                if idx == 0:
                    qn_ref[:, cols] = (yn * (HEAD_DIM ** -0.5)).astype(BF16)
                else:
                    kn_ref[:, cols] = yn.astype(BF16)
        ext_ref[idx, 0:HALO, :] = ext_ref[idx, tb:tb + HALO, :]

    ri = lax.broadcasted_iota(jnp.int32, (CHUNK, CHUNK), 0)
    ci = lax.broadcasted_iota(jnp.int32, (CHUNK, CHUNK), 1)
    incl = ri >= ci
    strict = ri > ci
    eye = jnp.where(ri == ci, 1.0, 0.0)
    blk16 = (ri // 16) == (ci // 16)
    blk32 = (ri // 32) == (ci // 32)

    def chunk_body(c, _):
        rows = pl.ds(pl.multiple_of(c * CHUNK, CHUNK), CHUNK)
        slab = slab_ref[rows, :]
        gct = gct_ref[c]
        for h in range(HEADS):
            cols = slice(h * HEAD_DIM, (h + 1) * HEAD_DIM)
            q = qn_ref[rows, cols]
            k = kn_ref[rows, cols]
            v = vv_ref[rows, cols]
            gc_col = slab[:, L_GC + h:L_GC + h + 1]
            beta_col = slab[:, L_BETA + h:L_BETA + h + 1]
            egc_col = slab[:, L_EGC + h:L_EGC + h + 1]
            ek_col = slab[:, L_EK + h:L_EK + h + 1]
            elast = slab[CHUNK - 1:CHUNK, L_ELAST + h:L_ELAST + h + 1]
            gc_row = gct[h:h + 1, :]

            decay = jnp.where(incl, jnp.exp(gc_col - gc_row), 0.0)
            kk = _dot_nt(k, k)
            qk = _dot_nt(q, k) * decay
            a = jnp.where(strict, kk * decay * beta_col, 0.0)
            t = _inv_unit_lower(a, eye, blk16, blk32)
            th, tl = _split_bf16(t)

            kf = k.astype(F32)
            vb = (v * beta_col).astype(BF16)
            kbg = (kf * (beta_col * egc_col)).astype(BF16)
            u = _dot(th, vb) + _dot(tl, vb)
            w = _dot(th, kbg) + _dot(tl, kbg)

            s = s_ref[h]
            sb = s.astype(BF16)
            v_new = u - _dot(w.astype(BF16), sb)
            vnb = v_new.astype(BF16)
            qd = (q.astype(F32) * egc_col).astype(BF16)
            o = _dot(qd, sb) + _dot(qk.astype(BF16), vnb)
            kdt = (kf * ek_col).T.astype(BF16)
            s_ref[h] = s * elast + _dot(kdt, vnb)

            z = z_ref[rows, cols].astype(F32)
            r = lax.rsqrt(jnp.mean(o * o, axis=-1, keepdims=True) + EPS)
            o_ref[rows, cols] = (((o * r) * og_ref[...]) * _silu(z)).astype(BF16)
        return 0

    lax.fori_loop(0, tb // CHUNK, chunk_body, 0)


def _deltanet(proj3, conv_w, slab, gct, onorm_g):
    bsz, seq, _ = proj3.shape
    tb = DN_TB
    nct = tb // CHUNK

    def colblk(j):
        return pl.BlockSpec((None, tb, HEAD_W), lambda b, t: (b, t, j))

    def wblk(j):
        return pl.BlockSpec((CONV_K, HEAD_W), lambda b, t: (0, j))

    return pl.pallas_call(
        _deltanet_kernel,
        grid=(bsz, seq // tb),
        in_specs=[
            colblk(0), colblk(1), colblk(2), colblk(3),
            wblk(0), wblk(1), wblk(2),
            pl.BlockSpec((None, tb, LANES), lambda b, t: (b, t, 0)),
            pl.BlockSpec((None, nct, HEADS, CHUNK), lambda b, t: (b, t, 0, 0)),
            pl.BlockSpec((1, HEAD_DIM), lambda b, t: (0, 0)),
        ],
        out_specs=pl.BlockSpec((None, tb, HEAD_W), lambda b, t: (b, t, 0)),
        out_shape=jax.ShapeDtypeStruct((bsz, seq, HEAD_W), BF16),
        scratch_shapes=[
            pltpu.VMEM((3, tb + HALO, HEAD_W), F32),
            pltpu.VMEM((tb, HEAD_W), BF16),
            pltpu.VMEM((tb, HEAD_W), BF16),
            pltpu.VMEM((tb, HEAD_W), F32),
            pltpu.VMEM((HEADS, HEAD_DIM, HEAD_DIM), F32),
        ],
        compiler_params=_cparams(("parallel", "arbitrary")),
        name="deltanet",
    )(proj3, proj3, proj3, proj3, conv_w, conv_w, conv_w, slab, gct, onorm_g)


FOX_T = 512
NEG_INF = float("-inf")


def _fox_kernel(q_ref, k_ref, v_ref, f_ref, o_ref):
    tq = FOX_T
    i = pl.program_id(2)
    qs = (q_ref[...].astype(F32) * (HEAD_DIM ** -0.5)).astype(BF16)
    f_q = f_ref[i]
    f0 = f_q[:, 0:1]

    def step(j, carry, masked):
        m, l, acc = carry
        rows = pl.ds(pl.multiple_of(j * tq, tq), tq)
        s = _dot_nt(qs, k_ref[rows, :]) + (f0 - f_ref[j])
        if masked:
            ri = lax.broadcasted_iota(jnp.int32, (tq, tq), 0)
            ci = lax.broadcasted_iota(jnp.int32, (tq, tq), 1)
            s = jnp.where(ci <= ri, s, NEG_INF)
        m_new = jnp.maximum(m, jnp.max(s, axis=-1, keepdims=True))
        p = jnp.exp(s - m_new)
        alpha = jnp.exp(m - m_new)
        l = alpha * l + jnp.sum(p, axis=-1, keepdims=True)
        acc = alpha * acc + _dot(p.astype(BF16), v_ref[rows, :])
        return m_new, l, acc

    init = (jnp.full((tq, 1), NEG_INF, F32), jnp.zeros((tq, 1), F32), jnp.zeros((tq, HEAD_DIM), F32))
    carry = lax.fori_loop(0, i, lambda j, c: step(j, c, False), init)
    _, l, acc = step(i, carry, True)
    o_ref[...] = (acc / l).astype(BF16)


def _fox(proj3, f_rows):
    bsz, seq, _ = proj3.shape
    tq = FOX_T
    qb, kb, vb = 4 * HEADS, 5 * HEADS, 6 * HEADS
    return pl.pallas_call(
        _fox_kernel,
        grid=(bsz, HEADS, seq // tq),
        in_specs=[
            pl.BlockSpec((None, tq, HEAD_DIM), lambda b, h, i: (b, i, qb + h)),
            pl.BlockSpec((None, seq, HEAD_DIM), lambda b, h, i: (b, 0, kb + h)),
            pl.BlockSpec((None, seq, HEAD_DIM), lambda b, h, i: (b, 0, vb + h)),
            pl.BlockSpec((None, None, seq // tq, 1, tq), lambda b, h, i: (b, h, 0, 0, 0)),
        ],
        out_specs=pl.BlockSpec((None, tq, HEAD_DIM), lambda b, h, i: (b, i, h)),
        out_shape=jax.ShapeDtypeStruct((bsz, seq, HEAD_W), BF16),
        compiler_params=_cparams(("parallel", "parallel", "arbitrary")),
        name="fox",
    )(proj3, proj3, proj3, f_rows)


OUT_TM = 512
R_E0, R_E1, R_W0, R_W1 = 0, 1, 2, 3


def _first_argmax(vals, lane):
    m = jnp.max(vals, axis=-1, keepdims=True)
    idx = jnp.min(jnp.where(vals == m, lane, LANES), axis=-1, keepdims=True)
    return m, idx


def _outproj_kernel(oa_ref, ob_ref, wa_ref, wb_ref, x_ref, g1_ref, sc_ref, sh_ref, g_ref,
                    wrh_ref, wrl_ref, br_ref, x1_ref, h2_ref, r_ref):
    mix = _dot(oa_ref[...], wa_ref[...]) + _dot(ob_ref[...], wb_ref[...])
    x1 = x_ref[...] + g1_ref[...] * mix
    x1_ref[...] = x1
    h2 = _modulated_norm(x1, g_ref[...], sc_ref[...], sh_ref[...])
    h2_ref[...] = h2

    logits = _dot3_pre(h2, wrh_ref[...], wrl_ref[...]) + br_ref[...]
    tm = logits.shape[0]
    lane = lax.broadcasted_iota(jnp.int32, (tm, LANES), 1)
    gl = jnp.where(lane < N_GROUPS, logits, NEG_INF)
    gmax, gidx = _first_argmax(gl, lane)
    pg = 1.0 / jnp.sum(jnp.exp(gl - gmax), axis=-1, keepdims=True)
    e_lane = lane - N_GROUPS
    in_grp = (e_lane >= gidx * EXPERTS_PER_GROUP) & (e_lane < (gidx + 1) * EXPERTS_PER_GROUP)
    el = jnp.where(in_grp, logits, NEG_INF)
    v0, i0 = _first_argmax(el, lane)
    v1, i1 = _first_argmax(jnp.where(lane == i0, NEG_INF, el), lane)
    ex = jnp.exp(v1 - v0)
    w0 = pg / (1.0 + ex)
    w1 = pg * ex / (1.0 + ex)
    e0 = (i0 - N_GROUPS).astype(F32)
    e1 = (i1 - N_GROUPS).astype(F32)
    r_ref[...] = jnp.where(lane == R_E0, e0, jnp.where(lane == R_E1, e1,
                           jnp.where(lane == R_W0, w0, jnp.where(lane == R_W1, w1, 0.0))))


def _outproj(o_a, o_b, wa, wb, x2, g1, sc2, sh2, g, wr_hi, wr_lo, br, seq):
    n = x2.shape[0]
    tm = OUT_TM
    per_b = seq // tm
    modrow = pl.BlockSpec((None, 1, D_MODEL), lambda i: (i // per_b, 0, 0))
    const = lambda shape: pl.BlockSpec(shape, lambda i: (0, 0))
    return pl.pallas_call(
        _outproj_kernel,
        grid=(n // tm,),
        in_specs=[
            pl.BlockSpec((tm, HEAD_W), lambda i: (i, 0)),
            pl.BlockSpec((tm, HEAD_W), lambda i: (i, 0)),
            const((HEAD_W, D_MODEL)), const((HEAD_W, D_MODEL)),
            pl.BlockSpec((tm, D_MODEL), lambda i: (i, 0)),
            modrow, modrow, modrow,
            const((1, D_MODEL)),
            const((D_MODEL, LANES)), const((D_MODEL, LANES)), const((1, LANES)),
        ],
        out_specs=[
            pl.BlockSpec((tm, D_MODEL), lambda i: (i, 0)),
            pl.BlockSpec((tm, D_MODEL), lambda i: (i, 0)),
            pl.BlockSpec((tm, LANES), lambda i: (i, 0)),
        ],
        out_shape=[
            jax.ShapeDtypeStruct((n, D_MODEL), F32),
            jax.ShapeDtypeStruct((n, D_MODEL), F32),
            jax.ShapeDtypeStruct((n, LANES), F32),
        ],
        compiler_params=_cparams(("parallel",)),
        name="outproj",
    )(o_a, o_b, wa, wb, x2, g1, sc2, sh2, g, wr_hi, wr_lo, br)


MOE_TM = 256
ROUTE_TB = 512


def _route_kernel(r_ref, pos_ref, cnt_ref, run_ref, base_ref):
    tb = ROUTE_TB
    phase = pl.program_id(0)
    t = pl.program_id(1)
    r = r_ref[...]
    lane = lax.broadcasted_iota(jnp.int32, (tb, LANES), 1)
    e0 = r[:, R_E0:R_E0 + 1].astype(jnp.int32)
    e1 = r[:, R_E1:R_E1 + 1].astype(jnp.int32)
    oh0 = lane == e0
    oh1 = lane == e1
    both = jnp.where(oh0 | oh1, 1.0, 0.0)

    @pl.when((phase == 0) & (t == 0))
    def _():
        run_ref[...] = jnp.zeros_like(run_ref)

    @pl.when(phase == 0)
    def _():
        run_ref[...] = run_ref[...] + jnp.sum(both, axis=0, keepdims=True)

    @pl.when((phase == 1) & (t == 0))
    def _():
        counts = run_ref[...]
        cnt_ref[...] = counts
        padded = jnp.ceil(counts / MOE_TM) * MOE_TM
        li = lax.broadcasted_iota(jnp.int32, (LANES, LANES), 0)
        lj = lax.broadcasted_iota(jnp.int32, (LANES, LANES), 1)
        upper = jnp.where(li < lj, 1.0, 0.0).astype(BF16)
        hi = jnp.floor(padded / 256.0)
        lo = padded - hi * 256.0
        hi8 = jnp.broadcast_to(hi, (SUBLANES, LANES)).astype(BF16)
        lo8 = jnp.broadcast_to(lo, (SUBLANES, LANES)).astype(BF16)
        base = _dot(hi8, upper) * 256.0 + _dot(lo8, upper)
        base_ref[...] = base[0:1, :]
        run_ref[...] = jnp.zeros_like(run_ref)

    @pl.when(phase == 1)
    def _():
        ri = lax.broadcasted_iota(jnp.int32, (tb, tb), 0)
        ci = lax.broadcasted_iota(jnp.int32, (tb, tb), 1)
        strict = jnp.where(ri > ci, 1.0, 0.0).astype(BF16)
        before = _dot(strict, both.astype(BF16)) + run_ref[...] + base_ref[...]
        p0 = jnp.sum(jnp.where(oh0, before, 0.0), axis=-1, keepdims=True)
        p1 = jnp.sum(jnp.where(oh1, before, 0.0), axis=-1, keepdims=True)
        pos_ref[...] = jnp.where(lane == 0, p0, jnp.where(lane == 1, p1, 0.0)).astype(jnp.int32)
        run_ref[...] = run_ref[...] + jnp.sum(both, axis=0, keepdims=True)


def _route(rslab):
    n = rslab.shape[0]
    tb = ROUTE_TB
    return pl.pallas_call(
        _route_kernel,
        grid=(2, n // tb),
        in_specs=[pl.BlockSpec((tb, LANES), lambda p, t: (t, 0))],
        out_specs=[
            pl.BlockSpec((tb, LANES), lambda p, t: (p * t, 0)),
            pl.BlockSpec((1, LANES), lambda p, t: (0, 0)),
        ],
        out_shape=[
            jax.ShapeDtypeStruct((n, LANES), jnp.int32),
            jax.ShapeDtypeStruct((1, LANES), F32),
        ],
        scratch_shapes=[pltpu.VMEM((1, LANES), F32), pltpu.VMEM((1, LANES), F32)],
        compiler_params=_cparams(("arbitrary", "arbitrary")),
        name="route",
    )(rslab)


DISP_TB = 256


def _dispatch_kernel(pad_start_ref, pad_len_ref, used_ref, pos_ref, h2_ref, xs_ref, zero_ref, sem, zsem):
    t = pl.program_id(0)
    tb = DISP_TB
    n_tiles = xs_ref.shape[0] // MOE_TM

    def row_copy(i, k):
        return pltpu.make_async_copy(h2_ref.at[pl.ds(t * tb + i, 1)], xs_ref.at[pl.ds(pos_ref[0, 2 * i + k], 1)], sem)

    def issue(i, _):
        row_copy(i, 0).start()
        row_copy(i, 1).start()
        return 0

    lax.fori_loop(0, tb, issue, 0)

    @pl.when(t == 0)
    def _():
        zero_ref[...] = jnp.zeros_like(zero_ref)

        def zero_rows(off, rows):
            cp = pltpu.make_async_copy(zero_ref.at[pl.ds(0, rows)], xs_ref.at[pl.ds(off, rows)], zsem)
            cp.start()
            cp.wait()

        def per_expert(e, _):
            start = pad_start_ref[e]
            head = (-start) & (SUBLANES - 1)
            for r in range(SUBLANES - 1):
                pl.when(r < head)(functools.partial(zero_rows, start + r, 1))
            off = start + head
            rest = pad_len_ref[e] - head
            piece = MOE_TM // 2
            while piece >= SUBLANES:
                take = (rest & piece) != 0
                pl.when(take)(functools.partial(zero_rows, pl.multiple_of(off, SUBLANES), piece))
                off = off + jnp.where(take, piece, 0)
                piece //= 2
            return 0

        lax.fori_loop(0, N_EXPERTS, per_expert, 0)

        def per_tile(i, _):
            zero_rows(pl.multiple_of(i * MOE_TM, MOE_TM), MOE_TM)
            return 0

        lax.fori_loop(used_ref[0], n_tiles, per_tile, 0)

    def drain(i, _):
        row_copy(i, 0).wait()
        row_copy(i, 1).wait()
        return 0

    lax.fori_loop(0, tb, drain, 0)


def _dispatch(pad_start, pad_len, used, pos2, h2, p_rows):
    n = h2.shape[0]
    tb = DISP_TB
    return pl.pallas_call(
        _dispatch_kernel,
        grid_spec=pltpu.PrefetchScalarGridSpec(
            num_scalar_prefetch=3,
            grid=(n // tb,),
            in_specs=[
                pl.BlockSpec((None, 1, 2 * tb), lambda t, *_: (t, 0, 0), memory_space=pltpu.SMEM),
                pl.BlockSpec(memory_space=pl.ANY),
            ],
            out_specs=pl.BlockSpec(memory_space=pl.ANY),
            scratch_shapes=[
                pltpu.VMEM((MOE_TM, D_MODEL), F32),
                pltpu.SemaphoreType.DMA(()),
                pltpu.SemaphoreType.DMA(()),
            ],
        ),
        out_shape=jax.ShapeDtypeStruct((p_rows, D_MODEL), F32),
        compiler_params=_cparams(("arbitrary",)),
        name="dispatch",
    )(pad_start, pad_len, used, pos2, h2)


def _experts_kernel(te_ref, tv_ref, tf_ref, xs_ref, w1_ref, w3_ref, w2_ref, ys_ref,
                    w1b_ref, w3b_ref, w2b_ref):
    i = pl.program_id(0)

    @pl.when(tv_ref[i] != 0)
    def _():
        @pl.when(tf_ref[i] != 0)
        def _():
            w1b_ref[...] = w1_ref[...].astype(BF16)
            w3b_ref[...] = w3_ref[...].astype(BF16)
            w2b_ref[...] = w2_ref[...].astype(BF16)

        x = xs_ref[...].astype(BF16)
        a = _dot(x, w1b_ref[...])
        b = _dot(x, w3b_ref[...])
        ys_ref[...] = _dot((_silu(a) * b).astype(BF16), w2b_ref[...])

    @pl.when(tv_ref[i] == 0)
    def _():
        ys_ref[...] = jnp.zeros_like(ys_ref)


def _experts(tile_expert, tile_valid, tile_first, xs, w1, w3, w2):
    p_rows = xs.shape[0]
    tm = MOE_TM
    wspec_in = pl.BlockSpec((None, D_MODEL, D_EXPERT), lambda i, te, tv, tf: (te[i], 0, 0))
    return pl.pallas_call(
        _experts_kernel,
        grid_spec=pltpu.PrefetchScalarGridSpec(
            num_scalar_prefetch=3,
            grid=(p_rows // tm,),
            in_specs=[
                pl.BlockSpec((tm, D_MODEL), lambda i, te, tv, tf: (i, 0)),
                wspec_in, wspec_in,
                pl.BlockSpec((None, D_EXPERT, D_MODEL), lambda i, te, tv, tf: (te[i], 0, 0)),
            ],
            out_specs=pl.BlockSpec((tm, D_MODEL), lambda i, te, tv, tf: (i, 0)),
            scratch_shapes=[
                pltpu.VMEM((D_MODEL, D_EXPERT), BF16),
                pltpu.VMEM((D_MODEL, D_EXPERT), BF16),
                pltpu.VMEM((D_EXPERT, D_MODEL), BF16),
            ],
        ),
        out_shape=jax.ShapeDtypeStruct((p_rows, D_MODEL), F32),
        compiler_params=_cparams(("arbitrary",)),
        name="experts",
    )(tile_expert, tile_valid, tile_first, xs, w1, w3, w2)


COMB_TB = 256


def _combine_kernel(pos_ref, ys_ref, r_ref, x1_ref, g2_ref, fg_ref, o_ref, buf_ref, sem, *, final):
    tb = COMB_TB

    def row_copy(i, k):
        return pltpu.make_async_copy(ys_ref.at[pl.ds(pos_ref[0, 2 * i + k], 1)], buf_ref.at[k, pl.ds(i, 1)], sem)

    def issue(i, _):
        row_copy(i, 0).start()
        row_copy(i, 1).start()
        return 0

    def drain(i, _):
        row_copy(i, 0).wait()
        row_copy(i, 1).wait()
        return 0

    lax.fori_loop(0, tb, issue, 0)
    lax.fori_loop(0, tb, drain, 0)
    r = r_ref[...]
    y = r[:, R_W0:R_W0 + 1] * buf_ref[0] + r[:, R_W1:R_W1 + 1] * buf_ref[1]
    x2 = x1_ref[...] + g2_ref[...] * y
    if final:
        x2 = (x2 * lax.rsqrt(jnp.mean(x2 * x2, axis=-1, keepdims=True) + EPS)) * fg_ref[...]
    o_ref[...] = x2


def _combine(pos2, ys, rslab, x1, g2, final_g, seq, final):
    n = x1.shape[0]
    tb = COMB_TB
    per_b = seq // tb
    return pl.pallas_call(
        functools.partial(_combine_kernel, final=final),
        grid=(n // tb,),
        in_specs=[
            pl.BlockSpec((None, 1, 2 * tb), lambda i: (i, 0, 0), memory_space=pltpu.SMEM),
            pl.BlockSpec(memory_space=pl.ANY),
            pl.BlockSpec((tb, LANES), lambda i: (i, 0)),
            pl.BlockSpec((tb, D_MODEL), lambda i: (i, 0)),
            pl.BlockSpec((None, 1, D_MODEL), lambda i: (i // per_b, 0, 0)),
            pl.BlockSpec((1, D_MODEL), lambda i: (0, 0)),
        ],
        out_specs=pl.BlockSpec((tb, D_MODEL), lambda i: (i, 0)),
        out_shape=jax.ShapeDtypeStruct((n, D_MODEL), F32),
        scratch_shapes=[pltpu.VMEM((2, tb, D_MODEL), F32), pltpu.SemaphoreType.DMA(())],
        compiler_params=_cparams(("arbitrary",)),
        name="combine",
    )(pos2, ys, rslab, x1, g2, final_g)


def _layer(x, c, w_ada, b_ada, norm1_g, w_in, conv_w, a_log, dt_bias, dn_onorm_g, fox_f_bias,
           w_out, norm2_g, w_rg, b_rg, w_re, b_re, w1, w3, w2, final_g, final):
    bsz, seq, d = x.shape
    n = bsz * seq
    x2 = x.reshape(n, d)

    mod = _adaln(c, w_ada, b_ada)
    sh1, sc1, g1, sh2, sc2, g2 = [m.reshape(bsz, 1, d) for m in jnp.split(mod, 6, axis=-1)]

    o_a = 4 * HEAD_W
    o_b = o_a + 2 * HEADS
    o_f = o_b + 3 * HEAD_W
    w_main = jnp.concatenate([w_in[:, :o_a], w_in[:, o_b:o_f]], axis=1).astype(BF16)
    w_small = jnp.zeros((d, LANES), F32)
    w_small = w_small.at[:, 0:2 * HEADS].set(w_in[:, o_a:o_b]).at[:, 2 * HEADS:3 * HEADS].set(w_in[:, o_f:])
    ws_hi, ws_lo = _split_bf16(w_small)

    proj, gates = _inproj(x2, sc1, sh1, norm1_g.reshape(1, d), w_main, ws_hi, ws_lo, seq)
    proj3 = proj.reshape(bsz, seq, MAIN_W)

    def lane_row(vals, off):
        return jnp.zeros((1, LANES), F32).at[0, off:off + HEADS].set(vals)

    slab = _gates(gates.reshape(bsz, seq, LANES), lane_row(a_log, 0), lane_row(dt_bias, 0),
                  lane_row(fox_f_bias, L_F))
    nc = seq // CHUNK
    gct = slab[:, :, L_GC:L_GC + HEADS].reshape(bsz, nc, CHUNK, HEADS).transpose(0, 1, 3, 2)
    f_rows = slab[:, :, L_F:L_F + HEADS].transpose(0, 2, 1).reshape(bsz, HEADS, seq // FOX_T, 1, FOX_T)

    o_dn = _deltanet(proj3, conv_w, slab, gct, dn_onorm_g.reshape(1, HEAD_DIM))
    o_fx = _fox(proj3, f_rows)

    wr = jnp.zeros((d, LANES), F32).at[:, :N_GROUPS].set(w_rg).at[:, N_GROUPS:N_GROUPS + N_EXPERTS].set(w_re)
    br = jnp.zeros((1, LANES), F32).at[0, :N_GROUPS].set(b_rg).at[0, N_GROUPS:N_GROUPS + N_EXPERTS].set(b_re)
    wr_hi, wr_lo = _split_bf16(wr)
    w_out_b = w_out.astype(BF16)
    x1, h2, rslab = _outproj(o_dn.reshape(n, HEAD_W), o_fx.reshape(n, HEAD_W), w_out_b[:HEAD_W], w_out_b[HEAD_W:],
                             x2, g1, sc2, sh2, norm2_g.reshape(1, d), wr_hi, wr_lo, br, seq)

    pos_slab, counts = _route(rslab)
    pos2 = pos_slab[:, 0:2].reshape(n // DISP_TB, 1, 2 * DISP_TB)

    cnt = counts[0, :N_EXPERTS].astype(jnp.int32)
    tiles_per = (cnt + MOE_TM - 1) // MOE_TM
    tile_end = jnp.cumsum(tiles_per)
    base = (tile_end - tiles_per) * MOE_TM
    n_tiles = (2 * n) // MOE_TM + N_EXPERTS
    p_rows = n_tiles * MOE_TM
    tid = jnp.arange(n_tiles, dtype=jnp.int32)
    tile_valid = (tid < tile_end[-1]).astype(jnp.int32)
    te_raw = jnp.minimum(jnp.searchsorted(tile_end, tid, side="right"), N_EXPERTS - 1).astype(jnp.int32)
    last_e = te_raw[jnp.maximum(tile_end[-1] - 1, 0)]
    tile_expert = jnp.where(tile_valid == 1, te_raw, last_e)
    tile_first = (jnp.concatenate([jnp.array([-1], jnp.int32), tile_expert[:-1]]) != tile_expert).astype(jnp.int32)
    pad_start = base + cnt
    pad_len = tiles_per * MOE_TM - cnt

    xs = _dispatch(pad_start, pad_len, tile_end[-1:], pos2, h2, p_rows)
    ys = _experts(tile_expert, tile_valid, tile_first, xs, w1.reshape(N_EXPERTS, d, D_EXPERT),
                  w3.reshape(N_EXPERTS, d, D_EXPERT), w2.reshape(N_EXPERTS, D_EXPERT, d))
    out = _combine(pos2, ys, rslab, x1, g2, final_g.reshape(1, d), seq, final)
    return out.reshape(bsz, seq, d)


def kernel(x, c, w_ada, b_ada, norm1_g, w_in, conv_w, a_log, dt_bias, dn_onorm_g, fox_f_bias, w_out, norm2_g,
           w_router_group, b_router_group, w_router_expert, b_router_expert, w1, w3, w2, final_g):
    depth = w_ada.shape[0]
    for l in range(depth):
        x = _layer(x, c, w_ada[l], b_ada[l], norm1_g[l], w_in[l], conv_w[l], a_log[l], dt_bias[l], dn_onorm_g[l],
                   fox_f_bias[l], w_out[l], norm2_g[l], w_router_group[l], b_router_group[l], w_router_expert[l],
                   b_router_expert[l], w1[l], w3[l], w2[l], final_g, l == depth - 1)
    return x
```

```python
import functools

import jax
import jax.numpy as jnp
from jax import lax
from jax.experimental import pallas as pl
from jax.experimental.pallas import tpu as pltpu

F32 = jnp.float32
BF16 = jnp.bfloat16

D_MODEL = 2048
EPS = 1e-6
CHUNK = 64
HEADS = 8
HEAD_DIM = 128
HEAD_W = HEADS * HEAD_DIM
CONV_K = 4
N_GROUPS = 4
EXPERTS_PER_GROUP = 8
N_EXPERTS = N_GROUPS * EXPERTS_PER_GROUP
D_EXPERT = 512
LANES = 128
SUBLANES = 8
MAIN_W = 7 * HEAD_W
VMEM_LIMIT = 56 * 1024 * 1024

L_GC, L_BETA, L_F, L_EGC, L_EK, L_ELAST = 0, 8, 16, 24, 32, 40


def _cparams(sem):
    return pltpu.CompilerParams(dimension_semantics=sem, vmem_limit_bytes=VMEM_LIMIT)


def _split_bf16(a):
    hi = a.astype(BF16)
    lo = (a - hi.astype(F32)).astype(BF16)
    return hi, lo


def _dot(a, b):
    return jnp.dot(a, b, preferred_element_type=F32)


def _dot_nt(a, b):
    return lax.dot_general(a, b, (((1,), (1,)), ((), ())), preferred_element_type=F32)


def _dot3(a, b):
    ah, al = _split_bf16(a)
    bh, bl = _split_bf16(b)
    return _dot(ah, bh) + (_dot(al, bh) + _dot(ah, bl))


def _dot3_pre(a, bh, bl):
    ah, al = _split_bf16(a)
    return _dot(ah, bh) + (_dot(al, bh) + _dot(ah, bl))


def _softplus(x):
    return jnp.maximum(x, 0.0) + jnp.log1p(jnp.exp(-jnp.abs(x)))


def _silu(x):
    return x * jax.nn.sigmoid(x)


def _adaln_kernel(c_ref, w_ref, b_ref, o_ref):
    c = c_ref[...]
    o_ref[...] = _dot(_silu(c).astype(BF16), w_ref[...].astype(BF16)) + b_ref[...]


def _adaln(c, w, b):
    bsz = c.shape[0]
    n = w.shape[1]
    tn = 1024
    cp = jnp.zeros((SUBLANES, D_MODEL), F32).at[:bsz].set(c)
    out = pl.pallas_call(
        _adaln_kernel,
        grid=(n // tn,),
        in_specs=[
            pl.BlockSpec((SUBLANES, D_MODEL), lambda j: (0, 0)),
            pl.BlockSpec((D_MODEL, tn), lambda j: (0, j)),
            pl.BlockSpec((1, tn), lambda j: (0, j)),
        ],
        out_specs=pl.BlockSpec((SUBLANES, tn), lambda j: (0, j)),
        out_shape=jax.ShapeDtypeStruct((SUBLANES, n), F32),
        compiler_params=_cparams(("parallel",)),
        name="adaln",
    )(cp, w, b.reshape(1, n))
    return out[:bsz]


INPROJ_TM = 1024
INPROJ_TN = 512
ROW_STEP = 128


def _modulated_norm(x, g, sc, sh):
    r = lax.rsqrt(jnp.mean(x * x, axis=-1, keepdims=True) + EPS)
    return ((x * r) * g) * (1.0 + sc) + sh


def _inproj_kernel(x_ref, sc_ref, sh_ref, g_ref, w_ref, wsh_ref, wsl_ref, o_ref, og_ref, h_ref):
    @pl.when(pl.program_id(1) == 0)
    def _():
        def body(i, _):
            rows = pl.ds(pl.multiple_of(i * ROW_STEP, ROW_STEP), ROW_STEP)
            h = _modulated_norm(x_ref[rows, :], g_ref[...], sc_ref[...], sh_ref[...])
            h_ref[rows, :] = h.astype(BF16)
            og_ref[rows, :] = _dot3_pre(h, wsh_ref[...], wsl_ref[...])
            return 0

        lax.fori_loop(0, INPROJ_TM // ROW_STEP, body, 0)

    o_ref[...] = _dot(h_ref[...], w_ref[...]).astype(BF16)


def _inproj(x2, sc1, sh1, g, w_main, ws_hi, ws_lo, seq):
    n = x2.shape[0]
    tm, tn = INPROJ_TM, INPROJ_TN
    per_b = seq // tm
    return pl.pallas_call(
        _inproj_kernel,
        grid=(n // tm, MAIN_W // tn),
        in_specs=[
            pl.BlockSpec((tm, D_MODEL), lambda i, j: (i, 0)),
            pl.BlockSpec((None, 1, D_MODEL), lambda i, j: (i // per_b, 0, 0)),
            pl.BlockSpec((None, 1, D_MODEL), lambda i, j: (i // per_b, 0, 0)),
            pl.BlockSpec((1, D_MODEL), lambda i, j: (0, 0)),
            pl.BlockSpec((D_MODEL, tn), lambda i, j: (0, j)),
            pl.BlockSpec((D_MODEL, LANES), lambda i, j: (0, 0)),
            pl.BlockSpec((D_MODEL, LANES), lambda i, j: (0, 0)),
        ],
        out_specs=[
            pl.BlockSpec((tm, tn), lambda i, j: (i, j)),
            pl.BlockSpec((tm, LANES), lambda i, j: (i, 0)),
        ],
        out_shape=[
            jax.ShapeDtypeStruct((n, MAIN_W), BF16),
            jax.ShapeDtypeStruct((n, LANES), F32),
        ],
        scratch_shapes=[pltpu.VMEM((tm, D_MODEL), BF16)],
        compiler_params=_cparams(("parallel", "arbitrary")),
        name="inproj",
    )(x2, sc1, sh1, g, w_main, ws_hi, ws_lo)


GATES_TB = 256


def _split3(a):
    p0 = a.astype(BF16)
    r1 = a - p0.astype(F32)
    p1 = r1.astype(BF16)
    p2 = (r1 - p1.astype(F32)).astype(BF16)
    return p0, p1, p2


def _dot_ones(m, a):
    p0, p1, p2 = _split3(a)
    return _dot(m, p0) + (_dot(m, p1) + _dot(m, p2))


def _gates_kernel(x_ref, alog_ref, dt_ref, fb_ref, o_ref, carry_ref):
    tb = GATES_TB

    @pl.when(pl.program_id(1) == 0)
    def _():
        carry_ref[...] = jnp.zeros_like(carry_ref)

    x = x_ref[...]
    lane = lax.broadcasted_iota(jnp.int32, (tb, LANES), 1)
    g = -jnp.exp(alog_ref[...]) * _softplus(x + dt_ref[...])
    beta = jax.nn.sigmoid(x)
    lf = -_softplus(-(x + fb_ref[...]))

    ri = lax.broadcasted_iota(jnp.int32, (tb, tb), 0)
    ci = lax.broadcasted_iota(jnp.int32, (tb, tb), 1)
    same_chunk = (ri // CHUNK) == (ci // CHUNK)
    tri = (ri >= ci)
    m_all = jnp.where(tri, 1.0, 0.0).astype(BF16)
    m_chunk = jnp.where(tri & same_chunk, 1.0, 0.0).astype(BF16)
    m_tot = jnp.where(same_chunk, 1.0, 0.0).astype(BF16)

    gc = _dot_ones(m_chunk, g)
    glast = _dot_ones(m_tot, g)
    fcum = _dot_ones(m_all, lf) + carry_ref[...]
    carry_ref[...] = fcum[tb - 1:tb, :]

    in_a = lane < HEADS
    egc = jnp.where(in_a, jnp.exp(gc), 0.0)
    ek = jnp.where(in_a, jnp.exp(glast - gc), 0.0)
    elast = jnp.where(in_a, jnp.exp(glast), 0.0)
    out = jnp.where(in_a, gc, jnp.where(lane < 2 * HEADS, beta, jnp.where(lane < 3 * HEADS, fcum, 0.0)))
    out = out + pltpu.roll(egc, L_EGC, 1) + pltpu.roll(ek, L_EK, 1) + pltpu.roll(elast, L_ELAST, 1)
    o_ref[...] = out


def _gates(gates, alog_row, dt_row, fb_row):
    bsz, seq, _ = gates.shape
    tb = GATES_TB
    row = pl.BlockSpec((1, LANES), lambda b, t: (0, 0))
    return pl.pallas_call(
        _gates_kernel,
        grid=(bsz, seq // tb),
        in_specs=[pl.BlockSpec((None, tb, LANES), lambda b, t: (b, t, 0)), row, row, row],
        out_specs=pl.BlockSpec((None, tb, LANES), lambda b, t: (b, t, 0)),
        out_shape=jax.ShapeDtypeStruct((bsz, seq, LANES), F32),
        scratch_shapes=[pltpu.VMEM((1, LANES), F32)],
        compiler_params=_cparams(("parallel", "arbitrary")),
        name="gates",
    )(gates, alog_row, dt_row, fb_row)


DN_TB = 256
HALO = SUBLANES


def _bdot(a, b):
    return lax.dot_general(a, b, (((2,), (1,)), ((0,), (0,))), preferred_element_type=F32)


def _bdot_nt(a, b):
    return lax.dot_general(a, b, (((2,), (2,)), ((0,), (0,))), preferred_element_type=F32)


def _bdot3(a, b):
    ah, al = _split_bf16(a)
    bh, bl = _split_bf16(b)
    return _bdot(ah, bh) + (_bdot(al, bh) + _bdot(ah, bl))


def _inv_unit_lower(a, eye, blk16, blk32):
    n = jnp.where(blk16, -a, 0.0)
    e1 = jnp.where(blk32 & jnp.logical_not(blk16), a, 0.0)
    e2 = jnp.where(blk32, 0.0, a)
    t = eye + n
    p = _bdot3(n, n)
    t = t + _bdot3(t, p)
    p = _bdot3(p, p)
    t = t + _bdot3(t, p)
    p = _bdot3(p, p)
    t = t + _bdot3(t, p)
    t = t - _bdot3(_bdot3(t, e1), t)
    t = t - _bdot3(_bdot3(t, e2), t)
    return t


def _deltanet_kernel(q_ref, k_ref, v_ref, z_ref, wq_ref, wk_ref, wv_ref, slab_ref, gct_ref, og_ref,
                     o_ref, ext_ref, qn_ref, kn_ref, vv_ref, s_ref):
    tb = DN_TB

    @pl.when(pl.program_id(1) == 0)
    def _():
        ext_ref[:, 0:HALO, :] = jnp.zeros((3, HALO, HEAD_W), F32)
        s_ref[...] = jnp.zeros_like(s_ref)

    for idx, (u_ref, w_ref) in enumerate(((q_ref, wq_ref), (k_ref, wk_ref), (v_ref, wv_ref))):
        for h in range(HEADS):
            cols = slice(h * HEAD_DIM, (h + 1) * HEAD_DIM)
            ext_ref[idx, HALO:HALO + tb, cols] = u_ref[:, cols].astype(F32)
            y = None
            for j in range(CONV_K):
                start = HALO - (CONV_K - 1) + j
                term = ext_ref[idx, start:start + tb, cols] * w_ref[j:j + 1, cols]
                y = term if y is None else y + term
            y = _silu(y)
            if idx == 2:
                vv_ref[h] = y
            else:
                yn = y * lax.rsqrt(jnp.sum(y * y, axis=-1, keepdims=True) + EPS)
                if idx == 0:
                    qn_ref[h] = (yn * (HEAD_DIM ** -0.5)).astype(BF16)
                else:
                    kn_ref[h] = yn.astype(BF16)
        ext_ref[idx, 0:HALO, :] = ext_ref[idx, tb:tb + HALO, :]

    ri = lax.broadcasted_iota(jnp.int32, (HEADS, CHUNK, CHUNK), 1)
    ci = lax.broadcasted_iota(jnp.int32, (HEADS, CHUNK, CHUNK), 2)
    incl = ri >= ci
    strict = ri > ci
    eye = jnp.where(ri == ci, 1.0, 0.0)
    blk16 = (ri // 16) == (ci // 16)
    blk32 = (ri // 32) == (ci // 32)

    def chunk_body(c, _):
        rows = pl.ds(pl.multiple_of(c * CHUNK, CHUNK), CHUNK)
        slab = slab_ref[rows, :]

        def col(off, width):
            return jnp.stack([jnp.broadcast_to(slab[:, off + h:off + h + 1], (CHUNK, width)) for h in range(HEADS)])

        q = qn_ref[:, rows, :]
        k = kn_ref[:, rows, :]
        v = vv_ref[:, rows, :]
        beta = col(L_BETA, HEAD_DIM)
        egc = col(L_EGC, HEAD_DIM)
        gc_row = gct_ref[c]

        decay = jnp.where(incl, jnp.exp(col(L_GC, CHUNK) - gc_row), 0.0)
        kk = _bdot_nt(k, k)
        qk = _bdot_nt(q, k) * decay
        a = jnp.where(strict, kk * decay * beta[:, :, :CHUNK], 0.0)
        t = _inv_unit_lower(a, eye, blk16, blk32)
        th, tl = _split_bf16(t)

        kf = k.astype(F32)
        vb = (v * beta).astype(BF16)
        kbg = (kf * (beta * egc)).astype(BF16)
        u = _bdot(th, vb) + _bdot(tl, vb)
        w = _bdot(th, kbg) + _bdot(tl, kbg)

        s = s_ref[...]
        sb = s.astype(BF16)
        v_new = u - _bdot(w.astype(BF16), sb)
        vnb = v_new.astype(BF16)
        qd = (q.astype(F32) * egc).astype(BF16)
        o = _bdot(qd, sb) + _bdot(qk.astype(BF16), vnb)
        kd = kf * col(L_EK, HEAD_DIM)
        kdt = jnp.stack([kd[h].T for h in range(HEADS)]).astype(BF16)
        elast = jnp.stack([jnp.broadcast_to(slab[CHUNK - 1:CHUNK, L_ELAST + h:L_ELAST + h + 1], (HEAD_DIM, HEAD_DIM))
                           for h in range(HEADS)])
        s_ref[...] = s * elast + _bdot(kdt, vnb)

        r = lax.rsqrt(jnp.mean(o * o, axis=-1, keepdims=True) + EPS)
        on = (o * r) * og_ref[...]
        for h in range(HEADS):
            cols = slice(h * HEAD_DIM, (h + 1) * HEAD_DIM)
            o_ref[rows, cols] = (on[h] * _silu(z_ref[rows, cols].astype(F32))).astype(BF16)
        return 0

    lax.fori_loop(0, tb // CHUNK, chunk_body, 0)


def _deltanet(proj3, conv_w, slab, gct, onorm_g):
    bsz, seq, _ = proj3.shape
    tb = DN_TB
    nct = tb // CHUNK

    def colblk(j):
        return pl.BlockSpec((None, tb, HEAD_W), lambda b, t: (b, t, j))

    def wblk(j):
        return pl.BlockSpec((CONV_K, HEAD_W), lambda b, t: (0, j))

    return pl.pallas_call(
        _deltanet_kernel,
        grid=(bsz, seq // tb),
        in_specs=[
            colblk(0), colblk(1), colblk(2), colblk(3),
            wblk(0), wblk(1), wblk(2),
            pl.BlockSpec((None, tb, LANES), lambda b, t: (b, t, 0)),
            pl.BlockSpec((None, nct, HEADS, 1, CHUNK), lambda b, t: (b, t, 0, 0, 0)),
            pl.BlockSpec((1, HEAD_DIM), lambda b, t: (0, 0)),
        ],
        out_specs=pl.BlockSpec((None, tb, HEAD_W), lambda b, t: (b, t, 0)),
        out_shape=jax.ShapeDtypeStruct((bsz, seq, HEAD_W), BF16),
        scratch_shapes=[
            pltpu.VMEM((3, tb + HALO, HEAD_W), F32),
            pltpu.VMEM((HEADS, tb, HEAD_DIM), BF16),
            pltpu.VMEM((HEADS, tb, HEAD_DIM), BF16),
            pltpu.VMEM((HEADS, tb, HEAD_DIM), F32),
            pltpu.VMEM((HEADS, HEAD_DIM, HEAD_DIM), F32),
        ],
        compiler_params=_cparams(("parallel", "arbitrary")),
        name="deltanet",
    )(proj3, proj3, proj3, proj3, conv_w, conv_w, conv_w, slab, gct, onorm_g)


FOX_T = 512
NEG_INF = float("-inf")


def _fox_kernel(q_ref, k_ref, v_ref, f_ref, o_ref):
    tq = FOX_T
    i = pl.program_id(2)
    qs = (q_ref[...].astype(F32) * (HEAD_DIM ** -0.5)).astype(BF16)
    f_q = f_ref[i]
    f0 = f_q[:, 0:1]

    def step(j, carry, masked):
        m, l, acc = carry
        rows = pl.ds(pl.multiple_of(j * tq, tq), tq)
        s = _dot_nt(qs, k_ref[rows, :]) + (f0 - f_ref[j])
        if masked:
            ri = lax.broadcasted_iota(jnp.int32, (tq, tq), 0)
            ci = lax.broadcasted_iota(jnp.int32, (tq, tq), 1)
            s = jnp.where(ci <= ri, s, NEG_INF)
        m_new = jnp.maximum(m, jnp.max(s, axis=-1, keepdims=True))
        p = jnp.exp(s - m_new)
        alpha = jnp.exp(m - m_new)
        l = alpha * l + jnp.sum(p, axis=-1, keepdims=True)
        acc = alpha * acc + _dot(p.astype(BF16), v_ref[rows, :])
        return m_new, l, acc

    init = (jnp.full((tq, 1), NEG_INF, F32), jnp.zeros((tq, 1), F32), jnp.zeros((tq, HEAD_DIM), F32))
    carry = lax.fori_loop(0, i, lambda j, c: step(j, c, False), init)
    _, l, acc = step(i, carry, True)
    o_ref[...] = (acc / l).astype(BF16)


def _fox(proj3, f_rows):
    bsz, seq, _ = proj3.shape
    tq = FOX_T
    qb, kb, vb = 4 * HEADS, 5 * HEADS, 6 * HEADS
    return pl.pallas_call(
        _fox_kernel,
        grid=(bsz, HEADS, seq // tq),
        in_specs=[
            pl.BlockSpec((None, tq, HEAD_DIM), lambda b, h, i: (b, i, qb + h)),
            pl.BlockSpec((None, seq, HEAD_DIM), lambda b, h, i: (b, 0, kb + h)),
            pl.BlockSpec((None, seq, HEAD_DIM), lambda b, h, i: (b, 0, vb + h)),
            pl.BlockSpec((None, None, seq // tq, 1, tq), lambda b, h, i: (b, h, 0, 0, 0)),
        ],
        out_specs=pl.BlockSpec((None, tq, HEAD_DIM), lambda b, h, i: (b, i, h)),
        out_shape=jax.ShapeDtypeStruct((bsz, seq, HEAD_W), BF16),
        compiler_params=_cparams(("parallel", "parallel", "arbitrary")),
        name="fox",
    )(proj3, proj3, proj3, f_rows)


OUT_TM = 512
R_E0, R_E1, R_W0, R_W1 = 0, 1, 2, 3


def _first_argmax(vals, lane):
    m = jnp.max(vals, axis=-1, keepdims=True)
    idx = jnp.min(jnp.where(vals == m, lane, LANES), axis=-1, keepdims=True)
    return m, idx


def _outproj_kernel(oa_ref, ob_ref, wa_ref, wb_ref, x_ref, g1_ref, sc_ref, sh_ref, g_ref,
                    wrh_ref, wrl_ref, br_ref, x1_ref, h2_ref, r_ref):
    mix = _dot(oa_ref[...], wa_ref[...]) + _dot(ob_ref[...], wb_ref[...])
    x1 = x_ref[...] + g1_ref[...] * mix
    x1_ref[...] = x1
    h2 = _modulated_norm(x1, g_ref[...], sc_ref[...], sh_ref[...])
    h2_ref[...] = h2

    logits = _dot3_pre(h2, wrh_ref[...], wrl_ref[...]) + br_ref[...]
    tm = logits.shape[0]
    lane = lax.broadcasted_iota(jnp.int32, (tm, LANES), 1)
    gl = jnp.where(lane < N_GROUPS, logits, NEG_INF)
    gmax, gidx = _first_argmax(gl, lane)
    pg = 1.0 / jnp.sum(jnp.exp(gl - gmax), axis=-1, keepdims=True)
    e_lane = lane - N_GROUPS
    in_grp = (e_lane >= gidx * EXPERTS_PER_GROUP) & (e_lane < (gidx + 1) * EXPERTS_PER_GROUP)
    el = jnp.where(in_grp, logits, NEG_INF)
    v0, i0 = _first_argmax(el, lane)
    v1, i1 = _first_argmax(jnp.where(lane == i0, NEG_INF, el), lane)
    ex = jnp.exp(v1 - v0)
    w0 = pg / (1.0 + ex)
    w1 = pg * ex / (1.0 + ex)
    e0 = (i0 - N_GROUPS).astype(F32)
    e1 = (i1 - N_GROUPS).astype(F32)
    r_ref[...] = jnp.where(lane == R_E0, e0, jnp.where(lane == R_E1, e1,
                           jnp.where(lane == R_W0, w0, jnp.where(lane == R_W1, w1, 0.0))))


def _outproj(o_a, o_b, wa, wb, x2, g1, sc2, sh2, g, wr_hi, wr_lo, br, seq):
    n = x2.shape[0]
    tm = OUT_TM
    per_b = seq // tm
    modrow = pl.BlockSpec((None, 1, D_MODEL), lambda i: (i // per_b, 0, 0))
    const = lambda shape: pl.BlockSpec(shape, lambda i: (0, 0))
    return pl.pallas_call(
        _outproj_kernel,
        grid=(n // tm,),
        in_specs=[
            pl.BlockSpec((tm, HEAD_W), lambda i: (i, 0)),
            pl.BlockSpec((tm, HEAD_W), lambda i: (i, 0)),
            const((HEAD_W, D_MODEL)), const((HEAD_W, D_MODEL)),
            pl.BlockSpec((tm, D_MODEL), lambda i: (i, 0)),
            modrow, modrow, modrow,
            const((1, D_MODEL)),
            const((D_MODEL, LANES)), const((D_MODEL, LANES)), const((1, LANES)),
        ],
        out_specs=[
            pl.BlockSpec((tm, D_MODEL), lambda i: (i, 0)),
            pl.BlockSpec((tm, D_MODEL), lambda i: (i, 0)),
            pl.BlockSpec((tm, LANES), lambda i: (i, 0)),
        ],
        out_shape=[
            jax.ShapeDtypeStruct((n, D_MODEL), F32),
            jax.ShapeDtypeStruct((n, D_MODEL), F32),
            jax.ShapeDtypeStruct((n, LANES), F32),
        ],
        compiler_params=_cparams(("parallel",)),
        name="outproj",
    )(o_a, o_b, wa, wb, x2, g1, sc2, sh2, g, wr_hi, wr_lo, br)


MOE_TM = 256
ROUTE_TB = 512


def _route_kernel(r_ref, pos_ref, cnt_ref, run_ref, base_ref):
    tb = ROUTE_TB
    phase = pl.program_id(0)
    t = pl.program_id(1)
    r = r_ref[...]
    lane = lax.broadcasted_iota(jnp.int32, (tb, LANES), 1)
    e0 = r[:, R_E0:R_E0 + 1].astype(jnp.int32)
    e1 = r[:, R_E1:R_E1 + 1].astype(jnp.int32)
    oh0 = lane == e0
    oh1 = lane == e1
    both = jnp.where(oh0 | oh1, 1.0, 0.0)

    @pl.when((phase == 0) & (t == 0))
    def _():
        run_ref[...] = jnp.zeros_like(run_ref)

    @pl.when(phase == 0)
    def _():
        run_ref[...] = run_ref[...] + jnp.sum(both, axis=0, keepdims=True)

    @pl.when((phase == 1) & (t == 0))
    def _():
        counts = run_ref[...]
        cnt_ref[...] = counts
        padded = jnp.ceil(counts / MOE_TM) * MOE_TM
        li = lax.broadcasted_iota(jnp.int32, (LANES, LANES), 0)
        lj = lax.broadcasted_iota(jnp.int32, (LANES, LANES), 1)
        upper = jnp.where(li < lj, 1.0, 0.0).astype(BF16)
        hi = jnp.floor(padded / 256.0)
        lo = padded - hi * 256.0
        hi8 = jnp.broadcast_to(hi, (SUBLANES, LANES)).astype(BF16)
        lo8 = jnp.broadcast_to(lo, (SUBLANES, LANES)).astype(BF16)
        base = _dot(hi8, upper) * 256.0 + _dot(lo8, upper)
        base_ref[...] = base[0:1, :]
        run_ref[...] = jnp.zeros_like(run_ref)

    @pl.when(phase == 1)
    def _():
        ri = lax.broadcasted_iota(jnp.int32, (tb, tb), 0)
        ci = lax.broadcasted_iota(jnp.int32, (tb, tb), 1)
        strict = jnp.where(ri > ci, 1.0, 0.0).astype(BF16)
        before = _dot(strict, both.astype(BF16)) + run_ref[...] + base_ref[...]
        p0 = jnp.sum(jnp.where(oh0, before, 0.0), axis=-1, keepdims=True)
        p1 = jnp.sum(jnp.where(oh1, before, 0.0), axis=-1, keepdims=True)
        pos_ref[...] = jnp.where(lane == 0, p0, jnp.where(lane == 1, p1, 0.0)).astype(jnp.int32)
        run_ref[...] = run_ref[...] + jnp.sum(both, axis=0, keepdims=True)


def _route(rslab):
    n = rslab.shape[0]
    tb = ROUTE_TB
    return pl.pallas_call(
        _route_kernel,
        grid=(2, n // tb),
        in_specs=[pl.BlockSpec((tb, LANES), lambda p, t: (t, 0))],
        out_specs=[
            pl.BlockSpec((tb, LANES), lambda p, t: (p * t, 0)),
            pl.BlockSpec((1, LANES), lambda p, t: (0, 0)),
        ],
        out_shape=[
            jax.ShapeDtypeStruct((n, LANES), jnp.int32),
            jax.ShapeDtypeStruct((1, LANES), F32),
        ],
        scratch_shapes=[pltpu.VMEM((1, LANES), F32), pltpu.VMEM((1, LANES), F32)],
        compiler_params=_cparams(("arbitrary", "arbitrary")),
        name="route",
    )(rslab)


DISP_TB = 256


def _dispatch_kernel(pad_start_ref, pad_len_ref, used_ref, pos_ref, h2_ref, xs_ref, zero_ref, sem, zsem):
    t = pl.program_id(0)
    tb = DISP_TB
    n_tiles = xs_ref.shape[0] // MOE_TM

    def row_copy(i, k):
        return pltpu.make_async_copy(h2_ref.at[pl.ds(i, 1)], xs_ref.at[pl.ds(pos_ref[0, 2 * i + k], 1)], sem)

    def issue(i, _):
        row_copy(i, 0).start()
        row_copy(i, 1).start()
        return 0

    lax.fori_loop(0, tb, issue, 0)

    @pl.when(t == 0)
    def _():
        zero_ref[...] = jnp.zeros_like(zero_ref)

        def zero_rows(off, rows):
            cp = pltpu.make_async_copy(zero_ref.at[pl.ds(0, rows)], xs_ref.at[pl.ds(off, rows)], zsem)
            cp.start()
            cp.wait()

        def per_expert(e, _):
            start = pad_start_ref[e]
            head = (-start) & (SUBLANES - 1)
            for r in range(SUBLANES - 1):
                pl.when(r < head)(functools.partial(zero_rows, start + r, 1))
            off = start + head
            rest = pad_len_ref[e] - head
            piece = MOE_TM // 2
            while piece >= SUBLANES:
                take = (rest & piece) != 0
                pl.when(take)(functools.partial(zero_rows, pl.multiple_of(off, SUBLANES), piece))
                off = off + jnp.where(take, piece, 0)
                piece //= 2
            return 0

        lax.fori_loop(0, N_EXPERTS, per_expert, 0)

        def per_tile(i, _):
            zero_rows(pl.multiple_of(i * MOE_TM, MOE_TM), MOE_TM)
            return 0

        lax.fori_loop(used_ref[0], n_tiles, per_tile, 0)

    def drain(i, _):
        row_copy(i, 0).wait()
        row_copy(i, 1).wait()
        return 0

    lax.fori_loop(0, tb, drain, 0)


def _dispatch(pad_start, pad_len, used, pos2, h2, p_rows):
    n = h2.shape[0]
    tb = DISP_TB
    return pl.pallas_call(
        _dispatch_kernel,
        grid_spec=pltpu.PrefetchScalarGridSpec(
            num_scalar_prefetch=3,
            grid=(n // tb,),
            in_specs=[
                pl.BlockSpec((None, 1, 2 * tb), lambda t, *_: (t, 0, 0), memory_space=pltpu.SMEM),
                pl.BlockSpec((tb, D_MODEL), lambda t, *_: (t, 0)),
            ],
            out_specs=pl.BlockSpec(memory_space=pl.ANY),
            scratch_shapes=[
                pltpu.VMEM((MOE_TM, D_MODEL), F32),
                pltpu.SemaphoreType.DMA(()),
                pltpu.SemaphoreType.DMA(()),
            ],
        ),
        out_shape=jax.ShapeDtypeStruct((p_rows, D_MODEL), F32),
        compiler_params=_cparams(("arbitrary",)),
        name="dispatch",
    )(pad_start, pad_len, used, pos2, h2)


def _experts_kernel(te_ref, tv_ref, tf_ref, xs_ref, w1_ref, w3_ref, w2_ref, ys_ref,
                    w1b_ref, w3b_ref, w2b_ref):
    i = pl.program_id(0)

    @pl.when(tv_ref[i] != 0)
    def _():
        @pl.when(tf_ref[i] != 0)
        def _():
            w1b_ref[...] = w1_ref[...].astype(BF16)
            w3b_ref[...] = w3_ref[...].astype(BF16)
            w2b_ref[...] = w2_ref[...].astype(BF16)

        x = xs_ref[...].astype(BF16)
        a = _dot(x, w1b_ref[...])
        b = _dot(x, w3b_ref[...])
        ys_ref[...] = _dot((_silu(a) * b).astype(BF16), w2b_ref[...])

    @pl.when(tv_ref[i] == 0)
    def _():
        ys_ref[...] = jnp.zeros_like(ys_ref)


def _experts(tile_expert, tile_valid, tile_first, xs, w1, w3, w2):
    p_rows = xs.shape[0]
    tm = MOE_TM
    wspec_in = pl.BlockSpec((None, D_MODEL, D_EXPERT), lambda i, te, tv, tf: (te[i], 0, 0))
    return pl.pallas_call(
        _experts_kernel,
        grid_spec=pltpu.PrefetchScalarGridSpec(
            num_scalar_prefetch=3,
            grid=(p_rows // tm,),
            in_specs=[
                pl.BlockSpec((tm, D_MODEL), lambda i, te, tv, tf: (i, 0)),
                wspec_in, wspec_in,
                pl.BlockSpec((None, D_EXPERT, D_MODEL), lambda i, te, tv, tf: (te[i], 0, 0)),
            ],
            out_specs=pl.BlockSpec((tm, D_MODEL), lambda i, te, tv, tf: (i, 0)),
            scratch_shapes=[
                pltpu.VMEM((D_MODEL, D_EXPERT), BF16),
                pltpu.VMEM((D_MODEL, D_EXPERT), BF16),
                pltpu.VMEM((D_EXPERT, D_MODEL), BF16),
            ],
        ),
        out_shape=jax.ShapeDtypeStruct((p_rows, D_MODEL), F32),
        compiler_params=_cparams(("arbitrary",)),
        name="experts",
    )(tile_expert, tile_valid, tile_first, xs, w1, w3, w2)


COMB_TB = 256


def _combine_kernel(pos_ref, ys_ref, r_ref, x1_ref, g2_ref, fg_ref, o_ref, buf_ref, sem, *, final):
    tb = COMB_TB

    def row_copy(i, k):
        return pltpu.make_async_copy(ys_ref.at[pl.ds(pos_ref[0, 2 * i + k], 1)], buf_ref.at[k, pl.ds(i, 1)], sem)

    def issue(i, _):
        row_copy(i, 0).start()
        row_copy(i, 1).start()
        return 0

    def drain(i, _):
        row_copy(i, 0).wait()
        row_copy(i, 1).wait()
        return 0

    lax.fori_loop(0, tb, issue, 0)
    lax.fori_loop(0, tb, drain, 0)
    r = r_ref[...]
    y = r[:, R_W0:R_W0 + 1] * buf_ref[0] + r[:, R_W1:R_W1 + 1] * buf_ref[1]
    x2 = x1_ref[...] + g2_ref[...] * y
    if final:
        x2 = (x2 * lax.rsqrt(jnp.mean(x2 * x2, axis=-1, keepdims=True) + EPS)) * fg_ref[...]
    o_ref[...] = x2


def _combine(pos2, ys, rslab, x1, g2, final_g, seq, final):
    n = x1.shape[0]
    tb = COMB_TB
    per_b = seq // tb
    return pl.pallas_call(
        functools.partial(_combine_kernel, final=final),
        grid=(n // tb,),
        in_specs=[
            pl.BlockSpec((None, 1, 2 * tb), lambda i: (i, 0, 0), memory_space=pltpu.SMEM),
            pl.BlockSpec(memory_space=pl.ANY),
            pl.BlockSpec((tb, LANES), lambda i: (i, 0)),
            pl.BlockSpec((tb, D_MODEL), lambda i: (i, 0)),
            pl.BlockSpec((None, 1, D_MODEL), lambda i: (i // per_b, 0, 0)),
            pl.BlockSpec((1, D_MODEL), lambda i: (0, 0)),
        ],
        out_specs=pl.BlockSpec((tb, D_MODEL), lambda i: (i, 0)),
        out_shape=jax.ShapeDtypeStruct((n, D_MODEL), F32),
        scratch_shapes=[pltpu.VMEM((2, tb, D_MODEL), F32), pltpu.SemaphoreType.DMA(())],
        compiler_params=_cparams(("arbitrary",)),
        name="combine",
    )(pos2, ys, rslab, x1, g2, final_g)


def _layer(x, c, w_ada, b_ada, norm1_g, w_in, conv_w, a_log, dt_bias, dn_onorm_g, fox_f_bias,
           w_out, norm2_g, w_rg, b_rg, w_re, b_re, w1, w3, w2, final_g, final):
    bsz, seq, d = x.shape
    n = bsz * seq
    x2 = x.reshape(n, d)

    mod = _adaln(c, w_ada, b_ada)
    sh1, sc1, g1, sh2, sc2, g2 = [m.reshape(bsz, 1, d) for m in jnp.split(mod, 6, axis=-1)]

    o_a = 4 * HEAD_W
    o_b = o_a + 2 * HEADS
    o_f = o_b + 3 * HEAD_W
    w_main = jnp.concatenate([w_in[:, :o_a], w_in[:, o_b:o_f]], axis=1).astype(BF16)
    w_small = jnp.zeros((d, LANES), F32)
    w_small = w_small.at[:, 0:2 * HEADS].set(w_in[:, o_a:o_b]).at[:, 2 * HEADS:3 * HEADS].set(w_in[:, o_f:])
    ws_hi, ws_lo = _split_bf16(w_small)

    proj, gates = _inproj(x2, sc1, sh1, norm1_g.reshape(1, d), w_main, ws_hi, ws_lo, seq)
    proj3 = proj.reshape(bsz, seq, MAIN_W)

    def lane_row(vals, off):
        return jnp.zeros((1, LANES), F32).at[0, off:off + HEADS].set(vals)

    slab = _gates(gates.reshape(bsz, seq, LANES), lane_row(a_log, 0), lane_row(dt_bias, 0),
                  lane_row(fox_f_bias, L_F))
    nc = seq // CHUNK
    gct = slab[:, :, L_GC:L_GC + HEADS].reshape(bsz, nc, CHUNK, HEADS).transpose(0, 1, 3, 2)
    gct = gct.reshape(bsz, nc, HEADS, 1, CHUNK)
    f_rows = slab[:, :, L_F:L_F + HEADS].transpose(0, 2, 1).reshape(bsz, HEADS, seq // FOX_T, 1, FOX_T)

    o_dn = _deltanet(proj3, conv_w, slab, gct, dn_onorm_g.reshape(1, HEAD_DIM))
    o_fx = _fox(proj3, f_rows)

    wr = jnp.zeros((d, LANES), F32).at[:, :N_GROUPS].set(w_rg).at[:, N_GROUPS:N_GROUPS + N_EXPERTS].set(w_re)
    br = jnp.zeros((1, LANES), F32).at[0, :N_GROUPS].set(b_rg).at[0, N_GROUPS:N_GROUPS + N_EXPERTS].set(b_re)
    wr_hi, wr_lo = _split_bf16(wr)
    w_out_b = w_out.astype(BF16)
    x1, h2, rslab = _outproj(o_dn.reshape(n, HEAD_W), o_fx.reshape(n, HEAD_W), w_out_b[:HEAD_W], w_out_b[HEAD_W:],
                             x2, g1, sc2, sh2, norm2_g.reshape(1, d), wr_hi, wr_lo, br, seq)

    pos_slab, counts = _route(rslab)
    pos2 = pos_slab[:, 0:2].reshape(n // DISP_TB, 1, 2 * DISP_TB)

    cnt = counts[0, :N_EXPERTS].astype(jnp.int32)
    tiles_per = (cnt + MOE_TM - 1) // MOE_TM
    tile_end = jnp.cumsum(tiles_per)
    base = (tile_end - tiles_per) * MOE_TM
    n_tiles = (2 * n) // MOE_TM + N_EXPERTS
    p_rows = n_tiles * MOE_TM
    tid = jnp.arange(n_tiles, dtype=jnp.int32)
    tile_valid = (tid < tile_end[-1]).astype(jnp.int32)
    te_raw = jnp.minimum(jnp.sum(tid[:, None] >= tile_end[None, :], axis=1), N_EXPERTS - 1).astype(jnp.int32)
    last_e = te_raw[jnp.maximum(tile_end[-1] - 1, 0)]
    tile_expert = jnp.where(tile_valid == 1, te_raw, last_e)
    tile_first = (jnp.concatenate([jnp.array([-1], jnp.int32), tile_expert[:-1]]) != tile_expert).astype(jnp.int32)
    pad_start = base + cnt
    pad_len = tiles_per * MOE_TM - cnt

    xs = _dispatch(pad_start, pad_len, tile_end[-1:], pos2, h2, p_rows)
    ys = _experts(tile_expert, tile_valid, tile_first, xs, w1.reshape(N_EXPERTS, d, D_EXPERT),
                  w3.reshape(N_EXPERTS, d, D_EXPERT), w2.reshape(N_EXPERTS, D_EXPERT, d))
    out = _combine(pos2, ys, rslab, x1, g2, final_g.reshape(1, d), seq, final)
    return out.reshape(bsz, seq, d)


def kernel(x, c, w_ada, b_ada, norm1_g, w_in, conv_w, a_log, dt_bias, dn_onorm_g, fox_f_bias, w_out, norm2_g,
           w_router_group, b_router_group, w_router_expert, b_router_expert, w1, w3, w2, final_g):
    depth = w_ada.shape[0]
    for l in range(depth):
        x = _layer(x, c, w_ada[l], b_ada[l], norm1_g[l], w_in[l], conv_w[l], a_log[l], dt_bias[l], dn_onorm_g[l],
                   fox_f_bias[l], w_out[l], norm2_g[l], w_router_group[l], b_router_group[l], w_router_expert[l],
                   b_router_expert[l], w1[l], w3[l], w2[l], final_g, l == depth - 1)
    return x
```

```python
import functools

import jax
import jax.numpy as jnp
from jax import lax
from jax.experimental import pallas as pl
from jax.experimental.pallas import tpu as pltpu

F32 = jnp.float32
BF16 = jnp.bfloat16

D_MODEL = 2048
EPS = 1e-6
CHUNK = 64
HEADS = 8
HEAD_DIM = 128
HEAD_W = HEADS * HEAD_DIM
CONV_K = 4
N_GROUPS = 4
EXPERTS_PER_GROUP = 8
N_EXPERTS = N_GROUPS * EXPERTS_PER_GROUP
D_EXPERT = 512
LANES = 128
SUBLANES = 8
MAIN_W = 7 * HEAD_W
VMEM_LIMIT = 56 * 1024 * 1024

L_GC, L_BETA, L_F, L_EGC, L_EK, L_ELAST = 0, 8, 16, 24, 32, 40


def _cparams(sem):
    return pltpu.CompilerParams(dimension_semantics=sem, vmem_limit_bytes=VMEM_LIMIT)


def _split_bf16(a):
    hi = a.astype(BF16)
    lo = (a - hi.astype(F32)).astype(BF16)
    return hi, lo


def _dot(a, b):
    return jnp.dot(a, b, preferred_element_type=F32)


def _dot_nt(a, b):
    return lax.dot_general(a, b, (((1,), (1,)), ((), ())), preferred_element_type=F32)


def _dot3(a, b):
    ah, al = _split_bf16(a)
    bh, bl = _split_bf16(b)
    return _dot(ah, bh) + (_dot(al, bh) + _dot(ah, bl))


def _dot3_pre(a, bh, bl):
    ah, al = _split_bf16(a)
    return _dot(ah, bh) + (_dot(al, bh) + _dot(ah, bl))


def _softplus(x):
    return jnp.maximum(x, 0.0) + jnp.log1p(jnp.exp(-jnp.abs(x)))


def _silu(x):
    return x * jax.nn.sigmoid(x)


def _adaln_kernel(c_ref, w_ref, b_ref, o_ref):
    c = c_ref[...]
    o_ref[...] = _dot(_silu(c).astype(BF16), w_ref[...].astype(BF16)) + b_ref[...]


def _adaln(c, w, b):
    bsz = c.shape[0]
    n = w.shape[1]
    tn = 1024
    cp = jnp.zeros((SUBLANES, D_MODEL), F32).at[:bsz].set(c)
    out = pl.pallas_call(
        _adaln_kernel,
        grid=(n // tn,),
        in_specs=[
            pl.BlockSpec((SUBLANES, D_MODEL), lambda j: (0, 0)),
            pl.BlockSpec((D_MODEL, tn), lambda j: (0, j)),
            pl.BlockSpec((1, tn), lambda j: (0, j)),
        ],
        out_specs=pl.BlockSpec((SUBLANES, tn), lambda j: (0, j)),
        out_shape=jax.ShapeDtypeStruct((SUBLANES, n), F32),
        compiler_params=_cparams(("parallel",)),
        name="adaln",
    )(cp, w, b.reshape(1, n))
    return out[:bsz]


INPROJ_TM = 1024
INPROJ_TN = 512
ROW_STEP = 128


def _modulated_norm(x, g, sc, sh):
    r = lax.rsqrt(jnp.mean(x * x, axis=-1, keepdims=True) + EPS)
    return ((x * r) * g) * (1.0 + sc) + sh


def _inproj_kernel(x_ref, sc_ref, sh_ref, g_ref, w_ref, wsh_ref, wsl_ref, o_ref, og_ref, h_ref):
    @pl.when(pl.program_id(1) == 0)
    def _():
        def body(i, _):
            rows = pl.ds(pl.multiple_of(i * ROW_STEP, ROW_STEP), ROW_STEP)
            h = _modulated_norm(x_ref[rows, :], g_ref[...], sc_ref[...], sh_ref[...])
            h_ref[rows, :] = h.astype(BF16)
            og_ref[rows, :] = _dot3_pre(h, wsh_ref[...], wsl_ref[...])
            return 0

        lax.fori_loop(0, INPROJ_TM // ROW_STEP, body, 0)

    o_ref[...] = _dot(h_ref[...], w_ref[...]).astype(BF16)


def _inproj(x2, sc1, sh1, g, w_main, ws_hi, ws_lo, seq):
    n = x2.shape[0]
    tm, tn = INPROJ_TM, INPROJ_TN
    per_b = seq // tm
    return pl.pallas_call(
        _inproj_kernel,
        grid=(n // tm, MAIN_W // tn),
        in_specs=[
            pl.BlockSpec((tm, D_MODEL), lambda i, j: (i, 0)),
            pl.BlockSpec((None, 1, D_MODEL), lambda i, j: (i // per_b, 0, 0)),
            pl.BlockSpec((None, 1, D_MODEL), lambda i, j: (i // per_b, 0, 0)),
            pl.BlockSpec((1, D_MODEL), lambda i, j: (0, 0)),
            pl.BlockSpec((D_MODEL, tn), lambda i, j: (0, j)),
            pl.BlockSpec((D_MODEL, LANES), lambda i, j: (0, 0)),
            pl.BlockSpec((D_MODEL, LANES), lambda i, j: (0, 0)),
        ],
        out_specs=[
            pl.BlockSpec((tm, tn), lambda i, j: (i, j)),
            pl.BlockSpec((tm, LANES), lambda i, j: (i, 0)),
        ],
        out_shape=[
            jax.ShapeDtypeStruct((n, MAIN_W), BF16),
            jax.ShapeDtypeStruct((n, LANES), F32),
        ],
        scratch_shapes=[pltpu.VMEM((tm, D_MODEL), BF16)],
        compiler_params=_cparams(("parallel", "arbitrary")),
        name="inproj",
    )(x2, sc1, sh1, g, w_main, ws_hi, ws_lo)


GATES_TB = 256


def _split3(a):
    p0 = a.astype(BF16)
    r1 = a - p0.astype(F32)
    p1 = r1.astype(BF16)
    p2 = (r1 - p1.astype(F32)).astype(BF16)
    return p0, p1, p2


def _dot_ones(m, a):
    p0, p1, p2 = _split3(a)
    return _dot(m, p0) + (_dot(m, p1) + _dot(m, p2))


def _gates_kernel(x_ref, alog_ref, dt_ref, fb_ref, o_ref, carry_ref):
    tb = GATES_TB

    @pl.when(pl.program_id(1) == 0)
    def _():
        carry_ref[...] = jnp.zeros_like(carry_ref)

    x = x_ref[...]
    lane = lax.broadcasted_iota(jnp.int32, (tb, LANES), 1)
    g = -jnp.exp(alog_ref[...]) * _softplus(x + dt_ref[...])
    beta = jax.nn.sigmoid(x)
    lf = -_softplus(-(x + fb_ref[...]))

    ri = lax.broadcasted_iota(jnp.int32, (tb, tb), 0)
    ci = lax.broadcasted_iota(jnp.int32, (tb, tb), 1)
    same_chunk = (ri // CHUNK) == (ci // CHUNK)
    tri = (ri >= ci)
    m_all = jnp.where(tri, 1.0, 0.0).astype(BF16)
    m_chunk = jnp.where(tri & same_chunk, 1.0, 0.0).astype(BF16)
    m_tot = jnp.where(same_chunk, 1.0, 0.0).astype(BF16)

    gc = _dot_ones(m_chunk, g)
    glast = _dot_ones(m_tot, g)
    fcum = _dot_ones(m_all, lf) + carry_ref[...]
    carry_ref[...] = fcum[tb - 1:tb, :]

    in_a = lane < HEADS
    egc = jnp.where(in_a, jnp.exp(gc), 0.0)
    ek = jnp.where(in_a, jnp.exp(glast - gc), 0.0)
    elast = jnp.where(in_a, jnp.exp(glast), 0.0)
    out = jnp.where(in_a, gc, jnp.where(lane < 2 * HEADS, beta, jnp.where(lane < 3 * HEADS, fcum, 0.0)))
    out = out + pltpu.roll(egc, L_EGC, 1) + pltpu.roll(ek, L_EK, 1) + pltpu.roll(elast, L_ELAST, 1)
    o_ref[...] = out


def _gates(gates, alog_row, dt_row, fb_row):
    bsz, seq, _ = gates.shape
    tb = GATES_TB
    row = pl.BlockSpec((1, LANES), lambda b, t: (0, 0))
    return pl.pallas_call(
        _gates_kernel,
        grid=(bsz, seq // tb),
        in_specs=[pl.BlockSpec((None, tb, LANES), lambda b, t: (b, t, 0)), row, row, row],
        out_specs=pl.BlockSpec((None, tb, LANES), lambda b, t: (b, t, 0)),
        out_shape=jax.ShapeDtypeStruct((bsz, seq, LANES), F32),
        scratch_shapes=[pltpu.VMEM((1, LANES), F32)],
        compiler_params=_cparams(("parallel", "arbitrary")),
        name="gates",
    )(gates, alog_row, dt_row, fb_row)


DN_TB = 256
HALO = SUBLANES


def _bdot(a, b):
    return lax.dot_general(a, b, (((2,), (1,)), ((0,), (0,))), preferred_element_type=F32)


def _bdot_nt(a, b):
    return lax.dot_general(a, b, (((2,), (2,)), ((0,), (0,))), preferred_element_type=F32)


def _bdot3(a, b):
    ah, al = _split_bf16(a)
    bh, bl = _split_bf16(b)
    return _bdot(ah, bh) + (_bdot(al, bh) + _bdot(ah, bl))


def _inv_unit_lower(a, eye, blk16, blk32):
    n = jnp.where(blk16, -a, 0.0)
    e1 = jnp.where(blk32 & jnp.logical_not(blk16), a, 0.0)
    e2 = jnp.where(blk32, 0.0, a)
    t = eye + n
    p = _bdot3(n, n)
    t = t + _bdot3(t, p)
    p = _bdot3(p, p)
    t = t + _bdot3(t, p)
    p = _bdot3(p, p)
    t = t + _bdot3(t, p)
    t = t - _bdot3(_bdot3(t, e1), t)
    t = t - _bdot3(_bdot3(t, e2), t)
    return t


def _deltanet_kernel(q_ref, k_ref, v_ref, z_ref, wq_ref, wk_ref, wv_ref, slab_ref, gct_ref, og_ref,
                     o_ref, ext_ref, qn_ref, kn_ref, vv_ref, s_ref):
    tb = DN_TB

    @pl.when(pl.program_id(1) == 0)
    def _():
        ext_ref[:, 0:HALO, :] = jnp.zeros((3, HALO, HEAD_W), F32)
        s_ref[...] = jnp.zeros_like(s_ref)

    for idx, (u_ref, w_ref) in enumerate(((q_ref, wq_ref), (k_ref, wk_ref), (v_ref, wv_ref))):
        for h in range(HEADS):
            cols = slice(h * HEAD_DIM, (h + 1) * HEAD_DIM)
            ext_ref[idx, HALO:HALO + tb, cols] = u_ref[:, cols].astype(F32)
            y = None
            for j in range(CONV_K):
                start = HALO - (CONV_K - 1) + j
                term = ext_ref[idx, start:start + tb, cols] * w_ref[j:j + 1, cols]
                y = term if y is None else y + term
            y = _silu(y)
            if idx == 2:
                vv_ref[h] = y
            else:
                yn = y * lax.rsqrt(jnp.sum(y * y, axis=-1, keepdims=True) + EPS)
                if idx == 0:
                    qn_ref[h] = (yn * (HEAD_DIM ** -0.5)).astype(BF16)
                else:
                    kn_ref[h] = yn.astype(BF16)
        ext_ref[idx, 0:HALO, :] = ext_ref[idx, tb:tb + HALO, :]

    ri = lax.broadcasted_iota(jnp.int32, (HEADS, CHUNK, CHUNK), 1)
    ci = lax.broadcasted_iota(jnp.int32, (HEADS, CHUNK, CHUNK), 2)
    incl = ri >= ci
    strict = ri > ci
    eye = jnp.where(ri == ci, 1.0, 0.0)
    blk16 = (ri // 16) == (ci // 16)
    blk32 = (ri // 32) == (ci // 32)

    def chunk_body(c, _):
        rows = pl.ds(pl.multiple_of(c * CHUNK, CHUNK), CHUNK)
        slab = slab_ref[rows, :]

        def col(off, width):
            return jnp.stack([jnp.broadcast_to(slab[:, off + h:off + h + 1], (CHUNK, width)) for h in range(HEADS)])

        q = qn_ref[:, rows, :]
        k = kn_ref[:, rows, :]
        v = vv_ref[:, rows, :]
        beta = col(L_BETA, HEAD_DIM)
        egc = col(L_EGC, HEAD_DIM)
        gc_row = gct_ref[c]

        decay = jnp.where(incl, jnp.exp(col(L_GC, CHUNK) - gc_row), 0.0)
        kk = _bdot_nt(k, k)
        qk = _bdot_nt(q, k) * decay
        a = jnp.where(strict, kk * decay * beta[:, :, :CHUNK], 0.0)
        t = _inv_unit_lower(a, eye, blk16, blk32)
        th, tl = _split_bf16(t)

        kf = k.astype(F32)
        vb = (v * beta).astype(BF16)
        kbg = (kf * (beta * egc)).astype(BF16)
        u = _bdot(th, vb) + _bdot(tl, vb)
        w = _bdot(th, kbg) + _bdot(tl, kbg)

        s = s_ref[...]
        sb = s.astype(BF16)
        v_new = u - _bdot(w.astype(BF16), sb)
        vnb = v_new.astype(BF16)
        qd = (q.astype(F32) * egc).astype(BF16)
        o = _bdot(qd, sb) + _bdot(qk.astype(BF16), vnb)
        kd = kf * col(L_EK, HEAD_DIM)
        kdt = jnp.stack([kd[h].T for h in range(HEADS)]).astype(BF16)
        elast = jnp.stack([jnp.broadcast_to(slab[CHUNK - 1:CHUNK, L_ELAST + h:L_ELAST + h + 1], (HEAD_DIM, HEAD_DIM))
                           for h in range(HEADS)])
        s_ref[...] = s * elast + _bdot(kdt, vnb)

        r = lax.rsqrt(jnp.mean(o * o, axis=-1, keepdims=True) + EPS)
        on = (o * r) * og_ref[...]
        for h in range(HEADS):
            cols = slice(h * HEAD_DIM, (h + 1) * HEAD_DIM)
            o_ref[rows, cols] = (on[h] * _silu(z_ref[rows, cols].astype(F32))).astype(BF16)
        return 0

    lax.fori_loop(0, tb // CHUNK, chunk_body, 0)


def _deltanet(proj3, conv_w, slab, gct, onorm_g):
    bsz, seq, _ = proj3.shape
    tb = DN_TB
    nct = tb // CHUNK

    def colblk(j):
        return pl.BlockSpec((None, tb, HEAD_W), lambda b, t: (b, t, j))

    def wblk(j):
        return pl.BlockSpec((CONV_K, HEAD_W), lambda b, t: (0, j))

    return pl.pallas_call(
        _deltanet_kernel,
        grid=(bsz, seq // tb),
        in_specs=[
            colblk(0), colblk(1), colblk(2), colblk(3),
            wblk(0), wblk(1), wblk(2),
            pl.BlockSpec((None, tb, LANES), lambda b, t: (b, t, 0)),
            pl.BlockSpec((None, nct, HEADS, 1, CHUNK), lambda b, t: (b, t, 0, 0, 0)),
            pl.BlockSpec((1, HEAD_DIM), lambda b, t: (0, 0)),
        ],
        out_specs=pl.BlockSpec((None, tb, HEAD_W), lambda b, t: (b, t, 0)),
        out_shape=jax.ShapeDtypeStruct((bsz, seq, HEAD_W), BF16),
        scratch_shapes=[
            pltpu.VMEM((3, tb + HALO, HEAD_W), F32),
            pltpu.VMEM((HEADS, tb, HEAD_DIM), BF16),
            pltpu.VMEM((HEADS, tb, HEAD_DIM), BF16),
            pltpu.VMEM((HEADS, tb, HEAD_DIM), F32),
            pltpu.VMEM((HEADS, HEAD_DIM, HEAD_DIM), F32),
        ],
        compiler_params=_cparams(("parallel", "arbitrary")),
        name="deltanet",
    )(proj3, proj3, proj3, proj3, conv_w, conv_w, conv_w, slab, gct, onorm_g)


FOX_T = 512
FOX_STRIP = 32
NEG_INF = float("-inf")
LOG2E = 1.4426950408889634


def _fox_kernel(q_ref, k_ref, v_ref, f_ref, o_ref, s_a_ref, s_b_ref, p_a_ref, p_b_ref, m_ref, l_ref, alpha_ref,
                acc_ref):
    tq = FOX_T
    i = pl.program_id(2)
    qs = (q_ref[...].astype(F32) * (HEAD_DIM ** -0.5 * LOG2E)).astype(BF16)
    f0 = f_ref[i][:, 0:1]
    m_ref[...] = jnp.full_like(m_ref, NEG_INF)
    l_ref[...] = jnp.zeros_like(l_ref)
    acc_ref[...] = jnp.zeros_like(acc_ref)
    lane_blocks = tq // LANES

    def keys(j):
        return pl.ds(pl.multiple_of(j * tq, tq), tq)

    def scores(j):
        return _dot_nt(qs, k_ref[keys(j), :])

    def softmax(j, s_view, p_view, masked):
        bias = (f0 - f_ref[j]) * LOG2E

        def logits(r):
            rows = slice(r * FOX_STRIP, (r + 1) * FOX_STRIP)
            s = s_view[rows, :] + bias
            if masked:
                ri = r * FOX_STRIP + lax.broadcasted_iota(jnp.int32, (FOX_STRIP, tq), 0)
                ci = lax.broadcasted_iota(jnp.int32, (FOX_STRIP, tq), 1)
                s = jnp.where(ci <= ri, s, NEG_INF)
            return rows, s

        for r in range(tq // FOX_STRIP):
            rows, s = logits(r)
            m_old = m_ref[rows, :]
            m_new = jnp.maximum(m_old, jnp.max(s, axis=-1, keepdims=True))
            alpha_ref[rows, :] = jnp.exp2(m_old - m_new)
            m_ref[rows, :] = m_new
        for r in range(tq // FOX_STRIP):
            rows, s = logits(r)
            p = jnp.exp2(s - jnp.concatenate([m_ref[rows, :]] * lane_blocks, axis=1))
            alpha = alpha_ref[rows, :]
            l_ref[rows, :] = alpha * l_ref[rows, :] + jnp.sum(p, axis=-1, keepdims=True)
            p_view[rows, :] = p.astype(BF16)

    s_a_ref[...] = scores(i)
    s_b_ref[...] = scores(jnp.maximum(i - 1, 0))
    softmax(i, s_a_ref, p_a_ref, True)

    def step(t, s_cur, p_cur, s_nxt, p_prv):
        j = i - t
        s_nxt[...] = scores(jnp.maximum(j - 1, 0))
        pv = _dot(p_prv[...], v_ref[keys(j + 1), :])
        softmax(j, s_cur, p_cur, False)
        acc_ref[...] = alpha_ref[...] * (acc_ref[...] + pv)

    def pair(u, _):
        step(2 * u + 1, s_b_ref, p_b_ref, s_a_ref, p_a_ref)
        step(2 * u + 2, s_a_ref, p_a_ref, s_b_ref, p_b_ref)
        return 0

    lax.fori_loop(0, lax.shift_right_logical(i, 1), pair, 0)

    def finish(p_last):
        acc = acc_ref[...] + _dot(p_last[...], v_ref[keys(0), :])
        o_ref[...] = (acc / l_ref[...]).astype(BF16)

    @pl.when((i & 1) == 1)
    def _():
        step(i, s_b_ref, p_b_ref, s_a_ref, p_a_ref)
        finish(p_b_ref)

    @pl.when((i & 1) == 0)
    def _():
        finish(p_a_ref)


def _fox(proj3, f_rows):
    bsz, seq, _ = proj3.shape
    tq = FOX_T
    qb, kb, vb = 4 * HEADS, 5 * HEADS, 6 * HEADS
    return pl.pallas_call(
        _fox_kernel,
        grid=(bsz, HEADS, seq // tq),
        in_specs=[
            pl.BlockSpec((None, tq, HEAD_DIM), lambda b, h, i: (b, i, qb + h)),
            pl.BlockSpec((None, seq, HEAD_DIM), lambda b, h, i: (b, 0, kb + h)),
            pl.BlockSpec((None, seq, HEAD_DIM), lambda b, h, i: (b, 0, vb + h)),
            pl.BlockSpec((None, None, seq // tq, 1, tq), lambda b, h, i: (b, h, 0, 0, 0)),
        ],
        out_specs=pl.BlockSpec((None, tq, HEAD_DIM), lambda b, h, i: (b, i, h)),
        out_shape=jax.ShapeDtypeStruct((bsz, seq, HEAD_W), BF16),
        scratch_shapes=[
            pltpu.VMEM((tq, tq), F32),
            pltpu.VMEM((tq, tq), F32),
            pltpu.VMEM((tq, tq), BF16),
            pltpu.VMEM((tq, tq), BF16),
            pltpu.VMEM((tq, LANES), F32),
            pltpu.VMEM((tq, LANES), F32),
            pltpu.VMEM((tq, LANES), F32),
            pltpu.VMEM((tq, HEAD_DIM), F32),
        ],
        compiler_params=_cparams(("parallel", "parallel", "arbitrary")),
        name="fox",
    )(proj3, proj3, proj3, f_rows)


OUT_TM = 512
R_E0, R_E1, R_W0, R_W1 = 0, 1, 2, 3


def _first_argmax(vals, lane):
    m = jnp.max(vals, axis=-1, keepdims=True)
    idx = jnp.min(jnp.where(vals == m, lane, LANES), axis=-1, keepdims=True)
    return m, idx


def _outproj_kernel(oa_ref, ob_ref, wa_ref, wb_ref, x_ref, g1_ref, sc_ref, sh_ref, g_ref,
                    wrh_ref, wrl_ref, br_ref, x1_ref, h2_ref, r_ref):
    mix = _dot(oa_ref[...], wa_ref[...]) + _dot(ob_ref[...], wb_ref[...])
    x1 = x_ref[...] + g1_ref[...] * mix
    x1_ref[...] = x1
    h2 = _modulated_norm(x1, g_ref[...], sc_ref[...], sh_ref[...])
    h2_ref[...] = h2

    logits = _dot3_pre(h2, wrh_ref[...], wrl_ref[...]) + br_ref[...]
    tm = logits.shape[0]
    lane = lax.broadcasted_iota(jnp.int32, (tm, LANES), 1)
    gl = jnp.where(lane < N_GROUPS, logits, NEG_INF)
    gmax, gidx = _first_argmax(gl, lane)
    pg = 1.0 / jnp.sum(jnp.exp(gl - gmax), axis=-1, keepdims=True)
    e_lane = lane - N_GROUPS
    in_grp = (e_lane >= gidx * EXPERTS_PER_GROUP) & (e_lane < (gidx + 1) * EXPERTS_PER_GROUP)
    el = jnp.where(in_grp, logits, NEG_INF)
    v0, i0 = _first_argmax(el, lane)
    v1, i1 = _first_argmax(jnp.where(lane == i0, NEG_INF, el), lane)
    ex = jnp.exp(v1 - v0)
    w0 = pg / (1.0 + ex)
    w1 = pg * ex / (1.0 + ex)
    e0 = (i0 - N_GROUPS).astype(F32)
    e1 = (i1 - N_GROUPS).astype(F32)
    r_ref[...] = jnp.where(lane == R_E0, e0, jnp.where(lane == R_E1, e1,
                           jnp.where(lane == R_W0, w0, jnp.where(lane == R_W1, w1, 0.0))))


def _outproj(o_a, o_b, wa, wb, x2, g1, sc2, sh2, g, wr_hi, wr_lo, br, seq):
    n = x2.shape[0]
    tm = OUT_TM
    per_b = seq // tm
    modrow = pl.BlockSpec((None, 1, D_MODEL), lambda i: (i // per_b, 0, 0))
    const = lambda shape: pl.BlockSpec(shape, lambda i: (0, 0))
    return pl.pallas_call(
        _outproj_kernel,
        grid=(n // tm,),
        in_specs=[
            pl.BlockSpec((tm, HEAD_W), lambda i: (i, 0)),
            pl.BlockSpec((tm, HEAD_W), lambda i: (i, 0)),
            const((HEAD_W, D_MODEL)), const((HEAD_W, D_MODEL)),
            pl.BlockSpec((tm, D_MODEL), lambda i: (i, 0)),
            modrow, modrow, modrow,
            const((1, D_MODEL)),
            const((D_MODEL, LANES)), const((D_MODEL, LANES)), const((1, LANES)),
        ],
        out_specs=[
            pl.BlockSpec((tm, D_MODEL), lambda i: (i, 0)),
            pl.BlockSpec((tm, D_MODEL), lambda i: (i, 0)),
            pl.BlockSpec((tm, LANES), lambda i: (i, 0)),
        ],
        out_shape=[
            jax.ShapeDtypeStruct((n, D_MODEL), F32),
            jax.ShapeDtypeStruct((n, D_MODEL), F32),
            jax.ShapeDtypeStruct((n, LANES), F32),
        ],
        compiler_params=_cparams(("parallel",)),
        name="outproj",
    )(o_a, o_b, wa, wb, x2, g1, sc2, sh2, g, wr_hi, wr_lo, br)


MOE_TM = 256
ROUTE_TB = 512


def _route_kernel(r_ref, pos_ref, cnt_ref, run_ref, base_ref):
    tb = ROUTE_TB
    phase = pl.program_id(0)
    t = pl.program_id(1)
    r = r_ref[...]
    lane = lax.broadcasted_iota(jnp.int32, (tb, LANES), 1)
    e0 = r[:, R_E0:R_E0 + 1].astype(jnp.int32)
    e1 = r[:, R_E1:R_E1 + 1].astype(jnp.int32)
    oh0 = lane == e0
    oh1 = lane == e1
    both = jnp.where(oh0 | oh1, 1.0, 0.0)

    @pl.when((phase == 0) & (t == 0))
    def _():
        run_ref[...] = jnp.zeros_like(run_ref)

    @pl.when(phase == 0)
    def _():
        run_ref[...] = run_ref[...] + jnp.sum(both, axis=0, keepdims=True)

    @pl.when((phase == 1) & (t == 0))
    def _():
        counts = run_ref[...]
        cnt_ref[...] = counts
        padded = jnp.ceil(counts / MOE_TM) * MOE_TM
        li = lax.broadcasted_iota(jnp.int32, (LANES, LANES), 0)
        lj = lax.broadcasted_iota(jnp.int32, (LANES, LANES), 1)
        upper = jnp.where(li < lj, 1.0, 0.0).astype(BF16)
        hi = jnp.floor(padded / 256.0)
        lo = padded - hi * 256.0
        hi8 = jnp.broadcast_to(hi, (SUBLANES, LANES)).astype(BF16)
        lo8 = jnp.broadcast_to(lo, (SUBLANES, LANES)).astype(BF16)
        base = _dot(hi8, upper) * 256.0 + _dot(lo8, upper)
        base_ref[...] = base[0:1, :]
        run_ref[...] = jnp.zeros_like(run_ref)

    @pl.when(phase == 1)
    def _():
        ri = lax.broadcasted_iota(jnp.int32, (tb, tb), 0)
        ci = lax.broadcasted_iota(jnp.int32, (tb, tb), 1)
        strict = jnp.where(ri > ci, 1.0, 0.0).astype(BF16)
        before = _dot(strict, both.astype(BF16)) + run_ref[...] + base_ref[...]
        p0 = jnp.sum(jnp.where(oh0, before, 0.0), axis=-1, keepdims=True)
        p1 = jnp.sum(jnp.where(oh1, before, 0.0), axis=-1, keepdims=True)
        pos_ref[...] = jnp.where(lane == 0, p0, jnp.where(lane == 1, p1, 0.0)).astype(jnp.int32)
        run_ref[...] = run_ref[...] + jnp.sum(both, axis=0, keepdims=True)


def _route(rslab):
    n = rslab.shape[0]
    tb = ROUTE_TB
    return pl.pallas_call(
        _route_kernel,
        grid=(2, n // tb),
        in_specs=[pl.BlockSpec((tb, LANES), lambda p, t: (t, 0))],
        out_specs=[
            pl.BlockSpec((tb, LANES), lambda p, t: (p * t, 0)),
            pl.BlockSpec((1, LANES), lambda p, t: (0, 0)),
        ],
        out_shape=[
            jax.ShapeDtypeStruct((n, LANES), jnp.int32),
            jax.ShapeDtypeStruct((1, LANES), F32),
        ],
        scratch_shapes=[pltpu.VMEM((1, LANES), F32), pltpu.VMEM((1, LANES), F32)],
        compiler_params=_cparams(("arbitrary", "arbitrary")),
        name="route",
    )(rslab)


DISP_TB = 256
DMA_UNROLL = 8


def _dispatch_kernel(pad_start_ref, pad_len_ref, used_ref, pos_ref, h2_ref, xs_ref, zero_ref, sem, zsem):
    t = pl.program_id(0)
    tb = DISP_TB
    n_tiles = xs_ref.shape[0] // MOE_TM

    def row_copy(i, k):
        return pltpu.make_async_copy(h2_ref.at[pl.ds(i, 1)], xs_ref.at[pl.ds(pos_ref[0, 2 * i + k], 1)], sem)

    def issue(i, _):
        row_copy(i, 0).start(priority=0)
        row_copy(i, 1).start(priority=1)
        return 0

    lax.fori_loop(0, tb, issue, 0, unroll=DMA_UNROLL)

    @pl.when(t == 0)
    def _():
        zero_ref[...] = jnp.zeros_like(zero_ref)

        def zero_rows(off, rows):
            cp = pltpu.make_async_copy(zero_ref.at[pl.ds(0, rows)], xs_ref.at[pl.ds(off, rows)], zsem)
            cp.start()
            cp.wait()

        def per_expert(e, _):
            start = pad_start_ref[e]
            head = (-start) & (SUBLANES - 1)
            for r in range(SUBLANES - 1):
                pl.when(r < head)(functools.partial(zero_rows, start + r, 1))
            off = start + head
            rest = pad_len_ref[e] - head
            piece = MOE_TM // 2
            while piece >= SUBLANES:
                take = (rest & piece) != 0
                pl.when(take)(functools.partial(zero_rows, pl.multiple_of(off, SUBLANES), piece))
                off = off + jnp.where(take, piece, 0)
                piece //= 2
            return 0

        lax.fori_loop(0, N_EXPERTS, per_expert, 0)

        def per_tile(i, _):
            zero_rows(pl.multiple_of(i * MOE_TM, MOE_TM), MOE_TM)
            return 0

        lax.fori_loop(used_ref[0], n_tiles, per_tile, 0)

    def drain(i, _):
        row_copy(i, 0).wait()
        row_copy(i, 1).wait()
        return 0

    lax.fori_loop(0, tb, drain, 0, unroll=DMA_UNROLL)


def _dispatch(pad_start, pad_len, used, pos2, h2, p_rows):
    n = h2.shape[0]
    tb = DISP_TB
    return pl.pallas_call(
        _dispatch_kernel,
        grid_spec=pltpu.PrefetchScalarGridSpec(
            num_scalar_prefetch=3,
            grid=(n // tb,),
            in_specs=[
                pl.BlockSpec((None, 1, 2 * tb), lambda t, *_: (t, 0, 0), memory_space=pltpu.SMEM),
                pl.BlockSpec((tb, D_MODEL), lambda t, *_: (t, 0)),
            ],
            out_specs=pl.BlockSpec(memory_space=pl.ANY),
            scratch_shapes=[
                pltpu.VMEM((MOE_TM, D_MODEL), F32),
                pltpu.SemaphoreType.DMA(()),
                pltpu.SemaphoreType.DMA(()),
            ],
        ),
        out_shape=jax.ShapeDtypeStruct((p_rows, D_MODEL), F32),
        compiler_params=_cparams(("arbitrary",)),
        name="dispatch",
    )(pad_start, pad_len, used, pos2, h2)


def _experts_kernel(te_ref, tv_ref, tf_ref, xs_ref, w1_ref, w3_ref, w2_ref, ys_ref,
                    w1b_ref, w3b_ref, w2b_ref):
    i = pl.program_id(0)

    @pl.when(tv_ref[i] != 0)
    def _():
        @pl.when(tf_ref[i] != 0)
        def _():
            w1b_ref[...] = w1_ref[...].astype(BF16)
            w3b_ref[...] = w3_ref[...].astype(BF16)
            w2b_ref[...] = w2_ref[...].astype(BF16)

        x = xs_ref[...].astype(BF16)
        a = _dot(x, w1b_ref[...])
        b = _dot(x, w3b_ref[...])
        ys_ref[...] = _dot((_silu(a) * b).astype(BF16), w2b_ref[...])

    @pl.when(tv_ref[i] == 0)
    def _():
        ys_ref[...] = jnp.zeros_like(ys_ref)


def _experts(tile_expert, tile_valid, tile_first, xs, w1, w3, w2):
    p_rows = xs.shape[0]
    tm = MOE_TM
    wspec_in = pl.BlockSpec((None, D_MODEL, D_EXPERT), lambda i, te, tv, tf: (te[i], 0, 0))
    return pl.pallas_call(
        _experts_kernel,
        grid_spec=pltpu.PrefetchScalarGridSpec(
            num_scalar_prefetch=3,
            grid=(p_rows // tm,),
            in_specs=[
                pl.BlockSpec((tm, D_MODEL), lambda i, te, tv, tf: (i, 0)),
                wspec_in, wspec_in,
                pl.BlockSpec((None, D_EXPERT, D_MODEL), lambda i, te, tv, tf: (te[i], 0, 0)),
            ],
            out_specs=pl.BlockSpec((tm, D_MODEL), lambda i, te, tv, tf: (i, 0)),
            scratch_shapes=[
                pltpu.VMEM((D_MODEL, D_EXPERT), BF16),
                pltpu.VMEM((D_MODEL, D_EXPERT), BF16),
                pltpu.VMEM((D_EXPERT, D_MODEL), BF16),
            ],
        ),
        out_shape=jax.ShapeDtypeStruct((p_rows, D_MODEL), F32),
        compiler_params=_cparams(("arbitrary",)),
        name="experts",
    )(tile_expert, tile_valid, tile_first, xs, w1, w3, w2)


COMB_TB = 256


def _combine_kernel(pos_ref, ys_ref, r_ref, x1_ref, g2_ref, fg_ref, o_ref, buf_ref, sem, *, final):
    tb = COMB_TB

    def row_copy(i, k):
        return pltpu.make_async_copy(ys_ref.at[pl.ds(pos_ref[0, 2 * i + k], 1)], buf_ref.at[k, pl.ds(i, 1)], sem)

    def issue(i, _):
        row_copy(i, 0).start(priority=0)
        row_copy(i, 1).start(priority=1)
        return 0

    def drain(i, _):
        row_copy(i, 0).wait()
        row_copy(i, 1).wait()
        return 0

    lax.fori_loop(0, tb, issue, 0, unroll=DMA_UNROLL)
    lax.fori_loop(0, tb, drain, 0, unroll=DMA_UNROLL)
    r = r_ref[...]
    y = r[:, R_W0:R_W0 + 1] * buf_ref[0] + r[:, R_W1:R_W1 + 1] * buf_ref[1]
    x2 = x1_ref[...] + g2_ref[...] * y
    if final:
        x2 = (x2 * lax.rsqrt(jnp.mean(x2 * x2, axis=-1, keepdims=True) + EPS)) * fg_ref[...]
    o_ref[...] = x2


def _combine(pos2, ys, rslab, x1, g2, final_g, seq, final):
    n = x1.shape[0]
    tb = COMB_TB
    per_b = seq // tb
    return pl.pallas_call(
        functools.partial(_combine_kernel, final=final),
        grid=(n // tb,),
        in_specs=[
            pl.BlockSpec((None, 1, 2 * tb), lambda i: (i, 0, 0), memory_space=pltpu.SMEM),
            pl.BlockSpec(memory_space=pl.ANY),
            pl.BlockSpec((tb, LANES), lambda i: (i, 0)),
            pl.BlockSpec((tb, D_MODEL), lambda i: (i, 0)),
            pl.BlockSpec((None, 1, D_MODEL), lambda i: (i // per_b, 0, 0)),
            pl.BlockSpec((1, D_MODEL), lambda i: (0, 0)),
        ],
        out_specs=pl.BlockSpec((tb, D_MODEL), lambda i: (i, 0)),
        out_shape=jax.ShapeDtypeStruct((n, D_MODEL), F32),
        scratch_shapes=[pltpu.VMEM((2, tb, D_MODEL), F32), pltpu.SemaphoreType.DMA(())],
        compiler_params=_cparams(("arbitrary",)),
        name="combine",
    )(pos2, ys, rslab, x1, g2, final_g)


def _layer(x, c, w_ada, b_ada, norm1_g, w_in, conv_w, a_log, dt_bias, dn_onorm_g, fox_f_bias,
           w_out, norm2_g, w_rg, b_rg, w_re, b_re, w1, w3, w2, final_g, final):
    bsz, seq, d = x.shape
    n = bsz * seq
    x2 = x.reshape(n, d)

    mod = _adaln(c, w_ada, b_ada)
    sh1, sc1, g1, sh2, sc2, g2 = [m.reshape(bsz, 1, d) for m in jnp.split(mod, 6, axis=-1)]

    o_a = 4 * HEAD_W
    o_b = o_a + 2 * HEADS
    o_f = o_b + 3 * HEAD_W
    w_main = jnp.concatenate([w_in[:, :o_a], w_in[:, o_b:o_f]], axis=1).astype(BF16)
    w_small = jnp.zeros((d, LANES), F32)
    w_small = w_small.at[:, 0:2 * HEADS].set(w_in[:, o_a:o_b]).at[:, 2 * HEADS:3 * HEADS].set(w_in[:, o_f:])
    ws_hi, ws_lo = _split_bf16(w_small)

    proj, gates = _inproj(x2, sc1, sh1, norm1_g.reshape(1, d), w_main, ws_hi, ws_lo, seq)
    proj3 = proj.reshape(bsz, seq, MAIN_W)

    def lane_row(vals, off):
        return jnp.zeros((1, LANES), F32).at[0, off:off + HEADS].set(vals)

    slab = _gates(gates.reshape(bsz, seq, LANES), lane_row(a_log, 0), lane_row(dt_bias, 0),
                  lane_row(fox_f_bias, L_F))
    nc = seq // CHUNK
    gct = slab[:, :, L_GC:L_GC + HEADS].reshape(bsz, nc, CHUNK, HEADS).transpose(0, 1, 3, 2)
    gct = gct.reshape(bsz, nc, HEADS, 1, CHUNK)
    f_rows = slab[:, :, L_F:L_F + HEADS].transpose(0, 2, 1).reshape(bsz, HEADS, seq // FOX_T, 1, FOX_T)

    o_dn = _deltanet(proj3, conv_w, slab, gct, dn_onorm_g.reshape(1, HEAD_DIM))
    o_fx = _fox(proj3, f_rows)

    wr = jnp.zeros((d, LANES), F32).at[:, :N_GROUPS].set(w_rg).at[:, N_GROUPS:N_GROUPS + N_EXPERTS].set(w_re)
    br = jnp.zeros((1, LANES), F32).at[0, :N_GROUPS].set(b_rg).at[0, N_GROUPS:N_GROUPS + N_EXPERTS].set(b_re)
    wr_hi, wr_lo = _split_bf16(wr)
    w_out_b = w_out.astype(BF16)
    x1, h2, rslab = _outproj(o_dn.reshape(n, HEAD_W), o_fx.reshape(n, HEAD_W), w_out_b[:HEAD_W], w_out_b[HEAD_W:],
                             x2, g1, sc2, sh2, norm2_g.reshape(1, d), wr_hi, wr_lo, br, seq)

    pos_slab, counts = _route(rslab)
    pos2 = pos_slab[:, 0:2].reshape(n // DISP_TB, 1, 2 * DISP_TB)

    cnt = counts[0, :N_EXPERTS].astype(jnp.int32)
    tiles_per = (cnt + MOE_TM - 1) // MOE_TM
    tile_end = jnp.cumsum(tiles_per)
    base = (tile_end - tiles_per) * MOE_TM
    n_tiles = (2 * n) // MOE_TM + N_EXPERTS
    p_rows = n_tiles * MOE_TM
    tid = jnp.arange(n_tiles, dtype=jnp.int32)
    tile_valid = (tid < tile_end[-1]).astype(jnp.int32)
    te_raw = jnp.minimum(jnp.sum(tid[:, None] >= tile_end[None, :], axis=1), N_EXPERTS - 1).astype(jnp.int32)
    last_e = te_raw[jnp.maximum(tile_end[-1] - 1, 0)]
    tile_expert = jnp.where(tile_valid == 1, te_raw, last_e)
    tile_first = (jnp.concatenate([jnp.array([-1], jnp.int32), tile_expert[:-1]]) != tile_expert).astype(jnp.int32)
    pad_start = base + cnt
    pad_len = tiles_per * MOE_TM - cnt

    xs = _dispatch(pad_start, pad_len, tile_end[-1:], pos2, h2, p_rows)
    ys = _experts(tile_expert, tile_valid, tile_first, xs, w1.reshape(N_EXPERTS, d, D_EXPERT),
                  w3.reshape(N_EXPERTS, d, D_EXPERT), w2.reshape(N_EXPERTS, D_EXPERT, d))
    out = _combine(pos2, ys, rslab, x1, g2, final_g.reshape(1, d), seq, final)
    return out.reshape(bsz, seq, d)


def kernel(x, c, w_ada, b_ada, norm1_g, w_in, conv_w, a_log, dt_bias, dn_onorm_g, fox_f_bias, w_out, norm2_g,
           w_router_group, b_router_group, w_router_expert, b_router_expert, w1, w3, w2, final_g):
    depth = w_ada.shape[0]
    for l in range(depth):
        x = _layer(x, c, w_ada[l], b_ada[l], norm1_g[l], w_in[l], conv_w[l], a_log[l], dt_bias[l], dn_onorm_g[l],
                   fox_f_bias[l], w_out[l], norm2_g[l], w_router_group[l], b_router_group[l], w_router_expert[l],
                   b_router_expert[l], w1[l], w3[l], w2[l], final_g, l == depth - 1)
    return x
```

```python
import functools

import jax
import jax.numpy as jnp
from jax import lax
from jax.experimental import pallas as pl
from jax.experimental.pallas import tpu as pltpu

F32 = jnp.float32
BF16 = jnp.bfloat16

D_MODEL = 2048
EPS = 1e-6
CHUNK = 64
HEADS = 8
HEAD_DIM = 128
HEAD_W = HEADS * HEAD_DIM
CONV_K = 4
N_GROUPS = 4
EXPERTS_PER_GROUP = 8
N_EXPERTS = N_GROUPS * EXPERTS_PER_GROUP
D_EXPERT = 512
LANES = 128
SUBLANES = 8
MAIN_W = 7 * HEAD_W
VMEM_LIMIT = 56 * 1024 * 1024

L_GC, L_BETA, L_F, L_EGC, L_EK, L_ELAST = 0, 8, 16, 24, 32, 40


def _cparams(sem):
    return pltpu.CompilerParams(dimension_semantics=sem, vmem_limit_bytes=VMEM_LIMIT)


def _split_bf16(a):
    hi = a.astype(BF16)
    lo = (a - hi.astype(F32)).astype(BF16)
    return hi, lo


def _dot(a, b):
    return jnp.dot(a, b, preferred_element_type=F32)


def _dot_nt(a, b):
    return lax.dot_general(a, b, (((1,), (1,)), ((), ())), preferred_element_type=F32)


def _dot3(a, b):
    ah, al = _split_bf16(a)
    bh, bl = _split_bf16(b)
    return _dot(ah, bh) + (_dot(al, bh) + _dot(ah, bl))


def _dot3_pre(a, bh, bl):
    ah, al = _split_bf16(a)
    return _dot(ah, bh) + (_dot(al, bh) + _dot(ah, bl))


def _softplus(x):
    return jnp.maximum(x, 0.0) + jnp.log1p(jnp.exp(-jnp.abs(x)))


def _silu(x):
    return x * jax.nn.sigmoid(x)


def _adaln_kernel(c_ref, w_ref, b_ref, o_ref):
    c = c_ref[...]
    o_ref[...] = _dot(_silu(c).astype(BF16), w_ref[...].astype(BF16)) + b_ref[...]


def _adaln(c, w, b):
    bsz = c.shape[0]
    n = w.shape[1]
    tn = 1024
    cp = jnp.zeros((SUBLANES, D_MODEL), F32).at[:bsz].set(c)
    out = pl.pallas_call(
        _adaln_kernel,
        grid=(n // tn,),
        in_specs=[
            pl.BlockSpec((SUBLANES, D_MODEL), lambda j: (0, 0)),
            pl.BlockSpec((D_MODEL, tn), lambda j: (0, j)),
            pl.BlockSpec((1, tn), lambda j: (0, j)),
        ],
        out_specs=pl.BlockSpec((SUBLANES, tn), lambda j: (0, j)),
        out_shape=jax.ShapeDtypeStruct((SUBLANES, n), F32),
        compiler_params=_cparams(("parallel",)),
        name="adaln",
    )(cp, w, b.reshape(1, n))
    return out[:bsz]


INPROJ_TM = 1024
INPROJ_TN = 512
ROW_STEP = 128


def _modulated_norm(x, g, sc, sh):
    r = lax.rsqrt(jnp.mean(x * x, axis=-1, keepdims=True) + EPS)
    return (x * r) * (g * (1.0 + sc)) + sh


def _inproj_kernel(x_ref, sc_ref, sh_ref, g_ref, w_ref, wsh_ref, wsl_ref, o_ref, og_ref, h_ref):
    @pl.when(pl.program_id(1) == 0)
    def _():
        def body(i, _):
            rows = pl.ds(pl.multiple_of(i * ROW_STEP, ROW_STEP), ROW_STEP)
            h = _modulated_norm(x_ref[rows, :], g_ref[...], sc_ref[...], sh_ref[...])
            hb = h.astype(BF16)
            h_ref[rows, :] = hb
            og_ref[rows, :] = _dot(hb, wsh_ref[...]) + _dot(hb, wsl_ref[...])
            return 0

        lax.fori_loop(0, INPROJ_TM // ROW_STEP, body, 0)

    o_ref[...] = _dot(h_ref[...], w_ref[...]).astype(BF16)


def _inproj(x2, sc1, sh1, g, w_main, ws_hi, ws_lo, seq):
    n = x2.shape[0]
    tm, tn = INPROJ_TM, INPROJ_TN
    per_b = seq // tm
    return pl.pallas_call(
        _inproj_kernel,
        grid=(n // tm, MAIN_W // tn),
        in_specs=[
            pl.BlockSpec((tm, D_MODEL), lambda i, j: (i, 0)),
            pl.BlockSpec((None, 1, D_MODEL), lambda i, j: (i // per_b, 0, 0)),
            pl.BlockSpec((None, 1, D_MODEL), lambda i, j: (i // per_b, 0, 0)),
            pl.BlockSpec((1, D_MODEL), lambda i, j: (0, 0)),
            pl.BlockSpec((D_MODEL, tn), lambda i, j: (0, j)),
            pl.BlockSpec((D_MODEL, LANES), lambda i, j: (0, 0)),
            pl.BlockSpec((D_MODEL, LANES), lambda i, j: (0, 0)),
        ],
        out_specs=[
            pl.BlockSpec((tm, tn), lambda i, j: (i, j)),
            pl.BlockSpec((tm, LANES), lambda i, j: (i, 0)),
        ],
        out_shape=[
            jax.ShapeDtypeStruct((n, MAIN_W), BF16),
            jax.ShapeDtypeStruct((n, LANES), F32),
        ],
        scratch_shapes=[pltpu.VMEM((tm, D_MODEL), BF16)],
        compiler_params=_cparams(("parallel", "arbitrary")),
        name="inproj",
    )(x2, sc1, sh1, g, w_main, ws_hi, ws_lo)


GATES_TB = 256


def _split3(a):
    p0 = a.astype(BF16)
    r1 = a - p0.astype(F32)
    p1 = r1.astype(BF16)
    p2 = (r1 - p1.astype(F32)).astype(BF16)
    return p0, p1, p2


def _dot_ones(m, a):
    p0, p1, p2 = _split3(a)
    return _dot(m, p0) + (_dot(m, p1) + _dot(m, p2))


def _gates_kernel(x_ref, alog_ref, dt_ref, fb_ref, o_ref, carry_ref):
    tb = GATES_TB

    @pl.when(pl.program_id(1) == 0)
    def _():
        carry_ref[...] = jnp.zeros_like(carry_ref)

    x = x_ref[...]
    lane = lax.broadcasted_iota(jnp.int32, (tb, LANES), 1)
    g = -jnp.exp(alog_ref[...]) * _softplus(x + dt_ref[...])
    beta = jax.nn.sigmoid(x)
    lf = -_softplus(-(x + fb_ref[...]))

    ri = lax.broadcasted_iota(jnp.int32, (tb, tb), 0)
    ci = lax.broadcasted_iota(jnp.int32, (tb, tb), 1)
    same_chunk = (ri // CHUNK) == (ci // CHUNK)
    tri = (ri >= ci)
    m_all = jnp.where(tri, 1.0, 0.0).astype(BF16)
    m_chunk = jnp.where(tri & same_chunk, 1.0, 0.0).astype(BF16)
    m_tot = jnp.where(same_chunk, 1.0, 0.0).astype(BF16)

    gc = _dot_ones(m_chunk, g)
    glast = _dot_ones(m_tot, g)
    fcum = _dot_ones(m_all, lf) + carry_ref[...]
    carry_ref[...] = fcum[tb - 1:tb, :]

    in_a = lane < HEADS
    egc = jnp.where(in_a, jnp.exp(gc), 0.0)
    ek = jnp.where(in_a, jnp.exp(glast - gc), 0.0)
    elast = jnp.where(in_a, jnp.exp(glast), 0.0)
    out = jnp.where(in_a, gc, jnp.where(lane < 2 * HEADS, beta, jnp.where(lane < 3 * HEADS, fcum, 0.0)))
    out = out + pltpu.roll(egc, L_EGC, 1) + pltpu.roll(ek, L_EK, 1) + pltpu.roll(elast, L_ELAST, 1)
    o_ref[...] = out


def _gates(gates, alog_row, dt_row, fb_row):
    bsz, seq, _ = gates.shape
    tb = GATES_TB
    row = pl.BlockSpec((1, LANES), lambda b, t: (0, 0))
    return pl.pallas_call(
        _gates_kernel,
        grid=(bsz, seq // tb),
        in_specs=[pl.BlockSpec((None, tb, LANES), lambda b, t: (b, t, 0)), row, row, row],
        out_specs=pl.BlockSpec((None, tb, LANES), lambda b, t: (b, t, 0)),
        out_shape=jax.ShapeDtypeStruct((bsz, seq, LANES), F32),
        scratch_shapes=[pltpu.VMEM((1, LANES), F32)],
        compiler_params=_cparams(("parallel", "arbitrary")),
        name="gates",
    )(gates, alog_row, dt_row, fb_row)


DN_TB = 256
DN_GROUP = 4
HALO = SUBLANES


def _bdot(a, b):
    return lax.dot_general(a, b, (((2,), (1,)), ((0,), (0,))), preferred_element_type=F32)


def _bdot_nt(a, b):
    return lax.dot_general(a, b, (((2,), (2,)), ((0,), (0,))), preferred_element_type=F32)


def _bdot3(a, b):
    ah, al = _split_bf16(a)
    bh, bl = _split_bf16(b)
    return _bdot(ah, bh) + (_bdot(al, bh) + _bdot(ah, bl))


def _inv_unit_lower(a, eye, blk16, blk32):
    n = jnp.where(blk16, -a, 0.0)
    e1 = jnp.where(blk32 & jnp.logical_not(blk16), a, 0.0)
    e2 = jnp.where(blk32, 0.0, a)
    t = eye + n
    p = _bdot3(n, n)
    t = t + _bdot3(t, p)
    p = _bdot3(p, p)
    t = t + _bdot3(t, p)
    p = _bdot3(p, p)
    t = t + _bdot3(t, p)
    t = t - _bdot3(_bdot3(t, e1), t)
    t = t - _bdot3(_bdot3(t, e2), t)
    return t


def _deltanet_kernel(q_ref, k_ref, v_ref, z_ref, wq_ref, wk_ref, wv_ref, slab_ref, gct_ref, og_ref,
                     o_ref, ext_ref, qn_ref, kn_ref, vv_ref, s_ref):
    tb = DN_TB

    @pl.when(pl.program_id(1) == 0)
    def _():
        ext_ref[:, 0:HALO, :] = jnp.zeros((3, HALO, HEAD_W), F32)
        s_ref[...] = jnp.zeros_like(s_ref)

    for idx, (u_ref, w_ref) in enumerate(((q_ref, wq_ref), (k_ref, wk_ref), (v_ref, wv_ref))):
        for h in range(HEADS):
            cols = slice(h * HEAD_DIM, (h + 1) * HEAD_DIM)
            ext_ref[idx, HALO:HALO + tb, cols] = u_ref[:, cols].astype(F32)
            y = None
            for j in range(CONV_K):
                start = HALO - (CONV_K - 1) + j
                term = ext_ref[idx, start:start + tb, cols] * w_ref[j:j + 1, cols]
                y = term if y is None else y + term
            y = _silu(y)
            if idx == 2:
                vv_ref[h] = y
            else:
                yn = y * lax.rsqrt(jnp.sum(y * y, axis=-1, keepdims=True) + EPS)
                if idx == 0:
                    qn_ref[h] = (yn * (HEAD_DIM ** -0.5)).astype(BF16)
                else:
                    kn_ref[h] = yn.astype(BF16)
        ext_ref[idx, 0:HALO, :] = ext_ref[idx, tb:tb + HALO, :]

    ri = lax.broadcasted_iota(jnp.int32, (DN_GROUP * HEADS, CHUNK, CHUNK), 1)
    ci = lax.broadcasted_iota(jnp.int32, (DN_GROUP * HEADS, CHUNK, CHUNK), 2)
    incl = ri >= ci
    strict = ri > ci
    eye = jnp.where(ri == ci, 1.0, 0.0)
    blk16 = (ri // 16) == (ci // 16)
    blk32 = (ri // 32) == (ci // 32)

    def group_body(c, _):
        rows = [pl.ds(pl.multiple_of((c * DN_GROUP + g) * CHUNK, CHUNK), CHUNK) for g in range(DN_GROUP)]
        slabs = [slab_ref[r, :] for r in rows]

        def col(off, width):
            return jnp.stack([jnp.broadcast_to(sl[:, off + h:off + h + 1], (CHUNK, width))
                              for sl in slabs for h in range(HEADS)])

        def grouped(ref):
            return jnp.concatenate([ref[:, r, :] for r in rows], axis=0)

        q = grouped(qn_ref)
        k = grouped(kn_ref)
        v = grouped(vv_ref)
        beta = col(L_BETA, HEAD_DIM)
        egc = col(L_EGC, HEAD_DIM)
        gc_row = jnp.concatenate([gct_ref[c * DN_GROUP + g] for g in range(DN_GROUP)], axis=0)

        decay = jnp.where(incl, jnp.exp(col(L_GC, CHUNK) - gc_row), 0.0)
        kk = _bdot_nt(k, k)
        qk = (_bdot_nt(q, k) * decay).astype(BF16)
        a = jnp.where(strict, kk * decay * beta[:, :, :CHUNK], 0.0)
        t = _inv_unit_lower(a, eye, blk16, blk32)
        th, tl = _split_bf16(t)

        kf = k.astype(F32)
        vb = (v * beta).astype(BF16)
        kbg = (kf * (beta * egc)).astype(BF16)
        u = _bdot(th, vb) + _bdot(tl, vb)
        w = (_bdot(th, kbg) + _bdot(tl, kbg)).astype(BF16)
        qd = (q.astype(F32) * egc).astype(BF16)
        kd = kf * col(L_EK, HEAD_DIM)
        kdt = jnp.stack([kd[n].T for n in range(DN_GROUP * HEADS)]).astype(BF16)

        for g in range(DN_GROUP):
            sel = slice(g * HEADS, (g + 1) * HEADS)
            s = s_ref[...]
            sb = s.astype(BF16)
            vnb = (u[sel] - _bdot(w[sel], sb)).astype(BF16)
            o = _bdot(qd[sel], sb) + _bdot(qk[sel], vnb)
            elast = jnp.stack([jnp.broadcast_to(slabs[g][CHUNK - 1:CHUNK, L_ELAST + h:L_ELAST + h + 1],
                                                (HEAD_DIM, HEAD_DIM)) for h in range(HEADS)])
            s_ref[...] = s * elast + _bdot(kdt[sel], vnb)

            r = lax.rsqrt(jnp.mean(o * o, axis=-1, keepdims=True) + EPS)
            on = (o * r) * og_ref[...]
            for h in range(HEADS):
                cols = slice(h * HEAD_DIM, (h + 1) * HEAD_DIM)
                o_ref[rows[g], cols] = (on[h] * _silu(z_ref[rows[g], cols].astype(F32))).astype(BF16)
        return 0

    lax.fori_loop(0, tb // (CHUNK * DN_GROUP), group_body, 0)


def _deltanet(proj3, conv_w, slab, gct, onorm_g):
    bsz, seq, _ = proj3.shape
    tb = DN_TB
    nct = tb // CHUNK

    def colblk(j):
        return pl.BlockSpec((None, tb, HEAD_W), lambda b, t: (b, t, j))

    def wblk(j):
        return pl.BlockSpec((CONV_K, HEAD_W), lambda b, t: (0, j))

    return pl.pallas_call(
        _deltanet_kernel,
        grid=(bsz, seq // tb),
        in_specs=[
            colblk(0), colblk(1), colblk(2), colblk(3),
            wblk(0), wblk(1), wblk(2),
            pl.BlockSpec((None, tb, LANES), lambda b, t: (b, t, 0)),
            pl.BlockSpec((None, nct, HEADS, 1, CHUNK), lambda b, t: (b, t, 0, 0, 0)),
            pl.BlockSpec((1, HEAD_DIM), lambda b, t: (0, 0)),
        ],
        out_specs=pl.BlockSpec((None, tb, HEAD_W), lambda b, t: (b, t, 0)),
        out_shape=jax.ShapeDtypeStruct((bsz, seq, HEAD_W), BF16),
        scratch_shapes=[
            pltpu.VMEM((3, tb + HALO, HEAD_W), F32),
            pltpu.VMEM((HEADS, tb, HEAD_DIM), BF16),
            pltpu.VMEM((HEADS, tb, HEAD_DIM), BF16),
            pltpu.VMEM((HEADS, tb, HEAD_DIM), F32),
            pltpu.VMEM((HEADS, HEAD_DIM, HEAD_DIM), F32),
        ],
        compiler_params=_cparams(("parallel", "arbitrary")),
        name="deltanet",
    )(proj3, proj3, proj3, proj3, conv_w, conv_w, conv_w, slab, gct, onorm_g)


FOX_T = 512
FOX_STRIP = 32
NEG_INF = float("-inf")
LOG2E = 1.4426950408889634


def _fox_kernel(q_ref, k_ref, v_ref, f_ref, o_ref, s_a_ref, s_b_ref, p_a_ref, p_b_ref, m_ref, l_ref, alpha_ref,
                acc_ref):
    tq = FOX_T
    i = pl.program_id(2)
    lane_blocks = tq // LANES
    qs = (q_ref[...].astype(F32) * (HEAD_DIM ** -0.5 * LOG2E)).astype(BF16)
    f0 = f_ref[i][:, 0:1]
    m_ref[...] = jnp.full_like(m_ref, NEG_INF)
    l_ref[...] = jnp.zeros_like(l_ref)
    acc_ref[...] = jnp.zeros_like(acc_ref)

    def keys(j):
        return pl.ds(pl.multiple_of(j * tq, tq), tq)

    def scores(j):
        return _dot_nt(qs, k_ref[keys(j), :])

    def softmax(j, s_view, p_view, masked):
        bias = (f0 - f_ref[j]) * LOG2E

        def logits(r):
            rows = slice(r * FOX_STRIP, (r + 1) * FOX_STRIP)
            s = s_view[rows, :] + bias
            if masked:
                ri = r * FOX_STRIP + lax.broadcasted_iota(jnp.int32, (FOX_STRIP, tq), 0)
                ci = lax.broadcasted_iota(jnp.int32, (FOX_STRIP, tq), 1)
                s = jnp.where(ci <= ri, s, NEG_INF)
            return rows, s

        for r in range(tq // FOX_STRIP):
            rows, s = logits(r)
            m_old = m_ref[rows, :]
            m_new = jnp.maximum(m_old, jnp.max(s, axis=-1, keepdims=True))
            alpha_ref[rows, :] = jnp.exp2(m_old - m_new)
            m_ref[rows, :] = m_new
        for r in range(tq // FOX_STRIP):
            rows, s = logits(r)
            p = jnp.exp2(s - jnp.concatenate([m_ref[rows, :]] * lane_blocks, axis=1))
            alpha = alpha_ref[rows, :]
            l_ref[rows, :] = alpha * l_ref[rows, :] + jnp.sum(p, axis=-1, keepdims=True)
            p_view[rows, :] = p.astype(BF16)

    s_a_ref[...] = scores(i)
    s_b_ref[...] = scores(jnp.maximum(i - 1, 0))
    softmax(i, s_a_ref, p_a_ref, True)

    def step(t, s_cur, p_cur, s_nxt, p_prv):
        j = i - t
        s_nxt[...] = scores(jnp.maximum(j - 1, 0))
        pv = _dot(p_prv[...], v_ref[keys(j + 1), :])
        softmax(j, s_cur, p_cur, False)
        acc_ref[...] = alpha_ref[...] * (acc_ref[...] + pv)

    def pair(u, _):
        step(2 * u + 1, s_b_ref, p_b_ref, s_a_ref, p_a_ref)
        step(2 * u + 2, s_a_ref, p_a_ref, s_b_ref, p_b_ref)
        return 0

    lax.fori_loop(0, lax.shift_right_logical(i, 1), pair, 0)

    def finish(p_last):
        acc = acc_ref[...] + _dot(p_last[...], v_ref[keys(0), :])
        o_ref[...] = (acc / l_ref[...]).astype(BF16)

    @pl.when((i & 1) == 1)
    def _():
        step(i, s_b_ref, p_b_ref, s_a_ref, p_a_ref)
        finish(p_b_ref)

    @pl.when((i & 1) == 0)
    def _():
        finish(p_a_ref)


def _fox(proj3, f_rows):
    bsz, seq, _ = proj3.shape
    tq = FOX_T
    qb, kb, vb = 4 * HEADS, 5 * HEADS, 6 * HEADS
    return pl.pallas_call(
        _fox_kernel,
        grid=(bsz, HEADS, seq // tq),
        in_specs=[
            pl.BlockSpec((None, tq, HEAD_DIM), lambda b, h, i: (b, i, qb + h)),
            pl.BlockSpec((None, seq, HEAD_DIM), lambda b, h, i: (b, 0, kb + h)),
            pl.BlockSpec((None, seq, HEAD_DIM), lambda b, h, i: (b, 0, vb + h)),
            pl.BlockSpec((None, None, seq // tq, 1, tq), lambda b, h, i: (b, h, 0, 0, 0)),
        ],
        out_specs=pl.BlockSpec((None, tq, HEAD_DIM), lambda b, h, i: (b, i, h)),
        out_shape=jax.ShapeDtypeStruct((bsz, seq, HEAD_W), BF16),
        scratch_shapes=[
            pltpu.VMEM((tq, tq), F32),
            pltpu.VMEM((tq, tq), F32),
            pltpu.VMEM((tq, tq), BF16),
            pltpu.VMEM((tq, tq), BF16),
            pltpu.VMEM((tq, LANES), F32),
            pltpu.VMEM((tq, LANES), F32),
            pltpu.VMEM((tq, LANES), F32),
            pltpu.VMEM((tq, HEAD_DIM), F32),
        ],
        compiler_params=_cparams(("parallel", "parallel", "arbitrary")),
        name="fox",
    )(proj3, proj3, proj3, f_rows)


OUT_TM = 512
R_E0, R_E1, R_W0, R_W1 = 0, 1, 2, 3


def _first_argmax(vals, lane):
    m = jnp.max(vals, axis=-1, keepdims=True)
    idx = jnp.min(jnp.where(vals == m, lane, LANES), axis=-1, keepdims=True)
    return m, idx


def _outproj_kernel(oa_ref, ob_ref, wa_ref, wb_ref, x_ref, g1_ref, sc_ref, sh_ref, g_ref,
                    wrh_ref, wrl_ref, br_ref, x1_ref, h2_ref, r_ref):
    mix = _dot(oa_ref[...], wa_ref[...]) + _dot(ob_ref[...], wb_ref[...])
    x1 = x_ref[...] + g1_ref[...] * mix
    x1_ref[...] = x1
    h2 = _modulated_norm(x1, g_ref[...], sc_ref[...], sh_ref[...])
    h2_ref[...] = h2

    logits = _dot3_pre(h2, wrh_ref[...], wrl_ref[...]) + br_ref[...]
    tm = logits.shape[0]
    lane = lax.broadcasted_iota(jnp.int32, (tm, LANES), 1)
    gl = jnp.where(lane < N_GROUPS, logits, NEG_INF)
    gmax, gidx = _first_argmax(gl, lane)
    pg = 1.0 / jnp.sum(jnp.exp(gl - gmax), axis=-1, keepdims=True)
    e_lane = lane - N_GROUPS
    in_grp = (e_lane >= gidx * EXPERTS_PER_GROUP) & (e_lane < (gidx + 1) * EXPERTS_PER_GROUP)
    el = jnp.where(in_grp, logits, NEG_INF)
    v0, i0 = _first_argmax(el, lane)
    v1, i1 = _first_argmax(jnp.where(lane == i0, NEG_INF, el), lane)
    ex = jnp.exp(v1 - v0)
    w0 = pg / (1.0 + ex)
    w1 = pg * ex / (1.0 + ex)
    e0 = (i0 - N_GROUPS).astype(F32)
    e1 = (i1 - N_GROUPS).astype(F32)
    r_ref[...] = jnp.where(lane == R_E0, e0, jnp.where(lane == R_E1, e1,
                           jnp.where(lane == R_W0, w0, jnp.where(lane == R_W1, w1, 0.0))))


def _outproj(o_a, o_b, wa, wb, x2, g1, sc2, sh2, g, wr_hi, wr_lo, br, seq):
    n = x2.shape[0]
    tm = OUT_TM
    per_b = seq // tm
    modrow = pl.BlockSpec((None, 1, D_MODEL), lambda i: (i // per_b, 0, 0))
    const = lambda shape: pl.BlockSpec(shape, lambda i: (0, 0))
    return pl.pallas_call(
        _outproj_kernel,
        grid=(n // tm,),
        in_specs=[
            pl.BlockSpec((tm, HEAD_W), lambda i: (i, 0)),
            pl.BlockSpec((tm, HEAD_W), lambda i: (i, 0)),
            const((HEAD_W, D_MODEL)), const((HEAD_W, D_MODEL)),
            pl.BlockSpec((tm, D_MODEL), lambda i: (i, 0)),
            modrow, modrow, modrow,
            const((1, D_MODEL)),
            const((D_MODEL, LANES)), const((D_MODEL, LANES)), const((1, LANES)),
        ],
        out_specs=[
            pl.BlockSpec((tm, D_MODEL), lambda i: (i, 0)),
            pl.BlockSpec((tm, D_MODEL), lambda i: (i, 0)),
            pl.BlockSpec((tm, LANES), lambda i: (i, 0)),
        ],
        out_shape=[
            jax.ShapeDtypeStruct((n, D_MODEL), F32),
            jax.ShapeDtypeStruct((n, D_MODEL), F32),
            jax.ShapeDtypeStruct((n, LANES), F32),
        ],
        compiler_params=_cparams(("parallel",)),
        name="outproj",
    )(o_a, o_b, wa, wb, x2, g1, sc2, sh2, g, wr_hi, wr_lo, br)


MOE_TM = 256
ROUTE_TB = 512


def _route_kernel(r_ref, pos_ref, cnt_ref, run_ref, base_ref):
    tb = ROUTE_TB
    phase = pl.program_id(0)
    t = pl.program_id(1)
    r = r_ref[...]
    lane = lax.broadcasted_iota(jnp.int32, (tb, LANES), 1)
    e0 = r[:, R_E0:R_E0 + 1].astype(jnp.int32)
    e1 = r[:, R_E1:R_E1 + 1].astype(jnp.int32)
    oh0 = lane == e0
    oh1 = lane == e1
    both = jnp.where(oh0 | oh1, 1.0, 0.0)

    @pl.when((phase == 0) & (t == 0))
    def _():
        run_ref[...] = jnp.zeros_like(run_ref)

    @pl.when(phase == 0)
    def _():
        run_ref[...] = run_ref[...] + jnp.sum(both, axis=0, keepdims=True)

    @pl.when((phase == 1) & (t == 0))
    def _():
        counts = run_ref[...]
        cnt_ref[...] = counts
        padded = jnp.ceil(counts / MOE_TM) * MOE_TM
        li = lax.broadcasted_iota(jnp.int32, (LANES, LANES), 0)
        lj = lax.broadcasted_iota(jnp.int32, (LANES, LANES), 1)
        upper = jnp.where(li < lj, 1.0, 0.0).astype(BF16)
        hi = jnp.floor(padded / 256.0)
        lo = padded - hi * 256.0
        hi8 = jnp.broadcast_to(hi, (SUBLANES, LANES)).astype(BF16)
        lo8 = jnp.broadcast_to(lo, (SUBLANES, LANES)).astype(BF16)
        base = _dot(hi8, upper) * 256.0 + _dot(lo8, upper)
        base_ref[...] = base[0:1, :]
        run_ref[...] = jnp.zeros_like(run_ref)

    @pl.when(phase == 1)
    def _():
        ri = lax.broadcasted_iota(jnp.int32, (tb, tb), 0)
        ci = lax.broadcasted_iota(jnp.int32, (tb, tb), 1)
        strict = jnp.where(ri > ci, 1.0, 0.0).astype(BF16)
        before = _dot(strict, both.astype(BF16)) + run_ref[...] + base_ref[...]
        p0 = jnp.sum(jnp.where(oh0, before, 0.0), axis=-1, keepdims=True)
        p1 = jnp.sum(jnp.where(oh1, before, 0.0), axis=-1, keepdims=True)
        pos_ref[...] = jnp.where(lane == 0, p0, jnp.where(lane == 1, p1, 0.0)).astype(jnp.int32)
        run_ref[...] = run_ref[...] + jnp.sum(both, axis=0, keepdims=True)


def _route(rslab):
    n = rslab.shape[0]
    tb = ROUTE_TB
    return pl.pallas_call(
        _route_kernel,
        grid=(2, n // tb),
        in_specs=[pl.BlockSpec((tb, LANES), lambda p, t: (t, 0))],
        out_specs=[
            pl.BlockSpec((tb, LANES), lambda p, t: (p * t, 0)),
            pl.BlockSpec((1, LANES), lambda p, t: (0, 0)),
        ],
        out_shape=[
            jax.ShapeDtypeStruct((n, LANES), jnp.int32),
            jax.ShapeDtypeStruct((1, LANES), F32),
        ],
        scratch_shapes=[pltpu.VMEM((1, LANES), F32), pltpu.VMEM((1, LANES), F32)],
        compiler_params=_cparams(("arbitrary", "arbitrary")),
        name="route",
    )(rslab)


DISP_TB = 1024
DMA_UNROLL = 8


def _dispatch_kernel(pad_start_ref, pad_len_ref, used_ref, pos_ref, h2_ref, xs_ref, zero_ref, sem, zsem):
    t = pl.program_id(0)
    tb = DISP_TB
    n_tiles = xs_ref.shape[0] // MOE_TM

    def row_copy(i, k):
        return pltpu.make_async_copy(h2_ref.at[pl.ds(i, 1)], xs_ref.at[pl.ds(pos_ref[0, 2 * i + k], 1)], sem)

    def issue(i, _):
        row_copy(i, 0).start(priority=0)
        row_copy(i, 1).start(priority=1)
        return 0

    lax.fori_loop(0, tb, issue, 0, unroll=DMA_UNROLL)

    @pl.when(t == 0)
    def _():
        zero_ref[...] = jnp.zeros_like(zero_ref)

        def zero_rows(off, rows):
            cp = pltpu.make_async_copy(zero_ref.at[pl.ds(0, rows)], xs_ref.at[pl.ds(off, rows)], zsem)
            cp.start()
            cp.wait()

        def per_expert(e, _):
            start = pad_start_ref[e]
            head = (-start) & (SUBLANES - 1)
            for r in range(SUBLANES - 1):
                pl.when(r < head)(functools.partial(zero_rows, start + r, 1))
            off = start + head
            rest = pad_len_ref[e] - head
            piece = MOE_TM // 2
            while piece >= SUBLANES:
                take = (rest & piece) != 0
                pl.when(take)(functools.partial(zero_rows, pl.multiple_of(off, SUBLANES), piece))
                off = off + jnp.where(take, piece, 0)
                piece //= 2
            return 0

        lax.fori_loop(0, N_EXPERTS, per_expert, 0)

        def per_tile(i, _):
            zero_rows(pl.multiple_of(i * MOE_TM, MOE_TM), MOE_TM)
            return 0

        lax.fori_loop(used_ref[0], n_tiles, per_tile, 0)

    def drain(i, _):
        row_copy(i, 0).wait()
        row_copy(i, 1).wait()
        return 0

    lax.fori_loop(0, tb, drain, 0, unroll=DMA_UNROLL)


def _dispatch(pad_start, pad_len, used, pos2, h2, p_rows):
    n = h2.shape[0]
    tb = DISP_TB
    return pl.pallas_call(
        _dispatch_kernel,
        grid_spec=pltpu.PrefetchScalarGridSpec(
            num_scalar_prefetch=3,
            grid=(n // tb,),
            in_specs=[
                pl.BlockSpec((None, 1, 2 * tb), lambda t, *_: (t, 0, 0), memory_space=pltpu.SMEM),
                pl.BlockSpec((tb, D_MODEL), lambda t, *_: (t, 0)),
            ],
            out_specs=pl.BlockSpec(memory_space=pl.ANY),
            scratch_shapes=[
                pltpu.VMEM((MOE_TM, D_MODEL), F32),
                pltpu.SemaphoreType.DMA(()),
                pltpu.SemaphoreType.DMA(()),
            ],
        ),
        out_shape=jax.ShapeDtypeStruct((p_rows, D_MODEL), F32),
        compiler_params=_cparams(("arbitrary",)),
        name="dispatch",
    )(pad_start, pad_len, used, pos2, h2)


def _experts_kernel(te_ref, tv_ref, tf_ref, ts_ref, tn_ref, xs_ref, w1_hbm, w3_hbm, w2_hbm, ys_ref,
                    w1f_ref, w3f_ref, w2f_ref, w1b_ref, w3b_ref, w2b_ref, sem):
    i = pl.program_id(0)

    def weight_copies(e, slot):
        return [pltpu.make_async_copy(src.at[e], dst.at[slot], sem.at[slot])
                for src, dst in ((w1_hbm, w1f_ref), (w3_hbm, w3f_ref), (w2_hbm, w2f_ref))]

    @pl.when(tv_ref[i] != 0)
    def _():
        @pl.when(tf_ref[i] != 0)
        def _():
            slot = ts_ref[i]

            @pl.when(i == 0)
            def _():
                for cp in weight_copies(te_ref[i], slot):
                    cp.start()

            for cp in weight_copies(te_ref[i], slot):
                cp.wait()

            @pl.when(tn_ref[i] >= 0)
            def _():
                for cp in weight_copies(tn_ref[i], 1 - slot):
                    cp.start()

            w1b_ref[...] = w1f_ref[slot].astype(BF16)
            w3b_ref[...] = w3f_ref[slot].astype(BF16)
            w2b_ref[...] = w2f_ref[slot].astype(BF16)

        x = xs_ref[...].astype(BF16)
        a = _dot(x, w1b_ref[...])
        b = _dot(x, w3b_ref[...])
        ys_ref[...] = _dot((_silu(a) * b).astype(BF16), w2b_ref[...])

    @pl.when(tv_ref[i] == 0)
    def _():
        ys_ref[...] = jnp.zeros_like(ys_ref)


def _experts(tile_expert, tile_valid, tile_first, tile_slot, tile_next, xs, w1, w3, w2):
    p_rows = xs.shape[0]
    tm = MOE_TM
    hbm = pl.BlockSpec(memory_space=pl.ANY)
    return pl.pallas_call(
        _experts_kernel,
        grid_spec=pltpu.PrefetchScalarGridSpec(
            num_scalar_prefetch=5,
            grid=(p_rows // tm,),
            in_specs=[pl.BlockSpec((tm, D_MODEL), lambda i, *_: (i, 0)), hbm, hbm, hbm],
            out_specs=pl.BlockSpec((tm, D_MODEL), lambda i, *_: (i, 0)),
            scratch_shapes=[
                pltpu.VMEM((2, D_MODEL, D_EXPERT), F32),
                pltpu.VMEM((2, D_MODEL, D_EXPERT), F32),
                pltpu.VMEM((2, D_EXPERT, D_MODEL), F32),
                pltpu.VMEM((D_MODEL, D_EXPERT), BF16),
                pltpu.VMEM((D_MODEL, D_EXPERT), BF16),
                pltpu.VMEM((D_EXPERT, D_MODEL), BF16),
                pltpu.SemaphoreType.DMA((2,)),
            ],
        ),
        out_shape=jax.ShapeDtypeStruct((p_rows, D_MODEL), F32),
        compiler_params=_cparams(("arbitrary",)),
        name="experts",
    )(tile_expert, tile_valid, tile_first, tile_slot, tile_next, xs, w1, w3, w2)


COMB_TB = 256


def _combine_kernel(pos_ref, pos_next_ref, ys_ref, r_ref, x1_ref, g2_ref, fg_ref, o_ref, buf_ref, sem, *, final):
    tb = COMB_TB
    t = pl.program_id(0)
    slot = t & 1

    def row_copy(p_ref, s, i, k):
        return pltpu.make_async_copy(ys_ref.at[pl.ds(p_ref[0, 2 * i + k], 1)], buf_ref.at[s, k, pl.ds(i, 1)],
                                     sem.at[s])

    def issue(p_ref, s):
        def body(i, _):
            row_copy(p_ref, s, i, 0).start(priority=0)
            row_copy(p_ref, s, i, 1).start(priority=1)
            return 0

        lax.fori_loop(0, tb, body, 0, unroll=DMA_UNROLL)

    def drain(i, _):
        row_copy(pos_ref, slot, i, 0).wait()
        row_copy(pos_ref, slot, i, 1).wait()
        return 0

    pl.when(t == 0)(functools.partial(issue, pos_ref, 0))
    pl.when(t + 1 < pl.num_programs(0))(functools.partial(issue, pos_next_ref, 1 - slot))
    lax.fori_loop(0, tb, drain, 0, unroll=DMA_UNROLL)
    r = r_ref[...]
    y = r[:, R_W0:R_W0 + 1] * buf_ref[slot, 0] + r[:, R_W1:R_W1 + 1] * buf_ref[slot, 1]
    x2 = x1_ref[...] + g2_ref[...] * y
    if final:
        x2 = (x2 * lax.rsqrt(jnp.mean(x2 * x2, axis=-1, keepdims=True) + EPS)) * fg_ref[...]
    o_ref[...] = x2


def _combine(pos2, ys, rslab, x1, g2, final_g, seq, final):
    n = x1.shape[0]
    tb = COMB_TB
    per_b = seq // tb
    return pl.pallas_call(
        functools.partial(_combine_kernel, final=final),
        grid=(n // tb,),
        in_specs=[
            pl.BlockSpec((None, 1, 2 * tb), lambda i: (i, 0, 0), memory_space=pltpu.SMEM),
            pl.BlockSpec((None, 1, 2 * tb), lambda i: (jnp.minimum(i + 1, n // tb - 1), 0, 0),
                         memory_space=pltpu.SMEM),
            pl.BlockSpec(memory_space=pl.ANY),
            pl.BlockSpec((tb, LANES), lambda i: (i, 0)),
            pl.BlockSpec((tb, D_MODEL), lambda i: (i, 0)),
            pl.BlockSpec((None, 1, D_MODEL), lambda i: (i // per_b, 0, 0)),
            pl.BlockSpec((1, D_MODEL), lambda i: (0, 0)),
        ],
        out_specs=pl.BlockSpec((tb, D_MODEL), lambda i: (i, 0)),
        out_shape=jax.ShapeDtypeStruct((n, D_MODEL), F32),
        scratch_shapes=[pltpu.VMEM((2, 2, tb, D_MODEL), F32), pltpu.SemaphoreType.DMA((2,))],
        compiler_params=_cparams(("arbitrary",)),
        name="combine",
    )(pos2, pos2, ys, rslab, x1, g2, final_g)


def _layer(x, c, w_ada, b_ada, norm1_g, w_in, conv_w, a_log, dt_bias, dn_onorm_g, fox_f_bias,
           w_out, norm2_g, w_rg, b_rg, w_re, b_re, w1, w3, w2, final_g, final):
    bsz, seq, d = x.shape
    n = bsz * seq
    x2 = x.reshape(n, d)

    mod = _adaln(c, w_ada, b_ada)
    sh1, sc1, g1, sh2, sc2, g2 = [m.reshape(bsz, 1, d) for m in jnp.split(mod, 6, axis=-1)]

    o_a = 4 * HEAD_W
    o_b = o_a + 2 * HEADS
    o_f = o_b + 3 * HEAD_W
    w_main = jnp.concatenate([w_in[:, :o_a], w_in[:, o_b:o_f]], axis=1).astype(BF16)
    w_small = jnp.zeros((d, LANES), F32)
    w_small = w_small.at[:, 0:2 * HEADS].set(w_in[:, o_a:o_b]).at[:, 2 * HEADS:3 * HEADS].set(w_in[:, o_f:])
    ws_hi, ws_lo = _split_bf16(w_small)

    proj, gates = _inproj(x2, sc1, sh1, norm1_g.reshape(1, d), w_main, ws_hi, ws_lo, seq)
    proj3 = proj.reshape(bsz, seq, MAIN_W)

    def lane_row(vals, off):
        return jnp.zeros((1, LANES), F32).at[0, off:off + HEADS].set(vals)

    slab = _gates(gates.reshape(bsz, seq, LANES), lane_row(a_log, 0), lane_row(dt_bias, 0),
                  lane_row(fox_f_bias, L_F))
    nc = seq // CHUNK
    gct = slab[:, :, L_GC:L_GC + HEADS].reshape(bsz, nc, CHUNK, HEADS).transpose(0, 1, 3, 2)
    gct = gct.reshape(bsz, nc, HEADS, 1, CHUNK)
    f_rows = slab[:, :, L_F:L_F + HEADS].transpose(0, 2, 1).reshape(bsz, HEADS, seq // FOX_T, 1, FOX_T)

    o_dn = _deltanet(proj3, conv_w, slab, gct, dn_onorm_g.reshape(1, HEAD_DIM))
    o_fx = _fox(proj3, f_rows)

    wr = jnp.zeros((d, LANES), F32).at[:, :N_GROUPS].set(w_rg).at[:, N_GROUPS:N_GROUPS + N_EXPERTS].set(w_re)
    br = jnp.zeros((1, LANES), F32).at[0, :N_GROUPS].set(b_rg).at[0, N_GROUPS:N_GROUPS + N_EXPERTS].set(b_re)
    wr_hi, wr_lo = _split_bf16(wr)
    w_out_b = w_out.astype(BF16)
    x1, h2, rslab = _outproj(o_dn.reshape(n, HEAD_W), o_fx.reshape(n, HEAD_W), w_out_b[:HEAD_W], w_out_b[HEAD_W:],
                             x2, g1, sc2, sh2, norm2_g.reshape(1, d), wr_hi, wr_lo, br, seq)

    pos_slab, counts = _route(rslab)
    pos = pos_slab[:, 0:2]

    cnt = counts[0, :N_EXPERTS].astype(jnp.int32)
    tiles_per = (cnt + MOE_TM - 1) // MOE_TM
    tile_end = jnp.cumsum(tiles_per)
    base = (tile_end - tiles_per) * MOE_TM
    n_tiles = (2 * n) // MOE_TM + N_EXPERTS
    p_rows = n_tiles * MOE_TM
    tid = jnp.arange(n_tiles, dtype=jnp.int32)
    tile_valid = (tid < tile_end[-1]).astype(jnp.int32)
    te_raw = jnp.minimum(jnp.sum(tid[:, None] >= tile_end[None, :], axis=1), N_EXPERTS - 1).astype(jnp.int32)
    last_e = te_raw[jnp.maximum(tile_end[-1] - 1, 0)]
    tile_expert = jnp.where(tile_valid == 1, te_raw, last_e)
    tile_first = (jnp.concatenate([jnp.array([-1], jnp.int32), tile_expert[:-1]]) != tile_expert).astype(jnp.int32)
    pad_start = base + cnt
    pad_len = tiles_per * MOE_TM - cnt
    eid = jnp.arange(N_EXPERTS, dtype=jnp.int32)
    has = tiles_per > 0
    slot_e = (jnp.cumsum(has.astype(jnp.int32)) - 1) & 1
    later = jnp.where(has[None, :] & (eid[None, :] > eid[:, None]), eid[None, :], N_EXPERTS)
    next_e = jnp.min(later, axis=1)
    next_e = jnp.where(next_e < N_EXPERTS, next_e, -1).astype(jnp.int32)
    tile_slot = slot_e[tile_expert].astype(jnp.int32)
    tile_next = next_e[tile_expert]

    xs = _dispatch(pad_start, pad_len, tile_end[-1:], pos.reshape(n // DISP_TB, 1, 2 * DISP_TB), h2, p_rows)
    ys = _experts(tile_expert, tile_valid, tile_first, tile_slot, tile_next, xs, w1.reshape(N_EXPERTS, d, D_EXPERT),
                  w3.reshape(N_EXPERTS, d, D_EXPERT), w2.reshape(N_EXPERTS, D_EXPERT, d))
    out = _combine(pos.reshape(n // COMB_TB, 1, 2 * COMB_TB), ys, rslab, x1, g2, final_g.reshape(1, d), seq, final)
    return out.reshape(bsz, seq, d)


def kernel(x, c, w_ada, b_ada, norm1_g, w_in, conv_w, a_log, dt_bias, dn_onorm_g, fox_f_bias, w_out, norm2_g,
           w_router_group, b_router_group, w_router_expert, b_router_expert, w1, w3, w2, final_g):
    depth = w_ada.shape[0]
    for l in range(depth):
        x = _layer(x, c, w_ada[l], b_ada[l], norm1_g[l], w_in[l], conv_w[l], a_log[l], dt_bias[l], dn_onorm_g[l],
                   fox_f_bias[l], w_out[l], norm2_g[l], w_router_group[l], b_router_group[l], w_router_expert[l],
                   b_router_expert[l], w1[l], w3[l], w2[l], final_g, l == depth - 1)
    return x
```

```python
import functools

import jax
import jax.numpy as jnp
from jax import lax
from jax.experimental import pallas as pl
from jax.experimental.pallas import tpu as pltpu

F32 = jnp.float32
BF16 = jnp.bfloat16

D_MODEL = 2048
EPS = 1e-6
CHUNK = 64
HEADS = 8
HEAD_DIM = 128
HEAD_W = HEADS * HEAD_DIM
CONV_K = 4
N_GROUPS = 4
EXPERTS_PER_GROUP = 8
N_EXPERTS = N_GROUPS * EXPERTS_PER_GROUP
D_EXPERT = 512
LANES = 128
SUBLANES = 8
MAIN_W = 7 * HEAD_W
VMEM_LIMIT = 56 * 1024 * 1024

L_GC, L_BETA, L_F, L_EGC, L_EK, L_ELAST = 0, 8, 16, 24, 32, 40


def _cparams(sem):
    return pltpu.CompilerParams(dimension_semantics=sem, vmem_limit_bytes=VMEM_LIMIT)


def _split_bf16(a):
    hi = a.astype(BF16)
    lo = (a - hi.astype(F32)).astype(BF16)
    return hi, lo


def _dot(a, b):
    return jnp.dot(a, b, preferred_element_type=F32)


def _dot_nt(a, b):
    return lax.dot_general(a, b, (((1,), (1,)), ((), ())), preferred_element_type=F32)


def _dot3(a, b):
    ah, al = _split_bf16(a)
    bh, bl = _split_bf16(b)
    return _dot(ah, bh) + (_dot(al, bh) + _dot(ah, bl))


def _dot3_pre(a, b_hi_lo):
    ah, al = _split_bf16(a)
    n = b_hi_lo.shape[1] // 2
    both = _dot(ah, b_hi_lo)
    return both[:, :n] + (both[:, n:] + _dot(al, b_hi_lo[:, :n]))


def _softplus(x):
    return jnp.maximum(x, 0.0) + jnp.log1p(jnp.exp(-jnp.abs(x)))


def _silu(x):
    return x * jax.nn.sigmoid(x)


def _adaln_kernel(c_ref, w_ref, b_ref, o_ref):
    c = c_ref[...]
    o_ref[...] = _dot(_silu(c).astype(BF16), w_ref[...].astype(BF16)) + b_ref[...]


def _adaln(c, w, b):
    bsz = c.shape[0]
    n = w.shape[1]
    tn = 1024
    cp = jnp.zeros((SUBLANES, D_MODEL), F32).at[:bsz].set(c)
    out = pl.pallas_call(
        _adaln_kernel,
        grid=(n // tn,),
        in_specs=[
            pl.BlockSpec((SUBLANES, D_MODEL), lambda j: (0, 0)),
            pl.BlockSpec((D_MODEL, tn), lambda j: (0, j)),
            pl.BlockSpec((1, tn), lambda j: (0, j)),
        ],
        out_specs=pl.BlockSpec((SUBLANES, tn), lambda j: (0, j)),
        out_shape=jax.ShapeDtypeStruct((SUBLANES, n), F32),
        compiler_params=_cparams(("parallel",)),
        name="adaln",
    )(cp, w, b.reshape(1, n))
    return out[:bsz]


INPROJ_TM = 1024
INPROJ_TN = 512
ROW_STEP = 128


def _modulated_norm(x, g, sc, sh):
    r = lax.rsqrt(jnp.mean(x * x, axis=-1, keepdims=True) + EPS)
    return (x * r) * (g * (1.0 + sc)) + sh


def _inproj_kernel(x_ref, sc_ref, sh_ref, g_ref, w_ref, ws_ref, o_ref, og_ref, h_ref):
    @pl.when(pl.program_id(1) == 0)
    def _():
        def body(i, _):
            rows = pl.ds(pl.multiple_of(i * ROW_STEP, ROW_STEP), ROW_STEP)
            h = _modulated_norm(x_ref[rows, :], g_ref[...], sc_ref[...], sh_ref[...])
            hb = h.astype(BF16)
            h_ref[rows, :] = hb
            both = _dot_nt(hb, ws_ref[...])
            og_ref[rows, :] = both[:, :LANES] + both[:, LANES:]
            return 0

        lax.fori_loop(0, INPROJ_TM // ROW_STEP, body, 0)

    o_ref[...] = _dot_nt(h_ref[...], w_ref[...]).astype(BF16)


def _inproj(x2, sc1, sh1, g, w_main_t, ws_t, seq):
    n = x2.shape[0]
    tm, tn = INPROJ_TM, INPROJ_TN
    per_b = seq // tm
    return pl.pallas_call(
        _inproj_kernel,
        grid=(n // tm, MAIN_W // tn),
        in_specs=[
            pl.BlockSpec((tm, D_MODEL), lambda i, j: (i, 0)),
            pl.BlockSpec((None, 1, D_MODEL), lambda i, j: (i // per_b, 0, 0)),
            pl.BlockSpec((None, 1, D_MODEL), lambda i, j: (i // per_b, 0, 0)),
            pl.BlockSpec((1, D_MODEL), lambda i, j: (0, 0)),
            pl.BlockSpec((tn, D_MODEL), lambda i, j: (j, 0)),
            pl.BlockSpec((2 * LANES, D_MODEL), lambda i, j: (0, 0)),
        ],
        out_specs=[
            pl.BlockSpec((tm, tn), lambda i, j: (i, j)),
            pl.BlockSpec((tm, LANES), lambda i, j: (i, 0)),
        ],
        out_shape=[
            jax.ShapeDtypeStruct((n, MAIN_W), BF16),
            jax.ShapeDtypeStruct((n, LANES), F32),
        ],
        scratch_shapes=[pltpu.VMEM((tm, D_MODEL), BF16)],
        compiler_params=_cparams(("parallel", "arbitrary")),
        name="inproj",
    )(x2, sc1, sh1, g, w_main_t, ws_t)


GATES_TB = 256


def _split3(a):
    p0 = a.astype(BF16)
    r1 = a - p0.astype(F32)
    p1 = r1.astype(BF16)
    p2 = (r1 - p1.astype(F32)).astype(BF16)
    return p0, p1, p2


def _dot_ones(m, a):
    p0, p1, p2 = _split3(a)
    return _dot(m, p0) + (_dot(m, p1) + _dot(m, p2))


def _gates_kernel(x_ref, alog_ref, dt_ref, fb_ref, o_ref, carry_ref):
    tb = GATES_TB

    @pl.when(pl.program_id(1) == 0)
    def _():
        carry_ref[...] = jnp.zeros_like(carry_ref)

    x = x_ref[...]
    lane = lax.broadcasted_iota(jnp.int32, (tb, LANES), 1)
    g = -jnp.exp(alog_ref[...]) * _softplus(x + dt_ref[...])
    beta = jax.nn.sigmoid(x)
    lf = -_softplus(-(x + fb_ref[...]))

    ri = lax.broadcasted_iota(jnp.int32, (tb, tb), 0)
    ci = lax.broadcasted_iota(jnp.int32, (tb, tb), 1)
    same_chunk = (ri // CHUNK) == (ci // CHUNK)
    tri = (ri >= ci)
    m_all = jnp.where(tri, 1.0, 0.0).astype(BF16)
    m_chunk = jnp.where(tri & same_chunk, 1.0, 0.0).astype(BF16)
    m_tot = jnp.where(same_chunk, 1.0, 0.0).astype(BF16)

    gc = _dot_ones(m_chunk, g)
    glast = _dot_ones(m_tot, g)
    fcum = _dot_ones(m_all, lf) + carry_ref[...]
    carry_ref[...] = fcum[tb - 1:tb, :]

    in_a = lane < HEADS
    egc = jnp.where(in_a, jnp.exp(gc), 0.0)
    ek = jnp.where(in_a, jnp.exp(glast - gc), 0.0)
    elast = jnp.where(in_a, jnp.exp(glast), 0.0)
    out = jnp.where(in_a, gc, jnp.where(lane < 2 * HEADS, beta, jnp.where(lane < 3 * HEADS, fcum, 0.0)))
    out = out + pltpu.roll(egc, L_EGC, 1) + pltpu.roll(ek, L_EK, 1) + pltpu.roll(elast, L_ELAST, 1)
    o_ref[...] = out


def _gates(gates, alog_row, dt_row, fb_row):
    bsz, seq, _ = gates.shape
    tb = GATES_TB
    row = pl.BlockSpec((1, LANES), lambda b, t: (0, 0))
    return pl.pallas_call(
        _gates_kernel,
        grid=(bsz, seq // tb),
        in_specs=[pl.BlockSpec((None, tb, LANES), lambda b, t: (b, t, 0)), row, row, row],
        out_specs=pl.BlockSpec((None, tb, LANES), lambda b, t: (b, t, 0)),
        out_shape=jax.ShapeDtypeStruct((bsz, seq, LANES), F32),
        scratch_shapes=[pltpu.VMEM((1, LANES), F32)],
        compiler_params=_cparams(("parallel", "arbitrary")),
        name="gates",
    )(gates, alog_row, dt_row, fb_row)


DN_TB = 256
DN_GROUP = 4
HALO = SUBLANES


def _bdot(a, b):
    return lax.dot_general(a, b, (((2,), (1,)), ((0,), (0,))), preferred_element_type=F32)


def _bdot_nt(a, b):
    return lax.dot_general(a, b, (((2,), (2,)), ((0,), (0,))), preferred_element_type=F32)


def _bdot3(a, b):
    ah, al = _split_bf16(a)
    bh, bl = _split_bf16(b)
    return _bdot(ah, bh) + (_bdot(al, bh) + _bdot(ah, bl))


def _inv_unit_lower(a, eye, blk16, blk32):
    n = jnp.where(blk16, -a, 0.0)
    e1 = jnp.where(blk32 & jnp.logical_not(blk16), a, 0.0)
    e2 = jnp.where(blk32, 0.0, a)
    t = eye + n
    p = _bdot3(n, n)
    t = t + _bdot3(t, p)
    p = _bdot3(p, p)
    t = t + _bdot3(t, p)
    p = _bdot3(p, p)
    t = t + _bdot3(t, p)
    t = t - _bdot3(_bdot3(t, e1), t)
    t = t - _bdot3(_bdot3(t, e2), t)
    return t


def _deltanet_kernel(q_ref, k_ref, v_ref, z_ref, wq_ref, wk_ref, wv_ref, slab_ref, gct_ref, og_ref,
                     o_ref, ext_ref, qn_ref, kn_ref, vv_ref, s_ref):
    tb = DN_TB

    @pl.when(pl.program_id(1) == 0)
    def _():
        ext_ref[:, 0:HALO, :] = jnp.zeros((3, HALO, HEAD_W), F32)
        s_ref[...] = jnp.zeros_like(s_ref)

    for idx, (u_ref, w_ref) in enumerate(((q_ref, wq_ref), (k_ref, wk_ref), (v_ref, wv_ref))):
        for h in range(HEADS):
            cols = slice(h * HEAD_DIM, (h + 1) * HEAD_DIM)
            ext_ref[idx, HALO:HALO + tb, cols] = u_ref[:, cols].astype(F32)
            y = None
            for j in range(CONV_K):
                start = HALO - (CONV_K - 1) + j
                term = ext_ref[idx, start:start + tb, cols] * w_ref[j:j + 1, cols]
                y = term if y is None else y + term
            y = _silu(y)
            if idx == 2:
                vv_ref[h] = y
            else:
                yn = y * lax.rsqrt(jnp.sum(y * y, axis=-1, keepdims=True) + EPS)
                if idx == 0:
                    qn_ref[h] = (yn * (HEAD_DIM ** -0.5)).astype(BF16)
                else:
                    kn_ref[h] = yn.astype(BF16)
        ext_ref[idx, 0:HALO, :] = ext_ref[idx, tb:tb + HALO, :]

    ri = lax.broadcasted_iota(jnp.int32, (DN_GROUP * HEADS, CHUNK, CHUNK), 1)
    ci = lax.broadcasted_iota(jnp.int32, (DN_GROUP * HEADS, CHUNK, CHUNK), 2)
    incl = ri >= ci
    strict = ri > ci
    eye = jnp.where(ri == ci, 1.0, 0.0)
    blk16 = (ri // 16) == (ci // 16)
    blk32 = (ri // 32) == (ci // 32)

    def group_body(c, _):
        rows = [pl.ds(pl.multiple_of((c * DN_GROUP + g) * CHUNK, CHUNK), CHUNK) for g in range(DN_GROUP)]
        slabs = [slab_ref[r, :] for r in rows]

        def col(off, width):
            return jnp.stack([jnp.broadcast_to(sl[:, off + h:off + h + 1], (CHUNK, width))
                              for sl in slabs for h in range(HEADS)])

        def grouped(ref):
            return jnp.concatenate([ref[:, r, :] for r in rows], axis=0)

        q = grouped(qn_ref)
        k = grouped(kn_ref)
        v = grouped(vv_ref)
        beta = col(L_BETA, HEAD_DIM)
        egc = col(L_EGC, HEAD_DIM)
        gc_row = jnp.concatenate([gct_ref[c * DN_GROUP + g] for g in range(DN_GROUP)], axis=0)

        decay = jnp.where(incl, jnp.exp(col(L_GC, CHUNK) - gc_row), 0.0)
        kk = _bdot_nt(k, k)
        qk = (_bdot_nt(q, k) * decay).astype(BF16)
        a = jnp.where(strict, kk * decay * beta[:, :, :CHUNK], 0.0)
        t = _inv_unit_lower(a, eye, blk16, blk32)
        th, tl = _split_bf16(t)

        kf = k.astype(F32)
        vb = (v * beta).astype(BF16)
        kbg = (kf * (beta * egc)).astype(BF16)
        u = _bdot(th, vb) + _bdot(tl, vb)
        w = (_bdot(th, kbg) + _bdot(tl, kbg)).astype(BF16)
        qd = (q.astype(F32) * egc).astype(BF16)
        kd = kf * col(L_EK, HEAD_DIM)
        kdt = jnp.stack([kd[n].T for n in range(DN_GROUP * HEADS)]).astype(BF16)

        for g in range(DN_GROUP):
            sel = slice(g * HEADS, (g + 1) * HEADS)
            s = s_ref[...]
            sb = s.astype(BF16)
            vnb = (u[sel] - _bdot(w[sel], sb)).astype(BF16)
            o = _bdot(qd[sel], sb) + _bdot(qk[sel], vnb)
            elast = jnp.stack([jnp.broadcast_to(slabs[g][CHUNK - 1:CHUNK, L_ELAST + h:L_ELAST + h + 1],
                                                (HEAD_DIM, HEAD_DIM)) for h in range(HEADS)])
            s_ref[...] = s * elast + _bdot(kdt[sel], vnb)

            r = lax.rsqrt(jnp.mean(o * o, axis=-1, keepdims=True) + EPS)
            on = (o * r) * og_ref[...]
            for h in range(HEADS):
                cols = slice(h * HEAD_DIM, (h + 1) * HEAD_DIM)
                o_ref[rows[g], cols] = (on[h] * _silu(z_ref[rows[g], cols].astype(F32))).astype(BF16)
        return 0

    lax.fori_loop(0, tb // (CHUNK * DN_GROUP), group_body, 0)


def _deltanet(proj3, conv_w, slab, gct, onorm_g):
    bsz, seq, _ = proj3.shape
    tb = DN_TB
    nct = tb // CHUNK

    def colblk(j):
        return pl.BlockSpec((None, tb, HEAD_W), lambda b, t: (b, t, j))

    def wblk(j):
        return pl.BlockSpec((CONV_K, HEAD_W), lambda b, t: (0, j))

    return pl.pallas_call(
        _deltanet_kernel,
        grid=(bsz, seq // tb),
        in_specs=[
            colblk(0), colblk(1), colblk(2), colblk(3),
            wblk(0), wblk(1), wblk(2),
            pl.BlockSpec((None, tb, LANES), lambda b, t: (b, t, 0)),
            pl.BlockSpec((None, nct, HEADS, 1, CHUNK), lambda b, t: (b, t, 0, 0, 0)),
            pl.BlockSpec((1, HEAD_DIM), lambda b, t: (0, 0)),
        ],
        out_specs=pl.BlockSpec((None, tb, HEAD_W), lambda b, t: (b, t, 0)),
        out_shape=jax.ShapeDtypeStruct((bsz, seq, HEAD_W), BF16),
        scratch_shapes=[
            pltpu.VMEM((3, tb + HALO, HEAD_W), F32),
            pltpu.VMEM((HEADS, tb, HEAD_DIM), BF16),
            pltpu.VMEM((HEADS, tb, HEAD_DIM), BF16),
            pltpu.VMEM((HEADS, tb, HEAD_DIM), F32),
            pltpu.VMEM((HEADS, HEAD_DIM, HEAD_DIM), F32),
        ],
        compiler_params=_cparams(("parallel", "arbitrary")),
        name="deltanet",
    )(proj3, proj3, proj3, proj3, conv_w, conv_w, conv_w, slab, gct, onorm_g)


FOX_T = 512
FOX_STRIP = 32
NEG_INF = float("-inf")
LOG2E = 1.4426950408889634


def _fox_kernel(q_ref, k_ref, v_ref, f_ref, o_ref, s_a_ref, s_b_ref, p_a_ref, p_b_ref, m_ref, l_ref, alpha_ref,
                acc_ref):
    tq = FOX_T
    i = pl.program_id(2)
    lane_blocks = tq // LANES
    qs = (q_ref[...].astype(F32) * (HEAD_DIM ** -0.5 * LOG2E)).astype(BF16)
    f0 = f_ref[i][:, 0:1]
    m_ref[...] = jnp.full_like(m_ref, NEG_INF)
    l_ref[...] = jnp.zeros_like(l_ref)
    acc_ref[...] = jnp.zeros_like(acc_ref)

    def keys(j):
        return pl.ds(pl.multiple_of(j * tq, tq), tq)

    def scores(j):
        return _dot_nt(qs, k_ref[keys(j), :])

    def softmax(j, s_view, p_view, masked):
        bias = (f0 - f_ref[j]) * LOG2E

        def logits(r):
            rows = slice(r * FOX_STRIP, (r + 1) * FOX_STRIP)
            s = s_view[rows, :] + bias
            if masked:
                ri = r * FOX_STRIP + lax.broadcasted_iota(jnp.int32, (FOX_STRIP, tq), 0)
                ci = lax.broadcasted_iota(jnp.int32, (FOX_STRIP, tq), 1)
                s = jnp.where(ci <= ri, s, NEG_INF)
            return rows, s

        for r in range(tq // FOX_STRIP):
            rows, s = logits(r)
            m_old = m_ref[rows, :]
            m_new = jnp.maximum(m_old, jnp.max(s, axis=-1, keepdims=True))
            alpha_ref[rows, :] = jnp.exp2(m_old - m_new)
            m_ref[rows, :] = m_new
        for r in range(tq // FOX_STRIP):
            rows, s = logits(r)
            p = jnp.exp2(s - jnp.concatenate([m_ref[rows, :]] * lane_blocks, axis=1))
            alpha = alpha_ref[rows, :]
            l_ref[rows, :] = alpha * l_ref[rows, :] + jnp.sum(p, axis=-1, keepdims=True)
            p_view[rows, :] = p.astype(BF16)

    s_a_ref[...] = scores(i)
    s_b_ref[...] = scores(jnp.maximum(i - 1, 0))
    softmax(i, s_a_ref, p_a_ref, True)

    def step(t, s_cur, p_cur, s_nxt, p_prv):
        j = i - t
        s_nxt[...] = scores(jnp.maximum(j - 1, 0))
        pv = _dot(p_prv[...], v_ref[keys(j + 1), :])
        softmax(j, s_cur, p_cur, False)
        acc_ref[...] = alpha_ref[...] * (acc_ref[...] + pv)

    def pair(u, _):
        step(2 * u + 1, s_b_ref, p_b_ref, s_a_ref, p_a_ref)
        step(2 * u + 2, s_a_ref, p_a_ref, s_b_ref, p_b_ref)
        return 0

    lax.fori_loop(0, lax.shift_right_logical(i, 1), pair, 0)

    def finish(p_last):
        acc = acc_ref[...] + _dot(p_last[...], v_ref[keys(0), :])
        o_ref[...] = (acc / l_ref[...]).astype(BF16)

    @pl.when((i & 1) == 1)
    def _():
        step(i, s_b_ref, p_b_ref, s_a_ref, p_a_ref)
        finish(p_b_ref)

    @pl.when((i & 1) == 0)
    def _():
        finish(p_a_ref)


def _fox(proj3, f_rows):
    bsz, seq, _ = proj3.shape
    tq = FOX_T
    qb, kb, vb = 4 * HEADS, 5 * HEADS, 6 * HEADS
    return pl.pallas_call(
        _fox_kernel,
        grid=(bsz, HEADS, seq // tq),
        in_specs=[
            pl.BlockSpec((None, tq, HEAD_DIM), lambda b, h, i: (b, i, qb + h)),
            pl.BlockSpec((None, seq, HEAD_DIM), lambda b, h, i: (b, 0, kb + h)),
            pl.BlockSpec((None, seq, HEAD_DIM), lambda b, h, i: (b, 0, vb + h)),
            pl.BlockSpec((None, None, seq // tq, 1, tq), lambda b, h, i: (b, h, 0, 0, 0)),
        ],
        out_specs=pl.BlockSpec((None, tq, HEAD_DIM), lambda b, h, i: (b, i, h)),
        out_shape=jax.ShapeDtypeStruct((bsz, seq, HEAD_W), BF16),
        scratch_shapes=[
            pltpu.VMEM((tq, tq), F32),
            pltpu.VMEM((tq, tq), F32),
            pltpu.VMEM((tq, tq), BF16),
            pltpu.VMEM((tq, tq), BF16),
            pltpu.VMEM((tq, LANES), F32),
            pltpu.VMEM((tq, LANES), F32),
            pltpu.VMEM((tq, LANES), F32),
            pltpu.VMEM((tq, HEAD_DIM), F32),
        ],
        compiler_params=_cparams(("parallel", "parallel", "arbitrary")),
        name="fox",
    )(proj3, proj3, proj3, f_rows)


OUT_TM = 512
R_E0, R_E1, R_W0, R_W1 = 0, 1, 2, 3


def _first_argmax(vals, lane):
    m = jnp.max(vals, axis=-1, keepdims=True)
    idx = jnp.min(jnp.where(vals == m, lane, LANES), axis=-1, keepdims=True)
    return m, idx


def _outproj_kernel(oa_ref, ob_ref, wa_ref, wb_ref, x_ref, g1_ref, sc_ref, sh_ref, g_ref,
                    wr_ref, br_ref, x1_ref, h2_ref, r_ref):
    mix = _dot(oa_ref[...], wa_ref[...]) + _dot(ob_ref[...], wb_ref[...])
    x1 = x_ref[...] + g1_ref[...] * mix
    x1_ref[...] = x1
    h2 = _modulated_norm(x1, g_ref[...], sc_ref[...], sh_ref[...])
    h2_ref[...] = h2

    logits = _dot3_pre(h2, wr_ref[...]) + br_ref[...]
    tm = logits.shape[0]
    lane = lax.broadcasted_iota(jnp.int32, (tm, LANES), 1)
    gl = jnp.where(lane < N_GROUPS, logits, NEG_INF)
    gmax, gidx = _first_argmax(gl, lane)
    pg = 1.0 / jnp.sum(jnp.exp(gl - gmax), axis=-1, keepdims=True)
    e_lane = lane - N_GROUPS
    in_grp = (e_lane >= gidx * EXPERTS_PER_GROUP) & (e_lane < (gidx + 1) * EXPERTS_PER_GROUP)
    el = jnp.where(in_grp, logits, NEG_INF)
    v0, i0 = _first_argmax(el, lane)
    v1, i1 = _first_argmax(jnp.where(lane == i0, NEG_INF, el), lane)
    ex = jnp.exp(v1 - v0)
    w0 = pg / (1.0 + ex)
    w1 = pg * ex / (1.0 + ex)
    e0 = (i0 - N_GROUPS).astype(F32)
    e1 = (i1 - N_GROUPS).astype(F32)
    r_ref[...] = jnp.where(lane == R_E0, e0, jnp.where(lane == R_E1, e1,
                           jnp.where(lane == R_W0, w0, jnp.where(lane == R_W1, w1, 0.0))))


def _outproj(o_a, o_b, wa, wb, x2, g1, sc2, sh2, g, wr_hi_lo, br, seq):
    n = x2.shape[0]
    tm = OUT_TM
    per_b = seq // tm
    modrow = pl.BlockSpec((None, 1, D_MODEL), lambda i: (i // per_b, 0, 0))
    const = lambda shape: pl.BlockSpec(shape, lambda i: (0, 0))
    return pl.pallas_call(
        _outproj_kernel,
        grid=(n // tm,),
        in_specs=[
            pl.BlockSpec((tm, HEAD_W), lambda i: (i, 0)),
            pl.BlockSpec((tm, HEAD_W), lambda i: (i, 0)),
            const((HEAD_W, D_MODEL)), const((HEAD_W, D_MODEL)),
            pl.BlockSpec((tm, D_MODEL), lambda i: (i, 0)),
            modrow, modrow, modrow,
            const((1, D_MODEL)),
            const((D_MODEL, 2 * LANES)), const((1, LANES)),
        ],
        out_specs=[
            pl.BlockSpec((tm, D_MODEL), lambda i: (i, 0)),
            pl.BlockSpec((tm, D_MODEL), lambda i: (i, 0)),
            pl.BlockSpec((tm, LANES), lambda i: (i, 0)),
        ],
        out_shape=[
            jax.ShapeDtypeStruct((n, D_MODEL), F32),
            jax.ShapeDtypeStruct((n, D_MODEL), F32),
            jax.ShapeDtypeStruct((n, LANES), F32),
        ],
        compiler_params=_cparams(("parallel",)),
        name="outproj",
    )(o_a, o_b, wa, wb, x2, g1, sc2, sh2, g, wr_hi_lo, br)


MOE_TM = 256
ROUTE_TB = 512


def _route_kernel(r_ref, pos_ref, cnt_ref, run_ref, base_ref):
    tb = ROUTE_TB
    phase = pl.program_id(0)
    t = pl.program_id(1)
    r = r_ref[...]
    lane = lax.broadcasted_iota(jnp.int32, (tb, LANES), 1)
    e0 = r[:, R_E0:R_E0 + 1].astype(jnp.int32)
    e1 = r[:, R_E1:R_E1 + 1].astype(jnp.int32)
    oh0 = lane == e0
    oh1 = lane == e1
    both = jnp.where(oh0 | oh1, 1.0, 0.0)

    @pl.when((phase == 0) & (t == 0))
    def _():
        run_ref[...] = jnp.zeros_like(run_ref)

    @pl.when(phase == 0)
    def _():
        run_ref[...] = run_ref[...] + jnp.sum(both, axis=0, keepdims=True)

    @pl.when((phase == 1) & (t == 0))
    def _():
        counts = run_ref[...]
        cnt_ref[...] = counts
        padded = jnp.ceil(counts / MOE_TM) * MOE_TM
        li = lax.broadcasted_iota(jnp.int32, (LANES, LANES), 0)
        lj = lax.broadcasted_iota(jnp.int32, (LANES, LANES), 1)
        upper = jnp.where(li < lj, 1.0, 0.0).astype(BF16)
        hi = jnp.floor(padded / 256.0)
        lo = padded - hi * 256.0
        hi8 = jnp.broadcast_to(hi, (SUBLANES, LANES)).astype(BF16)
        lo8 = jnp.broadcast_to(lo, (SUBLANES, LANES)).astype(BF16)
        base = _dot(hi8, upper) * 256.0 + _dot(lo8, upper)
        base_ref[...] = base[0:1, :]
        run_ref[...] = jnp.zeros_like(run_ref)

    @pl.when(phase == 1)
    def _():
        ri = lax.broadcasted_iota(jnp.int32, (tb, tb), 0)
        ci = lax.broadcasted_iota(jnp.int32, (tb, tb), 1)
        strict = jnp.where(ri > ci, 1.0, 0.0).astype(BF16)
        before = _dot(strict, both.astype(BF16)) + run_ref[...] + base_ref[...]
        p0 = jnp.sum(jnp.where(oh0, before, 0.0), axis=-1, keepdims=True)
        p1 = jnp.sum(jnp.where(oh1, before, 0.0), axis=-1, keepdims=True)
        pos_ref[...] = jnp.where(lane == 0, p0, jnp.where(lane == 1, p1, 0.0)).astype(jnp.int32)
        run_ref[...] = run_ref[...] + jnp.sum(both, axis=0, keepdims=True)


def _route(rslab):
    n = rslab.shape[0]
    tb = ROUTE_TB
    return pl.pallas_call(
        _route_kernel,
        grid=(2, n // tb),
        in_specs=[pl.BlockSpec((tb, LANES), lambda p, t: (t, 0))],
        out_specs=[
            pl.BlockSpec((tb, LANES), lambda p, t: (p * t, 0)),
            pl.BlockSpec((1, LANES), lambda p, t: (0, 0)),
        ],
        out_shape=[
            jax.ShapeDtypeStruct((n, LANES), jnp.int32),
            jax.ShapeDtypeStruct((1, LANES), F32),
        ],
        scratch_shapes=[pltpu.VMEM((1, LANES), F32), pltpu.VMEM((1, LANES), F32)],
        compiler_params=_cparams(("arbitrary", "arbitrary")),
        name="route",
    )(rslab)


DISP_TB = 1024
DMA_UNROLL = 8


def _dispatch_kernel(pad_start_ref, pad_len_ref, used_ref, pos_ref, h2_ref, xs_ref, zero_ref, sem, zsem):
    t = pl.program_id(0)
    tb = DISP_TB
    n_tiles = xs_ref.shape[0] // MOE_TM

    def row_copy(i, k):
        return pltpu.make_async_copy(h2_ref.at[pl.ds(i, 1)], xs_ref.at[pl.ds(pos_ref[0, 2 * i + k], 1)], sem)

    def issue(i, _):
        row_copy(i, 0).start(priority=0)
        row_copy(i, 1).start(priority=1)
        return 0

    lax.fori_loop(0, tb, issue, 0, unroll=DMA_UNROLL)

    @pl.when(t == 0)
    def _():
        zero_ref[...] = jnp.zeros_like(zero_ref)

        def zero_rows(wait, off, rows):
            cp = pltpu.make_async_copy(zero_ref.at[pl.ds(0, rows)], xs_ref.at[pl.ds(off, rows)], zsem)
            cp.wait() if wait else cp.start()

        def per_expert(wait, e, _):
            start = pad_start_ref[e]
            head = (-start) & (SUBLANES - 1)
            for r in range(SUBLANES - 1):
                pl.when(r < head)(functools.partial(zero_rows, wait, start + r, 1))
            off = start + head
            rest = pad_len_ref[e] - head
            piece = MOE_TM // 2
            while piece >= SUBLANES:
                take = (rest & piece) != 0
                pl.when(take)(functools.partial(zero_rows, wait, pl.multiple_of(off, SUBLANES), piece))
                off = off + jnp.where(take, piece, 0)
                piece //= 2
            return 0

        def per_tile(wait, i, _):
            zero_rows(wait, pl.multiple_of(i * MOE_TM, MOE_TM), MOE_TM)
            return 0

        for wait in (False, True):
            lax.fori_loop(0, N_EXPERTS, functools.partial(per_expert, wait), 0)
            lax.fori_loop(used_ref[0], n_tiles, functools.partial(per_tile, wait), 0)

    def drain(i, _):
        row_copy(i, 0).wait()
        row_copy(i, 1).wait()
        return 0

    lax.fori_loop(0, tb, drain, 0, unroll=DMA_UNROLL)


def _dispatch(pad_start, pad_len, used, pos2, h2, p_rows):
    n = h2.shape[0]
    tb = DISP_TB
    return pl.pallas_call(
        _dispatch_kernel,
        grid_spec=pltpu.PrefetchScalarGridSpec(
            num_scalar_prefetch=3,
            grid=(n // tb,),
            in_specs=[
                pl.BlockSpec((None, 1, 2 * tb), lambda t, *_: (t, 0, 0), memory_space=pltpu.SMEM),
                pl.BlockSpec((tb, D_MODEL), lambda t, *_: (t, 0)),
            ],
            out_specs=pl.BlockSpec(memory_space=pl.ANY),
            scratch_shapes=[
                pltpu.VMEM((MOE_TM, D_MODEL), F32),
                pltpu.SemaphoreType.DMA(()),
                pltpu.SemaphoreType.DMA(()),
            ],
        ),
        out_shape=jax.ShapeDtypeStruct((p_rows, D_MODEL), F32),
        compiler_params=_cparams(("arbitrary",)),
        name="dispatch",
    )(pad_start, pad_len, used, pos2, h2)


def _experts_kernel(te_ref, tv_ref, tf_ref, ts_ref, tn_ref, xs_ref, w1_hbm, w3_hbm, w2_hbm, ys_ref,
                    w1f_ref, w3f_ref, w2f_ref, w1b_ref, w3b_ref, w2b_ref, sem):
    i = pl.program_id(0)

    def weight_copies(e, slot):
        return [pltpu.make_async_copy(src.at[e], dst.at[slot], sem.at[slot])
                for src, dst in ((w1_hbm, w1f_ref), (w3_hbm, w3f_ref), (w2_hbm, w2f_ref))]

    @pl.when(tv_ref[i] != 0)
    def _():
        @pl.when(tf_ref[i] != 0)
        def _():
            slot = ts_ref[i]

            @pl.when(i == 0)
            def _():
                for cp in weight_copies(te_ref[i], slot):
                    cp.start()

            for cp in weight_copies(te_ref[i], slot):
                cp.wait()

            @pl.when(tn_ref[i] >= 0)
            def _():
                for cp in weight_copies(tn_ref[i], 1 - slot):
                    cp.start()

            w1b_ref[...] = w1f_ref[slot].astype(BF16)
            w3b_ref[...] = w3f_ref[slot].astype(BF16)
            w2b_ref[...] = w2f_ref[slot].astype(BF16)

        x = xs_ref[...].astype(BF16)
        a = _dot(x, w1b_ref[...])
        b = _dot(x, w3b_ref[...])
        ys_ref[...] = _dot((_silu(a) * b).astype(BF16), w2b_ref[...])

    @pl.when(tv_ref[i] == 0)
    def _():
        ys_ref[...] = jnp.zeros_like(ys_ref)


def _experts(tile_expert, tile_valid, tile_first, tile_slot, tile_next, xs, w1, w3, w2):
    p_rows = xs.shape[0]
    tm = MOE_TM
    hbm = pl.BlockSpec(memory_space=pl.ANY)
    return pl.pallas_call(
        _experts_kernel,
        grid_spec=pltpu.PrefetchScalarGridSpec(
            num_scalar_prefetch=5,
            grid=(p_rows // tm,),
            in_specs=[pl.BlockSpec((tm, D_MODEL), lambda i, *_: (i, 0)), hbm, hbm, hbm],
            out_specs=pl.BlockSpec((tm, D_MODEL), lambda i, *_: (i, 0)),
            scratch_shapes=[
                pltpu.VMEM((2, D_MODEL, D_EXPERT), F32),
                pltpu.VMEM((2, D_MODEL, D_EXPERT), F32),
                pltpu.VMEM((2, D_EXPERT, D_MODEL), F32),
                pltpu.VMEM((D_MODEL, D_EXPERT), BF16),
                pltpu.VMEM((D_MODEL, D_EXPERT), BF16),
                pltpu.VMEM((D_EXPERT, D_MODEL), BF16),
                pltpu.SemaphoreType.DMA((2,)),
            ],
        ),
        out_shape=jax.ShapeDtypeStruct((p_rows, D_MODEL), F32),
        compiler_params=_cparams(("arbitrary",)),
        name="experts",
    )(tile_expert, tile_valid, tile_first, tile_slot, tile_next, xs, w1, w3, w2)


COMB_TB = 256


def _combine_kernel(pos_ref, pos_next_ref, ys_ref, r_ref, x1_ref, g2_ref, fg_ref, o_ref, buf_ref, sem, *, final):
    tb = COMB_TB
    t = pl.program_id(0)
    slot = t & 1

    def row_copy(p_ref, s, i, k):
        return pltpu.make_async_copy(ys_ref.at[pl.ds(p_ref[0, 2 * i + k], 1)], buf_ref.at[s, k, pl.ds(i, 1)],
                                     sem.at[s])

    def issue(p_ref, s):
        def body(i, _):
            row_copy(p_ref, s, i, 0).start(priority=0)
            row_copy(p_ref, s, i, 1).start(priority=1)
            return 0

        lax.fori_loop(0, tb, body, 0, unroll=DMA_UNROLL)

    def drain(i, _):
        row_copy(pos_ref, slot, i, 0).wait()
        row_copy(pos_ref, slot, i, 1).wait()
        return 0

    pl.when(t == 0)(functools.partial(issue, pos_ref, 0))
    pl.when(t + 1 < pl.num_programs(0))(functools.partial(issue, pos_next_ref, 1 - slot))
    lax.fori_loop(0, tb, drain, 0, unroll=DMA_UNROLL)
    r = r_ref[...]
    y = r[:, R_W0:R_W0 + 1] * buf_ref[slot, 0] + r[:, R_W1:R_W1 + 1] * buf_ref[slot, 1]
    x2 = x1_ref[...] + g2_ref[...] * y
    if final:
        x2 = (x2 * lax.rsqrt(jnp.mean(x2 * x2, axis=-1, keepdims=True) + EPS)) * fg_ref[...]
    o_ref[...] = x2


def _combine(pos2, ys, rslab, x1, g2, final_g, seq, final):
    n = x1.shape[0]
    tb = COMB_TB
    per_b = seq // tb
    return pl.pallas_call(
        functools.partial(_combine_kernel, final=final),
        grid=(n // tb,),
        in_specs=[
            pl.BlockSpec((None, 1, 2 * tb), lambda i: (i, 0, 0), memory_space=pltpu.SMEM),
            pl.BlockSpec((None, 1, 2 * tb), lambda i: (jnp.minimum(i + 1, n // tb - 1), 0, 0),
                         memory_space=pltpu.SMEM),
            pl.BlockSpec(memory_space=pl.ANY),
            pl.BlockSpec((tb, LANES), lambda i: (i, 0)),
            pl.BlockSpec((tb, D_MODEL), lambda i: (i, 0)),
            pl.BlockSpec((None, 1, D_MODEL), lambda i: (i // per_b, 0, 0)),
            pl.BlockSpec((1, D_MODEL), lambda i: (0, 0)),
        ],
        out_specs=pl.BlockSpec((tb, D_MODEL), lambda i: (i, 0)),
        out_shape=jax.ShapeDtypeStruct((n, D_MODEL), F32),
        scratch_shapes=[pltpu.VMEM((2, 2, tb, D_MODEL), F32), pltpu.SemaphoreType.DMA((2,))],
        compiler_params=_cparams(("arbitrary",)),
        name="combine",
    )(pos2, pos2, ys, rslab, x1, g2, final_g)


def _layer(x, c, w_ada, b_ada, norm1_g, w_in, conv_w, a_log, dt_bias, dn_onorm_g, fox_f_bias,
           w_out, norm2_g, w_rg, b_rg, w_re, b_re, w1, w3, w2, final_g, final):
    bsz, seq, d = x.shape
    n = bsz * seq
    x2 = x.reshape(n, d)

    mod = _adaln(c, w_ada, b_ada)
    sh1, sc1, g1, sh2, sc2, g2 = [m.reshape(bsz, 1, d) for m in jnp.split(mod, 6, axis=-1)]

    o_a = 4 * HEAD_W
    o_b = o_a + 2 * HEADS
    o_f = o_b + 3 * HEAD_W
    w_t = w_in.T
    w_main_t = jnp.concatenate([w_t[:o_a], w_t[o_b:o_f]], axis=0).astype(BF16)
    w_small_t = jnp.zeros((LANES, d), F32)
    w_small_t = w_small_t.at[0:2 * HEADS].set(w_t[o_a:o_b]).at[2 * HEADS:3 * HEADS].set(w_t[o_f:])
    ws_t = jnp.concatenate(_split_bf16(w_small_t), axis=0)

    proj, gates = _inproj(x2, sc1, sh1, norm1_g.reshape(1, d), w_main_t, ws_t, seq)
    proj3 = proj.reshape(bsz, seq, MAIN_W)

    def lane_row(vals, off):
        return jnp.zeros((1, LANES), F32).at[0, off:off + HEADS].set(vals)

    slab = _gates(gates.reshape(bsz, seq, LANES), lane_row(a_log, 0), lane_row(dt_bias, 0),
                  lane_row(fox_f_bias, L_F))
    nc = seq // CHUNK
    gct = slab[:, :, L_GC:L_GC + HEADS].reshape(bsz, nc, CHUNK, HEADS).transpose(0, 1, 3, 2)
    gct = gct.reshape(bsz, nc, HEADS, 1, CHUNK)
    f_rows = slab[:, :, L_F:L_F + HEADS].transpose(0, 2, 1).reshape(bsz, HEADS, seq // FOX_T, 1, FOX_T)

    o_dn = _deltanet(proj3, conv_w, slab, gct, dn_onorm_g.reshape(1, HEAD_DIM))
    o_fx = _fox(proj3, f_rows)

    wr = jnp.zeros((d, LANES), F32).at[:, :N_GROUPS].set(w_rg).at[:, N_GROUPS:N_GROUPS + N_EXPERTS].set(w_re)
    br = jnp.zeros((1, LANES), F32).at[0, :N_GROUPS].set(b_rg).at[0, N_GROUPS:N_GROUPS + N_EXPERTS].set(b_re)
    wr_hi_lo = jnp.concatenate(_split_bf16(wr), axis=1)
    w_out_b = w_out.astype(BF16)
    x1, h2, rslab = _outproj(o_dn.reshape(n, HEAD_W), o_fx.reshape(n, HEAD_W), w_out_b[:HEAD_W], w_out_b[HEAD_W:],
                             x2, g1, sc2, sh2, norm2_g.reshape(1, d), wr_hi_lo, br, seq)

    pos_slab, counts = _route(rslab)
    pos = pos_slab[:, 0:2]

    cnt = counts[0, :N_EXPERTS].astype(jnp.int32)
    tiles_per = (cnt + MOE_TM - 1) // MOE_TM
    tile_end = jnp.cumsum(tiles_per)
    base = (tile_end - tiles_per) * MOE_TM
    n_tiles = (2 * n) // MOE_TM + N_EXPERTS
    p_rows = n_tiles * MOE_TM
    tid = jnp.arange(n_tiles, dtype=jnp.int32)
    tile_valid = (tid < tile_end[-1]).astype(jnp.int32)
    te_raw = jnp.minimum(jnp.sum(tid[:, None] >= tile_end[None, :], axis=1), N_EXPERTS - 1).astype(jnp.int32)
    last_e = te_raw[jnp.maximum(tile_end[-1] - 1, 0)]
    tile_expert = jnp.where(tile_valid == 1, te_raw, last_e)
    tile_first = (jnp.concatenate([jnp.array([-1], jnp.int32), tile_expert[:-1]]) != tile_expert).astype(jnp.int32)
    pad_start = base + cnt
    pad_len = tiles_per * MOE_TM - cnt
    eid = jnp.arange(N_EXPERTS, dtype=jnp.int32)
    has = tiles_per > 0
    slot_e = (jnp.cumsum(has.astype(jnp.int32)) - 1) & 1
    later = jnp.where(has[None, :] & (eid[None, :] > eid[:, None]), eid[None, :], N_EXPERTS)
    next_e = jnp.min(later, axis=1)
    next_e = jnp.where(next_e < N_EXPERTS, next_e, -1).astype(jnp.int32)
    tile_slot = slot_e[tile_expert].astype(jnp.int32)
    tile_next = next_e[tile_expert]

    xs = _dispatch(pad_start, pad_len, tile_end[-1:], pos.reshape(n // DISP_TB, 1, 2 * DISP_TB), h2, p_rows)
    ys = _experts(tile_expert, tile_valid, tile_first, tile_slot, tile_next, xs, w1.reshape(N_EXPERTS, d, D_EXPERT),
                  w3.reshape(N_EXPERTS, d, D_EXPERT), w2.reshape(N_EXPERTS, D_EXPERT, d))
    out = _combine(pos.reshape(n // COMB_TB, 1, 2 * COMB_TB), ys, rslab, x1, g2, final_g.reshape(1, d), seq, final)
    return out.reshape(bsz, seq, d)


def kernel(x, c, w_ada, b_ada, norm1_g, w_in, conv_w, a_log, dt_bias, dn_onorm_g, fox_f_bias, w_out, norm2_g,
           w_router_group, b_router_group, w_router_expert, b_router_expert, w1, w3, w2, final_g):
    depth = w_ada.shape[0]
    for l in range(depth):
        x = _layer(x, c, w_ada[l], b_ada[l], norm1_g[l], w_in[l], conv_w[l], a_log[l], dt_bias[l], dn_onorm_g[l],
                   fox_f_bias[l], w_out[l], norm2_g[l], w_router_group[l], b_router_group[l], w_router_expert[l],
                   b_router_expert[l], w1[l], w3[l], w2[l], final_g, l == depth - 1)
    return x
```

```python
import functools

import jax
import jax.numpy as jnp
from jax import lax
from jax.experimental import pallas as pl
from jax.experimental.pallas import tpu as pltpu

F32 = jnp.float32
BF16 = jnp.bfloat16

D_MODEL = 2048
EPS = 1e-6
CHUNK = 64
HEADS = 8
HEAD_DIM = 128
HEAD_W = HEADS * HEAD_DIM
CONV_K = 4
N_GROUPS = 4
EXPERTS_PER_GROUP = 8
N_EXPERTS = N_GROUPS * EXPERTS_PER_GROUP
D_EXPERT = 512
LANES = 128
SUBLANES = 8
MAIN_W = 7 * HEAD_W
VMEM_LIMIT = 56 * 1024 * 1024

L_GC, L_BETA, L_F, L_EGC, L_EK, L_ELAST = 0, 8, 16, 24, 32, 40


def _cparams(sem):
    return pltpu.CompilerParams(dimension_semantics=sem, vmem_limit_bytes=VMEM_LIMIT)


def _split_bf16(a):
    hi = a.astype(BF16)
    lo = (a - hi.astype(F32)).astype(BF16)
    return hi, lo


def _dot(a, b):
    return jnp.dot(a, b, preferred_element_type=F32)


def _dot_nt(a, b):
    return lax.dot_general(a, b, (((1,), (1,)), ((), ())), preferred_element_type=F32)


def _dot3(a, b):
    ah, al = _split_bf16(a)
    bh, bl = _split_bf16(b)
    return _dot(ah, bh) + (_dot(al, bh) + _dot(ah, bl))


def _dot3_pre(a, b_hi_lo):
    ah, al = _split_bf16(a)
    n = b_hi_lo.shape[1] // 2
    both = _dot(ah, b_hi_lo)
    return both[:, :n] + (both[:, n:] + _dot(al, b_hi_lo[:, :n]))


def _softplus(x):
    return jnp.maximum(x, 0.0) + jnp.log1p(jnp.exp(-jnp.abs(x)))


def _silu(x):
    return x * jax.nn.sigmoid(x)


def _adaln_kernel(c_ref, w_ref, b_ref, o_ref):
    c = c_ref[...]
    o_ref[...] = _dot(_silu(c).astype(BF16), w_ref[...].astype(BF16)) + b_ref[...]


def _adaln(c, w, b):
    bsz = c.shape[0]
    n = w.shape[1]
    tn = 1024
    cp = jnp.zeros((SUBLANES, D_MODEL), F32).at[:bsz].set(c)
    out = pl.pallas_call(
        _adaln_kernel,
        grid=(n // tn,),
        in_specs=[
            pl.BlockSpec((SUBLANES, D_MODEL), lambda j: (0, 0)),
            pl.BlockSpec((D_MODEL, tn), lambda j: (0, j)),
            pl.BlockSpec((1, tn), lambda j: (0, j)),
        ],
        out_specs=pl.BlockSpec((SUBLANES, tn), lambda j: (0, j)),
        out_shape=jax.ShapeDtypeStruct((SUBLANES, n), F32),
        compiler_params=_cparams(("parallel",)),
        name="adaln",
    )(cp, w, b.reshape(1, n))
    return out[:bsz]


INPROJ_TM = 1024
INPROJ_TN = 512
ROW_STEP = 128


def _modulated_norm(x, g, sc, sh):
    r = lax.rsqrt(jnp.mean(x * x, axis=-1, keepdims=True) + EPS)
    return (x * r) * (g * (1.0 + sc)) + sh


def _inproj_kernel(x_ref, sc_ref, sh_ref, g_ref, w_ref, ws_ref, o_ref, og_ref, h_ref):
    @pl.when(pl.program_id(1) == 0)
    def _():
        def body(i, _):
            rows = pl.ds(pl.multiple_of(i * ROW_STEP, ROW_STEP), ROW_STEP)
            h = _modulated_norm(x_ref[rows, :], g_ref[...], sc_ref[...], sh_ref[...])
            hb = h.astype(BF16)
            h_ref[rows, :] = hb
            both = _dot_nt(hb, ws_ref[...])
            og_ref[rows, :] = both[:, :LANES] + both[:, LANES:]
            return 0

        lax.fori_loop(0, INPROJ_TM // ROW_STEP, body, 0)

    o_ref[...] = _dot_nt(h_ref[...], w_ref[...]).astype(BF16)


def _inproj(x2, sc1, sh1, g, w_main_t, ws_t, seq):
    n = x2.shape[0]
    tm, tn = INPROJ_TM, INPROJ_TN
    per_b = seq // tm
    return pl.pallas_call(
        _inproj_kernel,
        grid=(n // tm, MAIN_W // tn),
        in_specs=[
            pl.BlockSpec((tm, D_MODEL), lambda i, j: (i, 0)),
            pl.BlockSpec((None, 1, D_MODEL), lambda i, j: (i // per_b, 0, 0)),
            pl.BlockSpec((None, 1, D_MODEL), lambda i, j: (i // per_b, 0, 0)),
            pl.BlockSpec((1, D_MODEL), lambda i, j: (0, 0)),
            pl.BlockSpec((tn, D_MODEL), lambda i, j: (j, 0)),
            pl.BlockSpec((2 * LANES, D_MODEL), lambda i, j: (0, 0)),
        ],
        out_specs=[
            pl.BlockSpec((tm, tn), lambda i, j: (i, j)),
            pl.BlockSpec((tm, LANES), lambda i, j: (i, 0)),
        ],
        out_shape=[
            jax.ShapeDtypeStruct((n, MAIN_W), BF16),
            jax.ShapeDtypeStruct((n, LANES), F32),
        ],
        scratch_shapes=[pltpu.VMEM((tm, D_MODEL), BF16)],
        compiler_params=_cparams(("parallel", "arbitrary")),
        name="inproj",
    )(x2, sc1, sh1, g, w_main_t, ws_t)


GATES_TB = 256


def _split3(a):
    p0 = a.astype(BF16)
    r1 = a - p0.astype(F32)
    p1 = r1.astype(BF16)
    p2 = (r1 - p1.astype(F32)).astype(BF16)
    return p0, p1, p2


def _dot_ones(m, a):
    p0, p1, p2 = _split3(a)
    return _dot(m, p0) + (_dot(m, p1) + _dot(m, p2))


def _gates_kernel(x_ref, alog_ref, dt_ref, fb_ref, o_ref, carry_ref):
    tb = GATES_TB

    @pl.when(pl.program_id(1) == 0)
    def _():
        carry_ref[...] = jnp.zeros_like(carry_ref)

    x = x_ref[...]
    lane = lax.broadcasted_iota(jnp.int32, (tb, LANES), 1)
    g = -jnp.exp(alog_ref[...]) * _softplus(x + dt_ref[...])
    beta = jax.nn.sigmoid(x)
    lf = -_softplus(-(x + fb_ref[...]))

    ri = lax.broadcasted_iota(jnp.int32, (tb, tb), 0)
    ci = lax.broadcasted_iota(jnp.int32, (tb, tb), 1)
    same_chunk = (ri // CHUNK) == (ci // CHUNK)
    tri = (ri >= ci)
    m_all = jnp.where(tri, 1.0, 0.0).astype(BF16)
    m_chunk = jnp.where(tri & same_chunk, 1.0, 0.0).astype(BF16)
    m_tot = jnp.where(same_chunk, 1.0, 0.0).astype(BF16)

    gc = _dot_ones(m_chunk, g)
    glast = _dot_ones(m_tot, g)
    fcum = _dot_ones(m_all, lf) + carry_ref[...]
    carry_ref[...] = fcum[tb - 1:tb, :]

    in_a = lane < HEADS
    egc = jnp.where(in_a, jnp.exp(gc), 0.0)
    ek = jnp.where(in_a, jnp.exp(glast - gc), 0.0)
    elast = jnp.where(in_a, jnp.exp(glast), 0.0)
    out = jnp.where(in_a, gc, jnp.where(lane < 2 * HEADS, beta, jnp.where(lane < 3 * HEADS, fcum, 0.0)))
    out = out + pltpu.roll(egc, L_EGC, 1) + pltpu.roll(ek, L_EK, 1) + pltpu.roll(elast, L_ELAST, 1)
    o_ref[...] = out


def _gates(gates, alog_row, dt_row, fb_row):
    bsz, seq, _ = gates.shape
    tb = GATES_TB
    row = pl.BlockSpec((1, LANES), lambda b, t: (0, 0))
    return pl.pallas_call(
        _gates_kernel,
        grid=(bsz, seq // tb),
        in_specs=[pl.BlockSpec((None, tb, LANES), lambda b, t: (b, t, 0)), row, row, row],
        out_specs=pl.BlockSpec((None, tb, LANES), lambda b, t: (b, t, 0)),
        out_shape=jax.ShapeDtypeStruct((bsz, seq, LANES), F32),
        scratch_shapes=[pltpu.VMEM((1, LANES), F32)],
        compiler_params=_cparams(("parallel", "arbitrary")),
        name="gates",
    )(gates, alog_row, dt_row, fb_row)


DN_TB = 256
DN_GROUP = 4
HALO = SUBLANES


def _bdot(a, b):
    return lax.dot_general(a, b, (((2,), (1,)), ((0,), (0,))), preferred_element_type=F32)


def _bdot_nt(a, b):
    return lax.dot_general(a, b, (((2,), (2,)), ((0,), (0,))), preferred_element_type=F32)


def _inv_unit_lower(a, eye, blk16, blk32):
    n = jnp.where(blk16, -a, 0.0)
    e1 = jnp.where(blk32 & jnp.logical_not(blk16), a, 0.0).astype(BF16)
    e2 = jnp.where(blk32, 0.0, a).astype(BF16)
    t = eye + n
    p = n.astype(BF16)
    for _ in range(3):
        p = _bdot(p, p).astype(BF16)
        t = t + _bdot(t.astype(BF16), p)
    for e in (e1, e2):
        tb = t.astype(BF16)
        t = t - _bdot(_bdot(tb, e).astype(BF16), tb)
    return t


def _deltanet_kernel(q_ref, k_ref, v_ref, z_ref, wq_ref, wk_ref, wv_ref, slab_ref, gct_ref, og_ref,
                     o_ref, ext_ref, qn_ref, kn_ref, vv_ref, s_ref):
    tb = DN_TB

    @pl.when(pl.program_id(1) == 0)
    def _():
        ext_ref[:, 0:HALO, :] = jnp.zeros((3, HALO, HEAD_W), F32)
        s_ref[...] = jnp.zeros_like(s_ref)

    for idx, (u_ref, w_ref) in enumerate(((q_ref, wq_ref), (k_ref, wk_ref), (v_ref, wv_ref))):
        for h in range(HEADS):
            cols = slice(h * HEAD_DIM, (h + 1) * HEAD_DIM)
            ext_ref[idx, HALO:HALO + tb, cols] = u_ref[:, cols].astype(F32)
            y = None
            for j in range(CONV_K):
                start = HALO - (CONV_K - 1) + j
                term = ext_ref[idx, start:start + tb, cols] * w_ref[j:j + 1, cols]
                y = term if y is None else y + term
            y = _silu(y)
            if idx == 2:
                vv_ref[h] = y
            else:
                yn = y * lax.rsqrt(jnp.sum(y * y, axis=-1, keepdims=True) + EPS)
                if idx == 0:
                    qn_ref[h] = (yn * (HEAD_DIM ** -0.5)).astype(BF16)
                else:
                    kn_ref[h] = yn.astype(BF16)
        ext_ref[idx, 0:HALO, :] = ext_ref[idx, tb:tb + HALO, :]

    ri = lax.broadcasted_iota(jnp.int32, (DN_GROUP * HEADS, CHUNK, CHUNK), 1)
    ci = lax.broadcasted_iota(jnp.int32, (DN_GROUP * HEADS, CHUNK, CHUNK), 2)
    incl = ri >= ci
    strict = ri > ci
    eye = jnp.where(ri == ci, 1.0, 0.0)
    blk16 = (ri // 16) == (ci // 16)
    blk32 = (ri // 32) == (ci // 32)

    def group_body(c, _):
        rows = [pl.ds(pl.multiple_of((c * DN_GROUP + g) * CHUNK, CHUNK), CHUNK) for g in range(DN_GROUP)]
        slabs = [slab_ref[r, :] for r in rows]

        def col(off, width):
            return jnp.stack([jnp.broadcast_to(sl[:, off + h:off + h + 1], (CHUNK, width))
                              for sl in slabs for h in range(HEADS)])

        def grouped(ref):
            return jnp.concatenate([ref[:, r, :] for r in rows], axis=0)

        q = grouped(qn_ref)
        k = grouped(kn_ref)
        v = grouped(vv_ref)
        beta = col(L_BETA, HEAD_DIM)
        egc = col(L_EGC, HEAD_DIM)
        gc_row = jnp.concatenate([gct_ref[c * DN_GROUP + g] for g in range(DN_GROUP)], axis=0)

        decay = jnp.where(incl, jnp.exp(col(L_GC, CHUNK) - gc_row), 0.0)
        kk = _bdot_nt(k, k)
        qk = (_bdot_nt(q, k) * decay).astype(BF16)
        a = jnp.where(strict, kk * decay * beta[:, :, :CHUNK], 0.0)
        t = _inv_unit_lower(a, eye, blk16, blk32).astype(BF16)

        kf = k.astype(F32)
        vb = (v * beta).astype(BF16)
        kbg = (kf * (beta * egc)).astype(BF16)
        u = _bdot(t, vb)
        w = _bdot(t, kbg).astype(BF16)
        qd = (q.astype(F32) * egc).astype(BF16)
        kd = kf * col(L_EK, HEAD_DIM)
        kdt = jnp.stack([kd[n].T for n in range(DN_GROUP * HEADS)]).astype(BF16)

        for g in range(DN_GROUP):
            sel = slice(g * HEADS, (g + 1) * HEADS)
            s = s_ref[...]
            sb = s.astype(BF16)
            vnb = (u[sel] - _bdot(w[sel], sb)).astype(BF16)
            o = _bdot(qd[sel], sb) + _bdot(qk[sel], vnb)
            elast = jnp.stack([jnp.broadcast_to(slabs[g][CHUNK - 1:CHUNK, L_ELAST + h:L_ELAST + h + 1],
                                                (HEAD_DIM, HEAD_DIM)) for h in range(HEADS)])
            s_ref[...] = s * elast + _bdot(kdt[sel], vnb)

            r = lax.rsqrt(jnp.mean(o * o, axis=-1, keepdims=True) + EPS)
            on = (o * r) * og_ref[...]
            for h in range(HEADS):
                cols = slice(h * HEAD_DIM, (h + 1) * HEAD_DIM)
                o_ref[rows[g], cols] = (on[h] * _silu(z_ref[rows[g], cols].astype(F32))).astype(BF16)
        return 0

    lax.fori_loop(0, tb // (CHUNK * DN_GROUP), group_body, 0)


def _deltanet(proj3, conv_w, slab, gct, onorm_g):
    bsz, seq, _ = proj3.shape
    tb = DN_TB
    nct = tb // CHUNK

    def colblk(j):
        return pl.BlockSpec((None, tb, HEAD_W), lambda b, t: (b, t, j))

    def wblk(j):
        return pl.BlockSpec((CONV_K, HEAD_W), lambda b, t: (0, j))

    return pl.pallas_call(
        _deltanet_kernel,
        grid=(bsz, seq // tb),
        in_specs=[
            colblk(0), colblk(1), colblk(2), colblk(3),
            wblk(0), wblk(1), wblk(2),
            pl.BlockSpec((None, tb, LANES), lambda b, t: (b, t, 0)),
            pl.BlockSpec((None, nct, HEADS, 1, CHUNK), lambda b, t: (b, t, 0, 0, 0)),
            pl.BlockSpec((1, HEAD_DIM), lambda b, t: (0, 0)),
        ],
        out_specs=pl.BlockSpec((None, tb, HEAD_W), lambda b, t: (b, t, 0)),
        out_shape=jax.ShapeDtypeStruct((bsz, seq, HEAD_W), BF16),
        scratch_shapes=[
            pltpu.VMEM((3, tb + HALO, HEAD_W), F32),
            pltpu.VMEM((HEADS, tb, HEAD_DIM), BF16),
            pltpu.VMEM((HEADS, tb, HEAD_DIM), BF16),
            pltpu.VMEM((HEADS, tb, HEAD_DIM), F32),
            pltpu.VMEM((HEADS, HEAD_DIM, HEAD_DIM), F32),
        ],
        compiler_params=_cparams(("parallel", "arbitrary")),
        name="deltanet",
    )(proj3, proj3, proj3, proj3, conv_w, conv_w, conv_w, slab, gct, onorm_g)


FOX_T = 512
FOX_STRIP = 32
NEG_INF = float("-inf")
LOG2E = 1.4426950408889634


def _fox_kernel(q_ref, k_ref, v_ref, f_ref, o_ref, s_a_ref, s_b_ref, p_a_ref, p_b_ref, m_ref, l_ref, alpha_ref,
                acc_ref):
    tq = FOX_T
    i = pl.program_id(2)
    lane_blocks = tq // LANES
    qs = (q_ref[...].astype(F32) * (HEAD_DIM ** -0.5 * LOG2E)).astype(BF16)
    f0 = f_ref[i][:, 0:1]
    m_ref[...] = jnp.full_like(m_ref, NEG_INF)
    l_ref[...] = jnp.zeros_like(l_ref)
    acc_ref[...] = jnp.zeros_like(acc_ref)

    def keys(j):
        return pl.ds(pl.multiple_of(j * tq, tq), tq)

    def scores(j):
        return _dot_nt(qs, k_ref[keys(j), :])

    def softmax(j, s_view, p_view, masked):
        bias = (f0 - f_ref[j]) * LOG2E

        def logits(r):
            rows = slice(r * FOX_STRIP, (r + 1) * FOX_STRIP)
            s = s_view[rows, :] + bias
            if masked:
                ri = r * FOX_STRIP + lax.broadcasted_iota(jnp.int32, (FOX_STRIP, tq), 0)
                ci = lax.broadcasted_iota(jnp.int32, (FOX_STRIP, tq), 1)
                s = jnp.where(ci <= ri, s, NEG_INF)
            return rows, s

        for r in range(tq // FOX_STRIP):
            rows, s = logits(r)
            m_old = m_ref[rows, :]
            m_new = jnp.maximum(m_old, jnp.max(s, axis=-1, keepdims=True))
            alpha_ref[rows, :] = jnp.exp2(m_old - m_new)
            m_ref[rows, :] = m_new
        for r in range(tq // FOX_STRIP):
            rows, s = logits(r)
            p = jnp.exp2(s - jnp.concatenate([m_ref[rows, :]] * lane_blocks, axis=1))
            alpha = alpha_ref[rows, :]
            l_ref[rows, :] = alpha * l_ref[rows, :] + jnp.sum(p, axis=-1, keepdims=True)
            p_view[rows, :] = p.astype(BF16)

    s_a_ref[...] = scores(i)
    s_b_ref[...] = scores(jnp.maximum(i - 1, 0))
    softmax(i, s_a_ref, p_a_ref, True)

    def step(t, s_cur, p_cur, s_nxt, p_prv):
        j = i - t
        s_nxt[...] = scores(jnp.maximum(j - 1, 0))
        pv = _dot(p_prv[...], v_ref[keys(j + 1), :])
        softmax(j, s_cur, p_cur, False)
        acc_ref[...] = alpha_ref[...] * (acc_ref[...] + pv)

    def pair(u, _):
        step(2 * u + 1, s_b_ref, p_b_ref, s_a_ref, p_a_ref)
        step(2 * u + 2, s_a_ref, p_a_ref, s_b_ref, p_b_ref)
        return 0

    lax.fori_loop(0, lax.shift_right_logical(i, 1), pair, 0)

    def finish(p_last):
        acc = acc_ref[...] + _dot(p_last[...], v_ref[keys(0), :])
        o_ref[...] = (acc / l_ref[...]).astype(BF16)

    @pl.when((i & 1) == 1)
    def _():
        step(i, s_b_ref, p_b_ref, s_a_ref, p_a_ref)
        finish(p_b_ref)

    @pl.when((i & 1) == 0)
    def _():
        finish(p_a_ref)


def _fox(proj3, f_rows):
    bsz, seq, _ = proj3.shape
    tq = FOX_T
    qb, kb, vb = 4 * HEADS, 5 * HEADS, 6 * HEADS
    return pl.pallas_call(
        _fox_kernel,
        grid=(bsz, HEADS, seq // tq),
        in_specs=[
            pl.BlockSpec((None, tq, HEAD_DIM), lambda b, h, i: (b, i, qb + h)),
            pl.BlockSpec((None, seq, HEAD_DIM), lambda b, h, i: (b, 0, kb + h)),
            pl.BlockSpec((None, seq, HEAD_DIM), lambda b, h, i: (b, 0, vb + h)),
            pl.BlockSpec((None, None, seq // tq, 1, tq), lambda b, h, i: (b, h, 0, 0, 0)),
        ],
        out_specs=pl.BlockSpec((None, tq, HEAD_DIM), lambda b, h, i: (b, i, h)),
        out_shape=jax.ShapeDtypeStruct((bsz, seq, HEAD_W), BF16),
        scratch_shapes=[
            pltpu.VMEM((tq, tq), F32),
            pltpu.VMEM((tq, tq), F32),
            pltpu.VMEM((tq, tq), BF16),
            pltpu.VMEM((tq, tq), BF16),
            pltpu.VMEM((tq, LANES), F32),
            pltpu.VMEM((tq, LANES), F32),
            pltpu.VMEM((tq, LANES), F32),
            pltpu.VMEM((tq, HEAD_DIM), F32),
        ],
        compiler_params=_cparams(("parallel", "parallel", "arbitrary")),
        name="fox",
    )(proj3, proj3, proj3, f_rows)


OUT_TM = 512
R_E0, R_E1, R_W0, R_W1 = 0, 1, 2, 3


def _first_argmax(vals, lane):
    m = jnp.max(vals, axis=-1, keepdims=True)
    idx = jnp.min(jnp.where(vals == m, lane, LANES), axis=-1, keepdims=True)
    return m, idx


def _outproj_kernel(oa_ref, ob_ref, wa_ref, wb_ref, x_ref, g1_ref, sc_ref, sh_ref, g_ref,
                    wr_ref, br_ref, x1_ref, h2_ref, r_ref):
    mix = _dot(oa_ref[...], wa_ref[...]) + _dot(ob_ref[...], wb_ref[...])
    x1 = x_ref[...] + g1_ref[...] * mix
    x1_ref[...] = x1
    h2 = _modulated_norm(x1, g_ref[...], sc_ref[...], sh_ref[...])
    h2_ref[...] = h2

    logits = _dot3_pre(h2, wr_ref[...]) + br_ref[...]
    tm = logits.shape[0]
    lane = lax.broadcasted_iota(jnp.int32, (tm, LANES), 1)
    gl = jnp.where(lane < N_GROUPS, logits, NEG_INF)
    gmax, gidx = _first_argmax(gl, lane)
    pg = 1.0 / jnp.sum(jnp.exp(gl - gmax), axis=-1, keepdims=True)
    e_lane = lane - N_GROUPS
    in_grp = (e_lane >= gidx * EXPERTS_PER_GROUP) & (e_lane < (gidx + 1) * EXPERTS_PER_GROUP)
    el = jnp.where(in_grp, logits, NEG_INF)
    v0, i0 = _first_argmax(el, lane)
    v1, i1 = _first_argmax(jnp.where(lane == i0, NEG_INF, el), lane)
    ex = jnp.exp(v1 - v0)
    w0 = pg / (1.0 + ex)
    w1 = pg * ex / (1.0 + ex)
    e0 = (i0 - N_GROUPS).astype(F32)
    e1 = (i1 - N_GROUPS).astype(F32)
    r_ref[...] = jnp.where(lane == R_E0, e0, jnp.where(lane == R_E1, e1,
                           jnp.where(lane == R_W0, w0, jnp.where(lane == R_W1, w1, 0.0))))


def _outproj(o_a, o_b, wa, wb, x2, g1, sc2, sh2, g, wr_hi_lo, br, seq):
    n = x2.shape[0]
    tm = OUT_TM
    per_b = seq // tm
    modrow = pl.BlockSpec((None, 1, D_MODEL), lambda i: (i // per_b, 0, 0))
    const = lambda shape: pl.BlockSpec(shape, lambda i: (0, 0))
    return pl.pallas_call(
        _outproj_kernel,
        grid=(n // tm,),
        in_specs=[
            pl.BlockSpec((tm, HEAD_W), lambda i: (i, 0)),
            pl.BlockSpec((tm, HEAD_W), lambda i: (i, 0)),
            const((HEAD_W, D_MODEL)), const((HEAD_W, D_MODEL)),
            pl.BlockSpec((tm, D_MODEL), lambda i: (i, 0)),
            modrow, modrow, modrow,
            const((1, D_MODEL)),
            const((D_MODEL, 2 * LANES)), const((1, LANES)),
        ],
        out_specs=[
            pl.BlockSpec((tm, D_MODEL), lambda i: (i, 0)),
            pl.BlockSpec((tm, D_MODEL), lambda i: (i, 0)),
            pl.BlockSpec((tm, LANES), lambda i: (i, 0)),
        ],
        out_shape=[
            jax.ShapeDtypeStruct((n, D_MODEL), F32),
            jax.ShapeDtypeStruct((n, D_MODEL), F32),
            jax.ShapeDtypeStruct((n, LANES), F32),
        ],
        compiler_params=_cparams(("parallel",)),
        name="outproj",
    )(o_a, o_b, wa, wb, x2, g1, sc2, sh2, g, wr_hi_lo, br)


MOE_TM = 256
ROUTE_TB = 512


def _route_kernel(r_ref, pos_ref, cnt_ref, run_ref, base_ref):
    tb = ROUTE_TB
    phase = pl.program_id(0)
    t = pl.program_id(1)
    r = r_ref[...]
    lane = lax.broadcasted_iota(jnp.int32, (tb, LANES), 1)
    e0 = r[:, R_E0:R_E0 + 1].astype(jnp.int32)
    e1 = r[:, R_E1:R_E1 + 1].astype(jnp.int32)
    oh0 = lane == e0
    oh1 = lane == e1
    both = jnp.where(oh0 | oh1, 1.0, 0.0)

    @pl.when((phase == 0) & (t == 0))
    def _():
        run_ref[...] = jnp.zeros_like(run_ref)

    @pl.when(phase == 0)
    def _():
        run_ref[...] = run_ref[...] + jnp.sum(both, axis=0, keepdims=True)

    @pl.when((phase == 1) & (t == 0))
    def _():
        counts = run_ref[...]
        cnt_ref[...] = counts
        padded = jnp.ceil(counts / MOE_TM) * MOE_TM
        li = lax.broadcasted_iota(jnp.int32, (LANES, LANES), 0)
        lj = lax.broadcasted_iota(jnp.int32, (LANES, LANES), 1)
        upper = jnp.where(li < lj, 1.0, 0.0).astype(BF16)
        hi = jnp.floor(padded / 256.0)
        lo = padded - hi * 256.0
        hi8 = jnp.broadcast_to(hi, (SUBLANES, LANES)).astype(BF16)
        lo8 = jnp.broadcast_to(lo, (SUBLANES, LANES)).astype(BF16)
        base = _dot(hi8, upper) * 256.0 + _dot(lo8, upper)
        base_ref[...] = base[0:1, :]
        run_ref[...] = jnp.zeros_like(run_ref)

    @pl.when(phase == 1)
    def _():
        ri = lax.broadcasted_iota(jnp.int32, (tb, tb), 0)
        ci = lax.broadcasted_iota(jnp.int32, (tb, tb), 1)
        strict = jnp.where(ri > ci, 1.0, 0.0).astype(BF16)
        before = _dot(strict, both.astype(BF16)) + run_ref[...] + base_ref[...]
        p0 = jnp.sum(jnp.where(oh0, before, 0.0), axis=-1, keepdims=True)
        p1 = jnp.sum(jnp.where(oh1, before, 0.0), axis=-1, keepdims=True)
        pos_ref[...] = jnp.where(lane == 0, p0, jnp.where(lane == 1, p1, 0.0)).astype(jnp.int32)
        run_ref[...] = run_ref[...] + jnp.sum(both, axis=0, keepdims=True)


def _route(rslab):
    n = rslab.shape[0]
    tb = ROUTE_TB
    return pl.pallas_call(
        _route_kernel,
        grid=(2, n // tb),
        in_specs=[pl.BlockSpec((tb, LANES), lambda p, t: (t, 0))],
        out_specs=[
            pl.BlockSpec((tb, LANES), lambda p, t: (p * t, 0)),
            pl.BlockSpec((1, LANES), lambda p, t: (0, 0)),
        ],
        out_shape=[
            jax.ShapeDtypeStruct((n, LANES), jnp.int32),
            jax.ShapeDtypeStruct((1, LANES), F32),
        ],
        scratch_shapes=[pltpu.VMEM((1, LANES), F32), pltpu.VMEM((1, LANES), F32)],
        compiler_params=_cparams(("arbitrary", "arbitrary")),
        name="route",
    )(rslab)


DISP_TB = 1024
DMA_UNROLL = 8


def _dispatch_kernel(pad_start_ref, pad_len_ref, used_ref, pos_ref, h2_ref, xs_ref, zero_ref, sem, zsem):
    t = pl.program_id(0)
    tb = DISP_TB
    n_tiles = xs_ref.shape[0] // MOE_TM

    def row_copy(i, k):
        return pltpu.make_async_copy(h2_ref.at[pl.ds(i, 1)], xs_ref.at[pl.ds(pos_ref[0, 2 * i + k], 1)], sem)

    def issue(i, _):
        row_copy(i, 0).start(priority=0)
        row_copy(i, 1).start(priority=1)
        return 0

    lax.fori_loop(0, tb, issue, 0, unroll=DMA_UNROLL)

    @pl.when(t == 0)
    def _():
        zero_ref[...] = jnp.zeros_like(zero_ref)

        def zero_rows(wait, off, rows):
            cp = pltpu.make_async_copy(zero_ref.at[pl.ds(0, rows)], xs_ref.at[pl.ds(off, rows)], zsem)
            cp.wait() if wait else cp.start()

        def per_expert(wait, e, _):
            start = pad_start_ref[e]
            head = (-start) & (SUBLANES - 1)
            for r in range(SUBLANES - 1):
                pl.when(r < head)(functools.partial(zero_rows, wait, start + r, 1))
            off = start + head
            rest = pad_len_ref[e] - head
            piece = MOE_TM // 2
            while piece >= SUBLANES:
                take = (rest & piece) != 0
                pl.when(take)(functools.partial(zero_rows, wait, pl.multiple_of(off, SUBLANES), piece))
                off = off + jnp.where(take, piece, 0)
                piece //= 2
            return 0

        def per_tile(wait, i, _):
            zero_rows(wait, pl.multiple_of(i * MOE_TM, MOE_TM), MOE_TM)
            return 0

        for wait in (False, True):
            lax.fori_loop(0, N_EXPERTS, functools.partial(per_expert, wait), 0)
            lax.fori_loop(used_ref[0], n_tiles, functools.partial(per_tile, wait), 0)

    def drain(i, _):
        row_copy(i, 0).wait()
        row_copy(i, 1).wait()
        return 0

    lax.fori_loop(0, tb, drain, 0, unroll=DMA_UNROLL)


def _dispatch(pad_start, pad_len, used, pos2, h2, p_rows):
    n = h2.shape[0]
    tb = DISP_TB
    return pl.pallas_call(
        _dispatch_kernel,
        grid_spec=pltpu.PrefetchScalarGridSpec(
            num_scalar_prefetch=3,
            grid=(n // tb,),
            in_specs=[
                pl.BlockSpec((None, 1, 2 * tb), lambda t, *_: (t, 0, 0), memory_space=pltpu.SMEM),
                pl.BlockSpec((tb, D_MODEL), lambda t, *_: (t, 0)),
            ],
            out_specs=pl.BlockSpec(memory_space=pl.ANY),
            scratch_shapes=[
                pltpu.VMEM((MOE_TM, D_MODEL), F32),
                pltpu.SemaphoreType.DMA(()),
                pltpu.SemaphoreType.DMA(()),
            ],
        ),
        out_shape=jax.ShapeDtypeStruct((p_rows, D_MODEL), F32),
        compiler_params=_cparams(("arbitrary",)),
        name="dispatch",
    )(pad_start, pad_len, used, pos2, h2)


def _experts_kernel(te_ref, tv_ref, tf_ref, ts_ref, tn_ref, xs_ref, w1_hbm, w3_hbm, w2_hbm, ys_ref,
                    w1f_ref, w3f_ref, w2f_ref, w1b_ref, w3b_ref, w2b_ref, sem):
    i = pl.program_id(0)

    def weight_copies(e, slot):
        return [pltpu.make_async_copy(src.at[e], dst.at[slot], sem.at[slot])
                for src, dst in ((w1_hbm, w1f_ref), (w3_hbm, w3f_ref), (w2_hbm, w2f_ref))]

    @pl.when(tv_ref[i] != 0)
    def _():
        @pl.when(tf_ref[i] != 0)
        def _():
            slot = ts_ref[i]

            @pl.when(i == 0)
            def _():
                for cp in weight_copies(te_ref[i], slot):
                    cp.start()

            for cp in weight_copies(te_ref[i], slot):
                cp.wait()

            @pl.when(tn_ref[i] >= 0)
            def _():
                for cp in weight_copies(tn_ref[i], 1 - slot):
                    cp.start()

            w1b_ref[...] = w1f_ref[slot].astype(BF16)
            w3b_ref[...] = w3f_ref[slot].astype(BF16)
            w2b_ref[...] = w2f_ref[slot].astype(BF16)

        x = xs_ref[...].astype(BF16)
        a = _dot(x, w1b_ref[...])
        b = _dot(x, w3b_ref[...])
        ys_ref[...] = _dot((_silu(a) * b).astype(BF16), w2b_ref[...])

    @pl.when(tv_ref[i] == 0)
    def _():
        ys_ref[...] = jnp.zeros_like(ys_ref)


def _experts(tile_expert, tile_valid, tile_first, tile_slot, tile_next, xs, w1, w3, w2):
    p_rows = xs.shape[0]
    tm = MOE_TM
    hbm = pl.BlockSpec(memory_space=pl.ANY)
    return pl.pallas_call(
        _experts_kernel,
        grid_spec=pltpu.PrefetchScalarGridSpec(
            num_scalar_prefetch=5,
            grid=(p_rows // tm,),
            in_specs=[pl.BlockSpec((tm, D_MODEL), lambda i, *_: (i, 0)), hbm, hbm, hbm],
            out_specs=pl.BlockSpec((tm, D_MODEL), lambda i, *_: (i, 0)),
            scratch_shapes=[
                pltpu.VMEM((2, D_MODEL, D_EXPERT), F32),
                pltpu.VMEM((2, D_MODEL, D_EXPERT), F32),
                pltpu.VMEM((2, D_EXPERT, D_MODEL), F32),
                pltpu.VMEM((D_MODEL, D_EXPERT), BF16),
                pltpu.VMEM((D_MODEL, D_EXPERT), BF16),
                pltpu.VMEM((D_EXPERT, D_MODEL), BF16),
                pltpu.SemaphoreType.DMA((2,)),
            ],
        ),
        out_shape=jax.ShapeDtypeStruct((p_rows, D_MODEL), F32),
        compiler_params=_cparams(("arbitrary",)),
        name="experts",
    )(tile_expert, tile_valid, tile_first, tile_slot, tile_next, xs, w1, w3, w2)


COMB_TB = 256


def _combine_kernel(pos_ref, pos_next_ref, ys_ref, r_ref, x1_ref, g2_ref, fg_ref, o_ref, buf_ref, sem, *, final):
    tb = COMB_TB
    t = pl.program_id(0)
    slot = t & 1

    def row_copy(p_ref, s, i, k):
        return pltpu.make_async_copy(ys_ref.at[pl.ds(p_ref[0, 2 * i + k], 1)], buf_ref.at[s, k, pl.ds(i, 1)],
                                     sem.at[s])

    def issue(p_ref, s):
        def body(i, _):
            row_copy(p_ref, s, i, 0).start(priority=0)
            row_copy(p_ref, s, i, 1).start(priority=1)
            return 0

        lax.fori_loop(0, tb, body, 0, unroll=DMA_UNROLL)

    def drain(i, _):
        row_copy(pos_ref, slot, i, 0).wait()
        row_copy(pos_ref, slot, i, 1).wait()
        return 0

    pl.when(t == 0)(functools.partial(issue, pos_ref, 0))
    pl.when(t + 1 < pl.num_programs(0))(functools.partial(issue, pos_next_ref, 1 - slot))
    lax.fori_loop(0, tb, drain, 0, unroll=DMA_UNROLL)
    r = r_ref[...]
    y = r[:, R_W0:R_W0 + 1] * buf_ref[slot, 0] + r[:, R_W1:R_W1 + 1] * buf_ref[slot, 1]
    x2 = x1_ref[...] + g2_ref[...] * y
    if final:
        x2 = (x2 * lax.rsqrt(jnp.mean(x2 * x2, axis=-1, keepdims=True) + EPS)) * fg_ref[...]
    o_ref[...] = x2


def _combine(pos2, ys, rslab, x1, g2, final_g, seq, final):
    n = x1.shape[0]
    tb = COMB_TB
    per_b = seq // tb
    return pl.pallas_call(
        functools.partial(_combine_kernel, final=final),
        grid=(n // tb,),
        in_specs=[
            pl.BlockSpec((None, 1, 2 * tb), lambda i: (i, 0, 0), memory_space=pltpu.SMEM),
            pl.BlockSpec((None, 1, 2 * tb), lambda i: (jnp.minimum(i + 1, n // tb - 1), 0, 0),
                         memory_space=pltpu.SMEM),
            pl.BlockSpec(memory_space=pl.ANY),
            pl.BlockSpec((tb, LANES), lambda i: (i, 0)),
            pl.BlockSpec((tb, D_MODEL), lambda i: (i, 0)),
            pl.BlockSpec((None, 1, D_MODEL), lambda i: (i // per_b, 0, 0)),
            pl.BlockSpec((1, D_MODEL), lambda i: (0, 0)),
        ],
        out_specs=pl.BlockSpec((tb, D_MODEL), lambda i: (i, 0)),
        out_shape=jax.ShapeDtypeStruct((n, D_MODEL), F32),
        scratch_shapes=[pltpu.VMEM((2, 2, tb, D_MODEL), F32), pltpu.SemaphoreType.DMA((2,))],
        compiler_params=_cparams(("arbitrary",)),
        name="combine",
    )(pos2, pos2, ys, rslab, x1, g2, final_g)


def _layer(x, c, w_ada, b_ada, norm1_g, w_in, conv_w, a_log, dt_bias, dn_onorm_g, fox_f_bias,
           w_out, norm2_g, w_rg, b_rg, w_re, b_re, w1, w3, w2, final_g, final):
    bsz, seq, d = x.shape
    n = bsz * seq
    x2 = x.reshape(n, d)

    mod = _adaln(c, w_ada, b_ada)
    sh1, sc1, g1, sh2, sc2, g2 = [m.reshape(bsz, 1, d) for m in jnp.split(mod, 6, axis=-1)]

    o_a = 4 * HEAD_W
    o_b = o_a + 2 * HEADS
    o_f = o_b + 3 * HEAD_W
    w_t = w_in.T
    w_main_t = jnp.concatenate([w_t[:o_a], w_t[o_b:o_f]], axis=0).astype(BF16)
    w_small_t = jnp.zeros((LANES, d), F32)
    w_small_t = w_small_t.at[0:2 * HEADS].set(w_t[o_a:o_b]).at[2 * HEADS:3 * HEADS].set(w_t[o_f:])
    ws_t = jnp.concatenate(_split_bf16(w_small_t), axis=0)

    proj, gates = _inproj(x2, sc1, sh1, norm1_g.reshape(1, d), w_main_t, ws_t, seq)
    proj3 = proj.reshape(bsz, seq, MAIN_W)

    def lane_row(vals, off):
        return jnp.zeros((1, LANES), F32).at[0, off:off + HEADS].set(vals)

    slab = _gates(gates.reshape(bsz, seq, LANES), lane_row(a_log, 0), lane_row(dt_bias, 0),
                  lane_row(fox_f_bias, L_F))
    nc = seq // CHUNK
    gct = slab[:, :, L_GC:L_GC + HEADS].reshape(bsz, nc, CHUNK, HEADS).transpose(0, 1, 3, 2)
    gct = gct.reshape(bsz, nc, HEADS, 1, CHUNK)
    f_rows = slab[:, :, L_F:L_F + HEADS].transpose(0, 2, 1).reshape(bsz, HEADS, seq // FOX_T, 1, FOX_T)

    o_dn = _deltanet(proj3, conv_w, slab, gct, dn_onorm_g.reshape(1, HEAD_DIM))
    o_fx = _fox(proj3, f_rows)

    wr = jnp.zeros((d, LANES), F32).at[:, :N_GROUPS].set(w_rg).at[:, N_GROUPS:N_GROUPS + N_EXPERTS].set(w_re)
    br = jnp.zeros((1, LANES), F32).at[0, :N_GROUPS].set(b_rg).at[0, N_GROUPS:N_GROUPS + N_EXPERTS].set(b_re)
    wr_hi_lo = jnp.concatenate(_split_bf16(wr), axis=1)
    w_out_b = w_out.astype(BF16)
    x1, h2, rslab = _outproj(o_dn.reshape(n, HEAD_W), o_fx.reshape(n, HEAD_W), w_out_b[:HEAD_W], w_out_b[HEAD_W:],
                             x2, g1, sc2, sh2, norm2_g.reshape(1, d), wr_hi_lo, br, seq)

    pos_slab, counts = _route(rslab)
    pos = pos_slab[:, 0:2]

    cnt = counts[0, :N_EXPERTS].astype(jnp.int32)
    tiles_per = (cnt + MOE_TM - 1) // MOE_TM
    tile_end = jnp.cumsum(tiles_per)
    base = (tile_end - tiles_per) * MOE_TM
    n_tiles = (2 * n) // MOE_TM + N_EXPERTS
    p_rows = n_tiles * MOE_TM
    tid = jnp.arange(n_tiles, dtype=jnp.int32)
    tile_valid = (tid < tile_end[-1]).astype(jnp.int32)
    te_raw = jnp.minimum(jnp.sum(tid[:, None] >= tile_end[None, :], axis=1), N_EXPERTS - 1).astype(jnp.int32)
    last_e = te_raw[jnp.maximum(tile_end[-1] - 1, 0)]
    tile_expert = jnp.where(tile_valid == 1, te_raw, last_e)
    tile_first = (jnp.concatenate([jnp.array([-1], jnp.int32), tile_expert[:-1]]) != tile_expert).astype(jnp.int32)
    pad_start = base + cnt
    pad_len = tiles_per * MOE_TM - cnt
    eid = jnp.arange(N_EXPERTS, dtype=jnp.int32)
    has = tiles_per > 0
    slot_e = (jnp.cumsum(has.astype(jnp.int32)) - 1) & 1
    later = jnp.where(has[None, :] & (eid[None, :] > eid[:, None]), eid[None, :], N_EXPERTS)
    next_e = jnp.min(later, axis=1)
    next_e = jnp.where(next_e < N_EXPERTS, next_e, -1).astype(jnp.int32)
    tile_slot = slot_e[tile_expert].astype(jnp.int32)
    tile_next = next_e[tile_expert]

    xs = _dispatch(pad_start, pad_len, tile_end[-1:], pos.reshape(n // DISP_TB, 1, 2 * DISP_TB), h2, p_rows)
    ys = _experts(tile_expert, tile_valid, tile_first, tile_slot, tile_next, xs, w1.reshape(N_EXPERTS, d, D_EXPERT),
                  w3.reshape(N_EXPERTS, d, D_EXPERT), w2.reshape(N_EXPERTS, D_EXPERT, d))
    out = _combine(pos.reshape(n // COMB_TB, 1, 2 * COMB_TB), ys, rslab, x1, g2, final_g.reshape(1, d), seq, final)
    return out.reshape(bsz, seq, d)


def kernel(x, c, w_ada, b_ada, norm1_g, w_in, conv_w, a_log, dt_bias, dn_onorm_g, fox_f_bias, w_out, norm2_g,
           w_router_group, b_router_group, w_router_expert, b_router_expert, w1, w3, w2, final_g):
    depth = w_ada.shape[0]
    for l in range(depth):
        x = _layer(x, c, w_ada[l], b_ada[l], norm1_g[l], w_in[l], conv_w[l], a_log[l], dt_bias[l], dn_onorm_g[l],
                   fox_f_bias[l], w_out[l], norm2_g[l], w_router_group[l], b_router_group[l], w_router_expert[l],
                   b_router_expert[l], w1[l], w3[l], w2[l], final_g, l == depth - 1)
    return x
```

```python
import functools

import jax
import jax.numpy as jnp
from jax import lax
from jax.experimental import pallas as pl
from jax.experimental.pallas import tpu as pltpu

F32 = jnp.float32
BF16 = jnp.bfloat16

D_MODEL = 2048
EPS = 1e-6
CHUNK = 64
HEADS = 8
HEAD_DIM = 128
HEAD_W = HEADS * HEAD_DIM
CONV_K = 4
N_GROUPS = 4
EXPERTS_PER_GROUP = 8
N_EXPERTS = N_GROUPS * EXPERTS_PER_GROUP
D_EXPERT = 512
LANES = 128
SUBLANES = 8
MAIN_W = 7 * HEAD_W
VMEM_LIMIT = 56 * 1024 * 1024

L_GC, L_BETA, L_F, L_EGC, L_EK, L_ELAST = 0, 8, 16, 24, 32, 40


def _cparams(sem):
    return pltpu.CompilerParams(dimension_semantics=sem, vmem_limit_bytes=VMEM_LIMIT)


def _split_bf16(a):
    hi = a.astype(BF16)
    lo = (a - hi.astype(F32)).astype(BF16)
    return hi, lo


def _dot(a, b):
    return jnp.dot(a, b, preferred_element_type=F32)


def _dot_nt(a, b):
    return lax.dot_general(a, b, (((1,), (1,)), ((), ())), preferred_element_type=F32)


def _dot3(a, b):
    ah, al = _split_bf16(a)
    bh, bl = _split_bf16(b)
    return _dot(ah, bh) + (_dot(al, bh) + _dot(ah, bl))


def _dot3_pre(a, b_hi_lo):
    ah, al = _split_bf16(a)
    n = b_hi_lo.shape[1] // 2
    both = _dot(ah, b_hi_lo)
    return both[:, :n] + (both[:, n:] + _dot(al, b_hi_lo[:, :n]))


def _softplus(x):
    return jnp.maximum(x, 0.0) + jnp.log1p(jnp.exp(-jnp.abs(x)))


def _silu(x):
    return x * jax.nn.sigmoid(x)


def _adaln_kernel(c_ref, w_ref, b_ref, o_ref):
    c = c_ref[...]
    o_ref[...] = _dot(_silu(c).astype(BF16), w_ref[...].astype(BF16)) + b_ref[...]


def _adaln(c, w, b):
    bsz = c.shape[0]
    n = w.shape[1]
    tn = 1024
    cp = jnp.zeros((SUBLANES, D_MODEL), F32).at[:bsz].set(c)
    out = pl.pallas_call(
        _adaln_kernel,
        grid=(n // tn,),
        in_specs=[
            pl.BlockSpec((SUBLANES, D_MODEL), lambda j: (0, 0)),
            pl.BlockSpec((D_MODEL, tn), lambda j: (0, j)),
            pl.BlockSpec((1, tn), lambda j: (0, j)),
        ],
        out_specs=pl.BlockSpec((SUBLANES, tn), lambda j: (0, j)),
        out_shape=jax.ShapeDtypeStruct((SUBLANES, n), F32),
        compiler_params=_cparams(("parallel",)),
        name="adaln",
    )(cp, w, b.reshape(1, n))
    return out[:bsz]


INPROJ_TM = 1024
INPROJ_TN = 512
ROW_STEP = 128


def _modulated_norm(x, g, sc, sh):
    r = lax.rsqrt(jnp.mean(x * x, axis=-1, keepdims=True) + EPS)
    return (x * r) * (g * (1.0 + sc)) + sh


def _inproj_kernel(x_ref, sc_ref, sh_ref, g_ref, w_ref, ws_ref, o_ref, og_ref, h_ref):
    @pl.when(pl.program_id(1) == 0)
    def _():
        def body(i, _):
            rows = pl.ds(pl.multiple_of(i * ROW_STEP, ROW_STEP), ROW_STEP)
            h = _modulated_norm(x_ref[rows, :], g_ref[...], sc_ref[...], sh_ref[...])
            hb = h.astype(BF16)
            h_ref[rows, :] = hb
            both = _dot_nt(hb, ws_ref[...])
            og_ref[rows, :] = both[:, :LANES] + both[:, LANES:]
            return 0

        lax.fori_loop(0, INPROJ_TM // ROW_STEP, body, 0)

    o_ref[...] = _dot_nt(h_ref[...], w_ref[...].astype(BF16)).astype(BF16)


def _inproj(x2, sc1, sh1, g, w_t, ws_t, seq):
    n = x2.shape[0]
    tm, tn = INPROJ_TM, INPROJ_TN
    per_b = seq // tm
    first_part = 4 * HEAD_W // tn
    skip = 2 * HEADS

    def w_rows(i, j):
        return (SUBLANES * (j * (tn // SUBLANES) + jnp.where(j >= first_part, skip // SUBLANES, 0)), 0)

    return pl.pallas_call(
        _inproj_kernel,
        grid=(n // tm, MAIN_W // tn),
        in_specs=[
            pl.BlockSpec((tm, D_MODEL), lambda i, j: (i, 0)),
            pl.BlockSpec((None, 1, D_MODEL), lambda i, j: (i // per_b, 0, 0)),
            pl.BlockSpec((None, 1, D_MODEL), lambda i, j: (i // per_b, 0, 0)),
            pl.BlockSpec((1, D_MODEL), lambda i, j: (0, 0)),
            pl.BlockSpec((pl.Element(tn), pl.Element(D_MODEL)), w_rows),
            pl.BlockSpec((2 * LANES, D_MODEL), lambda i, j: (0, 0)),
        ],
        out_specs=[
            pl.BlockSpec((tm, tn), lambda i, j: (i, j)),
            pl.BlockSpec((tm, LANES), lambda i, j: (i, 0)),
        ],
        out_shape=[
            jax.ShapeDtypeStruct((n, MAIN_W), BF16),
            jax.ShapeDtypeStruct((n, LANES), F32),
        ],
        scratch_shapes=[pltpu.VMEM((tm, D_MODEL), BF16)],
        compiler_params=_cparams(("parallel", "arbitrary")),
        name="inproj",
    )(x2, sc1, sh1, g, w_t, ws_t)


GATES_TB = 256


def _split3(a):
    p0 = a.astype(BF16)
    r1 = a - p0.astype(F32)
    p1 = r1.astype(BF16)
    p2 = (r1 - p1.astype(F32)).astype(BF16)
    return p0, p1, p2


def _dot_ones(m, a):
    p0, p1, p2 = _split3(a)
    return _dot(m, p0) + (_dot(m, p1) + _dot(m, p2))


def _gates_kernel(x_ref, alog_ref, dt_ref, fb_ref, o_ref, carry_ref):
    tb = GATES_TB

    @pl.when(pl.program_id(1) == 0)
    def _():
        carry_ref[...] = jnp.zeros_like(carry_ref)

    x = x_ref[...]
    lane = lax.broadcasted_iota(jnp.int32, (tb, LANES), 1)
    g = -jnp.exp(alog_ref[...]) * _softplus(x + dt_ref[...])
    beta = jax.nn.sigmoid(x)
    lf = -_softplus(-(x + fb_ref[...]))

    ri = lax.broadcasted_iota(jnp.int32, (tb, tb), 0)
    ci = lax.broadcasted_iota(jnp.int32, (tb, tb), 1)
    same_chunk = (ri // CHUNK) == (ci // CHUNK)
    tri = (ri >= ci)
    m_all = jnp.where(tri, 1.0, 0.0).astype(BF16)
    m_chunk = jnp.where(tri & same_chunk, 1.0, 0.0).astype(BF16)
    m_tot = jnp.where(same_chunk, 1.0, 0.0).astype(BF16)

    gc = _dot_ones(m_chunk, g)
    glast = _dot_ones(m_tot, g)
    fcum = _dot_ones(m_all, lf) + carry_ref[...]
    carry_ref[...] = fcum[tb - 1:tb, :]

    in_a = lane < HEADS
    egc = jnp.where(in_a, jnp.exp(gc), 0.0)
    ek = jnp.where(in_a, jnp.exp(glast - gc), 0.0)
    elast = jnp.where(in_a, jnp.exp(glast), 0.0)
    out = jnp.where(in_a, gc, jnp.where(lane < 2 * HEADS, beta, jnp.where(lane < 3 * HEADS, fcum, 0.0)))
    out = out + pltpu.roll(egc, L_EGC, 1) + pltpu.roll(ek, L_EK, 1) + pltpu.roll(elast, L_ELAST, 1)
    o_ref[...] = out


def _gates(gates, alog_row, dt_row, fb_row):
    bsz, seq, _ = gates.shape
    tb = GATES_TB
    row = pl.BlockSpec((1, LANES), lambda b, t: (0, 0))
    return pl.pallas_call(
        _gates_kernel,
        grid=(bsz, seq // tb),
        in_specs=[pl.BlockSpec((None, tb, LANES), lambda b, t: (b, t, 0)), row, row, row],
        out_specs=pl.BlockSpec((None, tb, LANES), lambda b, t: (b, t, 0)),
        out_shape=jax.ShapeDtypeStruct((bsz, seq, LANES), F32),
        scratch_shapes=[pltpu.VMEM((1, LANES), F32)],
        compiler_params=_cparams(("parallel", "arbitrary")),
        name="gates",
    )(gates, alog_row, dt_row, fb_row)


DN_TB = 256
DN_GROUP = 4
HALO = SUBLANES


def _bdot(a, b):
    return lax.dot_general(a, b, (((2,), (1,)), ((0,), (0,))), preferred_element_type=F32)


def _bdot_nt(a, b):
    return lax.dot_general(a, b, (((2,), (2,)), ((0,), (0,))), preferred_element_type=F32)


def _inv_unit_lower(a, eye, blk16, blk32):
    n = jnp.where(blk16, -a, 0.0)
    e1 = jnp.where(blk32 & jnp.logical_not(blk16), a, 0.0).astype(BF16)
    e2 = jnp.where(blk32, 0.0, a).astype(BF16)
    t = eye + n
    p = n.astype(BF16)
    for _ in range(3):
        p = _bdot(p, p).astype(BF16)
        t = t + _bdot(t.astype(BF16), p)
    for e in (e1, e2):
        tb = t.astype(BF16)
        t = t - _bdot(_bdot(tb, e).astype(BF16), tb)
    return t


def _deltanet_kernel(q_ref, k_ref, v_ref, z_ref, wq_ref, wk_ref, wv_ref, slab_ref, gct_ref, og_ref,
                     o_ref, ext_ref, qn_ref, kn_ref, vv_ref, s_ref):
    tb = DN_TB

    @pl.when(pl.program_id(1) == 0)
    def _():
        ext_ref[:, 0:HALO, :] = jnp.zeros((3, HALO, HEAD_W), F32)
        s_ref[...] = jnp.zeros_like(s_ref)

    for idx, (u_ref, w_ref) in enumerate(((q_ref, wq_ref), (k_ref, wk_ref), (v_ref, wv_ref))):
        for h in range(HEADS):
            cols = slice(h * HEAD_DIM, (h + 1) * HEAD_DIM)
            ext_ref[idx, HALO:HALO + tb, cols] = u_ref[:, cols].astype(F32)
            y = None
            for j in range(CONV_K):
                start = HALO - (CONV_K - 1) + j
                term = ext_ref[idx, start:start + tb, cols] * w_ref[j:j + 1, cols]
                y = term if y is None else y + term
            y = _silu(y)
            if idx == 2:
                vv_ref[h] = y
            else:
                yn = y * lax.rsqrt(jnp.sum(y * y, axis=-1, keepdims=True) + EPS)
                if idx == 0:
                    qn_ref[h] = (yn * (HEAD_DIM ** -0.5)).astype(BF16)
                else:
                    kn_ref[h] = yn.astype(BF16)
        ext_ref[idx, 0:HALO, :] = ext_ref[idx, tb:tb + HALO, :]

    ri = lax.broadcasted_iota(jnp.int32, (DN_GROUP * HEADS, CHUNK, CHUNK), 1)
    ci = lax.broadcasted_iota(jnp.int32, (DN_GROUP * HEADS, CHUNK, CHUNK), 2)
    incl = ri >= ci
    strict = ri > ci
    eye = jnp.where(ri == ci, 1.0, 0.0)
    blk16 = (ri // 16) == (ci // 16)
    blk32 = (ri // 32) == (ci // 32)

    def group_body(c, _):
        rows = [pl.ds(pl.multiple_of((c * DN_GROUP + g) * CHUNK, CHUNK), CHUNK) for g in range(DN_GROUP)]
        slabs = [slab_ref[r, :] for r in rows]

        def col(off, width):
            return jnp.stack([jnp.broadcast_to(sl[:, off + h:off + h + 1], (CHUNK, width))
                              for sl in slabs for h in range(HEADS)])

        def grouped(ref):
            return jnp.concatenate([ref[:, r, :] for r in rows], axis=0)

        q = grouped(qn_ref)
        k = grouped(kn_ref)
        v = grouped(vv_ref)
        beta = col(L_BETA, HEAD_DIM)
        egc = col(L_EGC, HEAD_DIM)
        gc_row = jnp.concatenate([gct_ref[c * DN_GROUP + g] for g in range(DN_GROUP)], axis=0)

        decay = jnp.where(incl, jnp.exp(col(L_GC, CHUNK) - gc_row), 0.0)
        kk = _bdot_nt(k, k)
        qk = (_bdot_nt(q, k) * decay).astype(BF16)
        a = jnp.where(strict, kk * decay * beta[:, :, :CHUNK], 0.0)
        t = _inv_unit_lower(a, eye, blk16, blk32).astype(BF16)

        kf = k.astype(F32)
        vb = (v * beta).astype(BF16)
        kbg = (kf * (beta * egc)).astype(BF16)
        u = _bdot(t, vb)
        w = _bdot(t, kbg).astype(BF16)
        qd = (q.astype(F32) * egc).astype(BF16)
        kd = kf * col(L_EK, HEAD_DIM)
        kdt = jnp.stack([kd[n].T for n in range(DN_GROUP * HEADS)]).astype(BF16)

        for g in range(DN_GROUP):
            sel = slice(g * HEADS, (g + 1) * HEADS)
            s = s_ref[...]
            sb = s.astype(BF16)
            vnb = (u[sel] - _bdot(w[sel], sb)).astype(BF16)
            o = _bdot(qd[sel], sb) + _bdot(qk[sel], vnb)
            elast = jnp.stack([jnp.broadcast_to(slabs[g][CHUNK - 1:CHUNK, L_ELAST + h:L_ELAST + h + 1],
                                                (HEAD_DIM, HEAD_DIM)) for h in range(HEADS)])
            s_ref[...] = s * elast + _bdot(kdt[sel], vnb)

            r = lax.rsqrt(jnp.mean(o * o, axis=-1, keepdims=True) + EPS)
            on = (o * r) * og_ref[...]
            for h in range(HEADS):
                cols = slice(h * HEAD_DIM, (h + 1) * HEAD_DIM)
                o_ref[rows[g], cols] = (on[h] * _silu(z_ref[rows[g], cols].astype(F32))).astype(BF16)
        return 0

    lax.fori_loop(0, tb // (CHUNK * DN_GROUP), group_body, 0)


def _deltanet(proj3, conv_w, slab, gct, onorm_g):
    bsz, seq, _ = proj3.shape
    tb = DN_TB
    nct = tb // CHUNK

    def colblk(j):
        return pl.BlockSpec((None, tb, HEAD_W), lambda b, t: (b, t, j))

    def wblk(j):
        return pl.BlockSpec((CONV_K, HEAD_W), lambda b, t: (0, j))

    return pl.pallas_call(
        _deltanet_kernel,
        grid=(bsz, seq // tb),
        in_specs=[
            colblk(0), colblk(1), colblk(2), colblk(3),
            wblk(0), wblk(1), wblk(2),
            pl.BlockSpec((None, tb, LANES), lambda b, t: (b, t, 0)),
            pl.BlockSpec((None, nct, HEADS, 1, CHUNK), lambda b, t: (b, t, 0, 0, 0)),
            pl.BlockSpec((1, HEAD_DIM), lambda b, t: (0, 0)),
        ],
        out_specs=pl.BlockSpec((None, tb, HEAD_W), lambda b, t: (b, t, 0)),
        out_shape=jax.ShapeDtypeStruct((bsz, seq, HEAD_W), BF16),
        scratch_shapes=[
            pltpu.VMEM((3, tb + HALO, HEAD_W), F32),
            pltpu.VMEM((HEADS, tb, HEAD_DIM), BF16),
            pltpu.VMEM((HEADS, tb, HEAD_DIM), BF16),
            pltpu.VMEM((HEADS, tb, HEAD_DIM), F32),
            pltpu.VMEM((HEADS, HEAD_DIM, HEAD_DIM), F32),
        ],
        compiler_params=_cparams(("parallel", "arbitrary")),
        name="deltanet",
    )(proj3, proj3, proj3, proj3, conv_w, conv_w, conv_w, slab, gct, onorm_g)


FOX_T = 512
FOX_STRIP = 32
NEG_INF = float("-inf")
LOG2E = 1.4426950408889634


def _fox_kernel(q_ref, k_ref, v_ref, f_ref, o_ref, s_a_ref, s_b_ref, p_a_ref, p_b_ref, m_ref, l_ref, alpha_ref,
                acc_ref):
    tq = FOX_T
    i = pl.program_id(2)
    lane_blocks = tq // LANES
    qs = (q_ref[...].astype(F32) * (HEAD_DIM ** -0.5 * LOG2E)).astype(BF16)
    f0 = f_ref[i][:, 0:1]
    m_ref[...] = jnp.full_like(m_ref, NEG_INF)
    l_ref[...] = jnp.zeros_like(l_ref)
    acc_ref[...] = jnp.zeros_like(acc_ref)

    def keys(j):
        return pl.ds(pl.multiple_of(j * tq, tq), tq)

    def scores(j):
        return _dot_nt(qs, k_ref[keys(j), :])

    def softmax(j, s_view, p_view, masked):
        bias = (f0 - f_ref[j]) * LOG2E

        def logits(r):
            rows = slice(r * FOX_STRIP, (r + 1) * FOX_STRIP)
            s = s_view[rows, :] + bias
            if masked:
                ri = r * FOX_STRIP + lax.broadcasted_iota(jnp.int32, (FOX_STRIP, tq), 0)
                ci = lax.broadcasted_iota(jnp.int32, (FOX_STRIP, tq), 1)
                s = jnp.where(ci <= ri, s, NEG_INF)
            return rows, s

        for r in range(tq // FOX_STRIP):
            rows, s = logits(r)
            m_old = m_ref[rows, :]
            m_new = jnp.maximum(m_old, jnp.max(s, axis=-1, keepdims=True))
            alpha_ref[rows, :] = jnp.exp2(m_old - m_new)
            m_ref[rows, :] = m_new
        for r in range(tq // FOX_STRIP):
            rows, s = logits(r)
            p = jnp.exp2(s - jnp.concatenate([m_ref[rows, :]] * lane_blocks, axis=1))
            alpha = alpha_ref[rows, :]
            l_ref[rows, :] = alpha * l_ref[rows, :] + jnp.sum(p, axis=-1, keepdims=True)
            p_view[rows, :] = p.astype(BF16)

    s_a_ref[...] = scores(i)
    s_b_ref[...] = scores(jnp.maximum(i - 1, 0))
    softmax(i, s_a_ref, p_a_ref, True)

    def step(t, s_cur, p_cur, s_nxt, p_prv):
        j = i - t
        s_nxt[...] = scores(jnp.maximum(j - 1, 0))
        pv = _dot(p_prv[...], v_ref[keys(j + 1), :])
        softmax(j, s_cur, p_cur, False)
        acc_ref[...] = alpha_ref[...] * (acc_ref[...] + pv)

    def pair(u):
        step(2 * u + 1, s_b_ref, p_b_ref, s_a_ref, p_a_ref)
        step(2 * u + 2, s_a_ref, p_a_ref, s_b_ref, p_b_ref)

    def quad(w, _):
        pair(2 * w)
        pair(2 * w + 1)
        return 0

    quads = lax.shift_right_logical(i, 2)
    lax.fori_loop(0, quads, quad, 0)
    pl.when((i & 2) != 0)(functools.partial(pair, 2 * quads))

    def finish(p_last):
        acc = acc_ref[...] + _dot(p_last[...], v_ref[keys(0), :])
        o_ref[...] = (acc / l_ref[...]).astype(BF16)

    @pl.when((i & 1) == 1)
    def _():
        step(i, s_b_ref, p_b_ref, s_a_ref, p_a_ref)
        finish(p_b_ref)

    @pl.when((i & 1) == 0)
    def _():
        finish(p_a_ref)


def _fox(proj3, f_rows):
    bsz, seq, _ = proj3.shape
    tq = FOX_T
    qb, kb, vb = 4 * HEADS, 5 * HEADS, 6 * HEADS
    return pl.pallas_call(
        _fox_kernel,
        grid=(bsz, HEADS, seq // tq),
        in_specs=[
            pl.BlockSpec((None, tq, HEAD_DIM), lambda b, h, i: (b, i, qb + h)),
            pl.BlockSpec((None, seq, HEAD_DIM), lambda b, h, i: (b, 0, kb + h)),
            pl.BlockSpec((None, seq, HEAD_DIM), lambda b, h, i: (b, 0, vb + h)),
            pl.BlockSpec((None, None, seq // tq, 1, tq), lambda b, h, i: (b, h, 0, 0, 0)),
        ],
        out_specs=pl.BlockSpec((None, tq, HEAD_DIM), lambda b, h, i: (b, i, h)),
        out_shape=jax.ShapeDtypeStruct((bsz, seq, HEAD_W), BF16),
        scratch_shapes=[
            pltpu.VMEM((tq, tq), F32),
            pltpu.VMEM((tq, tq), F32),
            pltpu.VMEM((tq, tq), BF16),
            pltpu.VMEM((tq, tq), BF16),
            pltpu.VMEM((tq, LANES), F32),
            pltpu.VMEM((tq, LANES), F32),
            pltpu.VMEM((tq, LANES), F32),
            pltpu.VMEM((tq, HEAD_DIM), F32),
        ],
        compiler_params=_cparams(("parallel", "parallel", "arbitrary")),
        name="fox",
    )(proj3, proj3, proj3, f_rows)


OUT_TM = 512
R_E0, R_E1, R_W0, R_W1 = 0, 1, 2, 3


def _first_argmax(vals, lane):
    m = jnp.max(vals, axis=-1, keepdims=True)
    idx = jnp.min(jnp.where(vals == m, lane, LANES), axis=-1, keepdims=True)
    return m, idx


def _outproj_kernel(oa_ref, ob_ref, wa_ref, wb_ref, x_ref, g1_ref, sc_ref, sh_ref, g_ref,
                    wr_ref, br_ref, x1_ref, h2_ref, r_ref):
    mix = _dot(oa_ref[...], wa_ref[...]) + _dot(ob_ref[...], wb_ref[...])
    x1 = x_ref[...] + g1_ref[...] * mix
    x1_ref[...] = x1
    h2 = _modulated_norm(x1, g_ref[...], sc_ref[...], sh_ref[...])
    h2_ref[...] = h2

    logits = _dot3_pre(h2, wr_ref[...]) + br_ref[...]
    tm = logits.shape[0]
    lane = lax.broadcasted_iota(jnp.int32, (tm, LANES), 1)
    gl = jnp.where(lane < N_GROUPS, logits, NEG_INF)
    gmax, gidx = _first_argmax(gl, lane)
    pg = 1.0 / jnp.sum(jnp.exp(gl - gmax), axis=-1, keepdims=True)
    e_lane = lane - N_GROUPS
    in_grp = (e_lane >= gidx * EXPERTS_PER_GROUP) & (e_lane < (gidx + 1) * EXPERTS_PER_GROUP)
    el = jnp.where(in_grp, logits, NEG_INF)
    v0, i0 = _first_argmax(el, lane)
    v1, i1 = _first_argmax(jnp.where(lane == i0, NEG_INF, el), lane)
    ex = jnp.exp(v1 - v0)
    w0 = pg / (1.0 + ex)
    w1 = pg * ex / (1.0 + ex)
    e0 = (i0 - N_GROUPS).astype(F32)
    e1 = (i1 - N_GROUPS).astype(F32)
    r_ref[...] = jnp.where(lane == R_E0, e0, jnp.where(lane == R_E1, e1,
                           jnp.where(lane == R_W0, w0, jnp.where(lane == R_W1, w1, 0.0))))


def _outproj(o_a, o_b, wa, wb, x2, g1, sc2, sh2, g, wr_hi_lo, br, seq):
    n = x2.shape[0]
    tm = OUT_TM
    per_b = seq // tm
    modrow = pl.BlockSpec((None, 1, D_MODEL), lambda i: (i // per_b, 0, 0))
    const = lambda shape: pl.BlockSpec(shape, lambda i: (0, 0))
    return pl.pallas_call(
        _outproj_kernel,
        grid=(n // tm,),
        in_specs=[
            pl.BlockSpec((tm, HEAD_W), lambda i: (i, 0)),
            pl.BlockSpec((tm, HEAD_W), lambda i: (i, 0)),
            const((HEAD_W, D_MODEL)), const((HEAD_W, D_MODEL)),
            pl.BlockSpec((tm, D_MODEL), lambda i: (i, 0)),
            modrow, modrow, modrow,
            const((1, D_MODEL)),
            const((D_MODEL, 2 * LANES)), const((1, LANES)),
        ],
        out_specs=[
            pl.BlockSpec((tm, D_MODEL), lambda i: (i, 0)),
            pl.BlockSpec((tm, D_MODEL), lambda i: (i, 0)),
            pl.BlockSpec((tm, LANES), lambda i: (i, 0)),
        ],
        out_shape=[
            jax.ShapeDtypeStruct((n, D_MODEL), F32),
            jax.ShapeDtypeStruct((n, D_MODEL), F32),
            jax.ShapeDtypeStruct((n, LANES), F32),
        ],
        compiler_params=_cparams(("parallel",)),
        name="outproj",
    )(o_a, o_b, wa, wb, x2, g1, sc2, sh2, g, wr_hi_lo, br)


MOE_TM = 256
ROUTE_TB = 512


def _route_kernel(r_ref, pos_ref, cnt_ref, run_ref, base_ref):
    tb = ROUTE_TB
    phase = pl.program_id(0)
    t = pl.program_id(1)
    r = r_ref[...]
    lane = lax.broadcasted_iota(jnp.int32, (tb, LANES), 1)
    e0 = r[:, R_E0:R_E0 + 1].astype(jnp.int32)
    e1 = r[:, R_E1:R_E1 + 1].astype(jnp.int32)
    oh0 = lane == e0
    oh1 = lane == e1
    both = jnp.where(oh0 | oh1, 1.0, 0.0)

    @pl.when((phase == 0) & (t == 0))
    def _():
        run_ref[...] = jnp.zeros_like(run_ref)

    @pl.when(phase == 0)
    def _():
        run_ref[...] = run_ref[...] + jnp.sum(both, axis=0, keepdims=True)

    @pl.when((phase == 1) & (t == 0))
    def _():
        counts = run_ref[...]
        cnt_ref[...] = counts
        padded = jnp.ceil(counts / MOE_TM) * MOE_TM
        li = lax.broadcasted_iota(jnp.int32, (LANES, LANES), 0)
        lj = lax.broadcasted_iota(jnp.int32, (LANES, LANES), 1)
        upper = jnp.where(li < lj, 1.0, 0.0).astype(BF16)
        hi = jnp.floor(padded / 256.0)
        lo = padded - hi * 256.0
        hi8 = jnp.broadcast_to(hi, (SUBLANES, LANES)).astype(BF16)
        lo8 = jnp.broadcast_to(lo, (SUBLANES, LANES)).astype(BF16)
        base = _dot(hi8, upper) * 256.0 + _dot(lo8, upper)
        base_ref[...] = base[0:1, :]
        run_ref[...] = jnp.zeros_like(run_ref)

    @pl.when(phase == 1)
    def _():
        ri = lax.broadcasted_iota(jnp.int32, (tb, tb), 0)
        ci = lax.broadcasted_iota(jnp.int32, (tb, tb), 1)
        strict = jnp.where(ri > ci, 1.0, 0.0).astype(BF16)
        before = _dot(strict, both.astype(BF16)) + run_ref[...] + base_ref[...]
        p0 = jnp.sum(jnp.where(oh0, before, 0.0), axis=-1, keepdims=True)
        p1 = jnp.sum(jnp.where(oh1, before, 0.0), axis=-1, keepdims=True)
        pos_ref[...] = jnp.where(lane == 0, p0, jnp.where(lane == 1, p1, 0.0)).astype(jnp.int32)
        run_ref[...] = run_ref[...] + jnp.sum(both, axis=0, keepdims=True)


def _route(rslab):
    n = rslab.shape[0]
    tb = ROUTE_TB
    return pl.pallas_call(
        _route_kernel,
        grid=(2, n // tb),
        in_specs=[pl.BlockSpec((tb, LANES), lambda p, t: (t, 0))],
        out_specs=[
            pl.BlockSpec((tb, LANES), lambda p, t: (p * t, 0)),
            pl.BlockSpec((1, LANES), lambda p, t: (0, 0)),
        ],
        out_shape=[
            jax.ShapeDtypeStruct((n, LANES), jnp.int32),
            jax.ShapeDtypeStruct((1, LANES), F32),
        ],
        scratch_shapes=[pltpu.VMEM((1, LANES), F32), pltpu.VMEM((1, LANES), F32)],
        compiler_params=_cparams(("arbitrary", "arbitrary")),
        name="route",
    )(rslab)


DISP_TB = 1024
DMA_UNROLL = 8


def _dispatch_kernel(pad_start_ref, pad_len_ref, used_ref, pos_ref, h2_ref, xs_ref, zero_ref, sem, zsem):
    t = pl.program_id(0)
    tb = DISP_TB
    n_tiles = xs_ref.shape[0] // MOE_TM

    def row_copy(i, k):
        return pltpu.make_async_copy(h2_ref.at[pl.ds(i, 1)], xs_ref.at[pl.ds(pos_ref[0, 2 * i + k], 1)], sem)

    def issue(i, _):
        row_copy(i, 0).start(priority=0)
        row_copy(i, 1).start(priority=1)
        return 0

    lax.fori_loop(0, tb, issue, 0, unroll=DMA_UNROLL)

    @pl.when(t == 0)
    def _():
        zero_ref[...] = jnp.zeros_like(zero_ref)

        def zero_rows(wait, off, rows):
            cp = pltpu.make_async_copy(zero_ref.at[pl.ds(0, rows)], xs_ref.at[pl.ds(off, rows)], zsem)
            cp.wait() if wait else cp.start()

        def per_expert(wait, e, _):
            start = pad_start_ref[e]
            head = (-start) & (SUBLANES - 1)
            for r in range(SUBLANES - 1):
                pl.when(r < head)(functools.partial(zero_rows, wait, start + r, 1))
            off = start + head
            rest = pad_len_ref[e] - head
            piece = MOE_TM // 2
            while piece >= SUBLANES:
                take = (rest & piece) != 0
                pl.when(take)(functools.partial(zero_rows, wait, pl.multiple_of(off, SUBLANES), piece))
                off = off + jnp.where(take, piece, 0)
                piece //= 2
            return 0

        def per_tile(wait, i, _):
            zero_rows(wait, pl.multiple_of(i * MOE_TM, MOE_TM), MOE_TM)
            return 0

        for wait in (False, True):
            lax.fori_loop(0, N_EXPERTS, functools.partial(per_expert, wait), 0)
            lax.fori_loop(used_ref[0], n_tiles, functools.partial(per_tile, wait), 0)

    def drain(i, _):
        row_copy(i, 0).wait()
        row_copy(i, 1).wait()
        return 0

    lax.fori_loop(0, tb, drain, 0, unroll=DMA_UNROLL)


def _dispatch(pad_start, pad_len, used, pos2, h2, p_rows):
    n = h2.shape[0]
    tb = DISP_TB
    return pl.pallas_call(
        _dispatch_kernel,
        grid_spec=pltpu.PrefetchScalarGridSpec(
            num_scalar_prefetch=3,
            grid=(n // tb,),
            in_specs=[
                pl.BlockSpec((None, 1, 2 * tb), lambda t, *_: (t, 0, 0), memory_space=pltpu.SMEM),
                pl.BlockSpec((tb, D_MODEL), lambda t, *_: (t, 0)),
            ],
            out_specs=pl.BlockSpec(memory_space=pl.ANY),
            scratch_shapes=[
                pltpu.VMEM((MOE_TM, D_MODEL), F32),
                pltpu.SemaphoreType.DMA(()),
                pltpu.SemaphoreType.DMA(()),
            ],
        ),
        out_shape=jax.ShapeDtypeStruct((p_rows, D_MODEL), F32),
        compiler_params=_cparams(("arbitrary",)),
        name="dispatch",
    )(pad_start, pad_len, used, pos2, h2)


def _experts_kernel(te_ref, tv_ref, tf_ref, ts_ref, tn_ref, xs_ref, w1_hbm, w3_hbm, w2_hbm, ys_ref,
                    w1f_ref, w3f_ref, w2f_ref, w1b_ref, w3b_ref, w2b_ref, sem):
    i = pl.program_id(0)

    def weight_copies(e, slot):
        return [pltpu.make_async_copy(src.at[e], dst.at[slot], sem.at[slot])
                for src, dst in ((w1_hbm, w1f_ref), (w3_hbm, w3f_ref), (w2_hbm, w2f_ref))]

    @pl.when(tv_ref[i] != 0)
    def _():
        @pl.when(tf_ref[i] != 0)
        def _():
            slot = ts_ref[i]

            @pl.when(i == 0)
            def _():
                for cp in weight_copies(te_ref[i], slot):
                    cp.start()

            for cp in weight_copies(te_ref[i], slot):
                cp.wait()

            @pl.when(tn_ref[i] >= 0)
            def _():
                for cp in weight_copies(tn_ref[i], 1 - slot):
                    cp.start()

            w1b_ref[...] = w1f_ref[slot].astype(BF16)
            w3b_ref[...] = w3f_ref[slot].astype(BF16)
            w2b_ref[...] = w2f_ref[slot].astype(BF16)

        x = xs_ref[...].astype(BF16)
        a = _dot(x, w1b_ref[...])
        b = _dot(x, w3b_ref[...])
        ys_ref[...] = _dot((_silu(a) * b).astype(BF16), w2b_ref[...])

    @pl.when(tv_ref[i] == 0)
    def _():
        ys_ref[...] = jnp.zeros_like(ys_ref)


def _experts(tile_expert, tile_valid, tile_first, tile_slot, tile_next, xs, w1, w3, w2):
    p_rows = xs.shape[0]
    tm = MOE_TM
    hbm = pl.BlockSpec(memory_space=pl.ANY)
    return pl.pallas_call(
        _experts_kernel,
        grid_spec=pltpu.PrefetchScalarGridSpec(
            num_scalar_prefetch=5,
            grid=(p_rows // tm,),
            in_specs=[pl.BlockSpec((tm, D_MODEL), lambda i, *_: (i, 0)), hbm, hbm, hbm],
            out_specs=pl.BlockSpec((tm, D_MODEL), lambda i, *_: (i, 0)),
            scratch_shapes=[
                pltpu.VMEM((2, D_MODEL, D_EXPERT), F32),
                pltpu.VMEM((2, D_MODEL, D_EXPERT), F32),
                pltpu.VMEM((2, D_EXPERT, D_MODEL), F32),
                pltpu.VMEM((D_MODEL, D_EXPERT), BF16),
                pltpu.VMEM((D_MODEL, D_EXPERT), BF16),
                pltpu.VMEM((D_EXPERT, D_MODEL), BF16),
                pltpu.SemaphoreType.DMA((2,)),
            ],
        ),
        out_shape=jax.ShapeDtypeStruct((p_rows, D_MODEL), F32),
        compiler_params=_cparams(("arbitrary",)),
        name="experts",
    )(tile_expert, tile_valid, tile_first, tile_slot, tile_next, xs, w1, w3, w2)


COMB_TB = 256


def _combine_kernel(pos_ref, pos_next_ref, ys_ref, r_ref, x1_ref, g2_ref, fg_ref, o_ref, buf_ref, sem, *, final):
    tb = COMB_TB
    t = pl.program_id(0)
    slot = t & 1

    def row_copy(p_ref, s, i, k):
        return pltpu.make_async_copy(ys_ref.at[pl.ds(p_ref[0, 2 * i + k], 1)], buf_ref.at[s, k, pl.ds(i, 1)],
                                     sem.at[s])

    def issue(p_ref, s):
        def body(i, _):
            row_copy(p_ref, s, i, 0).start(priority=0)
            row_copy(p_ref, s, i, 1).start(priority=1)
            return 0

        lax.fori_loop(0, tb, body, 0, unroll=DMA_UNROLL)

    def drain(i, _):
        row_copy(pos_ref, slot, i, 0).wait()
        row_copy(pos_ref, slot, i, 1).wait()
        return 0

    pl.when(t == 0)(functools.partial(issue, pos_ref, 0))
    pl.when(t + 1 < pl.num_programs(0))(functools.partial(issue, pos_next_ref, 1 - slot))
    lax.fori_loop(0, tb, drain, 0, unroll=DMA_UNROLL)
    r = r_ref[...]
    y = r[:, R_W0:R_W0 + 1] * buf_ref[slot, 0] + r[:, R_W1:R_W1 + 1] * buf_ref[slot, 1]
    x2 = x1_ref[...] + g2_ref[...] * y
    if final:
        x2 = (x2 * lax.rsqrt(jnp.mean(x2 * x2, axis=-1, keepdims=True) + EPS)) * fg_ref[...]
    o_ref[...] = x2


def _combine(pos2, ys, rslab, x1, g2, final_g, seq, final):
    n = x1.shape[0]
    tb = COMB_TB
    per_b = seq // tb
    return pl.pallas_call(
        functools.partial(_combine_kernel, final=final),
        grid=(n // tb,),
        in_specs=[
            pl.BlockSpec((None, 1, 2 * tb), lambda i: (i, 0, 0), memory_space=pltpu.SMEM),
            pl.BlockSpec((None, 1, 2 * tb), lambda i: (jnp.minimum(i + 1, n // tb - 1), 0, 0),
                         memory_space=pltpu.SMEM),
            pl.BlockSpec(memory_space=pl.ANY),
            pl.BlockSpec((tb, LANES), lambda i: (i, 0)),
            pl.BlockSpec((tb, D_MODEL), lambda i: (i, 0)),
            pl.BlockSpec((None, 1, D_MODEL), lambda i: (i // per_b, 0, 0)),
            pl.BlockSpec((1, D_MODEL), lambda i: (0, 0)),
        ],
        out_specs=pl.BlockSpec((tb, D_MODEL), lambda i: (i, 0)),
        out_shape=jax.ShapeDtypeStruct((n, D_MODEL), F32),
        scratch_shapes=[pltpu.VMEM((2, 2, tb, D_MODEL), F32), pltpu.SemaphoreType.DMA((2,))],
        compiler_params=_cparams(("arbitrary",)),
        name="combine",
    )(pos2, pos2, ys, rslab, x1, g2, final_g)


def _layer(x, c, w_ada, b_ada, norm1_g, w_in, conv_w, a_log, dt_bias, dn_onorm_g, fox_f_bias,
           w_out, norm2_g, w_rg, b_rg, w_re, b_re, w1, w3, w2, final_g, final):
    bsz, seq, d = x.shape
    n = bsz * seq
    x2 = x.reshape(n, d)

    mod = _adaln(c, w_ada, b_ada)
    sh1, sc1, g1, sh2, sc2, g2 = [m.reshape(bsz, 1, d) for m in jnp.split(mod, 6, axis=-1)]

    o_a = 4 * HEAD_W
    o_b = o_a + 2 * HEADS
    o_f = o_b + 3 * HEAD_W
    w_t = w_in.T
    w_small_t = jnp.zeros((LANES, d), F32)
    w_small_t = w_small_t.at[0:2 * HEADS].set(w_t[o_a:o_b]).at[2 * HEADS:3 * HEADS].set(w_t[o_f:])
    ws_t = jnp.concatenate(_split_bf16(w_small_t), axis=0)

    proj, gates = _inproj(x2, sc1, sh1, norm1_g.reshape(1, d), w_t, ws_t, seq)
    proj3 = proj.reshape(bsz, seq, MAIN_W)

    def lane_row(vals, off):
        return jnp.zeros((1, LANES), F32).at[0, off:off + HEADS].set(vals)

    slab = _gates(gates.reshape(bsz, seq, LANES), lane_row(a_log, 0), lane_row(dt_bias, 0),
                  lane_row(fox_f_bias, L_F))
    nc = seq // CHUNK
    gct = slab[:, :, L_GC:L_GC + HEADS].reshape(bsz, nc, CHUNK, HEADS).transpose(0, 1, 3, 2)
    gct = gct.reshape(bsz, nc, HEADS, 1, CHUNK)
    f_rows = slab[:, :, L_F:L_F + HEADS].transpose(0, 2, 1).reshape(bsz, HEADS, seq // FOX_T, 1, FOX_T)

    o_dn = _deltanet(proj3, conv_w, slab, gct, dn_onorm_g.reshape(1, HEAD_DIM))
    o_fx = _fox(proj3, f_rows)

    wr = jnp.zeros((d, LANES), F32).at[:, :N_GROUPS].set(w_rg).at[:, N_GROUPS:N_GROUPS + N_EXPERTS].set(w_re)
    br = jnp.zeros((1, LANES), F32).at[0, :N_GROUPS].set(b_rg).at[0, N_GROUPS:N_GROUPS + N_EXPERTS].set(b_re)
    wr_hi_lo = jnp.concatenate(_split_bf16(wr), axis=1)
    w_out_b = w_out.astype(BF16)
    x1, h2, rslab = _outproj(o_dn.reshape(n, HEAD_W), o_fx.reshape(n, HEAD_W), w_out_b[:HEAD_W], w_out_b[HEAD_W:],
                             x2, g1, sc2, sh2, norm2_g.reshape(1, d), wr_hi_lo, br, seq)

    pos_slab, counts = _route(rslab)
    pos = pos_slab[:, 0:2]

    cnt = counts[0, :N_EXPERTS].astype(jnp.int32)
    tiles_per = (cnt + MOE_TM - 1) // MOE_TM
    tile_end = jnp.cumsum(tiles_per)
    base = (tile_end - tiles_per) * MOE_TM
    n_tiles = (2 * n) // MOE_TM + N_EXPERTS
    p_rows = n_tiles * MOE_TM
    tid = jnp.arange(n_tiles, dtype=jnp.int32)
    tile_valid = (tid < tile_end[-1]).astype(jnp.int32)
    te_raw = jnp.minimum(jnp.sum(tid[:, None] >= tile_end[None, :], axis=1), N_EXPERTS - 1).astype(jnp.int32)
    last_e = te_raw[jnp.maximum(tile_end[-1] - 1, 0)]
    tile_expert = jnp.where(tile_valid == 1, te_raw, last_e)
    tile_first = (jnp.concatenate([jnp.array([-1], jnp.int32), tile_expert[:-1]]) != tile_expert).astype(jnp.int32)
    pad_start = base + cnt
    pad_len = tiles_per * MOE_TM - cnt
    eid = jnp.arange(N_EXPERTS, dtype=jnp.int32)
    has = tiles_per > 0
    slot_e = (jnp.cumsum(has.astype(jnp.int32)) - 1) & 1
    later = jnp.where(has[None, :] & (eid[None, :] > eid[:, None]), eid[None, :], N_EXPERTS)
    next_e = jnp.min(later, axis=1)
    next_e = jnp.where(next_e < N_EXPERTS, next_e, -1).astype(jnp.int32)
    tile_slot = slot_e[tile_expert].astype(jnp.int32)
    tile_next = next_e[tile_expert]

    xs = _dispatch(pad_start, pad_len, tile_end[-1:], pos.reshape(n // DISP_TB, 1, 2 * DISP_TB), h2, p_rows)
    ys = _experts(tile_expert, tile_valid, tile_first, tile_slot, tile_next, xs, w1.reshape(N_EXPERTS, d, D_EXPERT),
                  w3.reshape(N_EXPERTS, d, D_EXPERT), w2.reshape(N_EXPERTS, D_EXPERT, d))
    out = _combine(pos.reshape(n // COMB_TB, 1, 2 * COMB_TB), ys, rslab, x1, g2, final_g.reshape(1, d), seq, final)
    return out.reshape(bsz, seq, d)


def kernel(x, c, w_ada, b_ada, norm1_g, w_in, conv_w, a_log, dt_bias, dn_onorm_g, fox_f_bias, w_out, norm2_g,
           w_router_group, b_router_group, w_router_expert, b_router_expert, w1, w3, w2, final_g):
    depth = w_ada.shape[0]
    for l in range(depth):
        x = _layer(x, c, w_ada[l], b_ada[l], norm1_g[l], w_in[l], conv_w[l], a_log[l], dt_bias[l], dn_onorm_g[l],
                   fox_f_bias[l], w_out[l], norm2_g[l], w_router_group[l], b_router_group[l], w_router_expert[l],
                   b_router_expert[l], w1[l], w3[l], w2[l], final_g, l == depth - 1)
    return x
```

```python
import functools

import jax
import jax.numpy as jnp
from jax import lax
from jax.experimental import pallas as pl
from jax.experimental.pallas import tpu as pltpu

F32 = jnp.float32
BF16 = jnp.bfloat16

D_MODEL = 2048
EPS = 1e-6
CHUNK = 64
HEADS = 8
HEAD_DIM = 128
HEAD_W = HEADS * HEAD_DIM
CONV_K = 4
N_GROUPS = 4
EXPERTS_PER_GROUP = 8
N_EXPERTS = N_GROUPS * EXPERTS_PER_GROUP
D_EXPERT = 512
LANES = 128
SUBLANES = 8
MAIN_W = 7 * HEAD_W
VMEM_LIMIT = 56 * 1024 * 1024

L_GC, L_BETA, L_F, L_EGC, L_EK, L_ELAST = 0, 8, 16, 24, 32, 40


def _cparams(sem):
    return pltpu.CompilerParams(dimension_semantics=sem, vmem_limit_bytes=VMEM_LIMIT)


def _split_bf16(a):
    hi = a.astype(BF16)
    lo = (a - hi.astype(F32)).astype(BF16)
    return hi, lo


def _dot(a, b):
    return jnp.dot(a, b, preferred_element_type=F32)


def _dot_nt(a, b):
    return lax.dot_general(a, b, (((1,), (1,)), ((), ())), preferred_element_type=F32)


def _dot3(a, b):
    ah, al = _split_bf16(a)
    bh, bl = _split_bf16(b)
    return _dot(ah, bh) + (_dot(al, bh) + _dot(ah, bl))


def _dot3_pre(a, b_hi_lo):
    ah, al = _split_bf16(a)
    n = b_hi_lo.shape[1] // 2
    both = _dot(ah, b_hi_lo)
    return both[:, :n] + (both[:, n:] + _dot(al, b_hi_lo[:, :n]))


def _softplus(x):
    return jnp.maximum(x, 0.0) + jnp.log1p(jnp.exp(-jnp.abs(x)))


def _silu(x):
    return x * jax.nn.sigmoid(x)


def _adaln_kernel(c_ref, w_ref, b_ref, o_ref):
    c = c_ref[...]
    o_ref[...] = _dot(_silu(c).astype(BF16), w_ref[...].astype(BF16)) + b_ref[...]


def _adaln(c, w, b):
    bsz = c.shape[0]
    n = w.shape[1]
    tn = 1024
    cp = jnp.zeros((SUBLANES, D_MODEL), F32).at[:bsz].set(c)
    out = pl.pallas_call(
        _adaln_kernel,
        grid=(n // tn,),
        in_specs=[
            pl.BlockSpec((SUBLANES, D_MODEL), lambda j: (0, 0)),
            pl.BlockSpec((D_MODEL, tn), lambda j: (0, j)),
            pl.BlockSpec((1, tn), lambda j: (0, j)),
        ],
        out_specs=pl.BlockSpec((SUBLANES, tn), lambda j: (0, j)),
        out_shape=jax.ShapeDtypeStruct((SUBLANES, n), F32),
        compiler_params=_cparams(("parallel",)),
        name="adaln",
    )(cp, w, b.reshape(1, n))
    return out[:bsz]


INPROJ_TM = 1024
INPROJ_TN = 512
ROW_STEP = 128


def _modulated_norm(x, g, sc, sh):
    r = lax.rsqrt(jnp.mean(x * x, axis=-1, keepdims=True) + EPS)
    return (x * r) * (g * (1.0 + sc)) + sh


def _inproj_kernel(x_ref, sc_ref, sh_ref, g_ref, w_ref, wab_ref, wf_ref, o_ref, og_ref, h_ref, ws_ref):
    @pl.when(pl.program_id(1) == 0)
    def _():
        gate_rows = jnp.concatenate([wab_ref[...], wf_ref[...]], axis=0)
        pad = jnp.zeros((LANES - gate_rows.shape[0], D_MODEL), F32)
        hi, lo = _split_bf16(jnp.concatenate([gate_rows, pad], axis=0))
        ws_ref[0:LANES, :] = hi
        ws_ref[LANES:, :] = lo

        def body(i, _):
            rows = pl.ds(pl.multiple_of(i * ROW_STEP, ROW_STEP), ROW_STEP)
            h = _modulated_norm(x_ref[rows, :], g_ref[...], sc_ref[...], sh_ref[...])
            hb = h.astype(BF16)
            h_ref[rows, :] = hb
            both = _dot_nt(hb, ws_ref[...])
            og_ref[rows, :] = both[:, :LANES] + both[:, LANES:]
            return 0

        lax.fori_loop(0, INPROJ_TM // ROW_STEP, body, 0)

    o_ref[...] = _dot_nt(h_ref[...], w_ref[...].astype(BF16)).astype(BF16)


def _inproj(x2, sc1, sh1, g, w_t, seq):
    n = x2.shape[0]
    tm, tn = INPROJ_TM, INPROJ_TN
    per_b = seq // tm
    first_part = 4 * HEAD_W // tn
    skip = 2 * HEADS

    def w_rows(i, j):
        return (SUBLANES * (j * (tn // SUBLANES) + jnp.where(j >= first_part, skip // SUBLANES, 0)), 0)

    return pl.pallas_call(
        _inproj_kernel,
        grid=(n // tm, MAIN_W // tn),
        in_specs=[
            pl.BlockSpec((tm, D_MODEL), lambda i, j: (i, 0)),
            pl.BlockSpec((None, 1, D_MODEL), lambda i, j: (i // per_b, 0, 0)),
            pl.BlockSpec((None, 1, D_MODEL), lambda i, j: (i // per_b, 0, 0)),
            pl.BlockSpec((1, D_MODEL), lambda i, j: (0, 0)),
            pl.BlockSpec((pl.Element(tn), pl.Element(D_MODEL)), w_rows),
            pl.BlockSpec((pl.Element(skip), pl.Element(D_MODEL)), lambda i, j: (first_part * tn, 0)),
            pl.BlockSpec((pl.Element(HEADS), pl.Element(D_MODEL)), lambda i, j: (MAIN_W + skip, 0)),
        ],
        out_specs=[
            pl.BlockSpec((tm, tn), lambda i, j: (i, j)),
            pl.BlockSpec((tm, LANES), lambda i, j: (i, 0)),
        ],
        out_shape=[
            jax.ShapeDtypeStruct((n, MAIN_W), BF16),
            jax.ShapeDtypeStruct((n, LANES), F32),
        ],
        scratch_shapes=[pltpu.VMEM((tm, D_MODEL), BF16), pltpu.VMEM((2 * LANES, D_MODEL), BF16)],
        compiler_params=_cparams(("parallel", "arbitrary")),
        name="inproj",
    )(x2, sc1, sh1, g, w_t, w_t, w_t)


GATES_TB = 256


def _split3(a):
    p0 = a.astype(BF16)
    r1 = a - p0.astype(F32)
    p1 = r1.astype(BF16)
    p2 = (r1 - p1.astype(F32)).astype(BF16)
    return p0, p1, p2


def _dot_ones(m, a):
    p0, p1, p2 = _split3(a)
    return _dot(m, p0) + (_dot(m, p1) + _dot(m, p2))


def _gates_kernel(x_ref, alog_ref, dt_ref, fb_ref, o_ref, carry_ref):
    tb = GATES_TB

    @pl.when(pl.program_id(1) == 0)
    def _():
        carry_ref[...] = jnp.zeros_like(carry_ref)

    x = x_ref[...]
    lane = lax.broadcasted_iota(jnp.int32, (tb, LANES), 1)
    g = -jnp.exp(alog_ref[...]) * _softplus(x + dt_ref[...])
    beta = jax.nn.sigmoid(x)
    lf = -_softplus(-(x + fb_ref[...]))

    ri = lax.broadcasted_iota(jnp.int32, (tb, tb), 0)
    ci = lax.broadcasted_iota(jnp.int32, (tb, tb), 1)
    same_chunk = (ri // CHUNK) == (ci // CHUNK)
    tri = (ri >= ci)
    m_all = jnp.where(tri, 1.0, 0.0).astype(BF16)
    m_chunk = jnp.where(tri & same_chunk, 1.0, 0.0).astype(BF16)
    m_tot = jnp.where(same_chunk, 1.0, 0.0).astype(BF16)

    gc = _dot_ones(m_chunk, g)
    glast = _dot_ones(m_tot, g)
    fcum = _dot_ones(m_all, lf) + carry_ref[...]
    carry_ref[...] = fcum[tb - 1:tb, :]

    in_a = lane < HEADS
    egc = jnp.where(in_a, jnp.exp(gc), 0.0)
    ek = jnp.where(in_a, jnp.exp(glast - gc), 0.0)
    elast = jnp.where(in_a, jnp.exp(glast), 0.0)
    out = jnp.where(in_a, gc, jnp.where(lane < 2 * HEADS, beta, jnp.where(lane < 3 * HEADS, fcum, 0.0)))
    out = out + pltpu.roll(egc, L_EGC, 1) + pltpu.roll(ek, L_EK, 1) + pltpu.roll(elast, L_ELAST, 1)
    o_ref[...] = out


def _gates(gates, alog_row, dt_row, fb_row):
    bsz, seq, _ = gates.shape
    tb = GATES_TB
    row = pl.BlockSpec((1, LANES), lambda b, t: (0, 0))
    return pl.pallas_call(
        _gates_kernel,
        grid=(bsz, seq // tb),
        in_specs=[pl.BlockSpec((None, tb, LANES), lambda b, t: (b, t, 0)), row, row, row],
        out_specs=pl.BlockSpec((None, tb, LANES), lambda b, t: (b, t, 0)),
        out_shape=jax.ShapeDtypeStruct((bsz, seq, LANES), F32),
        scratch_shapes=[pltpu.VMEM((1, LANES), F32)],
        compiler_params=_cparams(("parallel", "arbitrary")),
        name="gates",
    )(gates, alog_row, dt_row, fb_row)


DN_TB = 256
DN_GROUP = 4
HALO = SUBLANES


def _bdot(a, b):
    return lax.dot_general(a, b, (((2,), (1,)), ((0,), (0,))), preferred_element_type=F32)


def _bdot_nt(a, b):
    return lax.dot_general(a, b, (((2,), (2,)), ((0,), (0,))), preferred_element_type=F32)


def _inv_unit_lower(a, eye, blk16, blk32):
    n = jnp.where(blk16, -a, 0.0)
    e1 = jnp.where(blk32 & jnp.logical_not(blk16), a, 0.0).astype(BF16)
    e2 = jnp.where(blk32, 0.0, a).astype(BF16)
    t = eye + n
    p = n.astype(BF16)
    for _ in range(3):
        p = _bdot(p, p).astype(BF16)
        t = t + _bdot(t.astype(BF16), p)
    for e in (e1, e2):
        tb = t.astype(BF16)
        t = t - _bdot(_bdot(tb, e).astype(BF16), tb)
    return t


def _deltanet_kernel(q_ref, k_ref, v_ref, z_ref, wq_ref, wk_ref, wv_ref, slab_ref, gct_ref, og_ref,
                     o_ref, ext_ref, qn_ref, kn_ref, vv_ref, s_ref):
    tb = DN_TB

    @pl.when(pl.program_id(1) == 0)
    def _():
        ext_ref[:, 0:HALO, :] = jnp.zeros((3, HALO, HEAD_W), F32)
        s_ref[...] = jnp.zeros_like(s_ref)

    for idx, (u_ref, w_ref) in enumerate(((q_ref, wq_ref), (k_ref, wk_ref), (v_ref, wv_ref))):
        for h in range(HEADS):
            cols = slice(h * HEAD_DIM, (h + 1) * HEAD_DIM)
            ext_ref[idx, HALO:HALO + tb, cols] = u_ref[:, cols].astype(F32)
            y = None
            for j in range(CONV_K):
                start = HALO - (CONV_K - 1) + j
                term = ext_ref[idx, start:start + tb, cols] * w_ref[j:j + 1, cols]
                y = term if y is None else y + term
            y = _silu(y)
            if idx == 2:
                vv_ref[h] = y
            else:
                yn = y * lax.rsqrt(jnp.sum(y * y, axis=-1, keepdims=True) + EPS)
                if idx == 0:
                    qn_ref[h] = (yn * (HEAD_DIM ** -0.5)).astype(BF16)
                else:
                    kn_ref[h] = yn.astype(BF16)
        ext_ref[idx, 0:HALO, :] = ext_ref[idx, tb:tb + HALO, :]

    ri = lax.broadcasted_iota(jnp.int32, (DN_GROUP * HEADS, CHUNK, CHUNK), 1)
    ci = lax.broadcasted_iota(jnp.int32, (DN_GROUP * HEADS, CHUNK, CHUNK), 2)
    incl = ri >= ci
    strict = ri > ci
    eye = jnp.where(ri == ci, 1.0, 0.0)
    blk16 = (ri // 16) == (ci // 16)
    blk32 = (ri // 32) == (ci // 32)

    def group_body(c, _):
        rows = [pl.ds(pl.multiple_of((c * DN_GROUP + g) * CHUNK, CHUNK), CHUNK) for g in range(DN_GROUP)]
        slabs = [slab_ref[r, :] for r in rows]

        def col(off, width):
            return jnp.stack([jnp.broadcast_to(sl[:, off + h:off + h + 1], (CHUNK, width))
                              for sl in slabs for h in range(HEADS)])

        def grouped(ref):
            return jnp.concatenate([ref[:, r, :] for r in rows], axis=0)

        q = grouped(qn_ref)
        k = grouped(kn_ref)
        v = grouped(vv_ref)
        beta = col(L_BETA, HEAD_DIM)
        egc = col(L_EGC, HEAD_DIM)
        gc_row = jnp.concatenate([gct_ref[c * DN_GROUP + g] for g in range(DN_GROUP)], axis=0)

        decay = jnp.where(incl, jnp.exp(col(L_GC, CHUNK) - gc_row), 0.0)
        kk = _bdot_nt(k, k)
        qk = (_bdot_nt(q, k) * decay).astype(BF16)
        a = jnp.where(strict, kk * decay * beta[:, :, :CHUNK], 0.0)
        t = _inv_unit_lower(a, eye, blk16, blk32).astype(BF16)

        kf = k.astype(F32)
        vb = (v * beta).astype(BF16)
        kbg = (kf * (beta * egc)).astype(BF16)
        u = _bdot(t, vb)
        w = _bdot(t, kbg).astype(BF16)
        qd = (q.astype(F32) * egc).astype(BF16)
        kd = kf * col(L_EK, HEAD_DIM)
        kdt = jnp.stack([kd[n].T for n in range(DN_GROUP * HEADS)]).astype(BF16)

        for g in range(DN_GROUP):
            sel = slice(g * HEADS, (g + 1) * HEADS)
            s = s_ref[...]
            sb = s.astype(BF16)
            vnb = (u[sel] - _bdot(w[sel], sb)).astype(BF16)
            o = _bdot(qd[sel], sb) + _bdot(qk[sel], vnb)
            elast = jnp.stack([jnp.broadcast_to(slabs[g][CHUNK - 1:CHUNK, L_ELAST + h:L_ELAST + h + 1],
                                                (HEAD_DIM, HEAD_DIM)) for h in range(HEADS)])
            s_ref[...] = s * elast + _bdot(kdt[sel], vnb)

            r = lax.rsqrt(jnp.mean(o * o, axis=-1, keepdims=True) + EPS)
            on = (o * r) * og_ref[...]
            for h in range(HEADS):
                cols = slice(h * HEAD_DIM, (h + 1) * HEAD_DIM)
                o_ref[rows[g], cols] = (on[h] * _silu(z_ref[rows[g], cols].astype(F32))).astype(BF16)
        return 0

    lax.fori_loop(0, tb // (CHUNK * DN_GROUP), group_body, 0)


def _deltanet(proj3, conv_w, slab, gct, onorm_g):
    bsz, seq, _ = proj3.shape
    tb = DN_TB
    nct = tb // CHUNK

    def colblk(j):
        return pl.BlockSpec((None, tb, HEAD_W), lambda b, t: (b, t, j))

    def wblk(j):
        return pl.BlockSpec((CONV_K, HEAD_W), lambda b, t: (0, j))

    return pl.pallas_call(
        _deltanet_kernel,
        grid=(bsz, seq // tb),
        in_specs=[
            colblk(0), colblk(1), colblk(2), colblk(3),
            wblk(0), wblk(1), wblk(2),
            pl.BlockSpec((None, tb, LANES), lambda b, t: (b, t, 0)),
            pl.BlockSpec((None, nct, HEADS, 1, CHUNK), lambda b, t: (b, t, 0, 0, 0)),
            pl.BlockSpec((1, HEAD_DIM), lambda b, t: (0, 0)),
        ],
        out_specs=pl.BlockSpec((None, tb, HEAD_W), lambda b, t: (b, t, 0)),
        out_shape=jax.ShapeDtypeStruct((bsz, seq, HEAD_W), BF16),
        scratch_shapes=[
            pltpu.VMEM((3, tb + HALO, HEAD_W), F32),
            pltpu.VMEM((HEADS, tb, HEAD_DIM), BF16),
            pltpu.VMEM((HEADS, tb, HEAD_DIM), BF16),
            pltpu.VMEM((HEADS, tb, HEAD_DIM), F32),
            pltpu.VMEM((HEADS, HEAD_DIM, HEAD_DIM), F32),
        ],
        compiler_params=_cparams(("parallel", "arbitrary")),
        name="deltanet",
    )(proj3, proj3, proj3, proj3, conv_w, conv_w, conv_w, slab, gct, onorm_g)


FOX_T = 512
FOX_STRIP = 32
NEG_INF = float("-inf")
LOG2E = 1.4426950408889634


def _fox_kernel(q_ref, k_ref, v_ref, f_ref, o_ref, s_a_ref, s_b_ref, p_a_ref, p_b_ref, m_ref, l_ref, alpha_ref,
                acc_ref):
    tq = FOX_T
    i = pl.program_id(2)
    lane_blocks = tq // LANES
    qs = (q_ref[...].astype(F32) * (HEAD_DIM ** -0.5 * LOG2E)).astype(BF16)
    f0 = f_ref[i][:, 0:1]
    m_ref[...] = jnp.full_like(m_ref, NEG_INF)
    l_ref[...] = jnp.zeros_like(l_ref)
    acc_ref[...] = jnp.zeros_like(acc_ref)

    def keys(j):
        return pl.ds(pl.multiple_of(j * tq, tq), tq)

    def scores(j):
        return _dot_nt(qs, k_ref[keys(j), :])

    def softmax(j, s_view, p_view, masked):
        bias = (f0 - f_ref[j]) * LOG2E

        def logits(r):
            rows = slice(r * FOX_STRIP, (r + 1) * FOX_STRIP)
            s = s_view[rows, :] + bias
            if masked:
                ri = r * FOX_STRIP + lax.broadcasted_iota(jnp.int32, (FOX_STRIP, tq), 0)
                ci = lax.broadcasted_iota(jnp.int32, (FOX_STRIP, tq), 1)
                s = jnp.where(ci <= ri, s, NEG_INF)
            return rows, s

        for r in range(tq // FOX_STRIP):
            rows, s = logits(r)
            m_old = m_ref[rows, :]
            m_new = jnp.maximum(m_old, jnp.max(s, axis=-1, keepdims=True))
            alpha_ref[rows, :] = jnp.exp2(m_old - m_new)
            m_ref[rows, :] = m_new
        for r in range(tq // FOX_STRIP):
            rows, s = logits(r)
            p = jnp.exp2(s - jnp.concatenate([m_ref[rows, :]] * lane_blocks, axis=1))
            alpha = alpha_ref[rows, :]
            l_ref[rows, :] = alpha * l_ref[rows, :] + jnp.sum(p, axis=-1, keepdims=True)
            p_view[rows, :] = p.astype(BF16)

    s_a_ref[...] = scores(i)
    s_b_ref[...] = scores(jnp.maximum(i - 1, 0))
    softmax(i, s_a_ref, p_a_ref, True)

    def step(t, s_cur, p_cur, s_nxt, p_prv):
        j = i - t
        s_nxt[...] = scores(jnp.maximum(j - 1, 0))
        pv = _dot(p_prv[...], v_ref[keys(j + 1), :])
        softmax(j, s_cur, p_cur, False)
        acc_ref[...] = alpha_ref[...] * (acc_ref[...] + pv)

    def pair(u):
        step(2 * u + 1, s_b_ref, p_b_ref, s_a_ref, p_a_ref)
        step(2 * u + 2, s_a_ref, p_a_ref, s_b_ref, p_b_ref)

    def quad(w, _):
        pair(2 * w)
        pair(2 * w + 1)
        return 0

    quads = lax.shift_right_logical(i, 2)
    lax.fori_loop(0, quads, quad, 0)
    pl.when((i & 2) != 0)(functools.partial(pair, 2 * quads))

    def finish(p_last):
        acc = acc_ref[...] + _dot(p_last[...], v_ref[keys(0), :])
        o_ref[...] = (acc / l_ref[...]).astype(BF16)

    @pl.when((i & 1) == 1)
    def _():
        step(i, s_b_ref, p_b_ref, s_a_ref, p_a_ref)
        finish(p_b_ref)

    @pl.when((i & 1) == 0)
    def _():
        finish(p_a_ref)


def _fox(proj3, f_rows):
    bsz, seq, _ = proj3.shape
    tq = FOX_T
    qb, kb, vb = 4 * HEADS, 5 * HEADS, 6 * HEADS
    return pl.pallas_call(
        _fox_kernel,
        grid=(bsz, HEADS, seq // tq),
        in_specs=[
            pl.BlockSpec((None, tq, HEAD_DIM), lambda b, h, i: (b, i, qb + h)),
            pl.BlockSpec((None, seq, HEAD_DIM), lambda b, h, i: (b, 0, kb + h)),
            pl.BlockSpec((None, seq, HEAD_DIM), lambda b, h, i: (b, 0, vb + h)),
            pl.BlockSpec((None, None, seq // tq, 1, tq), lambda b, h, i: (b, h, 0, 0, 0)),
        ],
        out_specs=pl.BlockSpec((None, tq, HEAD_DIM), lambda b, h, i: (b, i, h)),
        out_shape=jax.ShapeDtypeStruct((bsz, seq, HEAD_W), BF16),
        scratch_shapes=[
            pltpu.VMEM((tq, tq), F32),
            pltpu.VMEM((tq, tq), F32),
            pltpu.VMEM((tq, tq), BF16),
            pltpu.VMEM((tq, tq), BF16),
            pltpu.VMEM((tq, LANES), F32),
            pltpu.VMEM((tq, LANES), F32),
            pltpu.VMEM((tq, LANES), F32),
            pltpu.VMEM((tq, HEAD_DIM), F32),
        ],
        compiler_params=_cparams(("parallel", "parallel", "arbitrary")),
        name="fox",
    )(proj3, proj3, proj3, f_rows)


OUT_TM = 512
R_E0, R_E1, R_W0, R_W1 = 0, 1, 2, 3


def _first_argmax(vals, lane):
    m = jnp.max(vals, axis=-1, keepdims=True)
    idx = jnp.min(jnp.where(vals == m, lane, LANES), axis=-1, keepdims=True)
    return m, idx


def _outproj_kernel(oa_ref, ob_ref, wa_ref, wb_ref, x_ref, g1_ref, sc_ref, sh_ref, g_ref,
                    wr_ref, br_ref, x1_ref, h2_ref, r_ref, cnt_ref):
    mix = _dot(oa_ref[...], wa_ref[...]) + _dot(ob_ref[...], wb_ref[...])
    x1 = x_ref[...] + g1_ref[...] * mix
    x1_ref[...] = x1
    h2 = _modulated_norm(x1, g_ref[...], sc_ref[...], sh_ref[...])
    h2_ref[...] = h2

    logits = _dot3_pre(h2, wr_ref[...]) + br_ref[...]
    tm = logits.shape[0]
    lane = lax.broadcasted_iota(jnp.int32, (tm, LANES), 1)
    gl = jnp.where(lane < N_GROUPS, logits, NEG_INF)
    gmax, gidx = _first_argmax(gl, lane)
    pg = 1.0 / jnp.sum(jnp.exp(gl - gmax), axis=-1, keepdims=True)
    e_lane = lane - N_GROUPS
    in_grp = (e_lane >= gidx * EXPERTS_PER_GROUP) & (e_lane < (gidx + 1) * EXPERTS_PER_GROUP)
    el = jnp.where(in_grp, logits, NEG_INF)
    v0, i0 = _first_argmax(el, lane)
    v1, i1 = _first_argmax(jnp.where(lane == i0, NEG_INF, el), lane)
    ex = jnp.exp(v1 - v0)
    w0 = pg / (1.0 + ex)
    w1 = pg * ex / (1.0 + ex)
    e0 = (i0 - N_GROUPS).astype(F32)
    e1 = (i1 - N_GROUPS).astype(F32)
    r_ref[...] = jnp.where(lane == R_E0, e0, jnp.where(lane == R_E1, e1,
                           jnp.where(lane == R_W0, w0, jnp.where(lane == R_W1, w1, 0.0))))

    @pl.when(pl.program_id(0) == 0)
    def _():
        cnt_ref[...] = jnp.zeros_like(cnt_ref)

    picked = (lane == i0 - N_GROUPS) | (lane == i1 - N_GROUPS)
    cnt_ref[...] += jnp.sum(jnp.where(picked, 1.0, 0.0), axis=0, keepdims=True)


def _outproj(o_a, o_b, wa, wb, x2, g1, sc2, sh2, g, wr_hi_lo, br, seq):
    n = x2.shape[0]
    tm = OUT_TM
    per_b = seq // tm
    modrow = pl.BlockSpec((None, 1, D_MODEL), lambda i: (i // per_b, 0, 0))
    const = lambda shape: pl.BlockSpec(shape, lambda i: (0, 0))
    return pl.pallas_call(
        _outproj_kernel,
        grid=(n // tm,),
        in_specs=[
            pl.BlockSpec((tm, HEAD_W), lambda i: (i, 0)),
            pl.BlockSpec((tm, HEAD_W), lambda i: (i, 0)),
            const((HEAD_W, D_MODEL)), const((HEAD_W, D_MODEL)),
            pl.BlockSpec((tm, D_MODEL), lambda i: (i, 0)),
            modrow, modrow, modrow,
            const((1, D_MODEL)),
            const((D_MODEL, 2 * LANES)), const((1, LANES)),
        ],
        out_specs=[
            pl.BlockSpec((tm, D_MODEL), lambda i: (i, 0)),
            pl.BlockSpec((tm, D_MODEL), lambda i: (i, 0)),
            pl.BlockSpec((tm, LANES), lambda i: (i, 0)),
            pl.BlockSpec((1, LANES), lambda i: (0, 0)),
        ],
        out_shape=[
            jax.ShapeDtypeStruct((n, D_MODEL), F32),
            jax.ShapeDtypeStruct((n, D_MODEL), F32),
            jax.ShapeDtypeStruct((n, LANES), F32),
            jax.ShapeDtypeStruct((1, LANES), F32),
        ],
        compiler_params=_cparams(("arbitrary",)),
        name="outproj",
    )(o_a, o_b, wa, wb, x2, g1, sc2, sh2, g, wr_hi_lo, br)


MOE_TM = 256
ROUTE_TB = 512


def _route_kernel(r_ref, cnt_ref, pos_ref, run_ref, base_ref):
    tb = ROUTE_TB
    t = pl.program_id(0)
    r = r_ref[...]
    lane = lax.broadcasted_iota(jnp.int32, (tb, LANES), 1)
    e0 = r[:, R_E0:R_E0 + 1].astype(jnp.int32)
    e1 = r[:, R_E1:R_E1 + 1].astype(jnp.int32)
    oh0 = lane == e0
    oh1 = lane == e1
    both = jnp.where(oh0 | oh1, 1.0, 0.0)

    @pl.when(t == 0)
    def _():
        counts = cnt_ref[...]
        padded = jnp.ceil(counts / MOE_TM) * MOE_TM
        li = lax.broadcasted_iota(jnp.int32, (LANES, LANES), 0)
        lj = lax.broadcasted_iota(jnp.int32, (LANES, LANES), 1)
        upper = jnp.where(li < lj, 1.0, 0.0).astype(BF16)
        hi = jnp.floor(padded / 256.0)
        lo = padded - hi * 256.0
        hi8 = jnp.broadcast_to(hi, (SUBLANES, LANES)).astype(BF16)
        lo8 = jnp.broadcast_to(lo, (SUBLANES, LANES)).astype(BF16)
        base = _dot(hi8, upper) * 256.0 + _dot(lo8, upper)
        base_ref[...] = base[0:1, :]
        run_ref[...] = jnp.zeros_like(run_ref)

    ri = lax.broadcasted_iota(jnp.int32, (tb, tb), 0)
    ci = lax.broadcasted_iota(jnp.int32, (tb, tb), 1)
    strict = jnp.where(ri > ci, 1.0, 0.0).astype(BF16)
    before = _dot(strict, both.astype(BF16)) + run_ref[...] + base_ref[...]
    p0 = jnp.sum(jnp.where(oh0, before, 0.0), axis=-1, keepdims=True)
    p1 = jnp.sum(jnp.where(oh1, before, 0.0), axis=-1, keepdims=True)
    pos_ref[...] = jnp.where(lane == 0, p0, jnp.where(lane == 1, p1, 0.0)).astype(jnp.int32)
    run_ref[...] = run_ref[...] + jnp.sum(both, axis=0, keepdims=True)


def _route(rslab, counts):
    n = rslab.shape[0]
    tb = ROUTE_TB
    return pl.pallas_call(
        _route_kernel,
        grid=(n // tb,),
        in_specs=[pl.BlockSpec((tb, LANES), lambda t: (t, 0)), pl.BlockSpec((1, LANES), lambda t: (0, 0))],
        out_specs=pl.BlockSpec((tb, LANES), lambda t: (t, 0)),
        out_shape=jax.ShapeDtypeStruct((n, LANES), jnp.int32),
        scratch_shapes=[pltpu.VMEM((1, LANES), F32), pltpu.VMEM((1, LANES), F32)],
        compiler_params=_cparams(("arbitrary",)),
        name="route",
    )(rslab, counts)


DISP_TB = 1024
DMA_UNROLL = 8


def _dispatch_kernel(pad_start_ref, pad_len_ref, used_ref, pos_ref, h2_ref, xs_ref, zero_ref, sem, zsem):
    t = pl.program_id(0)
    tb = DISP_TB
    n_tiles = xs_ref.shape[0] // MOE_TM

    def row_copy(i, k):
        return pltpu.make_async_copy(h2_ref.at[pl.ds(i, 1)], xs_ref.at[pl.ds(pos_ref[0, 2 * i + k], 1)], sem)

    def issue(i, _):
        row_copy(i, 0).start(priority=0)
        row_copy(i, 1).start(priority=1)
        return 0

    lax.fori_loop(0, tb, issue, 0, unroll=DMA_UNROLL)

    @pl.when(t == 0)
    def _():
        zero_ref[...] = jnp.zeros_like(zero_ref)

        def zero_rows(wait, off, rows):
            cp = pltpu.make_async_copy(zero_ref.at[pl.ds(0, rows)], xs_ref.at[pl.ds(off, rows)], zsem)
            cp.wait() if wait else cp.start()

        def per_expert(wait, e, _):
            start = pad_start_ref[e]
            head = (-start) & (SUBLANES - 1)
            for r in range(SUBLANES - 1):
                pl.when(r < head)(functools.partial(zero_rows, wait, start + r, 1))
            off = start + head
            rest = pad_len_ref[e] - head
            piece = MOE_TM // 2
            while piece >= SUBLANES:
                take = (rest & piece) != 0
                pl.when(take)(functools.partial(zero_rows, wait, pl.multiple_of(off, SUBLANES), piece))
                off = off + jnp.where(take, piece, 0)
                piece //= 2
            return 0

        def per_tile(wait, i, _):
            zero_rows(wait, pl.multiple_of(i * MOE_TM, MOE_TM), MOE_TM)
            return 0

        for wait in (False, True):
            lax.fori_loop(0, N_EXPERTS, functools.partial(per_expert, wait), 0)
            lax.fori_loop(used_ref[0], n_tiles, functools.partial(per_tile, wait), 0)

    def drain(i, _):
        row_copy(i, 0).wait()
        row_copy(i, 1).wait()
        return 0

    lax.fori_loop(0, tb, drain, 0, unroll=DMA_UNROLL)


def _dispatch(pad_start, pad_len, used, pos2, h2, p_rows):
    n = h2.shape[0]
    tb = DISP_TB
    return pl.pallas_call(
        _dispatch_kernel,
        grid_spec=pltpu.PrefetchScalarGridSpec(
            num_scalar_prefetch=3,
            grid=(n // tb,),
            in_specs=[
                pl.BlockSpec((None, 1, 2 * tb), lambda t, *_: (t, 0, 0), memory_space=pltpu.SMEM),
                pl.BlockSpec((tb, D_MODEL), lambda t, *_: (t, 0)),
            ],
            out_specs=pl.BlockSpec(memory_space=pl.ANY),
            scratch_shapes=[
                pltpu.VMEM((MOE_TM, D_MODEL), F32),
                pltpu.SemaphoreType.DMA(()),
                pltpu.SemaphoreType.DMA(()),
            ],
        ),
        out_shape=jax.ShapeDtypeStruct((p_rows, D_MODEL), F32),
        compiler_params=_cparams(("arbitrary",)),
        name="dispatch",
    )(pad_start, pad_len, used, pos2, h2)


def _experts_kernel(te_ref, tv_ref, tf_ref, ts_ref, tn_ref, xs_ref, w1_hbm, w3_hbm, w2_hbm, ys_ref,
                    w1f_ref, w3f_ref, w2f_ref, w1b_ref, w3b_ref, w2b_ref, sem):
    i = pl.program_id(0)

    def weight_copies(e, slot):
        return [pltpu.make_async_copy(src.at[e], dst.at[slot], sem.at[slot])
                for src, dst in ((w1_hbm, w1f_ref), (w3_hbm, w3f_ref), (w2_hbm, w2f_ref))]

    @pl.when(tv_ref[i] != 0)
    def _():
        @pl.when(tf_ref[i] != 0)
        def _():
            slot = ts_ref[i]

            @pl.when(i == 0)
            def _():
                for cp in weight_copies(te_ref[i], slot):
                    cp.start()

            for cp in weight_copies(te_ref[i], slot):
                cp.wait()

            @pl.when(tn_ref[i] >= 0)
            def _():
                for cp in weight_copies(tn_ref[i], 1 - slot):
                    cp.start()

            w1b_ref[...] = w1f_ref[slot].astype(BF16)
            w3b_ref[...] = w3f_ref[slot].astype(BF16)
            w2b_ref[...] = w2f_ref[slot].astype(BF16)

        x = xs_ref[...].astype(BF16)
        a = _dot(x, w1b_ref[...])
        b = _dot(x, w3b_ref[...])
        ys_ref[...] = _dot((_silu(a) * b).astype(BF16), w2b_ref[...])

    @pl.when(tv_ref[i] == 0)
    def _():
        ys_ref[...] = jnp.zeros_like(ys_ref)


def _experts(tile_expert, tile_valid, tile_first, tile_slot, tile_next, xs, w1, w3, w2):
    p_rows = xs.shape[0]
    tm = MOE_TM
    hbm = pl.BlockSpec(memory_space=pl.ANY)
    return pl.pallas_call(
        _experts_kernel,
        grid_spec=pltpu.PrefetchScalarGridSpec(
            num_scalar_prefetch=5,
            grid=(p_rows // tm,),
            in_specs=[pl.BlockSpec((tm, D_MODEL), lambda i, *_: (i, 0)), hbm, hbm, hbm],
            out_specs=pl.BlockSpec((tm, D_MODEL), lambda i, *_: (i, 0)),
            scratch_shapes=[
                pltpu.VMEM((2, D_MODEL, D_EXPERT), F32),
                pltpu.VMEM((2, D_MODEL, D_EXPERT), F32),
                pltpu.VMEM((2, D_EXPERT, D_MODEL), F32),
                pltpu.VMEM((D_MODEL, D_EXPERT), BF16),
                pltpu.VMEM((D_MODEL, D_EXPERT), BF16),
                pltpu.VMEM((D_EXPERT, D_MODEL), BF16),
                pltpu.SemaphoreType.DMA((2,)),
            ],
        ),
        out_shape=jax.ShapeDtypeStruct((p_rows, D_MODEL), F32),
        compiler_params=_cparams(("arbitrary",)),
        name="experts",
    )(tile_expert, tile_valid, tile_first, tile_slot, tile_next, xs, w1, w3, w2)


COMB_TB = 256


def _combine_kernel(pos_ref, pos_next_ref, ys_ref, r_ref, x1_ref, g2_ref, fg_ref, o_ref, buf_ref, sem, *, final):
    tb = COMB_TB
    t = pl.program_id(0)
    slot = t & 1

    def row_copy(p_ref, s, i, k):
        return pltpu.make_async_copy(ys_ref.at[pl.ds(p_ref[0, 2 * i + k], 1)], buf_ref.at[s, k, pl.ds(i, 1)],
                                     sem.at[s])

    def issue(p_ref, s):
        def body(i, _):
            row_copy(p_ref, s, i, 0).start(priority=0)
            row_copy(p_ref, s, i, 1).start(priority=1)
            return 0

        lax.fori_loop(0, tb, body, 0, unroll=DMA_UNROLL)

    def drain(i, _):
        row_copy(pos_ref, slot, i, 0).wait()
        row_copy(pos_ref, slot, i, 1).wait()
        return 0

    pl.when(t == 0)(functools.partial(issue, pos_ref, 0))
    pl.when(t + 1 < pl.num_programs(0))(functools.partial(issue, pos_next_ref, 1 - slot))
    lax.fori_loop(0, tb, drain, 0, unroll=DMA_UNROLL)
    r = r_ref[...]
    y = r[:, R_W0:R_W0 + 1] * buf_ref[slot, 0] + r[:, R_W1:R_W1 + 1] * buf_ref[slot, 1]
    x2 = x1_ref[...] + g2_ref[...] * y
    if final:
        x2 = (x2 * lax.rsqrt(jnp.mean(x2 * x2, axis=-1, keepdims=True) + EPS)) * fg_ref[...]
    o_ref[...] = x2


def _combine(pos2, ys, rslab, x1, g2, final_g, seq, final):
    n = x1.shape[0]
    tb = COMB_TB
    per_b = seq // tb
    return pl.pallas_call(
        functools.partial(_combine_kernel, final=final),
        grid=(n // tb,),
        in_specs=[
            pl.BlockSpec((None, 1, 2 * tb), lambda i: (i, 0, 0), memory_space=pltpu.SMEM),
            pl.BlockSpec((None, 1, 2 * tb), lambda i: (jnp.minimum(i + 1, n // tb - 1), 0, 0),
                         memory_space=pltpu.SMEM),
            pl.BlockSpec(memory_space=pl.ANY),
            pl.BlockSpec((tb, LANES), lambda i: (i, 0)),
            pl.BlockSpec((tb, D_MODEL), lambda i: (i, 0)),
            pl.BlockSpec((None, 1, D_MODEL), lambda i: (i // per_b, 0, 0)),
            pl.BlockSpec((1, D_MODEL), lambda i: (0, 0)),
        ],
        out_specs=pl.BlockSpec((tb, D_MODEL), lambda i: (i, 0)),
        out_shape=jax.ShapeDtypeStruct((n, D_MODEL), F32),
        scratch_shapes=[pltpu.VMEM((2, 2, tb, D_MODEL), F32), pltpu.SemaphoreType.DMA((2,))],
        compiler_params=_cparams(("arbitrary",)),
        name="combine",
    )(pos2, pos2, ys, rslab, x1, g2, final_g)


def _layer(x, c, w_ada, b_ada, norm1_g, w_in, conv_w, a_log, dt_bias, dn_onorm_g, fox_f_bias,
           w_out, norm2_g, w_rg, b_rg, w_re, b_re, w1, w3, w2, final_g, final):
    bsz, seq, d = x.shape
    n = bsz * seq
    x2 = x.reshape(n, d)

    mod = _adaln(c, w_ada, b_ada)
    sh1, sc1, g1, sh2, sc2, g2 = [m.reshape(bsz, 1, d) for m in jnp.split(mod, 6, axis=-1)]

    w_t = w_in.T
    proj, gates = _inproj(x2, sc1, sh1, norm1_g.reshape(1, d), w_t, seq)
    proj3 = proj.reshape(bsz, seq, MAIN_W)

    def lane_row(vals, off):
        return jnp.zeros((1, LANES), F32).at[0, off:off + HEADS].set(vals)

    slab = _gates(gates.reshape(bsz, seq, LANES), lane_row(a_log, 0), lane_row(dt_bias, 0),
                  lane_row(fox_f_bias, L_F))
    nc = seq // CHUNK
    gct = slab[:, :, L_GC:L_GC + HEADS].reshape(bsz, nc, CHUNK, HEADS).transpose(0, 1, 3, 2)
    gct = gct.reshape(bsz, nc, HEADS, 1, CHUNK)
    f_rows = slab[:, :, L_F:L_F + HEADS].transpose(0, 2, 1).reshape(bsz, HEADS, seq // FOX_T, 1, FOX_T)

    o_dn = _deltanet(proj3, conv_w, slab, gct, dn_onorm_g.reshape(1, HEAD_DIM))
    o_fx = _fox(proj3, f_rows)

    wr = jnp.zeros((d, LANES), F32).at[:, :N_GROUPS].set(w_rg).at[:, N_GROUPS:N_GROUPS + N_EXPERTS].set(w_re)
    br = jnp.zeros((1, LANES), F32).at[0, :N_GROUPS].set(b_rg).at[0, N_GROUPS:N_GROUPS + N_EXPERTS].set(b_re)
    wr_hi_lo = jnp.concatenate(_split_bf16(wr), axis=1)
    w_out_b = w_out.astype(BF16)
    x1, h2, rslab, counts = _outproj(o_dn.reshape(n, HEAD_W), o_fx.reshape(n, HEAD_W), w_out_b[:HEAD_W],
                                     w_out_b[HEAD_W:], x2, g1, sc2, sh2, norm2_g.reshape(1, d), wr_hi_lo, br, seq)

    pos = _route(rslab, counts)[:, 0:2]

    cnt = counts[0, :N_EXPERTS].astype(jnp.int32)
    tiles_per = (cnt + MOE_TM - 1) // MOE_TM
    tile_end = jnp.cumsum(tiles_per)
    base = (tile_end - tiles_per) * MOE_TM
    n_tiles = (2 * n) // MOE_TM + N_EXPERTS
    p_rows = n_tiles * MOE_TM
    tid = jnp.arange(n_tiles, dtype=jnp.int32)
    tile_valid = (tid < tile_end[-1]).astype(jnp.int32)
    te_raw = jnp.minimum(jnp.sum(tid[:, None] >= tile_end[None, :], axis=1), N_EXPERTS - 1).astype(jnp.int32)
    last_e = te_raw[jnp.maximum(tile_end[-1] - 1, 0)]
    tile_expert = jnp.where(tile_valid == 1, te_raw, last_e)
    tile_first = (jnp.concatenate([jnp.array([-1], jnp.int32), tile_expert[:-1]]) != tile_expert).astype(jnp.int32)
    pad_start = base + cnt
    pad_len = tiles_per * MOE_TM - cnt
    eid = jnp.arange(N_EXPERTS, dtype=jnp.int32)
    has = tiles_per > 0
    slot_e = (jnp.cumsum(has.astype(jnp.int32)) - 1) & 1
    later = jnp.where(has[None, :] & (eid[None, :] > eid[:, None]), eid[None, :], N_EXPERTS)
    next_e = jnp.min(later, axis=1)
    next_e = jnp.where(next_e < N_EXPERTS, next_e, -1).astype(jnp.int32)
    tile_slot = slot_e[tile_expert].astype(jnp.int32)
    tile_next = next_e[tile_expert]

    xs = _dispatch(pad_start, pad_len, tile_end[-1:], pos.reshape(n // DISP_TB, 1, 2 * DISP_TB), h2, p_rows)
    ys = _experts(tile_expert, tile_valid, tile_first, tile_slot, tile_next, xs, w1.reshape(N_EXPERTS, d, D_EXPERT),
                  w3.reshape(N_EXPERTS, d, D_EXPERT), w2.reshape(N_EXPERTS, D_EXPERT, d))
    out = _combine(pos.reshape(n // COMB_TB, 1, 2 * COMB_TB), ys, rslab, x1, g2, final_g.reshape(1, d), seq, final)
    return out.reshape(bsz, seq, d)


def kernel(x, c, w_ada, b_ada, norm1_g, w_in, conv_w, a_log, dt_bias, dn_onorm_g, fox_f_bias, w_out, norm2_g,
           w_router_group, b_router_group, w_router_expert, b_router_expert, w1, w3, w2, final_g):
    depth = w_ada.shape[0]
    for l in range(depth):
        x = _layer(x, c, w_ada[l], b_ada[l], norm1_g[l], w_in[l], conv_w[l], a_log[l], dt_bias[l], dn_onorm_g[l],
                   fox_f_bias[l], w_out[l], norm2_g[l], w_router_group[l], b_router_group[l], w_router_expert[l],
                   b_router_expert[l], w1[l], w3[l], w2[l], final_g, l == depth - 1)
    return x
```

```python
import functools

import jax
import jax.numpy as jnp
from jax import lax
from jax.experimental import pallas as pl
from jax.experimental.pallas import tpu as pltpu

F32 = jnp.float32
BF16 = jnp.bfloat16

D_MODEL = 2048
EPS = 1e-6
CHUNK = 64
HEADS = 8
HEAD_DIM = 128
HEAD_W = HEADS * HEAD_DIM
CONV_K = 4
N_GROUPS = 4
EXPERTS_PER_GROUP = 8
N_EXPERTS = N_GROUPS * EXPERTS_PER_GROUP
D_EXPERT = 512
LANES = 128
SUBLANES = 8
MAIN_W = 7 * HEAD_W
VMEM_LIMIT = 56 * 1024 * 1024

L_GC, L_BETA, L_F, L_EGC, L_EK, L_ELAST = 0, 8, 16, 24, 32, 40


def _cparams(sem):
    return pltpu.CompilerParams(dimension_semantics=sem, vmem_limit_bytes=VMEM_LIMIT)


def _split_bf16(a):
    hi = a.astype(BF16)
    lo = (a - hi.astype(F32)).astype(BF16)
    return hi, lo


def _dot(a, b):
    return jnp.dot(a, b, preferred_element_type=F32)


def _dot_nt(a, b):
    return lax.dot_general(a, b, (((1,), (1,)), ((), ())), preferred_element_type=F32)


def _dot3(a, b):
    ah, al = _split_bf16(a)
    bh, bl = _split_bf16(b)
    return _dot(ah, bh) + (_dot(al, bh) + _dot(ah, bl))


def _dot3_pre(a, b_hi_lo):
    ah, al = _split_bf16(a)
    n = b_hi_lo.shape[1] // 2
    both = _dot(ah, b_hi_lo)
    return both[:, :n] + (both[:, n:] + _dot(al, b_hi_lo[:, :n]))


def _softplus(x):
    return jnp.maximum(x, 0.0) + jnp.log1p(jnp.exp(-jnp.abs(x)))


def _silu(x):
    return x * jax.nn.sigmoid(x)


def _adaln_kernel(c_ref, w_ref, b_ref, o_ref):
    c = c_ref[...]
    o_ref[...] = _dot(_silu(c).astype(BF16), w_ref[...].astype(BF16)) + b_ref[...]


def _adaln(c, w, b):
    bsz = c.shape[0]
    n = w.shape[1]
    tn = 1024
    cp = jnp.zeros((SUBLANES, D_MODEL), F32).at[:bsz].set(c)
    out = pl.pallas_call(
        _adaln_kernel,
        grid=(n // tn,),
        in_specs=[
            pl.BlockSpec((SUBLANES, D_MODEL), lambda j: (0, 0)),
            pl.BlockSpec((D_MODEL, tn), lambda j: (0, j)),
            pl.BlockSpec((1, tn), lambda j: (0, j)),
        ],
        out_specs=pl.BlockSpec((SUBLANES, tn), lambda j: (0, j)),
        out_shape=jax.ShapeDtypeStruct((SUBLANES, n), F32),
        compiler_params=_cparams(("parallel",)),
        name="adaln",
    )(cp, w, b.reshape(1, n))
    return out[:bsz]


INPROJ_TM = 1024
INPROJ_TN = 512
ROW_STEP = 128


def _modulated_norm(x, g, sc, sh):
    r = lax.rsqrt(jnp.mean(x * x, axis=-1, keepdims=True) + EPS)
    return (x * r) * (g * (1.0 + sc)) + sh


def _inproj_kernel(x_ref, sc_ref, sh_ref, g_ref, w_ref, wab_ref, wf_ref, o_ref, og_ref, h_ref, ws_ref):
    @pl.when(pl.program_id(1) == 0)
    def _():
        gate_rows = jnp.concatenate([wab_ref[...], wf_ref[...]], axis=0)
        pad = jnp.zeros((LANES - gate_rows.shape[0], D_MODEL), F32)
        hi, lo = _split_bf16(jnp.concatenate([gate_rows, pad], axis=0))
        ws_ref[0:LANES, :] = hi
        ws_ref[LANES:, :] = lo

        def body(i, _):
            rows = pl.ds(pl.multiple_of(i * ROW_STEP, ROW_STEP), ROW_STEP)
            h = _modulated_norm(x_ref[rows, :], g_ref[...], sc_ref[...], sh_ref[...])
            hb = h.astype(BF16)
            h_ref[rows, :] = hb
            both = _dot_nt(hb, ws_ref[...])
            og_ref[rows, :] = both[:, :LANES] + both[:, LANES:]
            return 0

        lax.fori_loop(0, INPROJ_TM // ROW_STEP, body, 0)

    o_ref[...] = _dot_nt(h_ref[...], w_ref[...].astype(BF16)).astype(BF16)


def _inproj(x2, sc1, sh1, g, w_t, seq):
    n = x2.shape[0]
    tm, tn = INPROJ_TM, INPROJ_TN
    per_b = seq // tm
    first_part = 4 * HEAD_W // tn
    skip = 2 * HEADS

    def w_rows(i, j):
        return (SUBLANES * (j * (tn // SUBLANES) + jnp.where(j >= first_part, skip // SUBLANES, 0)), 0)

    return pl.pallas_call(
        _inproj_kernel,
        grid=(n // tm, MAIN_W // tn),
        in_specs=[
            pl.BlockSpec((tm, D_MODEL), lambda i, j: (i, 0)),
            pl.BlockSpec((None, 1, D_MODEL), lambda i, j: (i // per_b, 0, 0)),
            pl.BlockSpec((None, 1, D_MODEL), lambda i, j: (i // per_b, 0, 0)),
            pl.BlockSpec((1, D_MODEL), lambda i, j: (0, 0)),
            pl.BlockSpec((pl.Element(tn), pl.Element(D_MODEL)), w_rows),
            pl.BlockSpec((pl.Element(skip), pl.Element(D_MODEL)), lambda i, j: (first_part * tn, 0)),
            pl.BlockSpec((pl.Element(HEADS), pl.Element(D_MODEL)), lambda i, j: (MAIN_W + skip, 0)),
        ],
        out_specs=[
            pl.BlockSpec((tm, tn), lambda i, j: (i, j)),
            pl.BlockSpec((tm, LANES), lambda i, j: (i, 0)),
        ],
        out_shape=[
            jax.ShapeDtypeStruct((n, MAIN_W), BF16),
            jax.ShapeDtypeStruct((n, LANES), F32),
        ],
        scratch_shapes=[pltpu.VMEM((tm, D_MODEL), BF16), pltpu.VMEM((2 * LANES, D_MODEL), BF16)],
        compiler_params=_cparams(("parallel", "arbitrary")),
        name="inproj",
    )(x2, sc1, sh1, g, w_t, w_t, w_t)


GATES_TB = 256


def _split3(a):
    p0 = a.astype(BF16)
    r1 = a - p0.astype(F32)
    p1 = r1.astype(BF16)
    p2 = (r1 - p1.astype(F32)).astype(BF16)
    return p0, p1, p2


def _dot_ones(m, a):
    p0, p1, p2 = _split3(a)
    return _dot(m, p0) + (_dot(m, p1) + _dot(m, p2))


def _gates_kernel(x_ref, alog_ref, dt_ref, fb_ref, o_ref, carry_ref):
    tb = GATES_TB

    @pl.when(pl.program_id(1) == 0)
    def _():
        carry_ref[...] = jnp.zeros_like(carry_ref)

    x = x_ref[...]
    lane = lax.broadcasted_iota(jnp.int32, (tb, LANES), 1)
    g = -jnp.exp(alog_ref[...]) * _softplus(x + dt_ref[...])
    beta = jax.nn.sigmoid(x)
    lf = -_softplus(-(x + fb_ref[...]))

    ri = lax.broadcasted_iota(jnp.int32, (tb, tb), 0)
    ci = lax.broadcasted_iota(jnp.int32, (tb, tb), 1)
    same_chunk = (ri // CHUNK) == (ci // CHUNK)
    tri = (ri >= ci)
    m_all = jnp.where(tri, 1.0, 0.0).astype(BF16)
    m_chunk = jnp.where(tri & same_chunk, 1.0, 0.0).astype(BF16)
    m_tot = jnp.where(same_chunk, 1.0, 0.0).astype(BF16)

    gc = _dot_ones(m_chunk, g)
    glast = _dot_ones(m_tot, g)
    fcum = _dot_ones(m_all, lf) + carry_ref[...]
    carry_ref[...] = fcum[tb - 1:tb, :]

    in_a = lane < HEADS
    egc = jnp.where(in_a, jnp.exp(gc), 0.0)
    ek = jnp.where(in_a, jnp.exp(glast - gc), 0.0)
    elast = jnp.where(in_a, jnp.exp(glast), 0.0)
    out = jnp.where(in_a, gc, jnp.where(lane < 2 * HEADS, beta, jnp.where(lane < 3 * HEADS, fcum, 0.0)))
    out = out + pltpu.roll(egc, L_EGC, 1) + pltpu.roll(ek, L_EK, 1) + pltpu.roll(elast, L_ELAST, 1)
    o_ref[...] = out


def _gates(gates, alog_row, dt_row, fb_row):
    bsz, seq, _ = gates.shape
    tb = GATES_TB
    row = pl.BlockSpec((1, LANES), lambda b, t: (0, 0))
    return pl.pallas_call(
        _gates_kernel,
        grid=(bsz, seq // tb),
        in_specs=[pl.BlockSpec((None, tb, LANES), lambda b, t: (b, t, 0)), row, row, row],
        out_specs=pl.BlockSpec((None, tb, LANES), lambda b, t: (b, t, 0)),
        out_shape=jax.ShapeDtypeStruct((bsz, seq, LANES), F32),
        scratch_shapes=[pltpu.VMEM((1, LANES), F32)],
        compiler_params=_cparams(("parallel", "arbitrary")),
        name="gates",
    )(gates, alog_row, dt_row, fb_row)


DN_TB = 256
DN_GROUP = 4
HALO = SUBLANES


def _bdot(a, b):
    return lax.dot_general(a, b, (((2,), (1,)), ((0,), (0,))), preferred_element_type=F32)


def _bdot_nt(a, b):
    return lax.dot_general(a, b, (((2,), (2,)), ((0,), (0,))), preferred_element_type=F32)


def _inv_unit_lower(a, eye, blk16, blk32):
    n = jnp.where(blk16, -a, 0.0)
    e1 = jnp.where(blk32 & jnp.logical_not(blk16), a, 0.0).astype(BF16)
    e2 = jnp.where(blk32, 0.0, a).astype(BF16)
    t = eye + n
    p = n.astype(BF16)
    for _ in range(3):
        p = _bdot(p, p).astype(BF16)
        t = t + _bdot(t.astype(BF16), p)
    for e in (e1, e2):
        tb = t.astype(BF16)
        t = t - _bdot(_bdot(tb, e).astype(BF16), tb)
    return t


def _deltanet_kernel(q_ref, k_ref, v_ref, z_ref, wq_ref, wk_ref, wv_ref, slab_ref, gct_ref, og_ref,
                     o_ref, ext_ref, qn_ref, kn_ref, vv_ref, s_ref):
    tb = DN_TB

    @pl.when(pl.program_id(1) == 0)
    def _():
        ext_ref[:, 0:HALO, :] = jnp.zeros((3, HALO, HEAD_W), F32)
        s_ref[...] = jnp.zeros_like(s_ref)

    for idx, (u_ref, w_ref) in enumerate(((q_ref, wq_ref), (k_ref, wk_ref), (v_ref, wv_ref))):
        for h in range(HEADS):
            cols = slice(h * HEAD_DIM, (h + 1) * HEAD_DIM)
            ext_ref[idx, HALO:HALO + tb, cols] = u_ref[:, cols].astype(F32)
            y = None
            for j in range(CONV_K):
                start = HALO - (CONV_K - 1) + j
                term = ext_ref[idx, start:start + tb, cols] * w_ref[j:j + 1, cols]
                y = term if y is None else y + term
            y = _silu(y)
            if idx == 2:
                vv_ref[h] = y
            else:
                yn = y * lax.rsqrt(jnp.sum(y * y, axis=-1, keepdims=True) + EPS)
                if idx == 0:
                    qn_ref[h] = (yn * (HEAD_DIM ** -0.5)).astype(BF16)
                else:
                    kn_ref[h] = yn.astype(BF16)
        ext_ref[idx, 0:HALO, :] = ext_ref[idx, tb:tb + HALO, :]

    ri = lax.broadcasted_iota(jnp.int32, (DN_GROUP * HEADS, CHUNK, CHUNK), 1)
    ci = lax.broadcasted_iota(jnp.int32, (DN_GROUP * HEADS, CHUNK, CHUNK), 2)
    incl = ri >= ci
    strict = ri > ci
    eye = jnp.where(ri == ci, 1.0, 0.0)
    blk16 = (ri // 16) == (ci // 16)
    blk32 = (ri // 32) == (ci // 32)

    def group_body(c, _):
        rows = [pl.ds(pl.multiple_of((c * DN_GROUP + g) * CHUNK, CHUNK), CHUNK) for g in range(DN_GROUP)]
        slabs = [slab_ref[r, :] for r in rows]

        def col(off, width):
            return jnp.stack([jnp.broadcast_to(sl[:, off + h:off + h + 1], (CHUNK, width))
                              for sl in slabs for h in range(HEADS)])

        def grouped(ref):
            return jnp.concatenate([ref[:, r, :] for r in rows], axis=0)

        q = grouped(qn_ref)
        k = grouped(kn_ref)
        v = grouped(vv_ref)
        beta = col(L_BETA, HEAD_DIM)
        egc = col(L_EGC, HEAD_DIM)
        gc_row = jnp.concatenate([gct_ref[c * DN_GROUP + g] for g in range(DN_GROUP)], axis=0)

        decay = jnp.where(incl, jnp.exp(col(L_GC, CHUNK) - gc_row), 0.0)
        kk = _bdot_nt(k, k)
        qk = (_bdot_nt(q, k) * decay).astype(BF16)
        a = jnp.where(strict, kk * decay * beta[:, :, :CHUNK], 0.0)
        t = _inv_unit_lower(a, eye, blk16, blk32).astype(BF16)

        kf = k.astype(F32)
        vb = (v * beta).astype(BF16)
        kbg = (kf * (beta * egc)).astype(BF16)
        u = _bdot(t, vb)
        w = _bdot(t, kbg).astype(BF16)
        qd = (q.astype(F32) * egc).astype(BF16)
        kd = kf * col(L_EK, HEAD_DIM)
        kdt = jnp.stack([kd[n].T for n in range(DN_GROUP * HEADS)]).astype(BF16)

        for g in range(DN_GROUP):
            sel = slice(g * HEADS, (g + 1) * HEADS)
            s = s_ref[...]
            sb = s.astype(BF16)
            vnb = (u[sel] - _bdot(w[sel], sb)).astype(BF16)
            o = _bdot(qd[sel], sb) + _bdot(qk[sel], vnb)
            elast = jnp.stack([jnp.broadcast_to(slabs[g][CHUNK - 1:CHUNK, L_ELAST + h:L_ELAST + h + 1],
                                                (HEAD_DIM, HEAD_DIM)) for h in range(HEADS)])
            s_ref[...] = s * elast + _bdot(kdt[sel], vnb)

            r = lax.rsqrt(jnp.mean(o * o, axis=-1, keepdims=True) + EPS)
            on = (o * r) * og_ref[...]
            for h in range(HEADS):
                cols = slice(h * HEAD_DIM, (h + 1) * HEAD_DIM)
                o_ref[rows[g], cols] = (on[h] * _silu(z_ref[rows[g], cols].astype(F32))).astype(BF16)
        return 0

    lax.fori_loop(0, tb // (CHUNK * DN_GROUP), group_body, 0)


def _deltanet(proj3, conv_w, slab, gct, onorm_g):
    bsz, seq, _ = proj3.shape
    tb = DN_TB
    nct = tb // CHUNK

    def colblk(j):
        return pl.BlockSpec((None, tb, HEAD_W), lambda b, t: (b, t, j))

    def wblk(j):
        return pl.BlockSpec((CONV_K, HEAD_W), lambda b, t: (0, j))

    return pl.pallas_call(
        _deltanet_kernel,
        grid=(bsz, seq // tb),
        in_specs=[
            colblk(0), colblk(1), colblk(2), colblk(3),
            wblk(0), wblk(1), wblk(2),
            pl.BlockSpec((None, tb, LANES), lambda b, t: (b, t, 0)),
            pl.BlockSpec((None, nct, HEADS, 1, CHUNK), lambda b, t: (b, t, 0, 0, 0)),
            pl.BlockSpec((1, HEAD_DIM), lambda b, t: (0, 0)),
        ],
        out_specs=pl.BlockSpec((None, tb, HEAD_W), lambda b, t: (b, t, 0)),
        out_shape=jax.ShapeDtypeStruct((bsz, seq, HEAD_W), BF16),
        scratch_shapes=[
            pltpu.VMEM((3, tb + HALO, HEAD_W), F32),
            pltpu.VMEM((HEADS, tb, HEAD_DIM), BF16),
            pltpu.VMEM((HEADS, tb, HEAD_DIM), BF16),
            pltpu.VMEM((HEADS, tb, HEAD_DIM), F32),
            pltpu.VMEM((HEADS, HEAD_DIM, HEAD_DIM), F32),
        ],
        compiler_params=_cparams(("parallel", "arbitrary")),
        name="deltanet",
    )(proj3, proj3, proj3, proj3, conv_w, conv_w, conv_w, slab, gct, onorm_g)


FOX_T = 512
FOX_STRIP = 32
NEG_INF = float("-inf")
LOG2E = 1.4426950408889634


def _fox_kernel(q_ref, k_ref, v_ref, f_ref, o_ref, s_a_ref, s_b_ref, p_a_ref, p_b_ref, m_ref, l_ref, alpha_ref,
                acc_ref):
    tq = FOX_T
    i = pl.program_id(2)
    lane_blocks = tq // LANES
    qs = (q_ref[...].astype(F32) * (HEAD_DIM ** -0.5 * LOG2E)).astype(BF16)
    f0 = f_ref[i][:, 0:1]
    m_ref[...] = jnp.full_like(m_ref, NEG_INF)
    l_ref[...] = jnp.zeros_like(l_ref)
    acc_ref[...] = jnp.zeros_like(acc_ref)

    def keys(j):
        return pl.ds(pl.multiple_of(j * tq, tq), tq)

    def scores(j):
        return _dot_nt(qs, k_ref[keys(j), :])

    def softmax(j, s_view, p_view, masked):
        bias = (f0 - f_ref[j]) * LOG2E

        def logits(r):
            rows = slice(r * FOX_STRIP, (r + 1) * FOX_STRIP)
            s = s_view[rows, :] + bias
            if masked:
                ri = r * FOX_STRIP + lax.broadcasted_iota(jnp.int32, (FOX_STRIP, tq), 0)
                ci = lax.broadcasted_iota(jnp.int32, (FOX_STRIP, tq), 1)
                s = jnp.where(ci <= ri, s, NEG_INF)
            return rows, s

        for r in range(tq // FOX_STRIP):
            rows, s = logits(r)
            m_old = m_ref[rows, :]
            m_new = jnp.maximum(m_old, jnp.max(s, axis=-1, keepdims=True))
            alpha_ref[rows, :] = jnp.exp2(m_old - m_new)
            m_ref[rows, :] = m_new
        for r in range(tq // FOX_STRIP):
            rows, s = logits(r)
            p = jnp.exp2(s - jnp.concatenate([m_ref[rows, :]] * lane_blocks, axis=1))
            alpha = alpha_ref[rows, :]
            l_ref[rows, :] = alpha * l_ref[rows, :] + jnp.sum(p, axis=-1, keepdims=True)
            p_view[rows, :] = p.astype(BF16)

    s_a_ref[...] = scores(i)
    s_b_ref[...] = scores(jnp.maximum(i - 1, 0))
    softmax(i, s_a_ref, p_a_ref, True)

    def step(t, s_cur, p_cur, s_nxt, p_prv):
        j = i - t
        s_nxt[...] = scores(jnp.maximum(j - 1, 0))
        pv = _dot(p_prv[...], v_ref[keys(j + 1), :])
        softmax(j, s_cur, p_cur, False)
        acc_ref[...] = alpha_ref[...] * (acc_ref[...] + pv)

    def pair(u):
        step(2 * u + 1, s_b_ref, p_b_ref, s_a_ref, p_a_ref)
        step(2 * u + 2, s_a_ref, p_a_ref, s_b_ref, p_b_ref)

    def quad(w, _):
        pair(2 * w)
        pair(2 * w + 1)
        return 0

    quads = lax.shift_right_logical(i, 2)
    lax.fori_loop(0, quads, quad, 0)
    pl.when((i & 2) != 0)(functools.partial(pair, 2 * quads))

    def finish(p_last):
        acc = acc_ref[...] + _dot(p_last[...], v_ref[keys(0), :])
        o_ref[...] = (acc / l_ref[...]).astype(BF16)

    @pl.when((i & 1) == 1)
    def _():
        step(i, s_b_ref, p_b_ref, s_a_ref, p_a_ref)
        finish(p_b_ref)

    @pl.when((i & 1) == 0)
    def _():
        finish(p_a_ref)


def _fox(proj3, f_rows):
    bsz, seq, _ = proj3.shape
    tq = FOX_T
    qb, kb, vb = 4 * HEADS, 5 * HEADS, 6 * HEADS
    return pl.pallas_call(
        _fox_kernel,
        grid=(bsz, HEADS, seq // tq),
        in_specs=[
            pl.BlockSpec((None, tq, HEAD_DIM), lambda b, h, i: (b, i, qb + h)),
            pl.BlockSpec((None, seq, HEAD_DIM), lambda b, h, i: (b, 0, kb + h)),
            pl.BlockSpec((None, seq, HEAD_DIM), lambda b, h, i: (b, 0, vb + h)),
            pl.BlockSpec((None, None, seq // tq, 1, tq), lambda b, h, i: (b, h, 0, 0, 0)),
        ],
        out_specs=pl.BlockSpec((None, tq, HEAD_DIM), lambda b, h, i: (b, i, h)),
        out_shape=jax.ShapeDtypeStruct((bsz, seq, HEAD_W), BF16),
        scratch_shapes=[
            pltpu.VMEM((tq, tq), F32),
            pltpu.VMEM((tq, tq), F32),
            pltpu.VMEM((tq, tq), BF16),
            pltpu.VMEM((tq, tq), BF16),
            pltpu.VMEM((tq, LANES), F32),
            pltpu.VMEM((tq, LANES), F32),
            pltpu.VMEM((tq, LANES), F32),
            pltpu.VMEM((tq, HEAD_DIM), F32),
        ],
        compiler_params=_cparams(("parallel", "parallel", "arbitrary")),
        name="fox",
    )(proj3, proj3, proj3, f_rows)


OUT_TM = 512
R_E0, R_E1, R_W0, R_W1 = 0, 1, 2, 3


def _first_argmax(vals, lane):
    m = jnp.max(vals, axis=-1, keepdims=True)
    idx = jnp.min(jnp.where(vals == m, lane, LANES), axis=-1, keepdims=True)
    return m, idx


def _outproj_kernel(oa_ref, ob_ref, wa_ref, wb_ref, x_ref, g1_ref, sc_ref, sh_ref, g_ref,
                    wr_ref, br_ref, x1_ref, h2_ref, r_ref, cnt_ref):
    mix = _dot(oa_ref[...], wa_ref[...]) + _dot(ob_ref[...], wb_ref[...])
    x1 = x_ref[...] + g1_ref[...] * mix
    x1_ref[...] = x1
    h2 = _modulated_norm(x1, g_ref[...], sc_ref[...], sh_ref[...])
    h2_ref[...] = h2

    logits = _dot3_pre(h2, wr_ref[...]) + br_ref[...]
    tm = logits.shape[0]
    lane = lax.broadcasted_iota(jnp.int32, (tm, LANES), 1)
    gl = jnp.where(lane < N_GROUPS, logits, NEG_INF)
    gmax, gidx = _first_argmax(gl, lane)
    pg = 1.0 / jnp.sum(jnp.exp(gl - gmax), axis=-1, keepdims=True)
    e_lane = lane - N_GROUPS
    in_grp = (e_lane >= gidx * EXPERTS_PER_GROUP) & (e_lane < (gidx + 1) * EXPERTS_PER_GROUP)
    el = jnp.where(in_grp, logits, NEG_INF)
    v0, i0 = _first_argmax(el, lane)
    v1, i1 = _first_argmax(jnp.where(lane == i0, NEG_INF, el), lane)
    ex = jnp.exp(v1 - v0)
    w0 = pg / (1.0 + ex)
    w1 = pg * ex / (1.0 + ex)
    e0 = (i0 - N_GROUPS).astype(F32)
    e1 = (i1 - N_GROUPS).astype(F32)
    r_ref[...] = jnp.where(lane == R_E0, e0, jnp.where(lane == R_E1, e1,
                           jnp.where(lane == R_W0, w0, jnp.where(lane == R_W1, w1, 0.0))))

    @pl.when(pl.program_id(0) == 0)
    def _():
        cnt_ref[...] = jnp.zeros_like(cnt_ref)

    picked = (lane == i0 - N_GROUPS) | (lane == i1 - N_GROUPS)
    cnt_ref[...] += jnp.sum(jnp.where(picked, 1.0, 0.0), axis=0, keepdims=True)


def _outproj(o_a, o_b, wa, wb, x2, g1, sc2, sh2, g, wr_hi_lo, br, seq):
    n = x2.shape[0]
    tm = OUT_TM
    per_b = seq // tm
    modrow = pl.BlockSpec((None, 1, D_MODEL), lambda i: (i // per_b, 0, 0))
    const = lambda shape: pl.BlockSpec(shape, lambda i: (0, 0))
    return pl.pallas_call(
        _outproj_kernel,
        grid=(n // tm,),
        in_specs=[
            pl.BlockSpec((tm, HEAD_W), lambda i: (i, 0)),
            pl.BlockSpec((tm, HEAD_W), lambda i: (i, 0)),
            const((HEAD_W, D_MODEL)), const((HEAD_W, D_MODEL)),
            pl.BlockSpec((tm, D_MODEL), lambda i: (i, 0)),
            modrow, modrow, modrow,
            const((1, D_MODEL)),
            const((D_MODEL, 2 * LANES)), const((1, LANES)),
        ],
        out_specs=[
            pl.BlockSpec((tm, D_MODEL), lambda i: (i, 0)),
            pl.BlockSpec((tm, D_MODEL), lambda i: (i, 0)),
            pl.BlockSpec((tm, LANES), lambda i: (i, 0)),
            pl.BlockSpec((1, LANES), lambda i: (0, 0)),
        ],
        out_shape=[
            jax.ShapeDtypeStruct((n, D_MODEL), F32),
            jax.ShapeDtypeStruct((n, D_MODEL), F32),
            jax.ShapeDtypeStruct((n, LANES), F32),
            jax.ShapeDtypeStruct((1, LANES), F32),
        ],
        compiler_params=_cparams(("arbitrary",)),
        name="outproj",
    )(o_a, o_b, wa, wb, x2, g1, sc2, sh2, g, wr_hi_lo, br)


MOE_TM = 256
ROUTE_TB = 512


def _route_kernel(r_ref, cnt_ref, pos_ref, run_ref, base_ref):
    tb = ROUTE_TB
    t = pl.program_id(0)
    r = r_ref[...]
    lane = lax.broadcasted_iota(jnp.int32, (tb, LANES), 1)
    e0 = r[:, R_E0:R_E0 + 1].astype(jnp.int32)
    e1 = r[:, R_E1:R_E1 + 1].astype(jnp.int32)
    oh0 = lane == e0
    oh1 = lane == e1
    both = jnp.where(oh0 | oh1, 1.0, 0.0)

    @pl.when(t == 0)
    def _():
        counts = cnt_ref[...]
        padded = jnp.ceil(counts / MOE_TM) * MOE_TM
        li = lax.broadcasted_iota(jnp.int32, (LANES, LANES), 0)
        lj = lax.broadcasted_iota(jnp.int32, (LANES, LANES), 1)
        upper = jnp.where(li < lj, 1.0, 0.0).astype(BF16)
        hi = jnp.floor(padded / 256.0)
        lo = padded - hi * 256.0
        hi8 = jnp.broadcast_to(hi, (SUBLANES, LANES)).astype(BF16)
        lo8 = jnp.broadcast_to(lo, (SUBLANES, LANES)).astype(BF16)
        base = _dot(hi8, upper) * 256.0 + _dot(lo8, upper)
        base_ref[...] = base[0:1, :]
        run_ref[...] = jnp.zeros_like(run_ref)

    ri = lax.broadcasted_iota(jnp.int32, (tb, tb), 0)
    ci = lax.broadcasted_iota(jnp.int32, (tb, tb), 1)
    strict = jnp.where(ri > ci, 1.0, 0.0).astype(BF16)
    before = _dot(strict, both.astype(BF16)) + run_ref[...] + base_ref[...]
    p0 = jnp.sum(jnp.where(oh0, before, 0.0), axis=-1, keepdims=True)
    p1 = jnp.sum(jnp.where(oh1, before, 0.0), axis=-1, keepdims=True)
    pos_ref[...] = jnp.where(lane == 0, p0, jnp.where(lane == 1, p1, 0.0)).astype(jnp.int32)
    run_ref[...] = run_ref[...] + jnp.sum(both, axis=0, keepdims=True)


def _route(rslab, counts):
    n = rslab.shape[0]
    tb = ROUTE_TB
    return pl.pallas_call(
        _route_kernel,
        grid=(n // tb,),
        in_specs=[pl.BlockSpec((tb, LANES), lambda t: (t, 0)), pl.BlockSpec((1, LANES), lambda t: (0, 0))],
        out_specs=pl.BlockSpec((tb, LANES), lambda t: (t, 0)),
        out_shape=jax.ShapeDtypeStruct((n, LANES), jnp.int32),
        scratch_shapes=[pltpu.VMEM((1, LANES), F32), pltpu.VMEM((1, LANES), F32)],
        compiler_params=_cparams(("arbitrary",)),
        name="route",
    )(rslab, counts)


DISP_TB = 1024
DMA_UNROLL = 8


def _dispatch_kernel(pad_start_ref, pad_len_ref, used_ref, pos_ref, h2_ref, xs_ref, zero_ref, sem, zsem):
    t = pl.program_id(0)
    tb = DISP_TB
    n_tiles = xs_ref.shape[0] // MOE_TM

    def row_copy(g, r, k):
        dst_row = pos_ref[0, 2 * SUBLANES * g + 2 * r + k]
        return pltpu.make_async_copy(h2_ref.at[g, pl.ds(r, 1)], xs_ref.at[pl.ds(dst_row, 1)], sem)

    def issue(g, _):
        for r in range(SUBLANES):
            row_copy(g, r, 0).start(priority=0)
            row_copy(g, r, 1).start(priority=1)
        return 0

    lax.fori_loop(0, tb // SUBLANES, issue, 0)

    @pl.when(t == 0)
    def _():
        zero_ref[...] = jnp.zeros_like(zero_ref)

        def zero_rows(wait, off, rows):
            cp = pltpu.make_async_copy(zero_ref.at[pl.ds(0, rows)], xs_ref.at[pl.ds(off, rows)], zsem)
            cp.wait() if wait else cp.start()

        def per_expert(wait, e, _):
            start = pad_start_ref[e]
            head = (-start) & (SUBLANES - 1)
            for r in range(SUBLANES - 1):
                pl.when(r < head)(functools.partial(zero_rows, wait, start + r, 1))
            off = start + head
            rest = pad_len_ref[e] - head
            piece = MOE_TM // 2
            while piece >= SUBLANES:
                take = (rest & piece) != 0
                pl.when(take)(functools.partial(zero_rows, wait, pl.multiple_of(off, SUBLANES), piece))
                off = off + jnp.where(take, piece, 0)
                piece //= 2
            return 0

        def per_tile(wait, i, _):
            zero_rows(wait, pl.multiple_of(i * MOE_TM, MOE_TM), MOE_TM)
            return 0

        for wait in (False, True):
            lax.fori_loop(0, N_EXPERTS, functools.partial(per_expert, wait), 0)
            lax.fori_loop(used_ref[0], n_tiles, functools.partial(per_tile, wait), 0)

    def drain(g, _):
        for r in range(SUBLANES):
            row_copy(g, r, 0).wait()
            row_copy(g, r, 1).wait()
        return 0

    lax.fori_loop(0, tb // SUBLANES, drain, 0)


def _dispatch(pad_start, pad_len, used, pos2, h2, p_rows):
    n = h2.shape[0]
    tb = DISP_TB
    return pl.pallas_call(
        _dispatch_kernel,
        grid_spec=pltpu.PrefetchScalarGridSpec(
            num_scalar_prefetch=3,
            grid=(n // tb,),
            in_specs=[
                pl.BlockSpec((None, 1, 2 * tb), lambda t, *_: (t, 0, 0), memory_space=pltpu.SMEM),
                pl.BlockSpec((tb // SUBLANES, SUBLANES, D_MODEL), lambda t, *_: (t, 0, 0)),
            ],
            out_specs=pl.BlockSpec(memory_space=pl.ANY),
            scratch_shapes=[
                pltpu.VMEM((MOE_TM, D_MODEL), F32),
                pltpu.SemaphoreType.DMA(()),
                pltpu.SemaphoreType.DMA(()),
            ],
        ),
        out_shape=jax.ShapeDtypeStruct((p_rows, D_MODEL), F32),
        compiler_params=_cparams(("arbitrary",)),
        name="dispatch",
    )(pad_start, pad_len, used, pos2, h2.reshape(n // SUBLANES, SUBLANES, D_MODEL))


def _experts_kernel(te_ref, tv_ref, tf_ref, ts_ref, tn_ref, xs_ref, w1_hbm, w3_hbm, w2_hbm, ys_ref,
                    w1f_ref, w3f_ref, w2f_ref, w1b_ref, w3b_ref, w2b_ref, sem):
    i = pl.program_id(0)

    def weight_copies(e, slot):
        return [pltpu.make_async_copy(src.at[e], dst.at[slot], sem.at[slot])
                for src, dst in ((w1_hbm, w1f_ref), (w3_hbm, w3f_ref), (w2_hbm, w2f_ref))]

    @pl.when(tv_ref[i] != 0)
    def _():
        @pl.when(tf_ref[i] != 0)
        def _():
            slot = ts_ref[i]

            @pl.when(i == 0)
            def _():
                for cp in weight_copies(te_ref[i], slot):
                    cp.start()

            for cp in weight_copies(te_ref[i], slot):
                cp.wait()

            @pl.when(tn_ref[i] >= 0)
            def _():
                for cp in weight_copies(tn_ref[i], 1 - slot):
                    cp.start()

            w1b_ref[...] = w1f_ref[slot].astype(BF16)
            w3b_ref[...] = w3f_ref[slot].astype(BF16)
            w2b_ref[...] = w2f_ref[slot].astype(BF16)

        x = xs_ref[...].astype(BF16)
        a = _dot(x, w1b_ref[...])
        b = _dot(x, w3b_ref[...])
        ys_ref[...] = _dot((_silu(a) * b).astype(BF16), w2b_ref[...])

    @pl.when(tv_ref[i] == 0)
    def _():
        ys_ref[...] = jnp.zeros_like(ys_ref)


def _experts(tile_expert, tile_valid, tile_first, tile_slot, tile_next, xs, w1, w3, w2):
    p_rows = xs.shape[0]
    tm = MOE_TM
    hbm = pl.BlockSpec(memory_space=pl.ANY)
    return pl.pallas_call(
        _experts_kernel,
        grid_spec=pltpu.PrefetchScalarGridSpec(
            num_scalar_prefetch=5,
            grid=(p_rows // tm,),
            in_specs=[pl.BlockSpec((tm, D_MODEL), lambda i, *_: (i, 0)), hbm, hbm, hbm],
            out_specs=pl.BlockSpec((tm, D_MODEL), lambda i, *_: (i, 0)),
            scratch_shapes=[
                pltpu.VMEM((2, D_MODEL, D_EXPERT), F32),
                pltpu.VMEM((2, D_MODEL, D_EXPERT), F32),
                pltpu.VMEM((2, D_EXPERT, D_MODEL), F32),
                pltpu.VMEM((D_MODEL, D_EXPERT), BF16),
                pltpu.VMEM((D_MODEL, D_EXPERT), BF16),
                pltpu.VMEM((D_EXPERT, D_MODEL), BF16),
                pltpu.SemaphoreType.DMA((2,)),
            ],
        ),
        out_shape=jax.ShapeDtypeStruct((p_rows, D_MODEL), F32),
        compiler_params=_cparams(("arbitrary",)),
        name="experts",
    )(tile_expert, tile_valid, tile_first, tile_slot, tile_next, xs, w1, w3, w2)


COMB_TB = 256


def _combine_kernel(pos_ref, pos_next_ref, ys_ref, r_ref, x1_ref, g2_ref, fg_ref, o_ref, buf_ref, sem, *, final):
    tb = COMB_TB
    t = pl.program_id(0)
    slot = t & 1

    def row_copy(p_ref, s, g, r, k):
        src_row = p_ref[0, 2 * SUBLANES * g + 2 * r + k]
        return pltpu.make_async_copy(ys_ref.at[pl.ds(src_row, 1)], buf_ref.at[s, k, g, pl.ds(r, 1)], sem.at[s])

    def issue(p_ref, s):
        def body(g, _):
            for r in range(SUBLANES):
                row_copy(p_ref, s, g, r, 0).start(priority=0)
                row_copy(p_ref, s, g, r, 1).start(priority=1)
            return 0

        lax.fori_loop(0, tb // SUBLANES, body, 0)

    def drain(g, _):
        for r in range(SUBLANES):
            row_copy(pos_ref, slot, g, r, 0).wait()
            row_copy(pos_ref, slot, g, r, 1).wait()
        return 0

    pl.when(t == 0)(functools.partial(issue, pos_ref, 0))
    pl.when(t + 1 < pl.num_programs(0))(functools.partial(issue, pos_next_ref, 1 - slot))
    lax.fori_loop(0, tb // SUBLANES, drain, 0)
    r = r_ref[...]
    y0 = buf_ref[slot, 0].reshape(tb, D_MODEL)
    y1 = buf_ref[slot, 1].reshape(tb, D_MODEL)
    y = r[:, R_W0:R_W0 + 1] * y0 + r[:, R_W1:R_W1 + 1] * y1
    x2 = x1_ref[...] + g2_ref[...] * y
    if final:
        x2 = (x2 * lax.rsqrt(jnp.mean(x2 * x2, axis=-1, keepdims=True) + EPS)) * fg_ref[...]
    o_ref[...] = x2


def _combine(pos2, ys, rslab, x1, g2, final_g, seq, final):
    n = x1.shape[0]
    tb = COMB_TB
    per_b = seq // tb
    return pl.pallas_call(
        functools.partial(_combine_kernel, final=final),
        grid=(n // tb,),
        in_specs=[
            pl.BlockSpec((None, 1, 2 * tb), lambda i: (i, 0, 0), memory_space=pltpu.SMEM),
            pl.BlockSpec((None, 1, 2 * tb), lambda i: (jnp.minimum(i + 1, n // tb - 1), 0, 0),
                         memory_space=pltpu.SMEM),
            pl.BlockSpec(memory_space=pl.ANY),
            pl.BlockSpec((tb, LANES), lambda i: (i, 0)),
            pl.BlockSpec((tb, D_MODEL), lambda i: (i, 0)),
            pl.BlockSpec((None, 1, D_MODEL), lambda i: (i // per_b, 0, 0)),
            pl.BlockSpec((1, D_MODEL), lambda i: (0, 0)),
        ],
        out_specs=pl.BlockSpec((tb, D_MODEL), lambda i: (i, 0)),
        out_shape=jax.ShapeDtypeStruct((n, D_MODEL), F32),
        scratch_shapes=[pltpu.VMEM((2, 2, tb // SUBLANES, SUBLANES, D_MODEL), F32), pltpu.SemaphoreType.DMA((2,))],
        compiler_params=_cparams(("arbitrary",)),
        name="combine",
    )(pos2, pos2, ys, rslab, x1, g2, final_g)


def _layer(x, c, w_ada, b_ada, norm1_g, w_in, conv_w, a_log, dt_bias, dn_onorm_g, fox_f_bias,
           w_out, norm2_g, w_rg, b_rg, w_re, b_re, w1, w3, w2, final_g, final):
    bsz, seq, d = x.shape
    n = bsz * seq
    x2 = x.reshape(n, d)

    mod = _adaln(c, w_ada, b_ada)
    sh1, sc1, g1, sh2, sc2, g2 = [m.reshape(bsz, 1, d) for m in jnp.split(mod, 6, axis=-1)]

    w_t = w_in.T
    proj, gates = _inproj(x2, sc1, sh1, norm1_g.reshape(1, d), w_t, seq)
    proj3 = proj.reshape(bsz, seq, MAIN_W)

    def lane_row(vals, off):
        return jnp.zeros((1, LANES), F32).at[0, off:off + HEADS].set(vals)

    slab = _gates(gates.reshape(bsz, seq, LANES), lane_row(a_log, 0), lane_row(dt_bias, 0),
                  lane_row(fox_f_bias, L_F))
    nc = seq // CHUNK
    gct = slab[:, :, L_GC:L_GC + HEADS].reshape(bsz, nc, CHUNK, HEADS).transpose(0, 1, 3, 2)
    gct = gct.reshape(bsz, nc, HEADS, 1, CHUNK)
    f_rows = slab[:, :, L_F:L_F + HEADS].transpose(0, 2, 1).reshape(bsz, HEADS, seq // FOX_T, 1, FOX_T)

    o_dn = _deltanet(proj3, conv_w, slab, gct, dn_onorm_g.reshape(1, HEAD_DIM))
    o_fx = _fox(proj3, f_rows)

    wr = jnp.zeros((d, LANES), F32).at[:, :N_GROUPS].set(w_rg).at[:, N_GROUPS:N_GROUPS + N_EXPERTS].set(w_re)
    br = jnp.zeros((1, LANES), F32).at[0, :N_GROUPS].set(b_rg).at[0, N_GROUPS:N_GROUPS + N_EXPERTS].set(b_re)
    wr_hi_lo = jnp.concatenate(_split_bf16(wr), axis=1)
    w_out_b = w_out.astype(BF16)
    x1, h2, rslab, counts = _outproj(o_dn.reshape(n, HEAD_W), o_fx.reshape(n, HEAD_W), w_out_b[:HEAD_W],
                                     w_out_b[HEAD_W:], x2, g1, sc2, sh2, norm2_g.reshape(1, d), wr_hi_lo, br, seq)

    pos = _route(rslab, counts)[:, 0:2]

    cnt = counts[0, :N_EXPERTS].astype(jnp.int32)
    tiles_per = (cnt + MOE_TM - 1) // MOE_TM
    tile_end = jnp.cumsum(tiles_per)
    base = (tile_end - tiles_per) * MOE_TM
    n_tiles = (2 * n) // MOE_TM + N_EXPERTS
    p_rows = n_tiles * MOE_TM
    tid = jnp.arange(n_tiles, dtype=jnp.int32)
    tile_valid = (tid < tile_end[-1]).astype(jnp.int32)
    te_raw = jnp.minimum(jnp.sum(tid[:, None] >= tile_end[None, :], axis=1), N_EXPERTS - 1).astype(jnp.int32)
    last_e = te_raw[jnp.maximum(tile_end[-1] - 1, 0)]
    tile_expert = jnp.where(tile_valid == 1, te_raw, last_e)
    tile_first = (jnp.concatenate([jnp.array([-1], jnp.int32), tile_expert[:-1]]) != tile_expert).astype(jnp.int32)
    pad_start = base + cnt
    pad_len = tiles_per * MOE_TM - cnt
    eid = jnp.arange(N_EXPERTS, dtype=jnp.int32)
    has = tiles_per > 0
    slot_e = (jnp.cumsum(has.astype(jnp.int32)) - 1) & 1
    later = jnp.where(has[None, :] & (eid[None, :] > eid[:, None]), eid[None, :], N_EXPERTS)
    next_e = jnp.min(later, axis=1)
    next_e = jnp.where(next_e < N_EXPERTS, next_e, -1).astype(jnp.int32)
    of_tile = tile_expert[:, None] == eid[None, :]
    tile_slot = jnp.sum(jnp.where(of_tile, slot_e[None, :], 0), axis=1).astype(jnp.int32)
    tile_next = jnp.sum(jnp.where(of_tile, next_e[None, :], 0), axis=1).astype(jnp.int32)

    xs = _dispatch(pad_start, pad_len, tile_end[-1:], pos.reshape(n // DISP_TB, 1, 2 * DISP_TB), h2, p_rows)
    ys = _experts(tile_expert, tile_valid, tile_first, tile_slot, tile_next, xs, w1.reshape(N_EXPERTS, d, D_EXPERT),
                  w3.reshape(N_EXPERTS, d, D_EXPERT), w2.reshape(N_EXPERTS, D_EXPERT, d))
    out = _combine(pos.reshape(n // COMB_TB, 1, 2 * COMB_TB), ys, rslab, x1, g2, final_g.reshape(1, d), seq, final)
    return out.reshape(bsz, seq, d)


def kernel(x, c, w_ada, b_ada, norm1_g, w_in, conv_w, a_log, dt_bias, dn_onorm_g, fox_f_bias, w_out, norm2_g,
           w_router_group, b_router_group, w_router_expert, b_router_expert, w1, w3, w2, final_g):
    depth = w_ada.shape[0]
    for l in range(depth):
        x = _layer(x, c, w_ada[l], b_ada[l], norm1_g[l], w_in[l], conv_w[l], a_log[l], dt_bias[l], dn_onorm_g[l],
                   fox_f_bias[l], w_out[l], norm2_g[l], w_router_group[l], b_router_group[l], w_router_expert[l],
                   b_router_expert[l], w1[l], w3[l], w2[l], final_g, l == depth - 1)
    return x
```

```python
import functools

import jax
import jax.numpy as jnp
from jax import lax
from jax.experimental import pallas as pl
from jax.experimental.pallas import tpu as pltpu

F32 = jnp.float32
BF16 = jnp.bfloat16

D_MODEL = 2048
EPS = 1e-6
CHUNK = 64
HEADS = 8
HEAD_DIM = 128
HEAD_W = HEADS * HEAD_DIM
CONV_K = 4
N_GROUPS = 4
EXPERTS_PER_GROUP = 8
N_EXPERTS = N_GROUPS * EXPERTS_PER_GROUP
D_EXPERT = 512
LANES = 128
SUBLANES = 8
MAIN_W = 7 * HEAD_W
VMEM_LIMIT = 56 * 1024 * 1024

L_GC, L_BETA, L_F, L_EGC, L_EK, L_ELAST = 0, 8, 16, 24, 32, 40


def _cparams(sem):
    return pltpu.CompilerParams(dimension_semantics=sem, vmem_limit_bytes=VMEM_LIMIT)


def _split_bf16(a):
    hi = a.astype(BF16)
    lo = (a - hi.astype(F32)).astype(BF16)
    return hi, lo


def _dot(a, b):
    return jnp.dot(a, b, preferred_element_type=F32)


def _dot_nt(a, b):
    return lax.dot_general(a, b, (((1,), (1,)), ((), ())), preferred_element_type=F32)


def _dot3(a, b):
    ah, al = _split_bf16(a)
    bh, bl = _split_bf16(b)
    return _dot(ah, bh) + (_dot(al, bh) + _dot(ah, bl))


def _dot3_pre(a, b_hi_lo):
    ah, al = _split_bf16(a)
    n = b_hi_lo.shape[1] // 2
    both = _dot(ah, b_hi_lo)
    return both[:, :n] + (both[:, n:] + _dot(al, b_hi_lo[:, :n]))


def _softplus(x):
    return jnp.maximum(x, 0.0) + jnp.log1p(jnp.exp(-jnp.abs(x)))


def _silu(x):
    return x * jax.nn.sigmoid(x)


def _adaln_kernel(c_ref, w_ref, b_ref, o_ref):
    c = c_ref[...]
    o_ref[...] = _dot(_silu(c).astype(BF16), w_ref[...].astype(BF16)) + b_ref[...]


def _adaln(c, w, b):
    bsz = c.shape[0]
    n = w.shape[1]
    tn = 1024
    cp = jnp.zeros((SUBLANES, D_MODEL), F32).at[:bsz].set(c)
    out = pl.pallas_call(
        _adaln_kernel,
        grid=(n // tn,),
        in_specs=[
            pl.BlockSpec((SUBLANES, D_MODEL), lambda j: (0, 0)),
            pl.BlockSpec((D_MODEL, tn), lambda j: (0, j)),
            pl.BlockSpec((1, tn), lambda j: (0, j)),
        ],
        out_specs=pl.BlockSpec((SUBLANES, tn), lambda j: (0, j)),
        out_shape=jax.ShapeDtypeStruct((SUBLANES, n), F32),
        compiler_params=_cparams(("parallel",)),
        name="adaln",
    )(cp, w, b.reshape(1, n))
    return out[:bsz]


INPROJ_TM = 2048
INPROJ_TN = 512
AB_ROW0 = 4 * HEAD_W
F_ROW0 = MAIN_W + 2 * HEADS


def _modulated_norm(x, g, sc, sh):
    r = lax.rsqrt(jnp.mean(x * x, axis=-1, keepdims=True) + EPS)
    return (x * r) * (g * (1.0 + sc)) + sh


def _inproj_kernel(h_ref, w_ref, o_ref):
    o_ref[...] = _dot_nt(h_ref[...], w_ref[...].astype(BF16)).astype(BF16)


def _inproj(h, w_t):
    n = h.shape[0]
    tm, tn = INPROJ_TM, INPROJ_TN
    first_part = AB_ROW0 // tn

    def w_rows(i, j):
        return (SUBLANES * (j * (tn // SUBLANES) + jnp.where(j >= first_part, 2 * HEADS // SUBLANES, 0)), 0)

    return pl.pallas_call(
        _inproj_kernel,
        grid=(n // tm, MAIN_W // tn),
        in_specs=[
            pl.BlockSpec((tm, D_MODEL), lambda i, j: (i, 0)),
            pl.BlockSpec((pl.Element(tn), pl.Element(D_MODEL)), w_rows),
        ],
        out_specs=pl.BlockSpec((tm, tn), lambda i, j: (i, j)),
        out_shape=jax.ShapeDtypeStruct((n, MAIN_W), BF16),
        compiler_params=_cparams(("parallel", "arbitrary")),
        name="inproj",
    )(h, w_t)


GATES_TB = 256


def _split3(a):
    p0 = a.astype(BF16)
    r1 = a - p0.astype(F32)
    p1 = r1.astype(BF16)
    p2 = (r1 - p1.astype(F32)).astype(BF16)
    return p0, p1, p2


def _dot_ones(m, a):
    p0, p1, p2 = _split3(a)
    return _dot(m, p0) + (_dot(m, p1) + _dot(m, p2))


def _prenorm_gates_kernel(x_ref, sc_ref, sh_ref, g_ref, wab_ref, wf_ref, alog_ref, dt_ref, fb_ref,
                          h_ref, o_ref, ws_ref, carry_ref):
    tb = GATES_TB

    @pl.when(pl.program_id(1) == 0)
    def _():
        carry_ref[...] = jnp.zeros_like(carry_ref)
        gate_rows = jnp.concatenate([wab_ref[...], wf_ref[...]], axis=0)
        pad = jnp.zeros((LANES - gate_rows.shape[0], D_MODEL), F32)
        hi, lo = _split_bf16(jnp.concatenate([gate_rows, pad], axis=0))
        ws_ref[0:LANES, :] = hi
        ws_ref[LANES:, :] = lo

    hb = _modulated_norm(x_ref[...], g_ref[...], sc_ref[...], sh_ref[...]).astype(BF16)
    h_ref[...] = hb
    both = _dot_nt(hb, ws_ref[...])
    x = both[:, :LANES] + both[:, LANES:]
    lane = lax.broadcasted_iota(jnp.int32, (tb, LANES), 1)
    g = -jnp.exp(alog_ref[...]) * _softplus(x + dt_ref[...])
    beta = jax.nn.sigmoid(x)
    lf = -_softplus(-(x + fb_ref[...]))

    ri = lax.broadcasted_iota(jnp.int32, (tb, tb), 0)
    ci = lax.broadcasted_iota(jnp.int32, (tb, tb), 1)
    same_chunk = (ri // CHUNK) == (ci // CHUNK)
    tri = (ri >= ci)
    m_all = jnp.where(tri, 1.0, 0.0).astype(BF16)
    m_chunk = jnp.where(tri & same_chunk, 1.0, 0.0).astype(BF16)
    m_tot = jnp.where(same_chunk, 1.0, 0.0).astype(BF16)

    gc = _dot_ones(m_chunk, g)
    glast = _dot_ones(m_tot, g)
    fcum = _dot_ones(m_all, lf) + carry_ref[...]
    carry_ref[...] = fcum[tb - 1:tb, :]

    in_a = lane < HEADS
    egc = jnp.where(in_a, jnp.exp(gc), 0.0)
    ek = jnp.where(in_a, jnp.exp(glast - gc), 0.0)
    elast = jnp.where(in_a, jnp.exp(glast), 0.0)
    out = jnp.where(in_a, gc, jnp.where(lane < 2 * HEADS, beta, jnp.where(lane < 3 * HEADS, fcum, 0.0)))
    out = out + pltpu.roll(egc, L_EGC, 1) + pltpu.roll(ek, L_EK, 1) + pltpu.roll(elast, L_ELAST, 1)
    o_ref[...] = out


def _prenorm_gates(x, sc1, sh1, g, w_t, alog_row, dt_row, fb_row):
    bsz, seq, _ = x.shape
    tb = GATES_TB
    row = pl.BlockSpec((1, LANES), lambda b, t: (0, 0))
    modrow = pl.BlockSpec((None, 1, D_MODEL), lambda b, t: (b, 0, 0))
    return pl.pallas_call(
        _prenorm_gates_kernel,
        grid=(bsz, seq // tb),
        in_specs=[
            pl.BlockSpec((None, tb, D_MODEL), lambda b, t: (b, t, 0)),
            modrow, modrow,
            pl.BlockSpec((1, D_MODEL), lambda b, t: (0, 0)),
            pl.BlockSpec((pl.Element(2 * HEADS), pl.Element(D_MODEL)), lambda b, t: (AB_ROW0, 0)),
            pl.BlockSpec((pl.Element(HEADS), pl.Element(D_MODEL)), lambda b, t: (F_ROW0, 0)),
            row, row, row,
        ],
        out_specs=[
            pl.BlockSpec((None, tb, D_MODEL), lambda b, t: (b, t, 0)),
            pl.BlockSpec((None, tb, LANES), lambda b, t: (b, t, 0)),
        ],
        out_shape=[
            jax.ShapeDtypeStruct((bsz, seq, D_MODEL), BF16),
            jax.ShapeDtypeStruct((bsz, seq, LANES), F32),
        ],
        scratch_shapes=[pltpu.VMEM((2 * LANES, D_MODEL), BF16), pltpu.VMEM((1, LANES), F32)],
        compiler_params=_cparams(("parallel", "arbitrary")),
        name="prenorm_gates",
    )(x, sc1, sh1, g, w_t, w_t, alog_row, dt_row, fb_row)


DN_TB = 256
DN_GROUP = 4
HALO = SUBLANES


def _bdot(a, b):
    return lax.dot_general(a, b, (((2,), (1,)), ((0,), (0,))), preferred_element_type=F32)


def _bdot_nt(a, b):
    return lax.dot_general(a, b, (((2,), (2,)), ((0,), (0,))), preferred_element_type=F32)


def _inv_unit_lower(a, eye, blk16, blk32):
    n = jnp.where(blk16, -a, 0.0)
    e1 = jnp.where(blk32 & jnp.logical_not(blk16), a, 0.0).astype(BF16)
    e2 = jnp.where(blk32, 0.0, a).astype(BF16)
    t = eye + n
    p = n.astype(BF16)
    for _ in range(3):
        p = _bdot(p, p).astype(BF16)
        t = t + _bdot(t.astype(BF16), p)
    for e in (e1, e2):
        tb = t.astype(BF16)
        t = t - _bdot(_bdot(tb, e).astype(BF16), tb)
    return t


def _deltanet_kernel(q_ref, k_ref, v_ref, z_ref, wq_ref, wk_ref, wv_ref, slab_ref, gct_ref, og_ref,
                     o_ref, ext_ref, qn_ref, kn_ref, vv_ref, s_ref):
    tb = DN_TB

    @pl.when(pl.program_id(1) == 0)
    def _():
        ext_ref[:, 0:HALO, :] = jnp.zeros((3, HALO, HEAD_W), F32)
        s_ref[...] = jnp.zeros_like(s_ref)

    for idx, (u_ref, w_ref) in enumerate(((q_ref, wq_ref), (k_ref, wk_ref), (v_ref, wv_ref))):
        for h in range(HEADS):
            cols = slice(h * HEAD_DIM, (h + 1) * HEAD_DIM)
            ext_ref[idx, HALO:HALO + tb, cols] = u_ref[:, cols].astype(F32)
            y = None
            for j in range(CONV_K):
                start = HALO - (CONV_K - 1) + j
                term = ext_ref[idx, start:start + tb, cols] * w_ref[j:j + 1, cols]
                y = term if y is None else y + term
            y = _silu(y)
            if idx == 2:
                vv_ref[h] = y
            else:
                yn = y * lax.rsqrt(jnp.sum(y * y, axis=-1, keepdims=True) + EPS)
                if idx == 0:
                    qn_ref[h] = (yn * (HEAD_DIM ** -0.5)).astype(BF16)
                else:
                    kn_ref[h] = yn.astype(BF16)
        ext_ref[idx, 0:HALO, :] = ext_ref[idx, tb:tb + HALO, :]

    ri = lax.broadcasted_iota(jnp.int32, (DN_GROUP * HEADS, CHUNK, CHUNK), 1)
    ci = lax.broadcasted_iota(jnp.int32, (DN_GROUP * HEADS, CHUNK, CHUNK), 2)
    incl = ri >= ci
    strict = ri > ci
    eye = jnp.where(ri == ci, 1.0, 0.0)
    blk16 = (ri // 16) == (ci // 16)
    blk32 = (ri // 32) == (ci // 32)

    def group_body(c, _):
        rows = [pl.ds(pl.multiple_of((c * DN_GROUP + g) * CHUNK, CHUNK), CHUNK) for g in range(DN_GROUP)]
        slabs = [slab_ref[r, :] for r in rows]

        def col(off, width):
            return jnp.stack([jnp.broadcast_to(sl[:, off + h:off + h + 1], (CHUNK, width))
                              for sl in slabs for h in range(HEADS)])

        def grouped(ref):
            return jnp.concatenate([ref[:, r, :] for r in rows], axis=0)

        q = grouped(qn_ref)
        k = grouped(kn_ref)
        v = grouped(vv_ref)
        beta = col(L_BETA, HEAD_DIM)
        egc = col(L_EGC, HEAD_DIM)
        gc_row = jnp.concatenate([gct_ref[c * DN_GROUP + g] for g in range(DN_GROUP)], axis=0)

        decay = jnp.where(incl, jnp.exp(col(L_GC, CHUNK) - gc_row), 0.0)
        kk = _bdot_nt(k, k)
        qk = (_bdot_nt(q, k) * decay).astype(BF16)
        a = jnp.where(strict, kk * decay * beta[:, :, :CHUNK], 0.0)
        t = _inv_unit_lower(a, eye, blk16, blk32).astype(BF16)

        kf = k.astype(F32)
        vb = (v * beta).astype(BF16)
        kbg = (kf * (beta * egc)).astype(BF16)
        u = _bdot(t, vb)
        w = _bdot(t, kbg).astype(BF16)
        qd = (q.astype(F32) * egc).astype(BF16)
        kd = kf * col(L_EK, HEAD_DIM)
        kdt = jnp.stack([kd[n].T for n in range(DN_GROUP * HEADS)]).astype(BF16)

        for g in range(DN_GROUP):
            sel = slice(g * HEADS, (g + 1) * HEADS)
            s = s_ref[...]
            sb = s.astype(BF16)
            vnb = (u[sel] - _bdot(w[sel], sb)).astype(BF16)
            o = _bdot(qd[sel], sb) + _bdot(qk[sel], vnb)
            elast = jnp.stack([jnp.broadcast_to(slabs[g][CHUNK - 1:CHUNK, L_ELAST + h:L_ELAST + h + 1],
                                                (HEAD_DIM, HEAD_DIM)) for h in range(HEADS)])
            s_ref[...] = s * elast + _bdot(kdt[sel], vnb)

            r = lax.rsqrt(jnp.mean(o * o, axis=-1, keepdims=True) + EPS)
            on = (o * r) * og_ref[...]
            for h in range(HEADS):
                cols = slice(h * HEAD_DIM, (h + 1) * HEAD_DIM)
                o_ref[rows[g], cols] = (on[h] * _silu(z_ref[rows[g], cols].astype(F32))).astype(BF16)
        return 0

    lax.fori_loop(0, tb // (CHUNK * DN_GROUP), group_body, 0)


def _deltanet(proj3, conv_w, slab, gct, onorm_g):
    bsz, seq, _ = proj3.shape
    tb = DN_TB
    nct = tb // CHUNK

    def colblk(j):
        return pl.BlockSpec((None, tb, HEAD_W), lambda b, t: (b, t, j))

    def wblk(j):
        return pl.BlockSpec((CONV_K, HEAD_W), lambda b, t: (0, j))

    return pl.pallas_call(
        _deltanet_kernel,
        grid=(bsz, seq // tb),
        in_specs=[
            colblk(0), colblk(1), colblk(2), colblk(3),
            wblk(0), wblk(1), wblk(2),
            pl.BlockSpec((None, tb, LANES), lambda b, t: (b, t, 0)),
            pl.BlockSpec((None, nct, HEADS, 1, CHUNK), lambda b, t: (b, t, 0, 0, 0)),
            pl.BlockSpec((1, HEAD_DIM), lambda b, t: (0, 0)),
        ],
        out_specs=pl.BlockSpec((None, tb, HEAD_W), lambda b, t: (b, t, 0)),
        out_shape=jax.ShapeDtypeStruct((bsz, seq, HEAD_W), BF16),
        scratch_shapes=[
            pltpu.VMEM((3, tb + HALO, HEAD_W), F32),
            pltpu.VMEM((HEADS, tb, HEAD_DIM), BF16),
            pltpu.VMEM((HEADS, tb, HEAD_DIM), BF16),
            pltpu.VMEM((HEADS, tb, HEAD_DIM), F32),
            pltpu.VMEM((HEADS, HEAD_DIM, HEAD_DIM), F32),
        ],
        compiler_params=_cparams(("parallel", "arbitrary")),
        name="deltanet",
    )(proj3, proj3, proj3, proj3, conv_w, conv_w, conv_w, slab, gct, onorm_g)


FOX_T = 512
FOX_STRIP = 32
NEG_INF = float("-inf")
LOG2E = 1.4426950408889634


def _fox_kernel(q_ref, k_ref, v_ref, f_ref, o_ref, s_a_ref, s_b_ref, p_a_ref, p_b_ref, m_ref, l_ref, alpha_ref,
                acc_ref):
    tq = FOX_T
    i = pl.program_id(2)
    lane_blocks = tq // LANES
    qs = (q_ref[...].astype(F32) * (HEAD_DIM ** -0.5 * LOG2E)).astype(BF16)
    f0 = f_ref[i][:, 0:1]
    m_ref[...] = jnp.full_like(m_ref, NEG_INF)
    l_ref[...] = jnp.zeros_like(l_ref)
    acc_ref[...] = jnp.zeros_like(acc_ref)

    def keys(j):
        return pl.ds(pl.multiple_of(j * tq, tq), tq)

    def scores(j):
        return _dot_nt(qs, k_ref[keys(j), :])

    def softmax(j, s_view, p_view, masked):
        bias = (f0 - f_ref[j]) * LOG2E

        def logits(r):
            rows = slice(r * FOX_STRIP, (r + 1) * FOX_STRIP)
            s = s_view[rows, :] + bias
            if masked:
                ri = r * FOX_STRIP + lax.broadcasted_iota(jnp.int32, (FOX_STRIP, tq), 0)
                ci = lax.broadcasted_iota(jnp.int32, (FOX_STRIP, tq), 1)
                s = jnp.where(ci <= ri, s, NEG_INF)
            return rows, s

        for r in range(tq // FOX_STRIP):
            rows, s = logits(r)
            m_old = m_ref[rows, :]
            m_new = jnp.maximum(m_old, jnp.max(s, axis=-1, keepdims=True))
            alpha_ref[rows, :] = jnp.exp2(m_old - m_new)
            m_ref[rows, :] = m_new
        for r in range(tq // FOX_STRIP):
            rows, s = logits(r)
            p = jnp.exp2(s - jnp.concatenate([m_ref[rows, :]] * lane_blocks, axis=1))
            alpha = alpha_ref[rows, :]
            l_ref[rows, :] = alpha * l_ref[rows, :] + jnp.sum(p, axis=-1, keepdims=True)
            p_view[rows, :] = p.astype(BF16)

    s_a_ref[...] = scores(i)
    s_b_ref[...] = scores(jnp.maximum(i - 1, 0))
    softmax(i, s_a_ref, p_a_ref, True)

    def step(t, s_cur, p_cur, s_nxt, p_prv):
        j = i - t
        s_nxt[...] = scores(jnp.maximum(j - 1, 0))
        pv = _dot(p_prv[...], v_ref[keys(j + 1), :])
        softmax(j, s_cur, p_cur, False)
        acc_ref[...] = alpha_ref[...] * (acc_ref[...] + pv)

    def pair(u):
        step(2 * u + 1, s_b_ref, p_b_ref, s_a_ref, p_a_ref)
        step(2 * u + 2, s_a_ref, p_a_ref, s_b_ref, p_b_ref)

    def quad(w, _):
        pair(2 * w)
        pair(2 * w + 1)
        return 0

    quads = lax.shift_right_logical(i, 2)
    lax.fori_loop(0, quads, quad, 0)
    pl.when((i & 2) != 0)(functools.partial(pair, 2 * quads))

    def finish(p_last):
        acc = acc_ref[...] + _dot(p_last[...], v_ref[keys(0), :])
        o_ref[...] = (acc / l_ref[...]).astype(BF16)

    @pl.when((i & 1) == 1)
    def _():
        step(i, s_b_ref, p_b_ref, s_a_ref, p_a_ref)
        finish(p_b_ref)

    @pl.when((i & 1) == 0)
    def _():
        finish(p_a_ref)


def _fox(proj3, f_rows):
    bsz, seq, _ = proj3.shape
    tq = FOX_T
    qb, kb, vb = 4 * HEADS, 5 * HEADS, 6 * HEADS
    return pl.pallas_call(
        _fox_kernel,
        grid=(bsz, HEADS, seq // tq),
        in_specs=[
            pl.BlockSpec((None, tq, HEAD_DIM), lambda b, h, i: (b, i, qb + h)),
            pl.BlockSpec((None, seq, HEAD_DIM), lambda b, h, i: (b, 0, kb + h)),
            pl.BlockSpec((None, seq, HEAD_DIM), lambda b, h, i: (b, 0, vb + h)),
            pl.BlockSpec((None, None, seq // tq, 1, tq), lambda b, h, i: (b, h, 0, 0, 0)),
        ],
        out_specs=pl.BlockSpec((None, tq, HEAD_DIM), lambda b, h, i: (b, i, h)),
        out_shape=jax.ShapeDtypeStruct((bsz, seq, HEAD_W), BF16),
        scratch_shapes=[
            pltpu.VMEM((tq, tq), F32),
            pltpu.VMEM((tq, tq), F32),
            pltpu.VMEM((tq, tq), BF16),
            pltpu.VMEM((tq, tq), BF16),
            pltpu.VMEM((tq, LANES), F32),
            pltpu.VMEM((tq, LANES), F32),
            pltpu.VMEM((tq, LANES), F32),
            pltpu.VMEM((tq, HEAD_DIM), F32),
        ],
        compiler_params=_cparams(("parallel", "parallel", "arbitrary")),
        name="fox",
    )(proj3, proj3, proj3, f_rows)


OUT_TM = 512
R_E0, R_E1, R_W0, R_W1 = 0, 1, 2, 3


def _first_argmax(vals, lane):
    m = jnp.max(vals, axis=-1, keepdims=True)
    idx = jnp.min(jnp.where(vals == m, lane, LANES), axis=-1, keepdims=True)
    return m, idx


def _outproj_kernel(oa_ref, ob_ref, wa_ref, wb_ref, x_ref, g1_ref, sc_ref, sh_ref, g_ref,
                    wr_ref, br_ref, x1_ref, h2_ref, r_ref, cnt_ref):
    mix = _dot(oa_ref[...], wa_ref[...]) + _dot(ob_ref[...], wb_ref[...])
    x1 = x_ref[...] + g1_ref[...] * mix
    x1_ref[...] = x1
    h2 = _modulated_norm(x1, g_ref[...], sc_ref[...], sh_ref[...])
    h2_ref[...] = h2

    logits = _dot3_pre(h2, wr_ref[...]) + br_ref[...]
    tm = logits.shape[0]
    lane = lax.broadcasted_iota(jnp.int32, (tm, LANES), 1)
    gl = jnp.where(lane < N_GROUPS, logits, NEG_INF)
    gmax, gidx = _first_argmax(gl, lane)
    pg = 1.0 / jnp.sum(jnp.exp(gl - gmax), axis=-1, keepdims=True)
    e_lane = lane - N_GROUPS
    in_grp = (e_lane >= gidx * EXPERTS_PER_GROUP) & (e_lane < (gidx + 1) * EXPERTS_PER_GROUP)
    el = jnp.where(in_grp, logits, NEG_INF)
    v0, i0 = _first_argmax(el, lane)
    v1, i1 = _first_argmax(jnp.where(lane == i0, NEG_INF, el), lane)
    ex = jnp.exp(v1 - v0)
    w0 = pg / (1.0 + ex)
    w1 = pg * ex / (1.0 + ex)
    e0 = (i0 - N_GROUPS).astype(F32)
    e1 = (i1 - N_GROUPS).astype(F32)
    r_ref[...] = jnp.where(lane == R_E0, e0, jnp.where(lane == R_E1, e1,
                           jnp.where(lane == R_W0, w0, jnp.where(lane == R_W1, w1, 0.0))))

    @pl.when(pl.program_id(0) == 0)
    def _():
        cnt_ref[...] = jnp.zeros_like(cnt_ref)

    picked = (lane == i0 - N_GROUPS) | (lane == i1 - N_GROUPS)
    cnt_ref[...] += jnp.sum(jnp.where(picked, 1.0, 0.0), axis=0, keepdims=True)


def _outproj(o_a, o_b, wa, wb, x2, g1, sc2, sh2, g, wr_hi_lo, br, seq):
    n = x2.shape[0]
    tm = OUT_TM
    per_b = seq // tm
    modrow = pl.BlockSpec((None, 1, D_MODEL), lambda i: (i // per_b, 0, 0))
    const = lambda shape: pl.BlockSpec(shape, lambda i: (0, 0))
    return pl.pallas_call(
        _outproj_kernel,
        grid=(n // tm,),
        in_specs=[
            pl.BlockSpec((tm, HEAD_W), lambda i: (i, 0)),
            pl.BlockSpec((tm, HEAD_W), lambda i: (i, 0)),
            const((HEAD_W, D_MODEL)), const((HEAD_W, D_MODEL)),
            pl.BlockSpec((tm, D_MODEL), lambda i: (i, 0)),
            modrow, modrow, modrow,
            const((1, D_MODEL)),
            const((D_MODEL, 2 * LANES)), const((1, LANES)),
        ],
        out_specs=[
            pl.BlockSpec((tm, D_MODEL), lambda i: (i, 0)),
            pl.BlockSpec((tm, D_MODEL), lambda i: (i, 0)),
            pl.BlockSpec((tm, LANES), lambda i: (i, 0)),
            pl.BlockSpec((1, LANES), lambda i: (0, 0)),
        ],
        out_shape=[
            jax.ShapeDtypeStruct((n, D_MODEL), F32),
            jax.ShapeDtypeStruct((n, D_MODEL), F32),
            jax.ShapeDtypeStruct((n, LANES), F32),
            jax.ShapeDtypeStruct((1, LANES), F32),
        ],
        compiler_params=_cparams(("arbitrary",)),
        name="outproj",
    )(o_a, o_b, wa, wb, x2, g1, sc2, sh2, g, wr_hi_lo, br)


MOE_TM = 256
ROUTE_TB = 512


def _route_kernel(r_ref, cnt_ref, pos_ref, run_ref, base_ref):
    tb = ROUTE_TB
    t = pl.program_id(0)
    r = r_ref[...]
    lane = lax.broadcasted_iota(jnp.int32, (tb, LANES), 1)
    e0 = r[:, R_E0:R_E0 + 1].astype(jnp.int32)
    e1 = r[:, R_E1:R_E1 + 1].astype(jnp.int32)
    oh0 = lane == e0
    oh1 = lane == e1
    both = jnp.where(oh0 | oh1, 1.0, 0.0)

    @pl.when(t == 0)
    def _():
        counts = cnt_ref[...]
        padded = jnp.ceil(counts / MOE_TM) * MOE_TM
        li = lax.broadcasted_iota(jnp.int32, (LANES, LANES), 0)
        lj = lax.broadcasted_iota(jnp.int32, (LANES, LANES), 1)
        upper = jnp.where(li < lj, 1.0, 0.0).astype(BF16)
        hi = jnp.floor(padded / 256.0)
        lo = padded - hi * 256.0
        hi8 = jnp.broadcast_to(hi, (SUBLANES, LANES)).astype(BF16)
        lo8 = jnp.broadcast_to(lo, (SUBLANES, LANES)).astype(BF16)
        base = _dot(hi8, upper) * 256.0 + _dot(lo8, upper)
        base_ref[...] = base[0:1, :]
        run_ref[...] = jnp.zeros_like(run_ref)

    ri = lax.broadcasted_iota(jnp.int32, (tb, tb), 0)
    ci = lax.broadcasted_iota(jnp.int32, (tb, tb), 1)
    strict = jnp.where(ri > ci, 1.0, 0.0).astype(BF16)
    before = _dot(strict, both.astype(BF16)) + run_ref[...] + base_ref[...]
    p0 = jnp.sum(jnp.where(oh0, before, 0.0), axis=-1, keepdims=True)
    p1 = jnp.sum(jnp.where(oh1, before, 0.0), axis=-1, keepdims=True)
    pos_ref[...] = jnp.where(lane == 0, p0, jnp.where(lane == 1, p1, 0.0)).astype(jnp.int32)
    run_ref[...] = run_ref[...] + jnp.sum(both, axis=0, keepdims=True)


def _route(rslab, counts):
    n = rslab.shape[0]
    tb = ROUTE_TB
    return pl.pallas_call(
        _route_kernel,
        grid=(n // tb,),
        in_specs=[pl.BlockSpec((tb, LANES), lambda t: (t, 0)), pl.BlockSpec((1, LANES), lambda t: (0, 0))],
        out_specs=pl.BlockSpec((tb, LANES), lambda t: (t, 0)),
        out_shape=jax.ShapeDtypeStruct((n, LANES), jnp.int32),
        scratch_shapes=[pltpu.VMEM((1, LANES), F32), pltpu.VMEM((1, LANES), F32)],
        compiler_params=_cparams(("arbitrary",)),
        name="route",
    )(rslab, counts)


DISP_TB = 1024
DMA_UNROLL = 8


def _dispatch_kernel(pad_start_ref, pad_len_ref, used_ref, pos_ref, h2_ref, xs_ref, zero_ref, sem, zsem):
    t = pl.program_id(0)
    tb = DISP_TB
    n_tiles = xs_ref.shape[0] // MOE_TM

    def row_copy(g, r, k):
        dst_row = pos_ref[0, 2 * SUBLANES * g + 2 * r + k]
        return pltpu.make_async_copy(h2_ref.at[g, pl.ds(r, 1)], xs_ref.at[pl.ds(dst_row, 1)], sem)

    def issue(g, _):
        for r in range(SUBLANES):
            row_copy(g, r, 0).start(priority=0)
            row_copy(g, r, 1).start(priority=1)
        return 0

    lax.fori_loop(0, tb // SUBLANES, issue, 0)

    @pl.when(t == 0)
    def _():
        zero_ref[...] = jnp.zeros_like(zero_ref)

        def zero_rows(wait, off, rows):
            cp = pltpu.make_async_copy(zero_ref.at[pl.ds(0, rows)], xs_ref.at[pl.ds(off, rows)], zsem)
            cp.wait() if wait else cp.start()

        def per_expert(wait, e, _):
            start = pad_start_ref[e]
            head = (-start) & (SUBLANES - 1)
            for r in range(SUBLANES - 1):
                pl.when(r < head)(functools.partial(zero_rows, wait, start + r, 1))
            off = start + head
            rest = pad_len_ref[e] - head
            piece = MOE_TM // 2
            while piece >= SUBLANES:
                take = (rest & piece) != 0
                pl.when(take)(functools.partial(zero_rows, wait, pl.multiple_of(off, SUBLANES), piece))
                off = off + jnp.where(take, piece, 0)
                piece //= 2
            return 0

        def per_tile(wait, i, _):
            zero_rows(wait, pl.multiple_of(i * MOE_TM, MOE_TM), MOE_TM)
            return 0

        for wait in (False, True):
            lax.fori_loop(0, N_EXPERTS, functools.partial(per_expert, wait), 0)
            lax.fori_loop(used_ref[0], n_tiles, functools.partial(per_tile, wait), 0)

    def drain(g, _):
        for r in range(SUBLANES):
            row_copy(g, r, 0).wait()
            row_copy(g, r, 1).wait()
        return 0

    lax.fori_loop(0, tb // SUBLANES, drain, 0)


def _dispatch(pad_start, pad_len, used, pos2, h2, p_rows):
    n = h2.shape[0]
    tb = DISP_TB
    return pl.pallas_call(
        _dispatch_kernel,
        grid_spec=pltpu.PrefetchScalarGridSpec(
            num_scalar_prefetch=3,
            grid=(n // tb,),
            in_specs=[
                pl.BlockSpec((None, 1, 2 * tb), lambda t, *_: (t, 0, 0), memory_space=pltpu.SMEM),
                pl.BlockSpec((tb // SUBLANES, SUBLANES, D_MODEL), lambda t, *_: (t, 0, 0)),
            ],
            out_specs=pl.BlockSpec(memory_space=pl.ANY),
            scratch_shapes=[
                pltpu.VMEM((MOE_TM, D_MODEL), F32),
                pltpu.SemaphoreType.DMA(()),
                pltpu.SemaphoreType.DMA(()),
            ],
        ),
        out_shape=jax.ShapeDtypeStruct((p_rows, D_MODEL), F32),
        compiler_params=_cparams(("arbitrary",)),
        name="dispatch",
    )(pad_start, pad_len, used, pos2, h2.reshape(n // SUBLANES, SUBLANES, D_MODEL))


def _experts_kernel(te_ref, tv_ref, tf_ref, ts_ref, tn_ref, xs_ref, w1_hbm, w3_hbm, w2_hbm, ys_ref,
                    w1f_ref, w3f_ref, w2f_ref, w1b_ref, w3b_ref, w2b_ref, sem):
    i = pl.program_id(0)

    def weight_copies(e, slot):
        return [pltpu.make_async_copy(src.at[e], dst.at[slot], sem.at[slot])
                for src, dst in ((w1_hbm, w1f_ref), (w3_hbm, w3f_ref), (w2_hbm, w2f_ref))]

    @pl.when(tv_ref[i] != 0)
    def _():
        @pl.when(tf_ref[i] != 0)
        def _():
            slot = ts_ref[i]

            @pl.when(i == 0)
            def _():
                for cp in weight_copies(te_ref[i], slot):
                    cp.start()

            for cp in weight_copies(te_ref[i], slot):
                cp.wait()

            @pl.when(tn_ref[i] >= 0)
            def _():
                for cp in weight_copies(tn_ref[i], 1 - slot):
                    cp.start()

            w1b_ref[...] = w1f_ref[slot].astype(BF16)
            w3b_ref[...] = w3f_ref[slot].astype(BF16)
            w2b_ref[...] = w2f_ref[slot].astype(BF16)

        x = xs_ref[...].astype(BF16)
        a = _dot(x, w1b_ref[...])
        b = _dot(x, w3b_ref[...])
        ys_ref[...] = _dot((_silu(a) * b).astype(BF16), w2b_ref[...])

    @pl.when(tv_ref[i] == 0)
    def _():
        ys_ref[...] = jnp.zeros_like(ys_ref)


def _experts(tile_expert, tile_valid, tile_first, tile_slot, tile_next, xs, w1, w3, w2):
    p_rows = xs.shape[0]
    tm = MOE_TM
    hbm = pl.BlockSpec(memory_space=pl.ANY)
    return pl.pallas_call(
        _experts_kernel,
        grid_spec=pltpu.PrefetchScalarGridSpec(
            num_scalar_prefetch=5,
            grid=(p_rows // tm,),
            in_specs=[pl.BlockSpec((tm, D_MODEL), lambda i, *_: (i, 0)), hbm, hbm, hbm],
            out_specs=pl.BlockSpec((tm, D_MODEL), lambda i, *_: (i, 0)),
            scratch_shapes=[
                pltpu.VMEM((2, D_MODEL, D_EXPERT), F32),
                pltpu.VMEM((2, D_MODEL, D_EXPERT), F32),
                pltpu.VMEM((2, D_EXPERT, D_MODEL), F32),
                pltpu.VMEM((D_MODEL, D_EXPERT), BF16),
                pltpu.VMEM((D_MODEL, D_EXPERT), BF16),
                pltpu.VMEM((D_EXPERT, D_MODEL), BF16),
                pltpu.SemaphoreType.DMA((2,)),
            ],
        ),
        out_shape=jax.ShapeDtypeStruct((p_rows, D_MODEL), F32),
        compiler_params=_cparams(("arbitrary",)),
        name="experts",
    )(tile_expert, tile_valid, tile_first, tile_slot, tile_next, xs, w1, w3, w2)


COMB_TB = 256


def _combine_kernel(pos_ref, pos_next_ref, ys_ref, r_ref, x1_ref, g2_ref, fg_ref, o_ref, buf_ref, sem, *, final):
    tb = COMB_TB
    t = pl.program_id(0)
    slot = t & 1

    def row_copy(p_ref, s, g, r, k):
        src_row = p_ref[0, 2 * SUBLANES * g + 2 * r + k]
        return pltpu.make_async_copy(ys_ref.at[pl.ds(src_row, 1)], buf_ref.at[s, k, g, pl.ds(r, 1)], sem.at[s])

    def issue(p_ref, s):
        def body(g, _):
            for r in range(SUBLANES):
                row_copy(p_ref, s, g, r, 0).start(priority=0)
                row_copy(p_ref, s, g, r, 1).start(priority=1)
            return 0

        lax.fori_loop(0, tb // SUBLANES, body, 0)

    def drain(g, _):
        for r in range(SUBLANES):
            row_copy(pos_ref, slot, g, r, 0).wait()
            row_copy(pos_ref, slot, g, r, 1).wait()
        return 0

    pl.when(t == 0)(functools.partial(issue, pos_ref, 0))
    pl.when(t + 1 < pl.num_programs(0))(functools.partial(issue, pos_next_ref, 1 - slot))
    lax.fori_loop(0, tb // SUBLANES, drain, 0)
    r = r_ref[...]
    y0 = buf_ref[slot, 0].reshape(tb, D_MODEL)
    y1 = buf_ref[slot, 1].reshape(tb, D_MODEL)
    y = r[:, R_W0:R_W0 + 1] * y0 + r[:, R_W1:R_W1 + 1] * y1
    x2 = x1_ref[...] + g2_ref[...] * y
    if final:
        x2 = (x2 * lax.rsqrt(jnp.mean(x2 * x2, axis=-1, keepdims=True) + EPS)) * fg_ref[...]
    o_ref[...] = x2


def _combine(pos2, ys, rslab, x1, g2, final_g, seq, final):
    n = x1.shape[0]
    tb = COMB_TB
    per_b = seq // tb
    return pl.pallas_call(
        functools.partial(_combine_kernel, final=final),
        grid=(n // tb,),
        in_specs=[
            pl.BlockSpec((None, 1, 2 * tb), lambda i: (i, 0, 0), memory_space=pltpu.SMEM),
            pl.BlockSpec((None, 1, 2 * tb), lambda i: (jnp.minimum(i + 1, n // tb - 1), 0, 0),
                         memory_space=pltpu.SMEM),
            pl.BlockSpec(memory_space=pl.ANY),
            pl.BlockSpec((tb, LANES), lambda i: (i, 0)),
            pl.BlockSpec((tb, D_MODEL), lambda i: (i, 0)),
            pl.BlockSpec((None, 1, D_MODEL), lambda i: (i // per_b, 0, 0)),
            pl.BlockSpec((1, D_MODEL), lambda i: (0, 0)),
        ],
        out_specs=pl.BlockSpec((tb, D_MODEL), lambda i: (i, 0)),
        out_shape=jax.ShapeDtypeStruct((n, D_MODEL), F32),
        scratch_shapes=[pltpu.VMEM((2, 2, tb // SUBLANES, SUBLANES, D_MODEL), F32), pltpu.SemaphoreType.DMA((2,))],
        compiler_params=_cparams(("arbitrary",)),
        name="combine",
    )(pos2, pos2, ys, rslab, x1, g2, final_g)


def _layer(x, c, w_ada, b_ada, norm1_g, w_in, conv_w, a_log, dt_bias, dn_onorm_g, fox_f_bias,
           w_out, norm2_g, w_rg, b_rg, w_re, b_re, w1, w3, w2, final_g, final):
    bsz, seq, d = x.shape
    n = bsz * seq
    x2 = x.reshape(n, d)

    mod = _adaln(c, w_ada, b_ada)
    sh1, sc1, g1, sh2, sc2, g2 = [m.reshape(bsz, 1, d) for m in jnp.split(mod, 6, axis=-1)]

    w_t = w_in.T

    def lane_row(vals, off):
        return jnp.zeros((1, LANES), F32).at[0, off:off + HEADS].set(vals)

    h, slab = _prenorm_gates(x, sc1, sh1, norm1_g.reshape(1, d), w_t, lane_row(a_log, 0), lane_row(dt_bias, 0),
                             lane_row(fox_f_bias, L_F))
    proj3 = _inproj(h.reshape(n, d), w_t).reshape(bsz, seq, MAIN_W)
    nc = seq // CHUNK
    gct = slab[:, :, L_GC:L_GC + HEADS].reshape(bsz, nc, CHUNK, HEADS).transpose(0, 1, 3, 2)
    gct = gct.reshape(bsz, nc, HEADS, 1, CHUNK)
    f_rows = slab[:, :, L_F:L_F + HEADS].transpose(0, 2, 1).reshape(bsz, HEADS, seq // FOX_T, 1, FOX_T)

    o_dn = _deltanet(proj3, conv_w, slab, gct, dn_onorm_g.reshape(1, HEAD_DIM))
    o_fx = _fox(proj3, f_rows)

    wr = jnp.zeros((d, LANES), F32).at[:, :N_GROUPS].set(w_rg).at[:, N_GROUPS:N_GROUPS + N_EXPERTS].set(w_re)
    br = jnp.zeros((1, LANES), F32).at[0, :N_GROUPS].set(b_rg).at[0, N_GROUPS:N_GROUPS + N_EXPERTS].set(b_re)
    wr_hi_lo = jnp.concatenate(_split_bf16(wr), axis=1)
    w_out_b = w_out.astype(BF16)
    x1, h2, rslab, counts = _outproj(o_dn.reshape(n, HEAD_W), o_fx.reshape(n, HEAD_W), w_out_b[:HEAD_W],
                                     w_out_b[HEAD_W:], x2, g1, sc2, sh2, norm2_g.reshape(1, d), wr_hi_lo, br, seq)

    pos = _route(rslab, counts)[:, 0:2]

    cnt = counts[0, :N_EXPERTS].astype(jnp.int32)
    tiles_per = (cnt + MOE_TM - 1) // MOE_TM
    tile_end = jnp.cumsum(tiles_per)
    base = (tile_end - tiles_per) * MOE_TM
    n_tiles = (2 * n) // MOE_TM + N_EXPERTS
    p_rows = n_tiles * MOE_TM
    tid = jnp.arange(n_tiles, dtype=jnp.int32)
    tile_valid = (tid < tile_end[-1]).astype(jnp.int32)
    te_raw = jnp.minimum(jnp.sum(tid[:, None] >= tile_end[None, :], axis=1), N_EXPERTS - 1).astype(jnp.int32)
    last_e = te_raw[jnp.maximum(tile_end[-1] - 1, 0)]
    tile_expert = jnp.where(tile_valid == 1, te_raw, last_e)
    tile_first = (jnp.concatenate([jnp.array([-1], jnp.int32), tile_expert[:-1]]) != tile_expert).astype(jnp.int32)
    pad_start = base + cnt
    pad_len = tiles_per * MOE_TM - cnt
    eid = jnp.arange(N_EXPERTS, dtype=jnp.int32)
    has = tiles_per > 0
    slot_e = (jnp.cumsum(has.astype(jnp.int32)) - 1) & 1
    later = jnp.where(has[None, :] & (eid[None, :] > eid[:, None]), eid[None, :], N_EXPERTS)
    next_e = jnp.min(later, axis=1)
    next_e = jnp.where(next_e < N_EXPERTS, next_e, -1).astype(jnp.int32)
    of_tile = tile_expert[:, None] == eid[None, :]
    tile_slot = jnp.sum(jnp.where(of_tile, slot_e[None, :], 0), axis=1).astype(jnp.int32)
    tile_next = jnp.sum(jnp.where(of_tile, next_e[None, :], 0), axis=1).astype(jnp.int32)

    xs = _dispatch(pad_start, pad_len, tile_end[-1:], pos.reshape(n // DISP_TB, 1, 2 * DISP_TB), h2, p_rows)
    ys = _experts(tile_expert, tile_valid, tile_first, tile_slot, tile_next, xs, w1.reshape(N_EXPERTS, d, D_EXPERT),
                  w3.reshape(N_EXPERTS, d, D_EXPERT), w2.reshape(N_EXPERTS, D_EXPERT, d))
    out = _combine(pos.reshape(n // COMB_TB, 1, 2 * COMB_TB), ys, rslab, x1, g2, final_g.reshape(1, d), seq, final)
    return out.reshape(bsz, seq, d)


def kernel(x, c, w_ada, b_ada, norm1_g, w_in, conv_w, a_log, dt_bias, dn_onorm_g, fox_f_bias, w_out, norm2_g,
           w_router_group, b_router_group, w_router_expert, b_router_expert, w1, w3, w2, final_g):
    depth = w_ada.shape[0]
    for l in range(depth):
        x = _layer(x, c, w_ada[l], b_ada[l], norm1_g[l], w_in[l], conv_w[l], a_log[l], dt_bias[l], dn_onorm_g[l],
                   fox_f_bias[l], w_out[l], norm2_g[l], w_router_group[l], b_router_group[l], w_router_expert[l],
                   b_router_expert[l], w1[l], w3[l], w2[l], final_g, l == depth - 1)
    return x
```

```python
import functools

import jax
import jax.numpy as jnp
from jax import lax
from jax.experimental import pallas as pl
from jax.experimental.pallas import tpu as pltpu

F32 = jnp.float32
BF16 = jnp.bfloat16

D_MODEL = 2048
EPS = 1e-6
CHUNK = 64
HEADS = 8
HEAD_DIM = 128
HEAD_W = HEADS * HEAD_DIM
CONV_K = 4
N_GROUPS = 4
EXPERTS_PER_GROUP = 8
N_EXPERTS = N_GROUPS * EXPERTS_PER_GROUP
D_EXPERT = 512
LANES = 128
SUBLANES = 8
MAIN_W = 7 * HEAD_W
VMEM_LIMIT = 56 * 1024 * 1024

L_GC, L_BETA, L_F, L_EGC, L_EK, L_ELAST = 0, 8, 16, 24, 32, 40


def _cparams(sem):
    return pltpu.CompilerParams(dimension_semantics=sem, vmem_limit_bytes=VMEM_LIMIT)


def _split_bf16(a):
    hi = a.astype(BF16)
    lo = (a - hi.astype(F32)).astype(BF16)
    return hi, lo


def _dot(a, b):
    return jnp.dot(a, b, preferred_element_type=F32)


def _dot_nt(a, b):
    return lax.dot_general(a, b, (((1,), (1,)), ((), ())), preferred_element_type=F32)


def _dot3(a, b):
    ah, al = _split_bf16(a)
    bh, bl = _split_bf16(b)
    return _dot(ah, bh) + (_dot(al, bh) + _dot(ah, bl))


def _dot3_pre(a, b_hi_lo):
    ah, al = _split_bf16(a)
    n = b_hi_lo.shape[1] // 2
    both = _dot(ah, b_hi_lo)
    return both[:, :n] + (both[:, n:] + _dot(al, b_hi_lo[:, :n]))


def _softplus(x):
    return jnp.maximum(x, 0.0) + jnp.log1p(jnp.exp(-jnp.abs(x)))


def _silu(x):
    return x * jax.nn.sigmoid(x)


def _adaln_kernel(c_ref, w_ref, b_ref, o_ref):
    c = c_ref[...]
    o_ref[...] = _dot(_silu(c).astype(BF16), w_ref[...].astype(BF16)) + b_ref[...]


def _adaln(c, w, b):
    bsz = c.shape[0]
    n = w.shape[1]
    tn = 1024
    cp = jnp.zeros((SUBLANES, D_MODEL), F32).at[:bsz].set(c)
    out = pl.pallas_call(
        _adaln_kernel,
        grid=(n // tn,),
        in_specs=[
            pl.BlockSpec((SUBLANES, D_MODEL), lambda j: (0, 0)),
            pl.BlockSpec((D_MODEL, tn), lambda j: (0, j)),
            pl.BlockSpec((1, tn), lambda j: (0, j)),
        ],
        out_specs=pl.BlockSpec((SUBLANES, tn), lambda j: (0, j)),
        out_shape=jax.ShapeDtypeStruct((SUBLANES, n), F32),
        compiler_params=_cparams(("parallel",)),
        name="adaln",
    )(cp, w, b.reshape(1, n))
    return out[:bsz]


INPROJ_TM = 2048
INPROJ_TN = 512
AB_ROW0 = 4 * HEAD_W
F_ROW0 = MAIN_W + 2 * HEADS


def _modulated_norm(x, g, sc, sh):
    r = lax.rsqrt(jnp.mean(x * x, axis=-1, keepdims=True) + EPS)
    return (x * r) * (g * (1.0 + sc)) + sh


def _inproj_kernel(h_ref, w_ref, o_ref):
    o_ref[...] = _dot_nt(h_ref[...], w_ref[...].astype(BF16)).astype(BF16)


def _inproj(h, w_t):
    n = h.shape[0]
    tm, tn = INPROJ_TM, INPROJ_TN
    first_part = AB_ROW0 // tn

    def w_rows(i, j):
        return (SUBLANES * (j * (tn // SUBLANES) + jnp.where(j >= first_part, 2 * HEADS // SUBLANES, 0)), 0)

    return pl.pallas_call(
        _inproj_kernel,
        grid=(n // tm, MAIN_W // tn),
        in_specs=[
            pl.BlockSpec((tm, D_MODEL), lambda i, j: (i, 0)),
            pl.BlockSpec((pl.Element(tn), pl.Element(D_MODEL)), w_rows),
        ],
        out_specs=pl.BlockSpec((tm, tn), lambda i, j: (i, j)),
        out_shape=jax.ShapeDtypeStruct((n, MAIN_W), BF16),
        compiler_params=_cparams(("parallel", "arbitrary")),
        name="inproj",
    )(h, w_t)


GATES_TB = 256


def _split3(a):
    p0 = a.astype(BF16)
    r1 = a - p0.astype(F32)
    p1 = r1.astype(BF16)
    p2 = (r1 - p1.astype(F32)).astype(BF16)
    return p0, p1, p2


def _prenorm_gates_kernel(x_ref, sc_ref, sh_ref, g_ref, wab_ref, wf_ref, alog_ref, dt_ref, fb_ref,
                          h_ref, o_ref, ws_ref, carry_ref):
    tb = GATES_TB

    @pl.when(pl.program_id(1) == 0)
    def _():
        carry_ref[...] = jnp.zeros_like(carry_ref)
        gate_rows = jnp.concatenate([wab_ref[...], wf_ref[...]], axis=0)
        pad = jnp.zeros((LANES - gate_rows.shape[0], D_MODEL), F32)
        hi, lo = _split_bf16(jnp.concatenate([gate_rows, pad], axis=0))
        ws_ref[0:LANES, :] = hi
        ws_ref[LANES:, :] = lo

    hb = _modulated_norm(x_ref[...], g_ref[...], sc_ref[...], sh_ref[...]).astype(BF16)
    h_ref[...] = hb
    both = _dot_nt(hb, ws_ref[...])
    x = both[:, :LANES] + both[:, LANES:]
    lane = lax.broadcasted_iota(jnp.int32, (tb, LANES), 1)
    g = -jnp.exp(alog_ref[...]) * _softplus(x + dt_ref[...])
    beta = jax.nn.sigmoid(x)
    lf = -_softplus(-(x + fb_ref[...]))

    ri = lax.broadcasted_iota(jnp.int32, (tb, tb), 0)
    ci = lax.broadcasted_iota(jnp.int32, (tb, tb), 1)
    same_chunk = (ri // CHUNK) == (ci // CHUNK)
    tri = (ri >= ci)
    m_all = jnp.where(tri, 1.0, 0.0).astype(BF16)
    m_chunk = jnp.where(tri & same_chunk, 1.0, 0.0).astype(BF16)
    m_tot = jnp.where(same_chunk, 1.0, 0.0).astype(BF16)

    in_a = lane < HEADS
    parts = jnp.concatenate(_split3(jnp.where(in_a, g, lf)), axis=1)
    sums = _dot(jnp.concatenate([m_chunk, m_tot, m_all], axis=0), parts)
    sums = sums[:, :LANES] + (sums[:, LANES:2 * LANES] + sums[:, 2 * LANES:])
    gc = sums[:tb]
    glast = sums[tb:2 * tb]
    fcum = sums[2 * tb:] + carry_ref[...]
    carry_ref[...] = fcum[tb - 1:tb, :]

    egc = jnp.where(in_a, jnp.exp(gc), 0.0)
    ek = jnp.where(in_a, jnp.exp(glast - gc), 0.0)
    elast = jnp.where(in_a, jnp.exp(glast), 0.0)
    out = jnp.where(in_a, gc, jnp.where(lane < 2 * HEADS, beta, jnp.where(lane < 3 * HEADS, fcum, 0.0)))
    out = out + pltpu.roll(egc, L_EGC, 1) + pltpu.roll(ek, L_EK, 1) + pltpu.roll(elast, L_ELAST, 1)
    o_ref[...] = out


def _prenorm_gates(x, sc1, sh1, g, w_t, alog_row, dt_row, fb_row):
    bsz, seq, _ = x.shape
    tb = GATES_TB
    row = pl.BlockSpec((1, LANES), lambda b, t: (0, 0))
    modrow = pl.BlockSpec((None, 1, D_MODEL), lambda b, t: (b, 0, 0))
    return pl.pallas_call(
        _prenorm_gates_kernel,
        grid=(bsz, seq // tb),
        in_specs=[
            pl.BlockSpec((None, tb, D_MODEL), lambda b, t: (b, t, 0)),
            modrow, modrow,
            pl.BlockSpec((1, D_MODEL), lambda b, t: (0, 0)),
            pl.BlockSpec((pl.Element(2 * HEADS), pl.Element(D_MODEL)), lambda b, t: (AB_ROW0, 0)),
            pl.BlockSpec((pl.Element(HEADS), pl.Element(D_MODEL)), lambda b, t: (F_ROW0, 0)),
            row, row, row,
        ],
        out_specs=[
            pl.BlockSpec((None, tb, D_MODEL), lambda b, t: (b, t, 0)),
            pl.BlockSpec((None, tb, LANES), lambda b, t: (b, t, 0)),
        ],
        out_shape=[
            jax.ShapeDtypeStruct((bsz, seq, D_MODEL), BF16),
            jax.ShapeDtypeStruct((bsz, seq, LANES), F32),
        ],
        scratch_shapes=[pltpu.VMEM((2 * LANES, D_MODEL), BF16), pltpu.VMEM((1, LANES), F32)],
        compiler_params=_cparams(("parallel", "arbitrary")),
        name="prenorm_gates",
    )(x, sc1, sh1, g, w_t, w_t, alog_row, dt_row, fb_row)


DN_TB = 256
DN_GROUP = 4
HALO = SUBLANES


def _bdot(a, b):
    return lax.dot_general(a, b, (((2,), (1,)), ((0,), (0,))), preferred_element_type=F32)


def _bdot_nt(a, b):
    return lax.dot_general(a, b, (((2,), (2,)), ((0,), (0,))), preferred_element_type=F32)


def _inv_unit_lower(a, eye, blk16, blk32):
    n = jnp.where(blk16, -a, 0.0)
    e1 = jnp.where(blk32 & jnp.logical_not(blk16), a, 0.0).astype(BF16)
    e2 = jnp.where(blk32, 0.0, a).astype(BF16)
    t = eye + n
    p = n.astype(BF16)
    for _ in range(3):
        p = _bdot(p, p).astype(BF16)
        t = t + _bdot(t.astype(BF16), p)
    for e in (e1, e2):
        tb = t.astype(BF16)
        t = t - _bdot(_bdot(tb, e).astype(BF16), tb)
    return t


def _deltanet_kernel(q_ref, k_ref, v_ref, z_ref, wq_ref, wk_ref, wv_ref, slab_ref, gct_ref, og_ref,
                     o_ref, ext_ref, qn_ref, kn_ref, vv_ref, s_ref):
    tb = DN_TB

    @pl.when(pl.program_id(1) == 0)
    def _():
        ext_ref[:, 0:HALO, :] = jnp.zeros((3, HALO, HEAD_W), F32)
        s_ref[...] = jnp.zeros_like(s_ref)

    for idx, (u_ref, w_ref) in enumerate(((q_ref, wq_ref), (k_ref, wk_ref), (v_ref, wv_ref))):
        for h in range(HEADS):
            cols = slice(h * HEAD_DIM, (h + 1) * HEAD_DIM)
            ext_ref[idx, HALO:HALO + tb, cols] = u_ref[:, cols].astype(F32)
            y = None
            for j in range(CONV_K):
                start = HALO - (CONV_K - 1) + j
                term = ext_ref[idx, start:start + tb, cols] * w_ref[j:j + 1, cols]
                y = term if y is None else y + term
            y = _silu(y)
            if idx == 2:
                vv_ref[h] = y
            else:
                yn = y * lax.rsqrt(jnp.sum(y * y, axis=-1, keepdims=True) + EPS)
                if idx == 0:
                    qn_ref[h] = (yn * (HEAD_DIM ** -0.5)).astype(BF16)
                else:
                    kn_ref[h] = yn.astype(BF16)
        ext_ref[idx, 0:HALO, :] = ext_ref[idx, tb:tb + HALO, :]

    ri = lax.broadcasted_iota(jnp.int32, (DN_GROUP * HEADS, CHUNK, CHUNK), 1)
    ci = lax.broadcasted_iota(jnp.int32, (DN_GROUP * HEADS, CHUNK, CHUNK), 2)
    incl = ri >= ci
    strict = ri > ci
    eye = jnp.where(ri == ci, 1.0, 0.0)
    blk16 = (ri // 16) == (ci // 16)
    blk32 = (ri // 32) == (ci // 32)

    def group_body(c, _):
        rows = [pl.ds(pl.multiple_of((c * DN_GROUP + g) * CHUNK, CHUNK), CHUNK) for g in range(DN_GROUP)]
        slabs = [slab_ref[r, :] for r in rows]

        def col(off, width):
            return jnp.stack([jnp.broadcast_to(sl[:, off + h:off + h + 1], (CHUNK, width))
                              for sl in slabs for h in range(HEADS)])

        def grouped(ref):
            return jnp.concatenate([ref[:, r, :] for r in rows], axis=0)

        q = grouped(qn_ref)
        k = grouped(kn_ref)
        v = grouped(vv_ref)
        beta = col(L_BETA, HEAD_DIM)
        egc = col(L_EGC, HEAD_DIM)
        gc_row = jnp.concatenate([gct_ref[c * DN_GROUP + g] for g in range(DN_GROUP)], axis=0)

        decay = jnp.where(incl, jnp.exp(col(L_GC, CHUNK) - gc_row), 0.0)
        kk = _bdot_nt(k, k)
        qk = (_bdot_nt(q, k) * decay).astype(BF16)
        a = jnp.where(strict, kk * decay * beta[:, :, :CHUNK], 0.0)
        t = _inv_unit_lower(a, eye, blk16, blk32).astype(BF16)

        kf = k.astype(F32)
        vb = (v * beta).astype(BF16)
        kbg = (kf * (beta * egc)).astype(BF16)
        u = _bdot(t, vb)
        w = _bdot(t, kbg).astype(BF16)
        qd = (q.astype(F32) * egc).astype(BF16)
        kd = kf * col(L_EK, HEAD_DIM)
        kdt = jnp.stack([kd[n].T for n in range(DN_GROUP * HEADS)]).astype(BF16)

        for g in range(DN_GROUP):
            sel = slice(g * HEADS, (g + 1) * HEADS)
            s = s_ref[...]
            sb = s.astype(BF16)
            vnb = (u[sel] - _bdot(w[sel], sb)).astype(BF16)
            o = _bdot(qd[sel], sb) + _bdot(qk[sel], vnb)
            elast = jnp.stack([jnp.broadcast_to(slabs[g][CHUNK - 1:CHUNK, L_ELAST + h:L_ELAST + h + 1],
                                                (HEAD_DIM, HEAD_DIM)) for h in range(HEADS)])
            s_ref[...] = s * elast + _bdot(kdt[sel], vnb)

            r = lax.rsqrt(jnp.mean(o * o, axis=-1, keepdims=True) + EPS)
            on = (o * r) * og_ref[...]
            for h in range(HEADS):
                cols = slice(h * HEAD_DIM, (h + 1) * HEAD_DIM)
                o_ref[rows[g], cols] = (on[h] * _silu(z_ref[rows[g], cols].astype(F32))).astype(BF16)
        return 0

    lax.fori_loop(0, tb // (CHUNK * DN_GROUP), group_body, 0)


def _deltanet(proj3, conv_w, slab, gct, onorm_g):
    bsz, seq, _ = proj3.shape
    tb = DN_TB
    nct = tb // CHUNK

    def colblk(j):
        return pl.BlockSpec((None, tb, HEAD_W), lambda b, t: (b, t, j))

    def wblk(j):
        return pl.BlockSpec((CONV_K, HEAD_W), lambda b, t: (0, j))

    return pl.pallas_call(
        _deltanet_kernel,
        grid=(bsz, seq // tb),
        in_specs=[
            colblk(0), colblk(1), colblk(2), colblk(3),
            wblk(0), wblk(1), wblk(2),
            pl.BlockSpec((None, tb, LANES), lambda b, t: (b, t, 0)),
            pl.BlockSpec((None, nct, HEADS, 1, CHUNK), lambda b, t: (b, t, 0, 0, 0)),
            pl.BlockSpec((1, HEAD_DIM), lambda b, t: (0, 0)),
        ],
        out_specs=pl.BlockSpec((None, tb, HEAD_W), lambda b, t: (b, t, 0)),
        out_shape=jax.ShapeDtypeStruct((bsz, seq, HEAD_W), BF16),
        scratch_shapes=[
            pltpu.VMEM((3, tb + HALO, HEAD_W), F32),
            pltpu.VMEM((HEADS, tb, HEAD_DIM), BF16),
            pltpu.VMEM((HEADS, tb, HEAD_DIM), BF16),
            pltpu.VMEM((HEADS, tb, HEAD_DIM), F32),
            pltpu.VMEM((HEADS, HEAD_DIM, HEAD_DIM), F32),
        ],
        compiler_params=_cparams(("parallel", "arbitrary")),
        name="deltanet",
    )(proj3, proj3, proj3, proj3, conv_w, conv_w, conv_w, slab, gct, onorm_g)


FOX_T = 512
FOX_STRIP = 32
NEG_INF = float("-inf")
LOG2E = 1.4426950408889634


def _fox_kernel(q_ref, k_ref, v_ref, f_ref, o_ref, s_a_ref, s_b_ref, p_a_ref, p_b_ref, m_ref, l_ref, alpha_ref,
                acc_ref):
    tq = FOX_T
    i = pl.program_id(2)
    lane_blocks = tq // LANES
    qs = (q_ref[...].astype(F32) * (HEAD_DIM ** -0.5 * LOG2E)).astype(BF16)
    f0 = f_ref[i][:, 0:1]
    m_ref[...] = jnp.full_like(m_ref, NEG_INF)
    l_ref[...] = jnp.zeros_like(l_ref)
    acc_ref[...] = jnp.zeros_like(acc_ref)

    def keys(j):
        return pl.ds(pl.multiple_of(j * tq, tq), tq)

    def scores(j):
        return _dot_nt(qs, k_ref[keys(j), :])

    def softmax(j, s_view, p_view, masked):
        bias = (f0 - f_ref[j]) * LOG2E

        def logits(r):
            rows = slice(r * FOX_STRIP, (r + 1) * FOX_STRIP)
            s = s_view[rows, :] + bias
            if masked:
                ri = r * FOX_STRIP + lax.broadcasted_iota(jnp.int32, (FOX_STRIP, tq), 0)
                ci = lax.broadcasted_iota(jnp.int32, (FOX_STRIP, tq), 1)
                s = jnp.where(ci <= ri, s, NEG_INF)
            return rows, s

        for r in range(tq // FOX_STRIP):
            rows, s = logits(r)
            m_old = m_ref[rows, :]
            m_new = jnp.maximum(m_old, jnp.max(s, axis=-1, keepdims=True))
            alpha_ref[rows, :] = jnp.exp2(m_old - m_new)
            m_ref[rows, :] = m_new
        for r in range(tq // FOX_STRIP):
            rows, s = logits(r)
            p = jnp.exp2(s - jnp.concatenate([m_ref[rows, :]] * lane_blocks, axis=1))
            alpha = alpha_ref[rows, :]
            l_ref[rows, :] = alpha * l_ref[rows, :] + jnp.sum(p, axis=-1, keepdims=True)
            p_view[rows, :] = p.astype(BF16)

    s_a_ref[...] = scores(i)
    s_b_ref[...] = scores(jnp.maximum(i - 1, 0))
    softmax(i, s_a_ref, p_a_ref, True)

    def step(t, s_cur, p_cur, s_nxt, p_prv):
        j = i - t
        s_nxt[...] = scores(jnp.maximum(j - 1, 0))
        pv = _dot(p_prv[...], v_ref[keys(j + 1), :])
        softmax(j, s_cur, p_cur, False)
        acc_ref[...] = alpha_ref[...] * (acc_ref[...] + pv)

    def pair(u):
        step(2 * u + 1, s_b_ref, p_b_ref, s_a_ref, p_a_ref)
        step(2 * u + 2, s_a_ref, p_a_ref, s_b_ref, p_b_ref)

    def quad(w, _):
        pair(2 * w)
        pair(2 * w + 1)
        return 0

    quads = lax.shift_right_logical(i, 2)
    lax.fori_loop(0, quads, quad, 0)
    pl.when((i & 2) != 0)(functools.partial(pair, 2 * quads))

    def finish(p_last):
        acc = acc_ref[...] + _dot(p_last[...], v_ref[keys(0), :])
        o_ref[...] = (acc / l_ref[...]).astype(BF16)

    @pl.when((i & 1) == 1)
    def _():
        step(i, s_b_ref, p_b_ref, s_a_ref, p_a_ref)
        finish(p_b_ref)

    @pl.when((i & 1) == 0)
    def _():
        finish(p_a_ref)


def _fox(proj3, f_rows):
    bsz, seq, _ = proj3.shape
    tq = FOX_T
    qb, kb, vb = 4 * HEADS, 5 * HEADS, 6 * HEADS
    return pl.pallas_call(
        _fox_kernel,
        grid=(bsz, HEADS, seq // tq),
        in_specs=[
            pl.BlockSpec((None, tq, HEAD_DIM), lambda b, h, i: (b, i, qb + h)),
            pl.BlockSpec((None, seq, HEAD_DIM), lambda b, h, i: (b, 0, kb + h)),
            pl.BlockSpec((None, seq, HEAD_DIM), lambda b, h, i: (b, 0, vb + h)),
            pl.BlockSpec((None, None, seq // tq, 1, tq), lambda b, h, i: (b, h, 0, 0, 0)),
        ],
        out_specs=pl.BlockSpec((None, tq, HEAD_DIM), lambda b, h, i: (b, i, h)),
        out_shape=jax.ShapeDtypeStruct((bsz, seq, HEAD_W), BF16),
        scratch_shapes=[
            pltpu.VMEM((tq, tq), F32),
            pltpu.VMEM((tq, tq), F32),
            pltpu.VMEM((tq, tq), BF16),
            pltpu.VMEM((tq, tq), BF16),
            pltpu.VMEM((tq, LANES), F32),
            pltpu.VMEM((tq, LANES), F32),
            pltpu.VMEM((tq, LANES), F32),
            pltpu.VMEM((tq, HEAD_DIM), F32),
        ],
        compiler_params=_cparams(("parallel", "parallel", "arbitrary")),
        name="fox",
    )(proj3, proj3, proj3, f_rows)


OUT_TM = 512
R_E0, R_E1, R_W0, R_W1 = 0, 1, 2, 3


def _first_argmax(vals, lane):
    m = jnp.max(vals, axis=-1, keepdims=True)
    idx = jnp.min(jnp.where(vals == m, lane, LANES), axis=-1, keepdims=True)
    return m, idx


def _outproj_kernel(oa_ref, ob_ref, wa_ref, wb_ref, x_ref, g1_ref, sc_ref, sh_ref, g_ref,
                    wr_ref, br_ref, x1_ref, h2_ref, r_ref, cnt_ref):
    mix = _dot(oa_ref[...], wa_ref[...]) + _dot(ob_ref[...], wb_ref[...])
    x1 = x_ref[...] + g1_ref[...] * mix
    x1_ref[...] = x1
    h2 = _modulated_norm(x1, g_ref[...], sc_ref[...], sh_ref[...])
    h2_ref[...] = h2

    logits = _dot3_pre(h2, wr_ref[...]) + br_ref[...]
    tm = logits.shape[0]
    lane = lax.broadcasted_iota(jnp.int32, (tm, LANES), 1)
    gl = jnp.where(lane < N_GROUPS, logits, NEG_INF)
    gmax, gidx = _first_argmax(gl, lane)
    pg = 1.0 / jnp.sum(jnp.exp(gl - gmax), axis=-1, keepdims=True)
    e_lane = lane - N_GROUPS
    in_grp = (e_lane >= gidx * EXPERTS_PER_GROUP) & (e_lane < (gidx + 1) * EXPERTS_PER_GROUP)
    el = jnp.where(in_grp, logits, NEG_INF)
    v0, i0 = _first_argmax(el, lane)
    v1, i1 = _first_argmax(jnp.where(lane == i0, NEG_INF, el), lane)
    ex = jnp.exp(v1 - v0)
    w0 = pg / (1.0 + ex)
    w1 = pg * ex / (1.0 + ex)
    e0 = (i0 - N_GROUPS).astype(F32)
    e1 = (i1 - N_GROUPS).astype(F32)
    r_ref[...] = jnp.where(lane == R_E0, e0, jnp.where(lane == R_E1, e1,
                           jnp.where(lane == R_W0, w0, jnp.where(lane == R_W1, w1, 0.0))))

    @pl.when(pl.program_id(0) == 0)
    def _():
        cnt_ref[...] = jnp.zeros_like(cnt_ref)

    picked = (lane == i0 - N_GROUPS) | (lane == i1 - N_GROUPS)
    cnt_ref[...] += jnp.sum(jnp.where(picked, 1.0, 0.0), axis=0, keepdims=True)


def _outproj(o_a, o_b, wa, wb, x2, g1, sc2, sh2, g, wr_hi_lo, br, seq):
    n = x2.shape[0]
    tm = OUT_TM
    per_b = seq // tm
    modrow = pl.BlockSpec((None, 1, D_MODEL), lambda i: (i // per_b, 0, 0))
    const = lambda shape: pl.BlockSpec(shape, lambda i: (0, 0))
    return pl.pallas_call(
        _outproj_kernel,
        grid=(n // tm,),
        in_specs=[
            pl.BlockSpec((tm, HEAD_W), lambda i: (i, 0)),
            pl.BlockSpec((tm, HEAD_W), lambda i: (i, 0)),
            const((HEAD_W, D_MODEL)), const((HEAD_W, D_MODEL)),
            pl.BlockSpec((tm, D_MODEL), lambda i: (i, 0)),
            modrow, modrow, modrow,
            const((1, D_MODEL)),
            const((D_MODEL, 2 * LANES)), const((1, LANES)),
        ],
        out_specs=[
            pl.BlockSpec((tm, D_MODEL), lambda i: (i, 0)),
            pl.BlockSpec((tm, D_MODEL), lambda i: (i, 0)),
            pl.BlockSpec((tm, LANES), lambda i: (i, 0)),
            pl.BlockSpec((1, LANES), lambda i: (0, 0)),
        ],
        out_shape=[
            jax.ShapeDtypeStruct((n, D_MODEL), F32),
            jax.ShapeDtypeStruct((n, D_MODEL), F32),
            jax.ShapeDtypeStruct((n, LANES), F32),
            jax.ShapeDtypeStruct((1, LANES), F32),
        ],
        compiler_params=_cparams(("arbitrary",)),
        name="outproj",
    )(o_a, o_b, wa, wb, x2, g1, sc2, sh2, g, wr_hi_lo, br)


MOE_TM = 256
ROUTE_TB = 512


def _route_kernel(r_ref, cnt_ref, pos_ref, run_ref, base_ref):
    tb = ROUTE_TB
    t = pl.program_id(0)
    r = r_ref[...]
    lane = lax.broadcasted_iota(jnp.int32, (tb, LANES), 1)
    e0 = r[:, R_E0:R_E0 + 1].astype(jnp.int32)
    e1 = r[:, R_E1:R_E1 + 1].astype(jnp.int32)
    oh0 = lane == e0
    oh1 = lane == e1
    both = jnp.where(oh0 | oh1, 1.0, 0.0)

    @pl.when(t == 0)
    def _():
        counts = cnt_ref[...]
        padded = jnp.ceil(counts / MOE_TM) * MOE_TM
        li = lax.broadcasted_iota(jnp.int32, (LANES, LANES), 0)
        lj = lax.broadcasted_iota(jnp.int32, (LANES, LANES), 1)
        upper = jnp.where(li < lj, 1.0, 0.0).astype(BF16)
        hi = jnp.floor(padded / 256.0)
        lo = padded - hi * 256.0
        hi8 = jnp.broadcast_to(hi, (SUBLANES, LANES)).astype(BF16)
        lo8 = jnp.broadcast_to(lo, (SUBLANES, LANES)).astype(BF16)
        base = _dot(hi8, upper) * 256.0 + _dot(lo8, upper)
        base_ref[...] = base[0:1, :]
        run_ref[...] = jnp.zeros_like(run_ref)

    ri = lax.broadcasted_iota(jnp.int32, (tb, tb), 0)
    ci = lax.broadcasted_iota(jnp.int32, (tb, tb), 1)
    strict = jnp.where(ri > ci, 1.0, 0.0).astype(BF16)
    before = _dot(strict, both.astype(BF16)) + run_ref[...] + base_ref[...]
    p0 = jnp.sum(jnp.where(oh0, before, 0.0), axis=-1, keepdims=True)
    p1 = jnp.sum(jnp.where(oh1, before, 0.0), axis=-1, keepdims=True)
    pos_ref[...] = jnp.where(lane == 0, p0, jnp.where(lane == 1, p1, 0.0)).astype(jnp.int32)
    run_ref[...] = run_ref[...] + jnp.sum(both, axis=0, keepdims=True)


def _route(rslab, counts):
    n = rslab.shape[0]
    tb = ROUTE_TB
    return pl.pallas_call(
        _route_kernel,
        grid=(n // tb,),
        in_specs=[pl.BlockSpec((tb, LANES), lambda t: (t, 0)), pl.BlockSpec((1, LANES), lambda t: (0, 0))],
        out_specs=pl.BlockSpec((tb, LANES), lambda t: (t, 0)),
        out_shape=jax.ShapeDtypeStruct((n, LANES), jnp.int32),
        scratch_shapes=[pltpu.VMEM((1, LANES), F32), pltpu.VMEM((1, LANES), F32)],
        compiler_params=_cparams(("arbitrary",)),
        name="route",
    )(rslab, counts)


DISP_TB = 1024
DMA_UNROLL = 8


def _dispatch_kernel(pad_start_ref, pad_len_ref, used_ref, pos_ref, h2_ref, xs_ref, zero_ref, sem, zsem):
    t = pl.program_id(0)
    tb = DISP_TB
    n_tiles = xs_ref.shape[0] // MOE_TM

    def row_copy(g, r, k):
        dst_row = pos_ref[0, 2 * SUBLANES * g + 2 * r + k]
        return pltpu.make_async_copy(h2_ref.at[g, pl.ds(r, 1)], xs_ref.at[pl.ds(dst_row, 1)], sem)

    def issue(g, _):
        for r in range(SUBLANES):
            row_copy(g, r, 0).start(priority=0)
            row_copy(g, r, 1).start(priority=1)
        return 0

    lax.fori_loop(0, tb // SUBLANES, issue, 0)

    @pl.when(t == 0)
    def _():
        zero_ref[...] = jnp.zeros_like(zero_ref)

        def zero_rows(wait, off, rows):
            cp = pltpu.make_async_copy(zero_ref.at[pl.ds(0, rows)], xs_ref.at[pl.ds(off, rows)], zsem)
            cp.wait() if wait else cp.start()

        def per_expert(wait, e, _):
            start = pad_start_ref[e]
            head = (-start) & (SUBLANES - 1)
            for r in range(SUBLANES - 1):
                pl.when(r < head)(functools.partial(zero_rows, wait, start + r, 1))
            off = start + head
            rest = pad_len_ref[e] - head
            piece = MOE_TM // 2
            while piece >= SUBLANES:
                take = (rest & piece) != 0
                pl.when(take)(functools.partial(zero_rows, wait, pl.multiple_of(off, SUBLANES), piece))
                off = off + jnp.where(take, piece, 0)
                piece //= 2
            return 0

        def per_tile(wait, i, _):
            zero_rows(wait, pl.multiple_of(i * MOE_TM, MOE_TM), MOE_TM)
            return 0

        for wait in (False, True):
            lax.fori_loop(0, N_EXPERTS, functools.partial(per_expert, wait), 0)
            lax.fori_loop(used_ref[0], n_tiles, functools.partial(per_tile, wait), 0)

    def drain(g, _):
        for r in range(SUBLANES):
            row_copy(g, r, 0).wait()
            row_copy(g, r, 1).wait()
        return 0

    lax.fori_loop(0, tb // SUBLANES, drain, 0)


def _dispatch(pad_start, pad_len, used, pos2, h2, p_rows):
    n = h2.shape[0]
    tb = DISP_TB
    return pl.pallas_call(
        _dispatch_kernel,
        grid_spec=pltpu.PrefetchScalarGridSpec(
            num_scalar_prefetch=3,
            grid=(n // tb,),
            in_specs=[
                pl.BlockSpec((None, 1, 2 * tb), lambda t, *_: (t, 0, 0), memory_space=pltpu.SMEM),
                pl.BlockSpec((tb // SUBLANES, SUBLANES, D_MODEL), lambda t, *_: (t, 0, 0)),
            ],
            out_specs=pl.BlockSpec(memory_space=pl.ANY),
            scratch_shapes=[
                pltpu.VMEM((MOE_TM, D_MODEL), F32),
                pltpu.SemaphoreType.DMA(()),
                pltpu.SemaphoreType.DMA(()),
            ],
        ),
        out_shape=jax.ShapeDtypeStruct((p_rows, D_MODEL), F32),
        compiler_params=_cparams(("arbitrary",)),
        name="dispatch",
    )(pad_start, pad_len, used, pos2, h2.reshape(n // SUBLANES, SUBLANES, D_MODEL))


def _experts_kernel(te_ref, tv_ref, tf_ref, ts_ref, tn_ref, xs_ref, w1_hbm, w3_hbm, w2_hbm, ys_ref,
                    w1f_ref, w3f_ref, w2f_ref, w1b_ref, w3b_ref, w2b_ref, sem):
    i = pl.program_id(0)

    def weight_copies(e, slot):
        return [pltpu.make_async_copy(src.at[e], dst.at[slot], sem.at[slot])
                for src, dst in ((w1_hbm, w1f_ref), (w3_hbm, w3f_ref), (w2_hbm, w2f_ref))]

    @pl.when(tv_ref[i] != 0)
    def _():
        @pl.when(tf_ref[i] != 0)
        def _():
            slot = ts_ref[i]

            @pl.when(i == 0)
            def _():
                for cp in weight_copies(te_ref[i], slot):
                    cp.start()

            for cp in weight_copies(te_ref[i], slot):
                cp.wait()

            @pl.when(tn_ref[i] >= 0)
            def _():
                for cp in weight_copies(tn_ref[i], 1 - slot):
                    cp.start()

            w1b_ref[...] = w1f_ref[slot].astype(BF16)
            w3b_ref[...] = w3f_ref[slot].astype(BF16)
            w2b_ref[...] = w2f_ref[slot].astype(BF16)

        x = xs_ref[...].astype(BF16)
        a = _dot(x, w1b_ref[...])
        b = _dot(x, w3b_ref[...])
        ys_ref[...] = _dot((_silu(a) * b).astype(BF16), w2b_ref[...])

    @pl.when(tv_ref[i] == 0)
    def _():
        ys_ref[...] = jnp.zeros_like(ys_ref)


def _experts(tile_expert, tile_valid, tile_first, tile_slot, tile_next, xs, w1, w3, w2):
    p_rows = xs.shape[0]
    tm = MOE_TM
    hbm = pl.BlockSpec(memory_space=pl.ANY)
    return pl.pallas_call(
        _experts_kernel,
        grid_spec=pltpu.PrefetchScalarGridSpec(
            num_scalar_prefetch=5,
            grid=(p_rows // tm,),
            in_specs=[pl.BlockSpec((tm, D_MODEL), lambda i, te, tv, *_: (jnp.where(tv[i] != 0, i, 0), 0)),
                      hbm, hbm, hbm],
            out_specs=pl.BlockSpec((tm, D_MODEL), lambda i, *_: (i, 0)),
            scratch_shapes=[
                pltpu.VMEM((2, D_MODEL, D_EXPERT), F32),
                pltpu.VMEM((2, D_MODEL, D_EXPERT), F32),
                pltpu.VMEM((2, D_EXPERT, D_MODEL), F32),
                pltpu.VMEM((D_MODEL, D_EXPERT), BF16),
                pltpu.VMEM((D_MODEL, D_EXPERT), BF16),
                pltpu.VMEM((D_EXPERT, D_MODEL), BF16),
                pltpu.SemaphoreType.DMA((2,)),
            ],
        ),
        out_shape=jax.ShapeDtypeStruct((p_rows, D_MODEL), F32),
        compiler_params=_cparams(("arbitrary",)),
        name="experts",
    )(tile_expert, tile_valid, tile_first, tile_slot, tile_next, xs, w1, w3, w2)


COMB_TB = 256


def _combine_kernel(pos_ref, pos_next_ref, ys_ref, r_ref, x1_ref, g2_ref, fg_ref, o_ref, buf_ref, sem, *, final):
    tb = COMB_TB
    t = pl.program_id(0)
    slot = t & 1

    def row_copy(p_ref, s, g, r, k):
        src_row = p_ref[0, 2 * SUBLANES * g + 2 * r + k]
        return pltpu.make_async_copy(ys_ref.at[pl.ds(src_row, 1)], buf_ref.at[s, k, g, pl.ds(r, 1)], sem.at[s])

    def issue(p_ref, s):
        def body(g, _):
            for r in range(SUBLANES):
                row_copy(p_ref, s, g, r, 0).start(priority=0)
                row_copy(p_ref, s, g, r, 1).start(priority=1)
            return 0

        lax.fori_loop(0, tb // SUBLANES, body, 0)

    def drain(g, _):
        for r in range(SUBLANES):
            row_copy(pos_ref, slot, g, r, 0).wait()
            row_copy(pos_ref, slot, g, r, 1).wait()
        return 0

    pl.when(t == 0)(functools.partial(issue, pos_ref, 0))
    pl.when(t + 1 < pl.num_programs(0))(functools.partial(issue, pos_next_ref, 1 - slot))
    lax.fori_loop(0, tb // SUBLANES, drain, 0)
    r = r_ref[...]
    y0 = buf_ref[slot, 0].reshape(tb, D_MODEL)
    y1 = buf_ref[slot, 1].reshape(tb, D_MODEL)
    y = r[:, R_W0:R_W0 + 1] * y0 + r[:, R_W1:R_W1 + 1] * y1
    x2 = x1_ref[...] + g2_ref[...] * y
    if final:
        x2 = (x2 * lax.rsqrt(jnp.mean(x2 * x2, axis=-1, keepdims=True) + EPS)) * fg_ref[...]
    o_ref[...] = x2


def _combine(pos2, ys, rslab, x1, g2, final_g, seq, final):
    n = x1.shape[0]
    tb = COMB_TB
    per_b = seq // tb
    return pl.pallas_call(
        functools.partial(_combine_kernel, final=final),
        grid=(n // tb,),
        in_specs=[
            pl.BlockSpec((None, 1, 2 * tb), lambda i: (i, 0, 0), memory_space=pltpu.SMEM),
            pl.BlockSpec((None, 1, 2 * tb), lambda i: (jnp.minimum(i + 1, n // tb - 1), 0, 0),
                         memory_space=pltpu.SMEM),
            pl.BlockSpec(memory_space=pl.ANY),
            pl.BlockSpec((tb, LANES), lambda i: (i, 0)),
            pl.BlockSpec((tb, D_MODEL), lambda i: (i, 0)),
            pl.BlockSpec((None, 1, D_MODEL), lambda i: (i // per_b, 0, 0)),
            pl.BlockSpec((1, D_MODEL), lambda i: (0, 0)),
        ],
        out_specs=pl.BlockSpec((tb, D_MODEL), lambda i: (i, 0)),
        out_shape=jax.ShapeDtypeStruct((n, D_MODEL), F32),
        scratch_shapes=[pltpu.VMEM((2, 2, tb // SUBLANES, SUBLANES, D_MODEL), F32), pltpu.SemaphoreType.DMA((2,))],
        compiler_params=_cparams(("arbitrary",)),
        name="combine",
    )(pos2, pos2, ys, rslab, x1, g2, final_g)


def _layer(x, c, w_ada, b_ada, norm1_g, w_in, conv_w, a_log, dt_bias, dn_onorm_g, fox_f_bias,
           w_out, norm2_g, w_rg, b_rg, w_re, b_re, w1, w3, w2, final_g, final):
    bsz, seq, d = x.shape
    n = bsz * seq
    x2 = x.reshape(n, d)

    mod = _adaln(c, w_ada, b_ada)
    sh1, sc1, g1, sh2, sc2, g2 = [m.reshape(bsz, 1, d) for m in jnp.split(mod, 6, axis=-1)]

    w_t = w_in.T

    def lane_row(vals, off):
        return jnp.zeros((1, LANES), F32).at[0, off:off + HEADS].set(vals)

    h, slab = _prenorm_gates(x, sc1, sh1, norm1_g.reshape(1, d), w_t, lane_row(a_log, 0), lane_row(dt_bias, 0),
                             lane_row(fox_f_bias, L_F))
    proj3 = _inproj(h.reshape(n, d), w_t).reshape(bsz, seq, MAIN_W)
    nc = seq // CHUNK
    gct = slab[:, :, L_GC:L_GC + HEADS].reshape(bsz, nc, CHUNK, HEADS).transpose(0, 1, 3, 2)
    gct = gct.reshape(bsz, nc, HEADS, 1, CHUNK)
    f_rows = slab[:, :, L_F:L_F + HEADS].transpose(0, 2, 1).reshape(bsz, HEADS, seq // FOX_T, 1, FOX_T)

    o_dn = _deltanet(proj3, conv_w, slab, gct, dn_onorm_g.reshape(1, HEAD_DIM))
    o_fx = _fox(proj3, f_rows)

    wr = jnp.zeros((d, LANES), F32).at[:, :N_GROUPS].set(w_rg).at[:, N_GROUPS:N_GROUPS + N_EXPERTS].set(w_re)
    br = jnp.zeros((1, LANES), F32).at[0, :N_GROUPS].set(b_rg).at[0, N_GROUPS:N_GROUPS + N_EXPERTS].set(b_re)
    wr_hi_lo = jnp.concatenate(_split_bf16(wr), axis=1)
    w_out_b = w_out.astype(BF16)
    x1, h2, rslab, counts = _outproj(o_dn.reshape(n, HEAD_W), o_fx.reshape(n, HEAD_W), w_out_b[:HEAD_W],
                                     w_out_b[HEAD_W:], x2, g1, sc2, sh2, norm2_g.reshape(1, d), wr_hi_lo, br, seq)

    pos = _route(rslab, counts)[:, 0:2]

    cnt = counts[0, :N_EXPERTS].astype(jnp.int32)
    tiles_per = (cnt + MOE_TM - 1) // MOE_TM
    tile_end = jnp.cumsum(tiles_per)
    base = (tile_end - tiles_per) * MOE_TM
    n_tiles = (2 * n) // MOE_TM + N_EXPERTS
    p_rows = n_tiles * MOE_TM
    tid = jnp.arange(n_tiles, dtype=jnp.int32)
    tile_valid = (tid < tile_end[-1]).astype(jnp.int32)
    te_raw = jnp.minimum(jnp.sum(tid[:, None] >= tile_end[None, :], axis=1), N_EXPERTS - 1).astype(jnp.int32)
    last_e = te_raw[jnp.maximum(tile_end[-1] - 1, 0)]
    tile_expert = jnp.where(tile_valid == 1, te_raw, last_e)
    tile_first = (jnp.concatenate([jnp.array([-1], jnp.int32), tile_expert[:-1]]) != tile_expert).astype(jnp.int32)
    pad_start = base + cnt
    pad_len = tiles_per * MOE_TM - cnt
    eid = jnp.arange(N_EXPERTS, dtype=jnp.int32)
    has = tiles_per > 0
    slot_e = (jnp.cumsum(has.astype(jnp.int32)) - 1) & 1
    later = jnp.where(has[None, :] & (eid[None, :] > eid[:, None]), eid[None, :], N_EXPERTS)
    next_e = jnp.min(later, axis=1)
    next_e = jnp.where(next_e < N_EXPERTS, next_e, -1).astype(jnp.int32)
    of_tile = tile_expert[:, None] == eid[None, :]
    tile_slot = jnp.sum(jnp.where(of_tile, slot_e[None, :], 0), axis=1).astype(jnp.int32)
    tile_next = jnp.sum(jnp.where(of_tile, next_e[None, :], 0), axis=1).astype(jnp.int32)

    xs = _dispatch(pad_start, pad_len, tile_end[-1:], pos.reshape(n // DISP_TB, 1, 2 * DISP_TB), h2, p_rows)
    ys = _experts(tile_expert, tile_valid, tile_first, tile_slot, tile_next, xs, w1.reshape(N_EXPERTS, d, D_EXPERT),
                  w3.reshape(N_EXPERTS, d, D_EXPERT), w2.reshape(N_EXPERTS, D_EXPERT, d))
    out = _combine(pos.reshape(n // COMB_TB, 1, 2 * COMB_TB), ys, rslab, x1, g2, final_g.reshape(1, d), seq, final)
    return out.reshape(bsz, seq, d)


def kernel(x, c, w_ada, b_ada, norm1_g, w_in, conv_w, a_log, dt_bias, dn_onorm_g, fox_f_bias, w_out, norm2_g,
           w_router_group, b_router_group, w_router_expert, b_router_expert, w1, w3, w2, final_g):
    depth = w_ada.shape[0]
    for l in range(depth):
        x = _layer(x, c, w_ada[l], b_ada[l], norm1_g[l], w_in[l], conv_w[l], a_log[l], dt_bias[l], dn_onorm_g[l],
                   fox_f_bias[l], w_out[l], norm2_g[l], w_router_group[l], b_router_group[l], w_router_expert[l],
                   b_router_expert[l], w1[l], w3[l], w2[l], final_g, l == depth - 1)
    return x
```

```python
import functools

import jax
import jax.numpy as jnp
from jax import lax
from jax.experimental import pallas as pl
from jax.experimental.pallas import tpu as pltpu

F32 = jnp.float32
BF16 = jnp.bfloat16

D_MODEL = 2048
EPS = 1e-6
CHUNK = 64
HEADS = 8
HEAD_DIM = 128
HEAD_W = HEADS * HEAD_DIM
CONV_K = 4
N_GROUPS = 4
EXPERTS_PER_GROUP = 8
N_EXPERTS = N_GROUPS * EXPERTS_PER_GROUP
D_EXPERT = 512
LANES = 128
SUBLANES = 8
MAIN_W = 7 * HEAD_W
VMEM_LIMIT = 56 * 1024 * 1024

L_GC, L_BETA, L_F, L_EGC, L_EK, L_ELAST = 0, 8, 16, 24, 32, 40


def _cparams(sem):
    return pltpu.CompilerParams(dimension_semantics=sem, vmem_limit_bytes=VMEM_LIMIT)


def _split_bf16(a):
    hi = a.astype(BF16)
    lo = (a - hi.astype(F32)).astype(BF16)
    return hi, lo


def _dot(a, b):
    return jnp.dot(a, b, preferred_element_type=F32)


def _dot_nt(a, b):
    return lax.dot_general(a, b, (((1,), (1,)), ((), ())), preferred_element_type=F32)


def _dot3(a, b):
    ah, al = _split_bf16(a)
    bh, bl = _split_bf16(b)
    return _dot(ah, bh) + (_dot(al, bh) + _dot(ah, bl))


def _dot3_pre(a, b_hi_lo):
    ah, al = _split_bf16(a)
    n = b_hi_lo.shape[1] // 2
    both = _dot(ah, b_hi_lo)
    return both[:, :n] + (both[:, n:] + _dot(al, b_hi_lo[:, :n]))


def _softplus(x):
    return jnp.maximum(x, 0.0) + jnp.log1p(jnp.exp(-jnp.abs(x)))


def _silu(x):
    return x * jax.nn.sigmoid(x)


def _adaln_kernel(c_ref, w_ref, b_ref, o_ref):
    c = c_ref[...]
    o_ref[...] = _dot(_silu(c).astype(BF16), w_ref[...].astype(BF16)) + b_ref[...]


def _adaln(c, w, b):
    bsz = c.shape[0]
    n = w.shape[1]
    tn = 1024
    cp = jnp.zeros((SUBLANES, D_MODEL), F32).at[:bsz].set(c)
    out = pl.pallas_call(
        _adaln_kernel,
        grid=(n // tn,),
        in_specs=[
            pl.BlockSpec((SUBLANES, D_MODEL), lambda j: (0, 0)),
            pl.BlockSpec((D_MODEL, tn), lambda j: (0, j)),
            pl.BlockSpec((1, tn), lambda j: (0, j)),
        ],
        out_specs=pl.BlockSpec((SUBLANES, tn), lambda j: (0, j)),
        out_shape=jax.ShapeDtypeStruct((SUBLANES, n), F32),
        compiler_params=_cparams(("parallel",)),
        name="adaln",
    )(cp, w, b.reshape(1, n))
    return out[:bsz]


INPROJ_TM = 2048
INPROJ_TN = 1024
AB_ROW0 = 4 * HEAD_W
F_ROW0 = MAIN_W + 2 * HEADS


def _modulated_norm(x, g, sc, sh):
    r = lax.rsqrt(jnp.mean(x * x, axis=-1, keepdims=True) + EPS)
    return (x * r) * (g * (1.0 + sc)) + sh


def _inproj_kernel(h_ref, w_ref, o_ref):
    o_ref[...] = _dot_nt(h_ref[...], w_ref[...].astype(BF16)).astype(BF16)


def _inproj(h, w_t):
    n = h.shape[0]
    tm, tn = INPROJ_TM, INPROJ_TN
    first_part = AB_ROW0 // tn

    def w_rows(i, j):
        return (SUBLANES * (j * (tn // SUBLANES) + jnp.where(j >= first_part, 2 * HEADS // SUBLANES, 0)), 0)

    return pl.pallas_call(
        _inproj_kernel,
        grid=(n // tm, MAIN_W // tn),
        in_specs=[
            pl.BlockSpec((tm, D_MODEL), lambda i, j: (i, 0)),
            pl.BlockSpec((pl.Element(tn), pl.Element(D_MODEL)), w_rows),
        ],
        out_specs=pl.BlockSpec((tm, tn), lambda i, j: (i, j)),
        out_shape=jax.ShapeDtypeStruct((n, MAIN_W), BF16),
        compiler_params=_cparams(("parallel", "arbitrary")),
        name="inproj",
    )(h, w_t)


GATES_TB = 256


def _split3(a):
    p0 = a.astype(BF16)
    r1 = a - p0.astype(F32)
    p1 = r1.astype(BF16)
    p2 = (r1 - p1.astype(F32)).astype(BF16)
    return p0, p1, p2


def _prenorm_gates_kernel(x_ref, sc_ref, sh_ref, g_ref, wab_ref, wf_ref, alog_ref, dt_ref, fb_ref,
                          h_ref, o_ref, ws_ref, carry_ref):
    tb = GATES_TB

    @pl.when(pl.program_id(1) == 0)
    def _():
        carry_ref[...] = jnp.zeros_like(carry_ref)
        gate_rows = jnp.concatenate([wab_ref[...], wf_ref[...]], axis=0)
        pad = jnp.zeros((LANES - gate_rows.shape[0], D_MODEL), F32)
        hi, lo = _split_bf16(jnp.concatenate([gate_rows, pad], axis=0))
        ws_ref[0:LANES, :] = hi
        ws_ref[LANES:, :] = lo

    hb = _modulated_norm(x_ref[...], g_ref[...], sc_ref[...], sh_ref[...]).astype(BF16)
    h_ref[...] = hb
    both = _dot_nt(hb, ws_ref[...])
    x = both[:, :LANES] + both[:, LANES:]
    lane = lax.broadcasted_iota(jnp.int32, (tb, LANES), 1)
    g = -jnp.exp(alog_ref[...]) * _softplus(x + dt_ref[...])
    beta = jax.nn.sigmoid(x)
    lf = -_softplus(-(x + fb_ref[...]))

    ri = lax.broadcasted_iota(jnp.int32, (tb, tb), 0)
    ci = lax.broadcasted_iota(jnp.int32, (tb, tb), 1)
    same_chunk = (ri // CHUNK) == (ci // CHUNK)
    tri = (ri >= ci)
    m_all = jnp.where(tri, 1.0, 0.0).astype(BF16)
    m_chunk = jnp.where(tri & same_chunk, 1.0, 0.0).astype(BF16)
    m_tot = jnp.where(same_chunk, 1.0, 0.0).astype(BF16)

    in_a = lane < HEADS
    parts = jnp.concatenate(_split3(jnp.where(in_a, g, lf)), axis=1)
    sums = _dot(jnp.concatenate([m_chunk, m_tot, m_all], axis=0), parts)
    sums = sums[:, :LANES] + (sums[:, LANES:2 * LANES] + sums[:, 2 * LANES:])
    gc = sums[:tb]
    glast = sums[tb:2 * tb]
    fcum = sums[2 * tb:] + carry_ref[...]
    carry_ref[...] = fcum[tb - 1:tb, :]

    egc = jnp.where(in_a, jnp.exp(gc), 0.0)
    ek = jnp.where(in_a, jnp.exp(glast - gc), 0.0)
    elast = jnp.where(in_a, jnp.exp(glast), 0.0)
    out = jnp.where(in_a, gc, jnp.where(lane < 2 * HEADS, beta, jnp.where(lane < 3 * HEADS, fcum, 0.0)))
    out = out + pltpu.roll(egc, L_EGC, 1) + pltpu.roll(ek, L_EK, 1) + pltpu.roll(elast, L_ELAST, 1)
    o_ref[...] = out


def _prenorm_gates(x, sc1, sh1, g, w_t, alog_row, dt_row, fb_row):
    bsz, seq, _ = x.shape
    tb = GATES_TB
    row = pl.BlockSpec((1, LANES), lambda b, t: (0, 0))
    modrow = pl.BlockSpec((None, 1, D_MODEL), lambda b, t: (b, 0, 0))
    return pl.pallas_call(
        _prenorm_gates_kernel,
        grid=(bsz, seq // tb),
        in_specs=[
            pl.BlockSpec((None, tb, D_MODEL), lambda b, t: (b, t, 0)),
            modrow, modrow,
            pl.BlockSpec((1, D_MODEL), lambda b, t: (0, 0)),
            pl.BlockSpec((pl.Element(2 * HEADS), pl.Element(D_MODEL)), lambda b, t: (AB_ROW0, 0)),
            pl.BlockSpec((pl.Element(HEADS), pl.Element(D_MODEL)), lambda b, t: (F_ROW0, 0)),
            row, row, row,
        ],
        out_specs=[
            pl.BlockSpec((None, tb, D_MODEL), lambda b, t: (b, t, 0)),
            pl.BlockSpec((None, tb, LANES), lambda b, t: (b, t, 0)),
        ],
        out_shape=[
            jax.ShapeDtypeStruct((bsz, seq, D_MODEL), BF16),
            jax.ShapeDtypeStruct((bsz, seq, LANES), F32),
        ],
        scratch_shapes=[pltpu.VMEM((2 * LANES, D_MODEL), BF16), pltpu.VMEM((1, LANES), F32)],
        compiler_params=_cparams(("parallel", "arbitrary")),
        name="prenorm_gates",
    )(x, sc1, sh1, g, w_t, w_t, alog_row, dt_row, fb_row)


DN_TB = 256
DN_GROUP = 4
HALO = SUBLANES


def _bdot(a, b):
    return lax.dot_general(a, b, (((2,), (1,)), ((0,), (0,))), preferred_element_type=F32)


def _bdot_nt(a, b):
    return lax.dot_general(a, b, (((2,), (2,)), ((0,), (0,))), preferred_element_type=F32)


def _inv_unit_lower(a, eye, blk16, blk32):
    n = jnp.where(blk16, -a, 0.0)
    e1 = jnp.where(blk32 & jnp.logical_not(blk16), a, 0.0).astype(BF16)
    e2 = jnp.where(blk32, 0.0, a).astype(BF16)
    t = eye + n
    p = n.astype(BF16)
    for _ in range(3):
        p = _bdot(p, p).astype(BF16)
        t = t + _bdot(t.astype(BF16), p)
    for e in (e1, e2):
        tb = t.astype(BF16)
        t = t - _bdot(_bdot(tb, e).astype(BF16), tb)
    return t


def _deltanet_kernel(q_ref, k_ref, v_ref, z_ref, wq_ref, wk_ref, wv_ref, slab_ref, gct_ref, og_ref,
                     o_ref, ext_ref, qn_ref, kn_ref, vv_ref, s_ref):
    tb = DN_TB

    @pl.when(pl.program_id(1) == 0)
    def _():
        ext_ref[:, 0:HALO, :] = jnp.zeros((3, HALO, HEAD_W), F32)
        s_ref[...] = jnp.zeros_like(s_ref)

    for idx, (u_ref, w_ref) in enumerate(((q_ref, wq_ref), (k_ref, wk_ref), (v_ref, wv_ref))):
        for h in range(HEADS):
            cols = slice(h * HEAD_DIM, (h + 1) * HEAD_DIM)
            ext_ref[idx, HALO:HALO + tb, cols] = u_ref[:, cols].astype(F32)
            y = None
            for j in range(CONV_K):
                start = HALO - (CONV_K - 1) + j
                term = ext_ref[idx, start:start + tb, cols] * w_ref[j:j + 1, cols]
                y = term if y is None else y + term
            y = _silu(y)
            if idx == 2:
                vv_ref[h] = y
            else:
                yn = y * lax.rsqrt(jnp.sum(y * y, axis=-1, keepdims=True) + EPS)
                if idx == 0:
                    qn_ref[h] = (yn * (HEAD_DIM ** -0.5)).astype(BF16)
                else:
                    kn_ref[h] = yn.astype(BF16)
        ext_ref[idx, 0:HALO, :] = ext_ref[idx, tb:tb + HALO, :]

    ri = lax.broadcasted_iota(jnp.int32, (DN_GROUP * HEADS, CHUNK, CHUNK), 1)
    ci = lax.broadcasted_iota(jnp.int32, (DN_GROUP * HEADS, CHUNK, CHUNK), 2)
    incl = ri >= ci
    strict = ri > ci
    eye = jnp.where(ri == ci, 1.0, 0.0)
    blk16 = (ri // 16) == (ci // 16)
    blk32 = (ri // 32) == (ci // 32)

    def group_body(c, _):
        rows = [pl.ds(pl.multiple_of((c * DN_GROUP + g) * CHUNK, CHUNK), CHUNK) for g in range(DN_GROUP)]
        slabs = [slab_ref[r, :] for r in rows]

        def col(off, width):
            return jnp.stack([jnp.broadcast_to(sl[:, off + h:off + h + 1], (CHUNK, width))
                              for sl in slabs for h in range(HEADS)])

        def grouped(ref):
            return jnp.concatenate([ref[:, r, :] for r in rows], axis=0)

        q = grouped(qn_ref)
        k = grouped(kn_ref)
        v = grouped(vv_ref)
        beta = col(L_BETA, HEAD_DIM)
        egc = col(L_EGC, HEAD_DIM)
        gc_row = jnp.concatenate([gct_ref[c * DN_GROUP + g] for g in range(DN_GROUP)], axis=0)

        decay = jnp.where(incl, jnp.exp(col(L_GC, CHUNK) - gc_row), 0.0)
        kk = _bdot_nt(k, k)
        qk = (_bdot_nt(q, k) * decay).astype(BF16)
        a = jnp.where(strict, kk * decay * beta[:, :, :CHUNK], 0.0)
        t = _inv_unit_lower(a, eye, blk16, blk32).astype(BF16)

        kf = k.astype(F32)
        vb = (v * beta).astype(BF16)
        kbg = (kf * (beta * egc)).astype(BF16)
        u = _bdot(t, vb)
        w = _bdot(t, kbg).astype(BF16)
        qd = (q.astype(F32) * egc).astype(BF16)
        kd = kf * col(L_EK, HEAD_DIM)
        kdt = jnp.stack([kd[n].T for n in range(DN_GROUP * HEADS)]).astype(BF16)

        for g in range(DN_GROUP):
            sel = slice(g * HEADS, (g + 1) * HEADS)
            s = s_ref[...]
            sb = s.astype(BF16)
            vnb = (u[sel] - _bdot(w[sel], sb)).astype(BF16)
            o = _bdot(qd[sel], sb) + _bdot(qk[sel], vnb)
            elast = jnp.stack([jnp.broadcast_to(slabs[g][CHUNK - 1:CHUNK, L_ELAST + h:L_ELAST + h + 1],
                                                (HEAD_DIM, HEAD_DIM)) for h in range(HEADS)])
            s_ref[...] = s * elast + _bdot(kdt[sel], vnb)

            r = lax.rsqrt(jnp.mean(o * o, axis=-1, keepdims=True) + EPS)
            on = (o * r) * og_ref[...]
            for h in range(HEADS):
                cols = slice(h * HEAD_DIM, (h + 1) * HEAD_DIM)
                o_ref[rows[g], cols] = (on[h] * _silu(z_ref[rows[g], cols].astype(F32))).astype(BF16)
        return 0

    lax.fori_loop(0, tb // (CHUNK * DN_GROUP), group_body, 0)


def _deltanet(proj3, conv_w, slab, gct, onorm_g):
    bsz, seq, _ = proj3.shape
    tb = DN_TB
    nct = tb // CHUNK

    def colblk(j):
        return pl.BlockSpec((None, tb, HEAD_W), lambda b, t: (b, t, j))

    def wblk(j):
        return pl.BlockSpec((CONV_K, HEAD_W), lambda b, t: (0, j))

    return pl.pallas_call(
        _deltanet_kernel,
        grid=(bsz, seq // tb),
        in_specs=[
            colblk(0), colblk(1), colblk(2), colblk(3),
            wblk(0), wblk(1), wblk(2),
            pl.BlockSpec((None, tb, LANES), lambda b, t: (b, t, 0)),
            pl.BlockSpec((None, nct, HEADS, 1, CHUNK), lambda b, t: (b, t, 0, 0, 0)),
            pl.BlockSpec((1, HEAD_DIM), lambda b, t: (0, 0)),
        ],
        out_specs=pl.BlockSpec((None, tb, HEAD_W), lambda b, t: (b, t, 0)),
        out_shape=jax.ShapeDtypeStruct((bsz, seq, HEAD_W), BF16),
        scratch_shapes=[
            pltpu.VMEM((3, tb + HALO, HEAD_W), F32),
            pltpu.VMEM((HEADS, tb, HEAD_DIM), BF16),
            pltpu.VMEM((HEADS, tb, HEAD_DIM), BF16),
            pltpu.VMEM((HEADS, tb, HEAD_DIM), F32),
            pltpu.VMEM((HEADS, HEAD_DIM, HEAD_DIM), F32),
        ],
        compiler_params=_cparams(("parallel", "arbitrary")),
        name="deltanet",
    )(proj3, proj3, proj3, proj3, conv_w, conv_w, conv_w, slab, gct, onorm_g)


FOX_T = 512
FOX_STRIP = 32
NEG_INF = float("-inf")
LOG2E = 1.4426950408889634


def _fox_kernel(q_ref, k_ref, v_ref, f_ref, o_ref, s_a_ref, s_b_ref, p_a_ref, p_b_ref, m_ref, l_ref, alpha_ref,
                acc_ref):
    tq = FOX_T
    i = pl.program_id(2)
    qs = (q_ref[...].astype(F32) * (HEAD_DIM ** -0.5 * LOG2E)).astype(BF16)
    f0 = f_ref[i][:, 0:1]
    m_ref[...] = jnp.full_like(m_ref, NEG_INF)
    l_ref[...] = jnp.zeros_like(l_ref)
    acc_ref[...] = jnp.zeros_like(acc_ref)

    def keys(j):
        return pl.ds(pl.multiple_of(j * tq, tq), tq)

    def scores(j):
        return _dot_nt(qs, k_ref[keys(j), :])

    def softmax(j, s_view, p_view, masked):
        bias = (f0 - f_ref[j]) * LOG2E

        def width(r):
            if not masked:
                return tq
            return min(tq, ((r + 1) * FOX_STRIP + LANES - 1) // LANES * LANES)

        def logits(r):
            rows = slice(r * FOX_STRIP, (r + 1) * FOX_STRIP)
            w = width(r)
            s = s_view[rows, :w] + bias[:, :w]
            if masked:
                ri = r * FOX_STRIP + lax.broadcasted_iota(jnp.int32, (FOX_STRIP, w), 0)
                ci = lax.broadcasted_iota(jnp.int32, (FOX_STRIP, w), 1)
                s = jnp.where(ci <= ri, s, NEG_INF)
            return rows, s

        for r in range(tq // FOX_STRIP):
            rows, s = logits(r)
            m_old = m_ref[rows, :]
            m_new = jnp.maximum(m_old, jnp.max(s, axis=-1, keepdims=True))
            alpha_ref[rows, :] = jnp.exp2(m_old - m_new)
            m_ref[rows, :] = m_new
        for r in range(tq // FOX_STRIP):
            rows, s = logits(r)
            w = width(r)
            p = jnp.exp2(s - jnp.concatenate([m_ref[rows, :]] * (w // LANES), axis=1))
            alpha = alpha_ref[rows, :]
            l_ref[rows, :] = alpha * l_ref[rows, :] + jnp.sum(p, axis=-1, keepdims=True)
            p_view[rows, :w] = p.astype(BF16)
            if w < tq:
                p_view[rows, w:] = jnp.zeros((FOX_STRIP, tq - w), BF16)

    s_a_ref[...] = scores(i)
    s_b_ref[...] = scores(jnp.maximum(i - 1, 0))
    softmax(i, s_a_ref, p_a_ref, True)

    def step(t, s_cur, p_cur, s_nxt, p_prv):
        j = i - t
        s_nxt[...] = scores(jnp.maximum(j - 1, 0))
        pv = _dot(p_prv[...], v_ref[keys(j + 1), :])
        softmax(j, s_cur, p_cur, False)
        acc_ref[...] = alpha_ref[...] * (acc_ref[...] + pv)

    def pair(u):
        step(2 * u + 1, s_b_ref, p_b_ref, s_a_ref, p_a_ref)
        step(2 * u + 2, s_a_ref, p_a_ref, s_b_ref, p_b_ref)

    def quad(w, _):
        pair(2 * w)
        pair(2 * w + 1)
        return 0

    quads = lax.shift_right_logical(i, 2)
    lax.fori_loop(0, quads, quad, 0)
    pl.when((i & 2) != 0)(functools.partial(pair, 2 * quads))

    def finish(p_last):
        acc = acc_ref[...] + _dot(p_last[...], v_ref[keys(0), :])
        o_ref[...] = (acc / l_ref[...]).astype(BF16)

    @pl.when((i & 1) == 1)
    def _():
        step(i, s_b_ref, p_b_ref, s_a_ref, p_a_ref)
        finish(p_b_ref)

    @pl.when((i & 1) == 0)
    def _():
        finish(p_a_ref)


def _fox(proj3, f_rows):
    bsz, seq, _ = proj3.shape
    tq = FOX_T
    qb, kb, vb = 4 * HEADS, 5 * HEADS, 6 * HEADS
    return pl.pallas_call(
        _fox_kernel,
        grid=(bsz, HEADS, seq // tq),
        in_specs=[
            pl.BlockSpec((None, tq, HEAD_DIM), lambda b, h, i: (b, i, qb + h)),
            pl.BlockSpec((None, seq, HEAD_DIM), lambda b, h, i: (b, 0, kb + h)),
            pl.BlockSpec((None, seq, HEAD_DIM), lambda b, h, i: (b, 0, vb + h)),
            pl.BlockSpec((None, None, seq // tq, 1, tq), lambda b, h, i: (b, h, 0, 0, 0)),
        ],
        out_specs=pl.BlockSpec((None, tq, HEAD_DIM), lambda b, h, i: (b, i, h)),
        out_shape=jax.ShapeDtypeStruct((bsz, seq, HEAD_W), BF16),
        scratch_shapes=[
            pltpu.VMEM((tq, tq), F32),
            pltpu.VMEM((tq, tq), F32),
            pltpu.VMEM((tq, tq), BF16),
            pltpu.VMEM((tq, tq), BF16),
            pltpu.VMEM((tq, LANES), F32),
            pltpu.VMEM((tq, LANES), F32),
            pltpu.VMEM((tq, LANES), F32),
            pltpu.VMEM((tq, HEAD_DIM), F32),
        ],
        compiler_params=_cparams(("parallel", "parallel", "arbitrary")),
        name="fox",
    )(proj3, proj3, proj3, f_rows)


OUT_TM = 512
R_E0, R_E1, R_W0, R_W1 = 0, 1, 2, 3


def _first_argmax(vals, lane):
    m = jnp.max(vals, axis=-1, keepdims=True)
    idx = jnp.min(jnp.where(vals == m, lane, LANES), axis=-1, keepdims=True)
    return m, idx


def _outproj_kernel(oa_ref, ob_ref, wa_ref, wb_ref, x_ref, g1_ref, sc_ref, sh_ref, g_ref,
                    wr_ref, br_ref, x1_ref, h2_ref, r_ref, cnt_ref):
    mix = _dot(oa_ref[...], wa_ref[...]) + _dot(ob_ref[...], wb_ref[...])
    x1 = x_ref[...] + g1_ref[...] * mix
    x1_ref[...] = x1
    h2 = _modulated_norm(x1, g_ref[...], sc_ref[...], sh_ref[...])
    h2_ref[...] = h2

    logits = _dot3_pre(h2, wr_ref[...]) + br_ref[...]
    tm = logits.shape[0]
    lane = lax.broadcasted_iota(jnp.int32, (tm, LANES), 1)
    gl = jnp.where(lane < N_GROUPS, logits, NEG_INF)
    gmax, gidx = _first_argmax(gl, lane)
    pg = 1.0 / jnp.sum(jnp.exp(gl - gmax), axis=-1, keepdims=True)
    e_lane = lane - N_GROUPS
    in_grp = (e_lane >= gidx * EXPERTS_PER_GROUP) & (e_lane < (gidx + 1) * EXPERTS_PER_GROUP)
    el = jnp.where(in_grp, logits, NEG_INF)
    v0, i0 = _first_argmax(el, lane)
    v1, i1 = _first_argmax(jnp.where(lane == i0, NEG_INF, el), lane)
    ex = jnp.exp(v1 - v0)
    w0 = pg / (1.0 + ex)
    w1 = pg * ex / (1.0 + ex)
    e0 = (i0 - N_GROUPS).astype(F32)
    e1 = (i1 - N_GROUPS).astype(F32)
    r_ref[...] = jnp.where(lane == R_E0, e0, jnp.where(lane == R_E1, e1,
                           jnp.where(lane == R_W0, w0, jnp.where(lane == R_W1, w1, 0.0))))

    @pl.when(pl.program_id(0) == 0)
    def _():
        cnt_ref[...] = jnp.zeros_like(cnt_ref)

    picked = (lane == i0 - N_GROUPS) | (lane == i1 - N_GROUPS)
    cnt_ref[...] += jnp.sum(jnp.where(picked, 1.0, 0.0), axis=0, keepdims=True)


def _outproj(o_a, o_b, wa, wb, x2, g1, sc2, sh2, g, wr_hi_lo, br, seq):
    n = x2.shape[0]
    tm = OUT_TM
    per_b = seq // tm
    modrow = pl.BlockSpec((None, 1, D_MODEL), lambda i: (i // per_b, 0, 0))
    const = lambda shape: pl.BlockSpec(shape, lambda i: (0, 0))
    return pl.pallas_call(
        _outproj_kernel,
        grid=(n // tm,),
        in_specs=[
            pl.BlockSpec((tm, HEAD_W), lambda i: (i, 0)),
            pl.BlockSpec((tm, HEAD_W), lambda i: (i, 0)),
            const((HEAD_W, D_MODEL)), const((HEAD_W, D_MODEL)),
            pl.BlockSpec((tm, D_MODEL), lambda i: (i, 0)),
            modrow, modrow, modrow,
            const((1, D_MODEL)),
            const((D_MODEL, 2 * LANES)), const((1, LANES)),
        ],
        out_specs=[
            pl.BlockSpec((tm, D_MODEL), lambda i: (i, 0)),
            pl.BlockSpec((tm, D_MODEL), lambda i: (i, 0)),
            pl.BlockSpec((tm, LANES), lambda i: (i, 0)),
            pl.BlockSpec((1, LANES), lambda i: (0, 0)),
        ],
        out_shape=[
            jax.ShapeDtypeStruct((n, D_MODEL), F32),
            jax.ShapeDtypeStruct((n, D_MODEL), F32),
            jax.ShapeDtypeStruct((n, LANES), F32),
            jax.ShapeDtypeStruct((1, LANES), F32),
        ],
        compiler_params=_cparams(("arbitrary",)),
        name="outproj",
    )(o_a, o_b, wa, wb, x2, g1, sc2, sh2, g, wr_hi_lo, br)


MOE_TM = 256
ROUTE_TB = 512


def _route_kernel(r_ref, cnt_ref, pos_ref, run_ref, base_ref):
    tb = ROUTE_TB
    t = pl.program_id(0)
    r = r_ref[...]
    lane = lax.broadcasted_iota(jnp.int32, (tb, LANES), 1)
    e0 = r[:, R_E0:R_E0 + 1].astype(jnp.int32)
    e1 = r[:, R_E1:R_E1 + 1].astype(jnp.int32)
    oh0 = lane == e0
    oh1 = lane == e1
    both = jnp.where(oh0 | oh1, 1.0, 0.0)

    @pl.when(t == 0)
    def _():
        counts = cnt_ref[...]
        padded = jnp.ceil(counts / MOE_TM) * MOE_TM
        li = lax.broadcasted_iota(jnp.int32, (LANES, LANES), 0)
        lj = lax.broadcasted_iota(jnp.int32, (LANES, LANES), 1)
        upper = jnp.where(li < lj, 1.0, 0.0).astype(BF16)
        hi = jnp.floor(padded / 256.0)
        lo = padded - hi * 256.0
        hi8 = jnp.broadcast_to(hi, (SUBLANES, LANES)).astype(BF16)
        lo8 = jnp.broadcast_to(lo, (SUBLANES, LANES)).astype(BF16)
        base = _dot(hi8, upper) * 256.0 + _dot(lo8, upper)
        base_ref[...] = base[0:1, :]
        run_ref[...] = jnp.zeros_like(run_ref)

    ri = lax.broadcasted_iota(jnp.int32, (tb, tb), 0)
    ci = lax.broadcasted_iota(jnp.int32, (tb, tb), 1)
    strict = jnp.where(ri > ci, 1.0, 0.0).astype(BF16)
    before = _dot(strict, both.astype(BF16)) + run_ref[...] + base_ref[...]
    p0 = jnp.sum(jnp.where(oh0, before, 0.0), axis=-1, keepdims=True)
    p1 = jnp.sum(jnp.where(oh1, before, 0.0), axis=-1, keepdims=True)
    pos_ref[...] = jnp.where(lane == 0, p0, jnp.where(lane == 1, p1, 0.0)).astype(jnp.int32)
    run_ref[...] = run_ref[...] + jnp.sum(both, axis=0, keepdims=True)


def _route(rslab, counts):
    n = rslab.shape[0]
    tb = ROUTE_TB
    return pl.pallas_call(
        _route_kernel,
        grid=(n // tb,),
        in_specs=[pl.BlockSpec((tb, LANES), lambda t: (t, 0)), pl.BlockSpec((1, LANES), lambda t: (0, 0))],
        out_specs=pl.BlockSpec((tb, LANES), lambda t: (t, 0)),
        out_shape=jax.ShapeDtypeStruct((n, LANES), jnp.int32),
        scratch_shapes=[pltpu.VMEM((1, LANES), F32), pltpu.VMEM((1, LANES), F32)],
        compiler_params=_cparams(("arbitrary",)),
        name="route",
    )(rslab, counts)


DISP_TB = 1024
DMA_UNROLL = 8


def _dispatch_kernel(pad_start_ref, pad_len_ref, used_ref, pos_ref, h2_ref, xs_ref, zero_ref, sem, zsem):
    t = pl.program_id(0)
    tb = DISP_TB
    n_tiles = xs_ref.shape[0] // MOE_TM

    def row_copy(g, r, k):
        dst_row = pos_ref[0, 2 * SUBLANES * g + 2 * r + k]
        return pltpu.make_async_copy(h2_ref.at[g, pl.ds(r, 1)], xs_ref.at[pl.ds(dst_row, 1)], sem)

    def issue(g, _):
        for r in range(SUBLANES):
            row_copy(g, r, 0).start(priority=0)
            row_copy(g, r, 1).start(priority=1)
        return 0

    lax.fori_loop(0, tb // SUBLANES, issue, 0)

    @pl.when(t == 0)
    def _():
        zero_ref[...] = jnp.zeros_like(zero_ref)

        def zero_rows(wait, off, rows):
            cp = pltpu.make_async_copy(zero_ref.at[pl.ds(0, rows)], xs_ref.at[pl.ds(off, rows)], zsem)
            cp.wait() if wait else cp.start()

        def per_expert(wait, e, _):
            start = pad_start_ref[e]
            head = (-start) & (SUBLANES - 1)
            for r in range(SUBLANES - 1):
                pl.when(r < head)(functools.partial(zero_rows, wait, start + r, 1))
            off = start + head
            rest = pad_len_ref[e] - head
            piece = MOE_TM // 2
            while piece >= SUBLANES:
                take = (rest & piece) != 0
                pl.when(take)(functools.partial(zero_rows, wait, pl.multiple_of(off, SUBLANES), piece))
                off = off + jnp.where(take, piece, 0)
                piece //= 2
            return 0

        def per_tile(wait, i, _):
            zero_rows(wait, pl.multiple_of(i * MOE_TM, MOE_TM), MOE_TM)
            return 0

        for wait in (False, True):
            lax.fori_loop(0, N_EXPERTS, functools.partial(per_expert, wait), 0)
            lax.fori_loop(used_ref[0], n_tiles, functools.partial(per_tile, wait), 0)

    def drain(g, _):
        for r in range(SUBLANES):
            row_copy(g, r, 0).wait()
            row_copy(g, r, 1).wait()
        return 0

    lax.fori_loop(0, tb // SUBLANES, drain, 0)


def _dispatch(pad_start, pad_len, used, pos2, h2, p_rows):
    n = h2.shape[0]
    tb = DISP_TB
    return pl.pallas_call(
        _dispatch_kernel,
        grid_spec=pltpu.PrefetchScalarGridSpec(
            num_scalar_prefetch=3,
            grid=(n // tb,),
            in_specs=[
                pl.BlockSpec((None, 1, 2 * tb), lambda t, *_: (t, 0, 0), memory_space=pltpu.SMEM),
                pl.BlockSpec((tb // SUBLANES, SUBLANES, D_MODEL), lambda t, *_: (t, 0, 0)),
            ],
            out_specs=pl.BlockSpec(memory_space=pl.ANY),
            scratch_shapes=[
                pltpu.VMEM((MOE_TM, D_MODEL), F32),
                pltpu.SemaphoreType.DMA(()),
                pltpu.SemaphoreType.DMA(()),
            ],
        ),
        out_shape=jax.ShapeDtypeStruct((p_rows, D_MODEL), F32),
        compiler_params=_cparams(("arbitrary",)),
        name="dispatch",
    )(pad_start, pad_len, used, pos2, h2.reshape(n // SUBLANES, SUBLANES, D_MODEL))


def _experts_kernel(te_ref, tv_ref, tf_ref, ts_ref, tn_ref, xs_ref, w1_hbm, w3_hbm, w2_hbm, ys_ref,
                    w1f_ref, w3f_ref, w2f_ref, w1b_ref, w3b_ref, w2b_ref, sem):
    i = pl.program_id(0)

    def weight_copies(e, slot):
        return [pltpu.make_async_copy(src.at[e], dst.at[slot], sem.at[slot])
                for src, dst in ((w1_hbm, w1f_ref), (w3_hbm, w3f_ref), (w2_hbm, w2f_ref))]

    @pl.when(tv_ref[i] != 0)
    def _():
        @pl.when(tf_ref[i] != 0)
        def _():
            slot = ts_ref[i]

            @pl.when(i == 0)
            def _():
                for cp in weight_copies(te_ref[i], slot):
                    cp.start()

            for cp in weight_copies(te_ref[i], slot):
                cp.wait()

            @pl.when(tn_ref[i] >= 0)
            def _():
                for cp in weight_copies(tn_ref[i], 1 - slot):
                    cp.start()

            w1b_ref[...] = w1f_ref[slot].astype(BF16)
            w3b_ref[...] = w3f_ref[slot].astype(BF16)
            w2b_ref[...] = w2f_ref[slot].astype(BF16)

        x = xs_ref[...].astype(BF16)
        a = _dot(x, w1b_ref[...])
        b = _dot(x, w3b_ref[...])
        ys_ref[...] = _dot((_silu(a) * b).astype(BF16), w2b_ref[...])

    @pl.when(tv_ref[i] == 0)
    def _():
        ys_ref[...] = jnp.zeros_like(ys_ref)


def _experts(tile_expert, tile_valid, tile_first, tile_slot, tile_next, xs, w1, w3, w2):
    p_rows = xs.shape[0]
    tm = MOE_TM
    hbm = pl.BlockSpec(memory_space=pl.ANY)
    return pl.pallas_call(
        _experts_kernel,
        grid_spec=pltpu.PrefetchScalarGridSpec(
            num_scalar_prefetch=5,
            grid=(p_rows // tm,),
            in_specs=[pl.BlockSpec((tm, D_MODEL), lambda i, te, tv, *_: (jnp.where(tv[i] != 0, i, 0), 0)),
                      hbm, hbm, hbm],
            out_specs=pl.BlockSpec((tm, D_MODEL), lambda i, *_: (i, 0)),
            scratch_shapes=[
                pltpu.VMEM((2, D_MODEL, D_EXPERT), F32),
                pltpu.VMEM((2, D_MODEL, D_EXPERT), F32),
                pltpu.VMEM((2, D_EXPERT, D_MODEL), F32),
                pltpu.VMEM((D_MODEL, D_EXPERT), BF16),
                pltpu.VMEM((D_MODEL, D_EXPERT), BF16),
                pltpu.VMEM((D_EXPERT, D_MODEL), BF16),
                pltpu.SemaphoreType.DMA((2,)),
            ],
        ),
        out_shape=jax.ShapeDtypeStruct((p_rows, D_MODEL), F32),
        compiler_params=_cparams(("arbitrary",)),
        name="experts",
    )(tile_expert, tile_valid, tile_first, tile_slot, tile_next, xs, w1, w3, w2)


COMB_TB = 512


def _combine_kernel(pos_ref, pos_next_ref, ys_ref, r_ref, x1_ref, g2_ref, fg_ref, o_ref, buf_ref, sem, *, final):
    tb = COMB_TB
    t = pl.program_id(0)
    slot = t & 1

    def row_copy(p_ref, s, g, r, k):
        src_row = p_ref[0, 2 * SUBLANES * g + 2 * r + k]
        return pltpu.make_async_copy(ys_ref.at[pl.ds(src_row, 1)], buf_ref.at[s, k, g, pl.ds(r, 1)], sem.at[s])

    def issue(p_ref, s):
        def body(g, _):
            for r in range(SUBLANES):
                row_copy(p_ref, s, g, r, 0).start(priority=0)
                row_copy(p_ref, s, g, r, 1).start(priority=1)
            return 0

        lax.fori_loop(0, tb // SUBLANES, body, 0)

    def drain(g, _):
        for r in range(SUBLANES):
            row_copy(pos_ref, slot, g, r, 0).wait()
            row_copy(pos_ref, slot, g, r, 1).wait()
        return 0

    pl.when(t == 0)(functools.partial(issue, pos_ref, 0))
    pl.when(t + 1 < pl.num_programs(0))(functools.partial(issue, pos_next_ref, 1 - slot))
    lax.fori_loop(0, tb // SUBLANES, drain, 0)
    r = r_ref[...]
    y0 = buf_ref[slot, 0].reshape(tb, D_MODEL)
    y1 = buf_ref[slot, 1].reshape(tb, D_MODEL)
    y = r[:, R_W0:R_W0 + 1] * y0 + r[:, R_W1:R_W1 + 1] * y1
    x2 = x1_ref[...] + g2_ref[...] * y
    if final:
        x2 = (x2 * lax.rsqrt(jnp.mean(x2 * x2, axis=-1, keepdims=True) + EPS)) * fg_ref[...]
    o_ref[...] = x2


def _combine(pos2, ys, rslab, x1, g2, final_g, seq, final):
    n = x1.shape[0]
    tb = COMB_TB
    per_b = seq // tb
    return pl.pallas_call(
        functools.partial(_combine_kernel, final=final),
        grid=(n // tb,),
        in_specs=[
            pl.BlockSpec((None, 1, 2 * tb), lambda i: (i, 0, 0), memory_space=pltpu.SMEM),
            pl.BlockSpec((None, 1, 2 * tb), lambda i: (jnp.minimum(i + 1, n // tb - 1), 0, 0),
                         memory_space=pltpu.SMEM),
            pl.BlockSpec(memory_space=pl.ANY),
            pl.BlockSpec((tb, LANES), lambda i: (i, 0)),
            pl.BlockSpec((tb, D_MODEL), lambda i: (i, 0)),
            pl.BlockSpec((None, 1, D_MODEL), lambda i: (i // per_b, 0, 0)),
            pl.BlockSpec((1, D_MODEL), lambda i: (0, 0)),
        ],
        out_specs=pl.BlockSpec((tb, D_MODEL), lambda i: (i, 0)),
        out_shape=jax.ShapeDtypeStruct((n, D_MODEL), F32),
        scratch_shapes=[pltpu.VMEM((2, 2, tb // SUBLANES, SUBLANES, D_MODEL), F32), pltpu.SemaphoreType.DMA((2,))],
        compiler_params=_cparams(("arbitrary",)),
        name="combine",
    )(pos2, pos2, ys, rslab, x1, g2, final_g)


def _layer(x, c, w_ada, b_ada, norm1_g, w_in, conv_w, a_log, dt_bias, dn_onorm_g, fox_f_bias,
           w_out, norm2_g, w_rg, b_rg, w_re, b_re, w1, w3, w2, final_g, final):
    bsz, seq, d = x.shape
    n = bsz * seq
    x2 = x.reshape(n, d)

    mod = _adaln(c, w_ada, b_ada)
    sh1, sc1, g1, sh2, sc2, g2 = [m.reshape(bsz, 1, d) for m in jnp.split(mod, 6, axis=-1)]

    w_t = w_in.T

    def lane_row(vals, off):
        return jnp.zeros((1, LANES), F32).at[0, off:off + HEADS].set(vals)

    h, slab = _prenorm_gates(x, sc1, sh1, norm1_g.reshape(1, d), w_t, lane_row(a_log, 0), lane_row(dt_bias, 0),
                             lane_row(fox_f_bias, L_F))
    proj3 = _inproj(h.reshape(n, d), w_t).reshape(bsz, seq, MAIN_W)
    nc = seq // CHUNK
    gct = slab[:, :, L_GC:L_GC + HEADS].reshape(bsz, nc, CHUNK, HEADS).transpose(0, 1, 3, 2)
    gct = gct.reshape(bsz, nc, HEADS, 1, CHUNK)
    f_rows = slab[:, :, L_F:L_F + HEADS].transpose(0, 2, 1).reshape(bsz, HEADS, seq // FOX_T, 1, FOX_T)

    o_dn = _deltanet(proj3, conv_w, slab, gct, dn_onorm_g.reshape(1, HEAD_DIM))
    o_fx = _fox(proj3, f_rows)

    wr = jnp.zeros((d, LANES), F32).at[:, :N_GROUPS].set(w_rg).at[:, N_GROUPS:N_GROUPS + N_EXPERTS].set(w_re)
    br = jnp.zeros((1, LANES), F32).at[0, :N_GROUPS].set(b_rg).at[0, N_GROUPS:N_GROUPS + N_EXPERTS].set(b_re)
    wr_hi_lo = jnp.concatenate(_split_bf16(wr), axis=1)
    w_out_b = w_out.astype(BF16)
    x1, h2, rslab, counts = _outproj(o_dn.reshape(n, HEAD_W), o_fx.reshape(n, HEAD_W), w_out_b[:HEAD_W],
                                     w_out_b[HEAD_W:], x2, g1, sc2, sh2, norm2_g.reshape(1, d), wr_hi_lo, br, seq)

    pos = _route(rslab, counts)[:, 0:2]

    cnt = counts[0, :N_EXPERTS].astype(jnp.int32)
    tiles_per = (cnt + MOE_TM - 1) // MOE_TM
    tile_end = jnp.cumsum(tiles_per)
    base = (tile_end - tiles_per) * MOE_TM
    n_tiles = (2 * n) // MOE_TM + N_EXPERTS
    p_rows = n_tiles * MOE_TM
    tid = jnp.arange(n_tiles, dtype=jnp.int32)
    tile_valid = (tid < tile_end[-1]).astype(jnp.int32)
    te_raw = jnp.minimum(jnp.sum(tid[:, None] >= tile_end[None, :], axis=1), N_EXPERTS - 1).astype(jnp.int32)
    last_e = te_raw[jnp.maximum(tile_end[-1] - 1, 0)]
    tile_expert = jnp.where(tile_valid == 1, te_raw, last_e)
    tile_first = (jnp.concatenate([jnp.array([-1], jnp.int32), tile_expert[:-1]]) != tile_expert).astype(jnp.int32)
    pad_start = base + cnt
    pad_len = tiles_per * MOE_TM - cnt
    eid = jnp.arange(N_EXPERTS, dtype=jnp.int32)
    has = tiles_per > 0
    slot_e = (jnp.cumsum(has.astype(jnp.int32)) - 1) & 1
    later = jnp.where(has[None, :] & (eid[None, :] > eid[:, None]), eid[None, :], N_EXPERTS)
    next_e = jnp.min(later, axis=1)
    next_e = jnp.where(next_e < N_EXPERTS, next_e, -1).astype(jnp.int32)
    of_tile = tile_expert[:, None] == eid[None, :]
    tile_slot = jnp.sum(jnp.where(of_tile, slot_e[None, :], 0), axis=1).astype(jnp.int32)
    tile_next = jnp.sum(jnp.where(of_tile, next_e[None, :], 0), axis=1).astype(jnp.int32)

    xs = _dispatch(pad_start, pad_len, tile_end[-1:], pos.reshape(n // DISP_TB, 1, 2 * DISP_TB), h2, p_rows)
    ys = _experts(tile_expert, tile_valid, tile_first, tile_slot, tile_next, xs, w1.reshape(N_EXPERTS, d, D_EXPERT),
                  w3.reshape(N_EXPERTS, d, D_EXPERT), w2.reshape(N_EXPERTS, D_EXPERT, d))
    out = _combine(pos.reshape(n // COMB_TB, 1, 2 * COMB_TB), ys, rslab, x1, g2, final_g.reshape(1, d), seq, final)
    return out.reshape(bsz, seq, d)


def kernel(x, c, w_ada, b_ada, norm1_g, w_in, conv_w, a_log, dt_bias, dn_onorm_g, fox_f_bias, w_out, norm2_g,
           w_router_group, b_router_group, w_router_expert, b_router_expert, w1, w3, w2, final_g):
    depth = w_ada.shape[0]
    for l in range(depth):
        x = _layer(x, c, w_ada[l], b_ada[l], norm1_g[l], w_in[l], conv_w[l], a_log[l], dt_bias[l], dn_onorm_g[l],
                   fox_f_bias[l], w_out[l], norm2_g[l], w_router_group[l], b_router_group[l], w_router_expert[l],
                   b_router_expert[l], w1[l], w3[l], w2[l], final_g, l == depth - 1)
    return x
```

```python
import functools

import jax
import jax.numpy as jnp
from jax import lax
from jax.experimental import pallas as pl
from jax.experimental.pallas import tpu as pltpu

F32 = jnp.float32
BF16 = jnp.bfloat16

D_MODEL = 2048
EPS = 1e-6
CHUNK = 64
HEADS = 8
HEAD_DIM = 128
HEAD_W = HEADS * HEAD_DIM
CONV_K = 4
N_GROUPS = 4
EXPERTS_PER_GROUP = 8
N_EXPERTS = N_GROUPS * EXPERTS_PER_GROUP
D_EXPERT = 512
LANES = 128
SUBLANES = 8
MAIN_W = 7 * HEAD_W
VMEM_LIMIT = 56 * 1024 * 1024

L_GC, L_BETA, L_F, L_EGC, L_EK, L_ELAST = 0, 8, 16, 24, 32, 40


def _cparams(sem):
    return pltpu.CompilerParams(dimension_semantics=sem, vmem_limit_bytes=VMEM_LIMIT)


def _split_bf16(a):
    hi = a.astype(BF16)
    lo = (a - hi.astype(F32)).astype(BF16)
    return hi, lo


def _dot(a, b):
    return jnp.dot(a, b, preferred_element_type=F32)


def _dot_nt(a, b):
    return lax.dot_general(a, b, (((1,), (1,)), ((), ())), preferred_element_type=F32)


def _dot3(a, b):
    ah, al = _split_bf16(a)
    bh, bl = _split_bf16(b)
    return _dot(ah, bh) + (_dot(al, bh) + _dot(ah, bl))


def _dot3_pre(a, b_hi_lo):
    ah, al = _split_bf16(a)
    n = b_hi_lo.shape[1] // 2
    both = _dot(ah, b_hi_lo)
    return both[:, :n] + (both[:, n:] + _dot(al, b_hi_lo[:, :n]))


def _softplus(x):
    return jnp.maximum(x, 0.0) + jnp.log1p(jnp.exp(-jnp.abs(x)))


def _silu(x):
    return x * jax.nn.sigmoid(x)


def _adaln_kernel(c_ref, w_ref, b_ref, o_ref):
    c = c_ref[...]
    o_ref[...] = _dot(_silu(c).astype(BF16), w_ref[...].astype(BF16)) + b_ref[...]


def _adaln(c, w, b):
    bsz = c.shape[0]
    n = w.shape[1]
    tn = 1024
    cp = jnp.zeros((SUBLANES, D_MODEL), F32).at[:bsz].set(c)
    out = pl.pallas_call(
        _adaln_kernel,
        grid=(n // tn,),
        in_specs=[
            pl.BlockSpec((SUBLANES, D_MODEL), lambda j: (0, 0)),
            pl.BlockSpec((D_MODEL, tn), lambda j: (0, j)),
            pl.BlockSpec((1, tn), lambda j: (0, j)),
        ],
        out_specs=pl.BlockSpec((SUBLANES, tn), lambda j: (0, j)),
        out_shape=jax.ShapeDtypeStruct((SUBLANES, n), F32),
        compiler_params=_cparams(("parallel",)),
        name="adaln",
    )(cp, w, b.reshape(1, n))
    return out[:bsz]


INPROJ_TM = 2048
INPROJ_TN = 1024
AB_ROW0 = 4 * HEAD_W
F_ROW0 = MAIN_W + 2 * HEADS


def _modulated_norm(x, g, sc, sh):
    r = lax.rsqrt(jnp.mean(x * x, axis=-1, keepdims=True) + EPS)
    return (x * r) * (g * (1.0 + sc)) + sh


def _inproj_kernel(h_ref, w_ref, o_ref):
    o_ref[...] = _dot_nt(h_ref[...], w_ref[...].astype(BF16)).astype(BF16)


def _inproj(h, w_t):
    n = h.shape[0]
    tm, tn = INPROJ_TM, INPROJ_TN
    first_part = AB_ROW0 // tn

    def w_rows(i, j):
        return (SUBLANES * (j * (tn // SUBLANES) + jnp.where(j >= first_part, 2 * HEADS // SUBLANES, 0)), 0)

    return pl.pallas_call(
        _inproj_kernel,
        grid=(n // tm, MAIN_W // tn),
        in_specs=[
            pl.BlockSpec((tm, D_MODEL), lambda i, j: (i, 0)),
            pl.BlockSpec((pl.Element(tn), pl.Element(D_MODEL)), w_rows),
        ],
        out_specs=pl.BlockSpec((tm, tn), lambda i, j: (i, j)),
        out_shape=jax.ShapeDtypeStruct((n, MAIN_W), BF16),
        compiler_params=_cparams(("parallel", "arbitrary")),
        name="inproj",
    )(h, w_t)


GATES_TB = 256


def _split3(a):
    p0 = a.astype(BF16)
    r1 = a - p0.astype(F32)
    p1 = r1.astype(BF16)
    p2 = (r1 - p1.astype(F32)).astype(BF16)
    return p0, p1, p2


def _prenorm_gates_kernel(x_ref, sc_ref, sh_ref, g_ref, wab_ref, wf_ref, alog_ref, dt_ref, fb_ref,
                          h_ref, o_ref, ws_ref, carry_ref):
    tb = GATES_TB

    @pl.when(pl.program_id(1) == 0)
    def _():
        carry_ref[...] = jnp.zeros_like(carry_ref)
        gate_rows = jnp.concatenate([wab_ref[...], wf_ref[...]], axis=0)
        pad = jnp.zeros((LANES - gate_rows.shape[0], D_MODEL), F32)
        hi, lo = _split_bf16(jnp.concatenate([gate_rows, pad], axis=0))
        ws_ref[0:LANES, :] = hi
        ws_ref[LANES:, :] = lo

    hb = _modulated_norm(x_ref[...], g_ref[...], sc_ref[...], sh_ref[...]).astype(BF16)
    h_ref[...] = hb
    both = _dot_nt(hb, ws_ref[...])
    x = both[:, :LANES] + both[:, LANES:]
    lane = lax.broadcasted_iota(jnp.int32, (tb, LANES), 1)
    g = -jnp.exp(alog_ref[...]) * _softplus(x + dt_ref[...])
    beta = jax.nn.sigmoid(x)
    lf = -_softplus(-(x + fb_ref[...]))

    ri = lax.broadcasted_iota(jnp.int32, (tb, tb), 0)
    ci = lax.broadcasted_iota(jnp.int32, (tb, tb), 1)
    same_chunk = (ri // CHUNK) == (ci // CHUNK)
    tri = (ri >= ci)
    m_all = jnp.where(tri, 1.0, 0.0).astype(BF16)
    m_chunk = jnp.where(tri & same_chunk, 1.0, 0.0).astype(BF16)
    m_tot = jnp.where(same_chunk, 1.0, 0.0).astype(BF16)

    in_a = lane < HEADS
    parts = jnp.concatenate(_split3(jnp.where(in_a, g, lf)), axis=1)
    sums = _dot(jnp.concatenate([m_chunk, m_tot, m_all], axis=0), parts)
    sums = sums[:, :LANES] + (sums[:, LANES:2 * LANES] + sums[:, 2 * LANES:])
    gc = sums[:tb]
    glast = sums[tb:2 * tb]
    fcum = sums[2 * tb:] + carry_ref[...]
    carry_ref[...] = fcum[tb - 1:tb, :]

    egc = jnp.where(in_a, jnp.exp(gc), 0.0)
    ek = jnp.where(in_a, jnp.exp(glast - gc), 0.0)
    elast = jnp.where(in_a, jnp.exp(glast), 0.0)
    out = jnp.where(in_a, gc, jnp.where(lane < 2 * HEADS, beta, jnp.where(lane < 3 * HEADS, fcum, 0.0)))
    out = out + pltpu.roll(egc, L_EGC, 1) + pltpu.roll(ek, L_EK, 1) + pltpu.roll(elast, L_ELAST, 1)
    o_ref[...] = out


def _prenorm_gates(x, sc1, sh1, g, w_t, alog_row, dt_row, fb_row):
    bsz, seq, _ = x.shape
    tb = GATES_TB
    row = pl.BlockSpec((1, LANES), lambda b, t: (0, 0))
    modrow = pl.BlockSpec((None, 1, D_MODEL), lambda b, t: (b, 0, 0))
    return pl.pallas_call(
        _prenorm_gates_kernel,
        grid=(bsz, seq // tb),
        in_specs=[
            pl.BlockSpec((None, tb, D_MODEL), lambda b, t: (b, t, 0)),
            modrow, modrow,
            pl.BlockSpec((1, D_MODEL), lambda b, t: (0, 0)),
            pl.BlockSpec((pl.Element(2 * HEADS), pl.Element(D_MODEL)), lambda b, t: (AB_ROW0, 0)),
            pl.BlockSpec((pl.Element(HEADS), pl.Element(D_MODEL)), lambda b, t: (F_ROW0, 0)),
            row, row, row,
        ],
        out_specs=[
            pl.BlockSpec((None, tb, D_MODEL), lambda b, t: (b, t, 0)),
            pl.BlockSpec((None, tb, LANES), lambda b, t: (b, t, 0)),
        ],
        out_shape=[
            jax.ShapeDtypeStruct((bsz, seq, D_MODEL), BF16),
            jax.ShapeDtypeStruct((bsz, seq, LANES), F32),
        ],
        scratch_shapes=[pltpu.VMEM((2 * LANES, D_MODEL), BF16), pltpu.VMEM((1, LANES), F32)],
        compiler_params=_cparams(("parallel", "arbitrary")),
        name="prenorm_gates",
    )(x, sc1, sh1, g, w_t, w_t, alog_row, dt_row, fb_row)


DN_TB = 256
DN_GROUP = 4
HALO = SUBLANES


def _bdot(a, b):
    return lax.dot_general(a, b, (((2,), (1,)), ((0,), (0,))), preferred_element_type=F32)


def _bdot_nt(a, b):
    return lax.dot_general(a, b, (((2,), (2,)), ((0,), (0,))), preferred_element_type=F32)


def _inv_unit_lower(a, eye, blk16, blk32):
    n = jnp.where(blk16, -a, 0.0)
    e1 = jnp.where(blk32 & jnp.logical_not(blk16), a, 0.0).astype(BF16)
    e2 = jnp.where(blk32, 0.0, a).astype(BF16)
    t = eye + n
    p = n.astype(BF16)
    for _ in range(3):
        p = _bdot(p, p).astype(BF16)
        t = t + _bdot(t.astype(BF16), p)
    for e in (e1, e2):
        tb = t.astype(BF16)
        t = t - _bdot(_bdot(tb, e).astype(BF16), tb)
    return t


def _deltanet_kernel(q_ref, k_ref, v_ref, z_ref, wq_ref, wk_ref, wv_ref, slab_ref, gct_ref, og_ref,
                     o_ref, ext_ref, qn_ref, kn_ref, vv_ref, s_ref):
    tb = DN_TB

    @pl.when(pl.program_id(1) == 0)
    def _():
        ext_ref[:, 0:HALO, :] = jnp.zeros((3, HALO, HEAD_W), F32)
        s_ref[...] = jnp.zeros_like(s_ref)

    for idx, (u_ref, w_ref) in enumerate(((q_ref, wq_ref), (k_ref, wk_ref), (v_ref, wv_ref))):
        for h in range(HEADS):
            cols = slice(h * HEAD_DIM, (h + 1) * HEAD_DIM)
            ext_ref[idx, HALO:HALO + tb, cols] = u_ref[:, cols].astype(F32)
            y = None
            for j in range(CONV_K):
                start = HALO - (CONV_K - 1) + j
                term = ext_ref[idx, start:start + tb, cols] * w_ref[j:j + 1, cols]
                y = term if y is None else y + term
            y = _silu(y)
            if idx == 2:
                vv_ref[h] = y
            else:
                yn = y * lax.rsqrt(jnp.sum(y * y, axis=-1, keepdims=True) + EPS)
                if idx == 0:
                    qn_ref[h] = (yn * (HEAD_DIM ** -0.5)).astype(BF16)
                else:
                    kn_ref[h] = yn.astype(BF16)
        ext_ref[idx, 0:HALO, :] = ext_ref[idx, tb:tb + HALO, :]

    ri = lax.broadcasted_iota(jnp.int32, (DN_GROUP * HEADS, CHUNK, CHUNK), 1)
    ci = lax.broadcasted_iota(jnp.int32, (DN_GROUP * HEADS, CHUNK, CHUNK), 2)
    incl = ri >= ci
    strict = ri > ci
    eye = jnp.where(ri == ci, 1.0, 0.0)
    blk16 = (ri // 16) == (ci // 16)
    blk32 = (ri // 32) == (ci // 32)

    def group_body(c, _):
        rows = [pl.ds(pl.multiple_of((c * DN_GROUP + g) * CHUNK, CHUNK), CHUNK) for g in range(DN_GROUP)]
        slabs = [slab_ref[r, :] for r in rows]

        def col(off, width):
            return jnp.stack([jnp.broadcast_to(sl[:, off + h:off + h + 1], (CHUNK, width))
                              for sl in slabs for h in range(HEADS)])

        def grouped(ref):
            return jnp.concatenate([ref[:, r, :] for r in rows], axis=0)

        q = grouped(qn_ref)
        k = grouped(kn_ref)
        v = grouped(vv_ref)
        beta = col(L_BETA, HEAD_DIM)
        egc = col(L_EGC, HEAD_DIM)
        gc_row = jnp.concatenate([gct_ref[c * DN_GROUP + g] for g in range(DN_GROUP)], axis=0)

        decay = jnp.where(incl, jnp.exp(col(L_GC, CHUNK) - gc_row), 0.0)
        kk = _bdot_nt(k, k)
        qk = (_bdot_nt(q, k) * decay).astype(BF16)
        a = jnp.where(strict, kk * decay * beta[:, :, :CHUNK], 0.0)
        t = _inv_unit_lower(a, eye, blk16, blk32).astype(BF16)

        kf = k.astype(F32)
        vb = (v * beta).astype(BF16)
        kbg = (kf * (beta * egc)).astype(BF16)
        u = _bdot(t, vb)
        w = _bdot(t, kbg).astype(BF16)
        qd = (q.astype(F32) * egc).astype(BF16)
        kd = kf * col(L_EK, HEAD_DIM)
        kdt = jnp.stack([kd[n].T for n in range(DN_GROUP * HEADS)]).astype(BF16)

        for g in range(DN_GROUP):
            sel = slice(g * HEADS, (g + 1) * HEADS)
            s = s_ref[...]
            sb = s.astype(BF16)
            vnb = (u[sel] - _bdot(w[sel], sb)).astype(BF16)
            o = _bdot(qd[sel], sb) + _bdot(qk[sel], vnb)
            elast = jnp.stack([jnp.broadcast_to(slabs[g][CHUNK - 1:CHUNK, L_ELAST + h:L_ELAST + h + 1],
                                                (HEAD_DIM, HEAD_DIM)) for h in range(HEADS)])
            s_ref[...] = s * elast + _bdot(kdt[sel], vnb)

            r = lax.rsqrt(jnp.mean(o * o, axis=-1, keepdims=True) + EPS)
            on = (o * r) * og_ref[...]
            for h in range(HEADS):
                cols = slice(h * HEAD_DIM, (h + 1) * HEAD_DIM)
                o_ref[rows[g], cols] = (on[h] * _silu(z_ref[rows[g], cols].astype(F32))).astype(BF16)
        return 0

    lax.fori_loop(0, tb // (CHUNK * DN_GROUP), group_body, 0)


def _deltanet(proj3, conv_w, slab, gct, onorm_g):
    bsz, seq, _ = proj3.shape
    tb = DN_TB
    nct = tb // CHUNK

    def colblk(j):
        return pl.BlockSpec((None, tb, HEAD_W), lambda b, t: (b, t, j))

    def wblk(j):
        return pl.BlockSpec((CONV_K, HEAD_W), lambda b, t: (0, j))

    return pl.pallas_call(
        _deltanet_kernel,
        grid=(bsz, seq // tb),
        in_specs=[
            colblk(0), colblk(1), colblk(2), colblk(3),
            wblk(0), wblk(1), wblk(2),
            pl.BlockSpec((None, tb, LANES), lambda b, t: (b, t, 0)),
            pl.BlockSpec((None, nct, HEADS, 1, CHUNK), lambda b, t: (b, t, 0, 0, 0)),
            pl.BlockSpec((1, HEAD_DIM), lambda b, t: (0, 0)),
        ],
        out_specs=pl.BlockSpec((None, tb, HEAD_W), lambda b, t: (b, t, 0)),
        out_shape=jax.ShapeDtypeStruct((bsz, seq, HEAD_W), BF16),
        scratch_shapes=[
            pltpu.VMEM((3, tb + HALO, HEAD_W), F32),
            pltpu.VMEM((HEADS, tb, HEAD_DIM), BF16),
            pltpu.VMEM((HEADS, tb, HEAD_DIM), BF16),
            pltpu.VMEM((HEADS, tb, HEAD_DIM), F32),
            pltpu.VMEM((HEADS, HEAD_DIM, HEAD_DIM), F32),
        ],
        compiler_params=_cparams(("parallel", "arbitrary")),
        name="deltanet",
    )(proj3, proj3, proj3, proj3, conv_w, conv_w, conv_w, slab, gct, onorm_g)


FOX_T = 512
FOX_STRIP = 32
NEG_INF = float("-inf")
LOG2E = 1.4426950408889634


def _fox_kernel(q_ref, k_ref, v_ref, f_ref, o_ref, s_a_ref, s_b_ref, p_a_ref, p_b_ref, m_ref, l_ref, alpha_ref,
                acc_ref):
    tq = FOX_T
    i = pl.program_id(2)
    qs = (q_ref[...].astype(F32) * (HEAD_DIM ** -0.5 * LOG2E)).astype(BF16)
    f0 = f_ref[i][:, 0:1]
    m_ref[...] = jnp.full_like(m_ref, NEG_INF)
    l_ref[...] = jnp.zeros_like(l_ref)
    acc_ref[...] = jnp.zeros_like(acc_ref)

    def keys(j):
        return pl.ds(pl.multiple_of(j * tq, tq), tq)

    def scores(j):
        return _dot_nt(qs, k_ref[keys(j), :])

    def softmax(j, s_view, p_view, masked):
        bias = (f0 - f_ref[j]) * LOG2E

        def width(r):
            if not masked:
                return tq
            return min(tq, ((r + 1) * FOX_STRIP + LANES - 1) // LANES * LANES)

        def logits(r):
            rows = slice(r * FOX_STRIP, (r + 1) * FOX_STRIP)
            w = width(r)
            s = s_view[rows, :w] + bias[:, :w]
            if masked:
                ri = r * FOX_STRIP + lax.broadcasted_iota(jnp.int32, (FOX_STRIP, w), 0)
                ci = lax.broadcasted_iota(jnp.int32, (FOX_STRIP, w), 1)
                s = jnp.where(ci <= ri, s, NEG_INF)
            return rows, s

        for r in range(tq // FOX_STRIP):
            rows, s = logits(r)
            m_old = m_ref[rows, :]
            m_new = jnp.maximum(m_old, jnp.max(s, axis=-1, keepdims=True))
            alpha_ref[rows, :] = jnp.exp2(m_old - m_new)
            m_ref[rows, :] = m_new
        for r in range(tq // FOX_STRIP):
            rows, s = logits(r)
            w = width(r)
            p = jnp.exp2(s - jnp.concatenate([m_ref[rows, :]] * (w // LANES), axis=1))
            alpha = alpha_ref[rows, :]
            l_ref[rows, :] = alpha * l_ref[rows, :] + jnp.sum(p, axis=-1, keepdims=True)
            p_view[rows, :w] = p.astype(BF16)
            if w < tq:
                p_view[rows, w:] = jnp.zeros((FOX_STRIP, tq - w), BF16)

    s_a_ref[...] = scores(i)
    s_b_ref[...] = scores(jnp.maximum(i - 1, 0))
    softmax(i, s_a_ref, p_a_ref, True)

    def step(t, s_cur, p_cur, s_nxt, p_prv):
        j = i - t
        s_nxt[...] = scores(jnp.maximum(j - 1, 0))
        pv = _dot(p_prv[...], v_ref[keys(j + 1), :])
        softmax(j, s_cur, p_cur, False)
        acc_ref[...] = alpha_ref[...] * (acc_ref[...] + pv)

    def pair(u):
        step(2 * u + 1, s_b_ref, p_b_ref, s_a_ref, p_a_ref)
        step(2 * u + 2, s_a_ref, p_a_ref, s_b_ref, p_b_ref)

    def quad(w, _):
        pair(2 * w)
        pair(2 * w + 1)
        return 0

    quads = lax.shift_right_logical(i, 2)
    lax.fori_loop(0, quads, quad, 0)
    pl.when((i & 2) != 0)(functools.partial(pair, 2 * quads))

    def finish(p_last):
        acc = acc_ref[...] + _dot(p_last[...], v_ref[keys(0), :])
        o_ref[...] = (acc / l_ref[...]).astype(BF16)

    @pl.when((i & 1) == 1)
    def _():
        step(i, s_b_ref, p_b_ref, s_a_ref, p_a_ref)
        finish(p_b_ref)

    @pl.when((i & 1) == 0)
    def _():
        finish(p_a_ref)


def _fox(proj3, f_rows):
    bsz, seq, _ = proj3.shape
    tq = FOX_T
    qb, kb, vb = 4 * HEADS, 5 * HEADS, 6 * HEADS
    return pl.pallas_call(
        _fox_kernel,
        grid=(bsz, HEADS, seq // tq),
        in_specs=[
            pl.BlockSpec((None, tq, HEAD_DIM), lambda b, h, i: (b, i, qb + h)),
            pl.BlockSpec((None, seq, HEAD_DIM), lambda b, h, i: (b, 0, kb + h)),
            pl.BlockSpec((None, seq, HEAD_DIM), lambda b, h, i: (b, 0, vb + h)),
            pl.BlockSpec((None, None, seq // tq, 1, tq), lambda b, h, i: (b, h, 0, 0, 0)),
        ],
        out_specs=pl.BlockSpec((None, tq, HEAD_DIM), lambda b, h, i: (b, i, h)),
        out_shape=jax.ShapeDtypeStruct((bsz, seq, HEAD_W), BF16),
        scratch_shapes=[
            pltpu.VMEM((tq, tq), F32),
            pltpu.VMEM((tq, tq), F32),
            pltpu.VMEM((tq, tq), BF16),
            pltpu.VMEM((tq, tq), BF16),
            pltpu.VMEM((tq, LANES), F32),
            pltpu.VMEM((tq, LANES), F32),
            pltpu.VMEM((tq, LANES), F32),
            pltpu.VMEM((tq, HEAD_DIM), F32),
        ],
        compiler_params=_cparams(("parallel", "parallel", "arbitrary")),
        name="fox",
    )(proj3, proj3, proj3, f_rows)


OUT_TM = 512
R_E0, R_E1, R_W0, R_W1 = 0, 1, 2, 3


def _first_argmax(vals, lane):
    m = jnp.max(vals, axis=-1, keepdims=True)
    idx = jnp.min(jnp.where(vals == m, lane, LANES), axis=-1, keepdims=True)
    return m, idx


def _outproj_kernel(oa_ref, ob_ref, wa_ref, wb_ref, x_ref, g1_ref, sc_ref, sh_ref, g_ref,
                    wr_ref, br_ref, x1_ref, h2_ref, r_ref, cnt_ref):
    mix = _dot(oa_ref[...], wa_ref[...]) + _dot(ob_ref[...], wb_ref[...])
    x1 = x_ref[...] + g1_ref[...] * mix
    x1_ref[...] = x1
    h2 = _modulated_norm(x1, g_ref[...], sc_ref[...], sh_ref[...])
    h2_ref[...] = h2

    logits = _dot3_pre(h2, wr_ref[...]) + br_ref[...]
    tm = logits.shape[0]
    lane = lax.broadcasted_iota(jnp.int32, (tm, LANES), 1)
    gl = jnp.where(lane < N_GROUPS, logits, NEG_INF)
    gmax, gidx = _first_argmax(gl, lane)
    pg = 1.0 / jnp.sum(jnp.exp(gl - gmax), axis=-1, keepdims=True)
    e_lane = lane - N_GROUPS
    in_grp = (e_lane >= gidx * EXPERTS_PER_GROUP) & (e_lane < (gidx + 1) * EXPERTS_PER_GROUP)
    el = jnp.where(in_grp, logits, NEG_INF)
    v0, i0 = _first_argmax(el, lane)
    v1, i1 = _first_argmax(jnp.where(lane == i0, NEG_INF, el), lane)
    ex = jnp.exp(v1 - v0)
    w0 = pg / (1.0 + ex)
    w1 = pg * ex / (1.0 + ex)
    e0 = (i0 - N_GROUPS).astype(F32)
    e1 = (i1 - N_GROUPS).astype(F32)
    r_ref[...] = jnp.where(lane == R_E0, e0, jnp.where(lane == R_E1, e1,
                           jnp.where(lane == R_W0, w0, jnp.where(lane == R_W1, w1, 0.0))))

    @pl.when(pl.program_id(0) == 0)
    def _():
        cnt_ref[...] = jnp.zeros_like(cnt_ref)

    picked = (lane == i0 - N_GROUPS) | (lane == i1 - N_GROUPS)
    cnt_ref[...] += jnp.sum(jnp.where(picked, 1.0, 0.0), axis=0, keepdims=True)


def _outproj(o_a, o_b, wa, wb, x2, g1, sc2, sh2, g, wr_hi_lo, br, seq):
    n = x2.shape[0]
    tm = OUT_TM
    per_b = seq // tm
    modrow = pl.BlockSpec((None, 1, D_MODEL), lambda i: (i // per_b, 0, 0))
    const = lambda shape: pl.BlockSpec(shape, lambda i: (0, 0))
    return pl.pallas_call(
        _outproj_kernel,
        grid=(n // tm,),
        in_specs=[
            pl.BlockSpec((tm, HEAD_W), lambda i: (i, 0)),
            pl.BlockSpec((tm, HEAD_W), lambda i: (i, 0)),
            const((HEAD_W, D_MODEL)), const((HEAD_W, D_MODEL)),
            pl.BlockSpec((tm, D_MODEL), lambda i: (i, 0)),
            modrow, modrow, modrow,
            const((1, D_MODEL)),
            const((D_MODEL, 2 * LANES)), const((1, LANES)),
        ],
        out_specs=[
            pl.BlockSpec((tm, D_MODEL), lambda i: (i, 0)),
            pl.BlockSpec((tm, D_MODEL), lambda i: (i, 0)),
            pl.BlockSpec((tm, LANES), lambda i: (i, 0)),
            pl.BlockSpec((1, LANES), lambda i: (0, 0)),
        ],
        out_shape=[
            jax.ShapeDtypeStruct((n, D_MODEL), F32),
            jax.ShapeDtypeStruct((n, D_MODEL), F32),
            jax.ShapeDtypeStruct((n, LANES), F32),
            jax.ShapeDtypeStruct((1, LANES), F32),
        ],
        compiler_params=_cparams(("arbitrary",)),
        name="outproj",
    )(o_a, o_b, wa, wb, x2, g1, sc2, sh2, g, wr_hi_lo, br)


MOE_TM = 256
ROUTE_TB = 512


def _route_kernel(r_ref, cnt_ref, pos_ref, run_ref, base_ref):
    tb = ROUTE_TB
    t = pl.program_id(0)
    r = r_ref[...]
    lane = lax.broadcasted_iota(jnp.int32, (tb, LANES), 1)
    e0 = r[:, R_E0:R_E0 + 1].astype(jnp.int32)
    e1 = r[:, R_E1:R_E1 + 1].astype(jnp.int32)
    oh0 = lane == e0
    oh1 = lane == e1
    both = jnp.where(oh0 | oh1, 1.0, 0.0)

    @pl.when(t == 0)
    def _():
        counts = cnt_ref[...]
        padded = jnp.ceil(counts / MOE_TM) * MOE_TM
        li = lax.broadcasted_iota(jnp.int32, (LANES, LANES), 0)
        lj = lax.broadcasted_iota(jnp.int32, (LANES, LANES), 1)
        upper = jnp.where(li < lj, 1.0, 0.0).astype(BF16)
        hi = jnp.floor(padded / 256.0)
        lo = padded - hi * 256.0
        hi8 = jnp.broadcast_to(hi, (SUBLANES, LANES)).astype(BF16)
        lo8 = jnp.broadcast_to(lo, (SUBLANES, LANES)).astype(BF16)
        base = _dot(hi8, upper) * 256.0 + _dot(lo8, upper)
        base_ref[...] = base[0:1, :]
        run_ref[...] = jnp.zeros_like(run_ref)

    ri = lax.broadcasted_iota(jnp.int32, (tb, tb), 0)
    ci = lax.broadcasted_iota(jnp.int32, (tb, tb), 1)
    strict = jnp.where(ri > ci, 1.0, 0.0).astype(BF16)
    before = _dot(strict, both.astype(BF16)) + run_ref[...] + base_ref[...]
    p0 = jnp.sum(jnp.where(oh0, before, 0.0), axis=-1, keepdims=True)
    p1 = jnp.sum(jnp.where(oh1, before, 0.0), axis=-1, keepdims=True)
    pos_ref[...] = jnp.where(lane == 0, p0, jnp.where(lane == 1, p1, 0.0)).astype(jnp.int32)
    run_ref[...] = run_ref[...] + jnp.sum(both, axis=0, keepdims=True)


def _route(rslab, counts):
    n = rslab.shape[0]
    tb = ROUTE_TB
    return pl.pallas_call(
        _route_kernel,
        grid=(n // tb,),
        in_specs=[pl.BlockSpec((tb, LANES), lambda t: (t, 0)), pl.BlockSpec((1, LANES), lambda t: (0, 0))],
        out_specs=pl.BlockSpec((tb, LANES), lambda t: (t, 0)),
        out_shape=jax.ShapeDtypeStruct((n, LANES), jnp.int32),
        scratch_shapes=[pltpu.VMEM((1, LANES), F32), pltpu.VMEM((1, LANES), F32)],
        compiler_params=_cparams(("arbitrary",)),
        name="route",
    )(rslab, counts)


DISP_TB = 1024
DMA_UNROLL = 8


def _dispatch_kernel(pad_start_ref, pad_len_ref, used_ref, pos_ref, h2_ref, xs_ref, zero_ref, sem, zsem):
    t = pl.program_id(0)
    tb = DISP_TB
    n_tiles = xs_ref.shape[0] // MOE_TM

    def row_copy(g, r, k):
        dst_row = pos_ref[0, 2 * SUBLANES * g + 2 * r + k]
        return pltpu.make_async_copy(h2_ref.at[g, pl.ds(r, 1)], xs_ref.at[pl.ds(dst_row, 1)], sem)

    def issue(g, _):
        for r in range(SUBLANES):
            row_copy(g, r, 0).start(priority=0)
            row_copy(g, r, 1).start(priority=1)
        return 0

    lax.fori_loop(0, tb // SUBLANES, issue, 0)

    @pl.when(t == 0)
    def _():
        zero_ref[...] = jnp.zeros_like(zero_ref)

        def zero_rows(wait, off, rows):
            cp = pltpu.make_async_copy(zero_ref.at[pl.ds(0, rows)], xs_ref.at[pl.ds(off, rows)], zsem)
            cp.wait() if wait else cp.start()

        def per_expert(wait, e, _):
            start = pad_start_ref[e]
            head = (-start) & (SUBLANES - 1)
            for r in range(SUBLANES - 1):
                pl.when(r < head)(functools.partial(zero_rows, wait, start + r, 1))
            off = start + head
            rest = pad_len_ref[e] - head
            piece = MOE_TM // 2
            while piece >= SUBLANES:
                take = (rest & piece) != 0
                pl.when(take)(functools.partial(zero_rows, wait, pl.multiple_of(off, SUBLANES), piece))
                off = off + jnp.where(take, piece, 0)
                piece //= 2
            return 0

        def per_tile(wait, i, _):
            zero_rows(wait, pl.multiple_of(i * MOE_TM, MOE_TM), MOE_TM)
            return 0

        for wait in (False, True):
            lax.fori_loop(0, N_EXPERTS, functools.partial(per_expert, wait), 0)
            lax.fori_loop(used_ref[0], n_tiles, functools.partial(per_tile, wait), 0)

    def drain(g, _):
        for r in range(SUBLANES):
            row_copy(g, r, 0).wait()
            row_copy(g, r, 1).wait()
        return 0

    lax.fori_loop(0, tb // SUBLANES, drain, 0)


def _dispatch(pad_start, pad_len, used, pos2, h2, p_rows):
    n = h2.shape[0]
    tb = DISP_TB
    return pl.pallas_call(
        _dispatch_kernel,
        grid_spec=pltpu.PrefetchScalarGridSpec(
            num_scalar_prefetch=3,
            grid=(n // tb,),
            in_specs=[
                pl.BlockSpec((None, 1, 2 * tb), lambda t, *_: (t, 0, 0), memory_space=pltpu.SMEM),
                pl.BlockSpec((tb // SUBLANES, SUBLANES, D_MODEL), lambda t, *_: (t, 0, 0)),
            ],
            out_specs=pl.BlockSpec(memory_space=pl.ANY),
            scratch_shapes=[
                pltpu.VMEM((MOE_TM, D_MODEL), F32),
                pltpu.SemaphoreType.DMA(()),
                pltpu.SemaphoreType.DMA(()),
            ],
        ),
        out_shape=jax.ShapeDtypeStruct((p_rows, D_MODEL), F32),
        compiler_params=_cparams(("arbitrary",)),
        name="dispatch",
    )(pad_start, pad_len, used, pos2, h2.reshape(n // SUBLANES, SUBLANES, D_MODEL))


def _experts_kernel(te_ref, tv_ref, tf_ref, ts_ref, tn_ref, xs_ref, w1_hbm, w3_hbm, w2_hbm, ys_ref,
                    w1f_ref, w3f_ref, w2f_ref, w1b_ref, w3b_ref, w2b_ref, sem):
    i = pl.program_id(0)

    def weight_copies(e, slot):
        return [pltpu.make_async_copy(src.at[e], dst.at[slot], sem.at[slot])
                for src, dst in ((w1_hbm, w1f_ref), (w3_hbm, w3f_ref), (w2_hbm, w2f_ref))]

    @pl.when(tv_ref[i] != 0)
    def _():
        @pl.when(tf_ref[i] != 0)
        def _():
            slot = ts_ref[i]

            @pl.when(i == 0)
            def _():
                for cp in weight_copies(te_ref[i], slot):
                    cp.start()

            for cp in weight_copies(te_ref[i], slot):
                cp.wait()

            @pl.when(tn_ref[i] >= 0)
            def _():
                for cp in weight_copies(tn_ref[i], 1 - slot):
                    cp.start()

            w1b_ref[...] = w1f_ref[slot].astype(BF16)
            w3b_ref[...] = w3f_ref[slot].astype(BF16)
            w2b_ref[...] = w2f_ref[slot].astype(BF16)

        x = xs_ref[...].astype(BF16)
        a = _dot(x, w1b_ref[...])
        b = _dot(x, w3b_ref[...])
        ys_ref[...] = _dot((_silu(a) * b).astype(BF16), w2b_ref[...])

    @pl.when(tv_ref[i] == 0)
    def _():
        ys_ref[...] = jnp.zeros_like(ys_ref)


def _experts(tile_expert, tile_valid, tile_first, tile_slot, tile_next, xs, w1, w3, w2):
    p_rows = xs.shape[0]
    tm = MOE_TM
    hbm = pl.BlockSpec(memory_space=pl.ANY)
    return pl.pallas_call(
        _experts_kernel,
        grid_spec=pltpu.PrefetchScalarGridSpec(
            num_scalar_prefetch=5,
            grid=(p_rows // tm,),
            in_specs=[pl.BlockSpec((tm, D_MODEL), lambda i, te, tv, *_: (jnp.where(tv[i] != 0, i, 0), 0)),
                      hbm, hbm, hbm],
            out_specs=pl.BlockSpec((tm, D_MODEL), lambda i, *_: (i, 0)),
            scratch_shapes=[
                pltpu.VMEM((2, D_MODEL, D_EXPERT), F32),
                pltpu.VMEM((2, D_MODEL, D_EXPERT), F32),
                pltpu.VMEM((2, D_EXPERT, D_MODEL), F32),
                pltpu.VMEM((D_MODEL, D_EXPERT), BF16),
                pltpu.VMEM((D_MODEL, D_EXPERT), BF16),
                pltpu.VMEM((D_EXPERT, D_MODEL), BF16),
                pltpu.SemaphoreType.DMA((2,)),
            ],
        ),
        out_shape=jax.ShapeDtypeStruct((p_rows, D_MODEL), F32),
        compiler_params=_cparams(("arbitrary",)),
        name="experts",
    )(tile_expert, tile_valid, tile_first, tile_slot, tile_next, xs, w1, w3, w2)


COMB_TB = 256


def _combine_kernel(pos_ref, pos_next_ref, ys_ref, r_ref, x1_ref, g2_ref, fg_ref, o_ref, buf_ref, sem, *, final):
    tb = COMB_TB
    t = pl.program_id(0)
    slot = t & 1

    def row_copy(p_ref, s, g, r, k):
        src_row = p_ref[0, 2 * SUBLANES * g + 2 * r + k]
        return pltpu.make_async_copy(ys_ref.at[pl.ds(src_row, 1)], buf_ref.at[s, k, g, pl.ds(r, 1)], sem.at[s])

    def issue(p_ref, s):
        def body(g, _):
            for r in range(SUBLANES):
                row_copy(p_ref, s, g, r, 0).start(priority=0)
                row_copy(p_ref, s, g, r, 1).start(priority=1)
            return 0

        lax.fori_loop(0, tb // SUBLANES, body, 0)

    def drain(g, _):
        for r in range(SUBLANES):
            row_copy(pos_ref, slot, g, r, 0).wait()
            row_copy(pos_ref, slot, g, r, 1).wait()
        return 0

    pl.when(t == 0)(functools.partial(issue, pos_ref, 0))
    pl.when(t + 1 < pl.num_programs(0))(functools.partial(issue, pos_next_ref, 1 - slot))
    lax.fori_loop(0, tb // SUBLANES, drain, 0)
    r = r_ref[...]
    y0 = buf_ref[slot, 0].reshape(tb, D_MODEL)
    y1 = buf_ref[slot, 1].reshape(tb, D_MODEL)
    y = r[:, R_W0:R_W0 + 1] * y0 + r[:, R_W1:R_W1 + 1] * y1
    x2 = x1_ref[...] + g2_ref[...] * y
    if final:
        x2 = (x2 * lax.rsqrt(jnp.mean(x2 * x2, axis=-1, keepdims=True) + EPS)) * fg_ref[...]
    o_ref[...] = x2


def _combine(pos2, ys, rslab, x1, g2, final_g, seq, final):
    n = x1.shape[0]
    tb = COMB_TB
    per_b = seq // tb
    return pl.pallas_call(
        functools.partial(_combine_kernel, final=final),
        grid=(n // tb,),
        in_specs=[
            pl.BlockSpec((None, 1, 2 * tb), lambda i: (i, 0, 0), memory_space=pltpu.SMEM),
            pl.BlockSpec((None, 1, 2 * tb), lambda i: (jnp.minimum(i + 1, n // tb - 1), 0, 0),
                         memory_space=pltpu.SMEM),
            pl.BlockSpec(memory_space=pl.ANY),
            pl.BlockSpec((tb, LANES), lambda i: (i, 0)),
            pl.BlockSpec((tb, D_MODEL), lambda i: (i, 0)),
            pl.BlockSpec((None, 1, D_MODEL), lambda i: (i // per_b, 0, 0)),
            pl.BlockSpec((1, D_MODEL), lambda i: (0, 0)),
        ],
        out_specs=pl.BlockSpec((tb, D_MODEL), lambda i: (i, 0)),
        out_shape=jax.ShapeDtypeStruct((n, D_MODEL), F32),
        scratch_shapes=[pltpu.VMEM((2, 2, tb // SUBLANES, SUBLANES, D_MODEL), F32), pltpu.SemaphoreType.DMA((2,))],
        compiler_params=_cparams(("arbitrary",)),
        name="combine",
    )(pos2, pos2, ys, rslab, x1, g2, final_g)


def _layer(x, c, w_ada, b_ada, norm1_g, w_in, conv_w, a_log, dt_bias, dn_onorm_g, fox_f_bias,
           w_out, norm2_g, w_rg, b_rg, w_re, b_re, w1, w3, w2, final_g, final):
    bsz, seq, d = x.shape
    n = bsz * seq
    x2 = x.reshape(n, d)

    mod = _adaln(c, w_ada, b_ada)
    sh1, sc1, g1, sh2, sc2, g2 = [m.reshape(bsz, 1, d) for m in jnp.split(mod, 6, axis=-1)]

    w_t = w_in.T

    def lane_row(vals, off):
        return jnp.zeros((1, LANES), F32).at[0, off:off + HEADS].set(vals)

    h, slab = _prenorm_gates(x, sc1, sh1, norm1_g.reshape(1, d), w_t, lane_row(a_log, 0), lane_row(dt_bias, 0),
                             lane_row(fox_f_bias, L_F))
    proj3 = _inproj(h.reshape(n, d), w_t).reshape(bsz, seq, MAIN_W)
    nc = seq // CHUNK
    gct = slab[:, :, L_GC:L_GC + HEADS].reshape(bsz, nc, CHUNK, HEADS).transpose(0, 1, 3, 2)
    gct = gct.reshape(bsz, nc, HEADS, 1, CHUNK)
    f_rows = slab[:, :, L_F:L_F + HEADS].transpose(0, 2, 1).reshape(bsz, HEADS, seq // FOX_T, 1, FOX_T)

    o_dn = _deltanet(proj3, conv_w, slab, gct, dn_onorm_g.reshape(1, HEAD_DIM))
    o_fx = _fox(proj3, f_rows)

    wr = jnp.zeros((d, LANES), F32).at[:, :N_GROUPS].set(w_rg).at[:, N_GROUPS:N_GROUPS + N_EXPERTS].set(w_re)
    br = jnp.zeros((1, LANES), F32).at[0, :N_GROUPS].set(b_rg).at[0, N_GROUPS:N_GROUPS + N_EXPERTS].set(b_re)
    wr_hi_lo = jnp.concatenate(_split_bf16(wr), axis=1)
    w_out_b = w_out.astype(BF16)
    x1, h2, rslab, counts = _outproj(o_dn.reshape(n, HEAD_W), o_fx.reshape(n, HEAD_W), w_out_b[:HEAD_W],
                                     w_out_b[HEAD_W:], x2, g1, sc2, sh2, norm2_g.reshape(1, d), wr_hi_lo, br, seq)

    pos = _route(rslab, counts)[:, 0:2]

    cnt = counts[0, :N_EXPERTS].astype(jnp.int32)
    tiles_per = (cnt + MOE_TM - 1) // MOE_TM
    tile_end = jnp.cumsum(tiles_per)
    base = (tile_end - tiles_per) * MOE_TM
    n_tiles = (2 * n) // MOE_TM + N_EXPERTS
    p_rows = n_tiles * MOE_TM
    tid = jnp.arange(n_tiles, dtype=jnp.int32)
    tile_valid = (tid < tile_end[-1]).astype(jnp.int32)
    te_raw = jnp.minimum(jnp.sum(tid[:, None] >= tile_end[None, :], axis=1), N_EXPERTS - 1).astype(jnp.int32)
    last_e = te_raw[jnp.maximum(tile_end[-1] - 1, 0)]
    tile_expert = jnp.where(tile_valid == 1, te_raw, last_e)
    tile_first = (jnp.concatenate([jnp.array([-1], jnp.int32), tile_expert[:-1]]) != tile_expert).astype(jnp.int32)
    pad_start = base + cnt
    pad_len = tiles_per * MOE_TM - cnt
    eid = jnp.arange(N_EXPERTS, dtype=jnp.int32)
    has = tiles_per > 0
    slot_e = (jnp.cumsum(has.astype(jnp.int32)) - 1) & 1
    later = jnp.where(has[None, :] & (eid[None, :] > eid[:, None]), eid[None, :], N_EXPERTS)
    next_e = jnp.min(later, axis=1)
    next_e = jnp.where(next_e < N_EXPERTS, next_e, -1).astype(jnp.int32)
    of_tile = tile_expert[:, None] == eid[None, :]
    tile_slot = jnp.sum(jnp.where(of_tile, slot_e[None, :], 0), axis=1).astype(jnp.int32)
    tile_next = jnp.sum(jnp.where(of_tile, next_e[None, :], 0), axis=1).astype(jnp.int32)

    xs = _dispatch(pad_start, pad_len, tile_end[-1:], pos.reshape(n // DISP_TB, 1, 2 * DISP_TB), h2, p_rows)
    ys = _experts(tile_expert, tile_valid, tile_first, tile_slot, tile_next, xs, w1.reshape(N_EXPERTS, d, D_EXPERT),
                  w3.reshape(N_EXPERTS, d, D_EXPERT), w2.reshape(N_EXPERTS, D_EXPERT, d))
    out = _combine(pos.reshape(n // COMB_TB, 1, 2 * COMB_TB), ys, rslab, x1, g2, final_g.reshape(1, d), seq, final)
    return out.reshape(bsz, seq, d)


def kernel(x, c, w_ada, b_ada, norm1_g, w_in, conv_w, a_log, dt_bias, dn_onorm_g, fox_f_bias, w_out, norm2_g,
           w_router_group, b_router_group, w_router_expert, b_router_expert, w1, w3, w2, final_g):
    depth = w_ada.shape[0]
    for l in range(depth):
        x = _layer(x, c, w_ada[l], b_ada[l], norm1_g[l], w_in[l], conv_w[l], a_log[l], dt_bias[l], dn_onorm_g[l],
                   fox_f_bias[l], w_out[l], norm2_g[l], w_router_group[l], b_router_group[l], w_router_expert[l],
                   b_router_expert[l], w1[l], w3[l], w2[l], final_g, l == depth - 1)
    return x
```

```python
import functools

import jax
import jax.numpy as jnp
from jax import lax
from jax.experimental import pallas as pl
from jax.experimental.pallas import tpu as pltpu

F32 = jnp.float32
BF16 = jnp.bfloat16

D_MODEL = 2048
EPS = 1e-6
CHUNK = 64
HEADS = 8
HEAD_DIM = 128
HEAD_W = HEADS * HEAD_DIM
CONV_K = 4
N_GROUPS = 4
EXPERTS_PER_GROUP = 8
N_EXPERTS = N_GROUPS * EXPERTS_PER_GROUP
D_EXPERT = 512
LANES = 128
SUBLANES = 8
MAIN_W = 7 * HEAD_W
VMEM_LIMIT = 56 * 1024 * 1024

L_GC, L_BETA, L_F, L_EGC, L_EK, L_ELAST = 0, 8, 16, 24, 32, 40


def _cparams(sem):
    return pltpu.CompilerParams(dimension_semantics=sem, vmem_limit_bytes=VMEM_LIMIT)


def _split_bf16(a):
    hi = a.astype(BF16)
    lo = (a - hi.astype(F32)).astype(BF16)
    return hi, lo


def _dot(a, b):
    return jnp.dot(a, b, preferred_element_type=F32)


def _dot_nt(a, b):
    return lax.dot_general(a, b, (((1,), (1,)), ((), ())), preferred_element_type=F32)


def _dot3_pre(a, b_hi_lo):
    ah, al = _split_bf16(a)
    n = b_hi_lo.shape[1] // 2
    both = _dot(ah, b_hi_lo)
    return both[:, :n] + (both[:, n:] + _dot(al, b_hi_lo[:, :n]))


def _softplus(x):
    return jnp.maximum(x, 0.0) + jnp.log1p(jnp.exp(-jnp.abs(x)))


def _silu(x):
    return x * jax.nn.sigmoid(x)


ADALN_TN = 1024

def _adaln_kernel(c_ref, w_ref, b_ref, o_ref):
    c = c_ref[...]
    o_ref[...] = _dot(_silu(c).astype(BF16), w_ref[...].astype(BF16)) + b_ref[...]


def _adaln(c, w, b):
    bsz = c.shape[0]
    n = w.shape[1]
    tn = ADALN_TN
    cp = jnp.zeros((SUBLANES, D_MODEL), F32).at[:bsz].set(c)
    out = pl.pallas_call(
        _adaln_kernel,
        grid=(n // tn,),
        in_specs=[
            pl.BlockSpec((SUBLANES, D_MODEL), lambda j: (0, 0)),
            pl.BlockSpec((D_MODEL, tn), lambda j: (0, j)),
            pl.BlockSpec((1, tn), lambda j: (0, j)),
        ],
        out_specs=pl.BlockSpec((SUBLANES, tn), lambda j: (0, j)),
        out_shape=jax.ShapeDtypeStruct((SUBLANES, n), F32),
        compiler_params=_cparams(("parallel",)),
        name="adaln",
    )(cp, w, b.reshape(1, n))
    return out[:bsz]


INPROJ_TM = 2048
INPROJ_TN = 1024
AB_ROW0 = 4 * HEAD_W
F_ROW0 = MAIN_W + 2 * HEADS


def _modulated_norm(x, g, sc, sh):
    r = lax.rsqrt(jnp.mean(x * x, axis=-1, keepdims=True) + EPS)
    return (x * r) * (g * (1.0 + sc)) + sh


def _inproj_kernel(h_ref, w_ref, o_ref):
    o_ref[...] = _dot_nt(h_ref[...], w_ref[...].astype(BF16)).astype(BF16)


def _inproj(h, w_t):
    n = h.shape[0]
    tm, tn = INPROJ_TM, INPROJ_TN
    first_part = AB_ROW0 // tn

    def w_rows(i, j):
        return (SUBLANES * (j * (tn // SUBLANES) + jnp.where(j >= first_part, 2 * HEADS // SUBLANES, 0)), 0)

    return pl.pallas_call(
        _inproj_kernel,
        grid=(n // tm, MAIN_W // tn),
        in_specs=[
            pl.BlockSpec((tm, D_MODEL), lambda i, j: (i, 0)),
            pl.BlockSpec((pl.Element(tn), pl.Element(D_MODEL)), w_rows),
        ],
        out_specs=pl.BlockSpec((tm, tn), lambda i, j: (i, j)),
        out_shape=jax.ShapeDtypeStruct((n, MAIN_W), BF16),
        compiler_params=_cparams(("parallel", "arbitrary")),
        name="inproj",
    )(h, w_t)


GATES_TB = 256


def _split3(a):
    p0 = a.astype(BF16)
    r1 = a - p0.astype(F32)
    p1 = r1.astype(BF16)
    p2 = (r1 - p1.astype(F32)).astype(BF16)
    return p0, p1, p2


def _prenorm_gates_kernel(x_ref, sc_ref, sh_ref, g_ref, wab_ref, wf_ref, alog_ref, dt_ref, fb_ref,
                          h_ref, o_ref, ws_ref, carry_ref):
    tb = GATES_TB

    @pl.when(pl.program_id(1) == 0)
    def _():
        carry_ref[...] = jnp.zeros_like(carry_ref)
        gate_rows = jnp.concatenate([wab_ref[...], wf_ref[...]], axis=0)
        pad = jnp.zeros((LANES - gate_rows.shape[0], D_MODEL), F32)
        hi, lo = _split_bf16(jnp.concatenate([gate_rows, pad], axis=0))
        ws_ref[0:LANES, :] = hi
        ws_ref[LANES:, :] = lo

    hb = _modulated_norm(x_ref[...], g_ref[...], sc_ref[...], sh_ref[...]).astype(BF16)
    h_ref[...] = hb
    both = _dot_nt(hb, ws_ref[...])
    pre = both[:, :LANES] + both[:, LANES:]
    lane = lax.broadcasted_iota(jnp.int32, (tb, LANES), 1)
    g = -jnp.exp(alog_ref[...]) * _softplus(pre + dt_ref[...])
    beta = jax.nn.sigmoid(pre)
    lf = -_softplus(-(pre + fb_ref[...]))

    ri = lax.broadcasted_iota(jnp.int32, (tb, tb), 0)
    ci = lax.broadcasted_iota(jnp.int32, (tb, tb), 1)
    same_chunk = (ri // CHUNK) == (ci // CHUNK)
    tri = (ri >= ci)
    m_all = jnp.where(tri, 1.0, 0.0).astype(BF16)
    m_chunk = jnp.where(tri & same_chunk, 1.0, 0.0).astype(BF16)
    m_tot = jnp.where(same_chunk, 1.0, 0.0).astype(BF16)

    in_a = lane < HEADS
    parts = jnp.concatenate(_split3(jnp.where(in_a, g, lf)), axis=1)
    sums = _dot(jnp.concatenate([m_chunk, m_tot, m_all], axis=0), parts)
    sums = sums[:, :LANES] + (sums[:, LANES:2 * LANES] + sums[:, 2 * LANES:])
    gc = sums[:tb]
    glast = sums[tb:2 * tb]
    fcum = sums[2 * tb:] + carry_ref[...]
    carry_ref[...] = fcum[tb - 1:tb, :]

    egc = jnp.where(in_a, jnp.exp(gc), 0.0)
    ek = jnp.where(in_a, jnp.exp(glast - gc), 0.0)
    elast = jnp.where(in_a, jnp.exp(glast), 0.0)
    out = jnp.where(in_a, gc, jnp.where(lane < 2 * HEADS, beta, jnp.where(lane < 3 * HEADS, fcum, 0.0)))
    out = out + pltpu.roll(egc, L_EGC, 1) + pltpu.roll(ek, L_EK, 1) + pltpu.roll(elast, L_ELAST, 1)
    o_ref[...] = out


def _prenorm_gates(x, sc1, sh1, g, w_t, alog_row, dt_row, fb_row):
    bsz, seq, _ = x.shape
    tb = GATES_TB
    row = pl.BlockSpec((1, LANES), lambda b, t: (0, 0))
    modrow = pl.BlockSpec((None, 1, D_MODEL), lambda b, t: (b, 0, 0))
    return pl.pallas_call(
        _prenorm_gates_kernel,
        grid=(bsz, seq // tb),
        in_specs=[
            pl.BlockSpec((None, tb, D_MODEL), lambda b, t: (b, t, 0)),
            modrow, modrow,
            pl.BlockSpec((1, D_MODEL), lambda b, t: (0, 0)),
            pl.BlockSpec((pl.Element(2 * HEADS), pl.Element(D_MODEL)), lambda b, t: (AB_ROW0, 0)),
            pl.BlockSpec((pl.Element(HEADS), pl.Element(D_MODEL)), lambda b, t: (F_ROW0, 0)),
            row, row, row,
        ],
        out_specs=[
            pl.BlockSpec((None, tb, D_MODEL), lambda b, t: (b, t, 0)),
            pl.BlockSpec((None, tb, LANES), lambda b, t: (b, t, 0)),
        ],
        out_shape=[
            jax.ShapeDtypeStruct((bsz, seq, D_MODEL), BF16),
            jax.ShapeDtypeStruct((bsz, seq, LANES), F32),
        ],
        scratch_shapes=[pltpu.VMEM((2 * LANES, D_MODEL), BF16), pltpu.VMEM((1, LANES), F32)],
        compiler_params=_cparams(("parallel", "arbitrary")),
        name="prenorm_gates",
    )(x, sc1, sh1, g, w_t, w_t, alog_row, dt_row, fb_row)


DN_TB = 256
DN_GROUP = 4
HALO = SUBLANES
INV_BLOCK = CHUNK // 4


def _bdot(a, b):
    return lax.dot_general(a, b, (((2,), (1,)), ((0,), (0,))), preferred_element_type=F32)


def _bdot_nt(a, b):
    return lax.dot_general(a, b, (((2,), (2,)), ((0,), (0,))), preferred_element_type=F32)


def _inv_unit_lower(a, eye, blk16, blk32):
    n = jnp.where(blk16, -a, 0.0)
    e1 = jnp.where(blk32 & jnp.logical_not(blk16), a, 0.0).astype(BF16)
    e2 = jnp.where(blk32, 0.0, a).astype(BF16)
    t = eye + n
    p = n.astype(BF16)
    for _ in range(INV_BLOCK.bit_length() - 2):
        p = _bdot(p, p).astype(BF16)
        t = t + _bdot(t.astype(BF16), p)
    for e in (e1, e2):
        tb = t.astype(BF16)
        t = t - _bdot(_bdot(tb, e).astype(BF16), tb)
    return t


def _deltanet_kernel(q_ref, k_ref, v_ref, z_ref, wq_ref, wk_ref, wv_ref, slab_ref, gct_ref, og_ref,
                     o_ref, ext_ref, qn_ref, kn_ref, vv_ref, s_ref):
    tb = DN_TB

    @pl.when(pl.program_id(1) == 0)
    def _():
        ext_ref[:, 0:HALO, :] = jnp.zeros((3, HALO, HEAD_W), F32)
        s_ref[...] = jnp.zeros_like(s_ref)

    for idx, (u_ref, w_ref) in enumerate(((q_ref, wq_ref), (k_ref, wk_ref), (v_ref, wv_ref))):
        for h in range(HEADS):
            cols = slice(h * HEAD_DIM, (h + 1) * HEAD_DIM)
            ext_ref[idx, HALO:HALO + tb, cols] = u_ref[:, cols].astype(F32)
            y = None
            for j in range(CONV_K):
                start = HALO - (CONV_K - 1) + j
                term = ext_ref[idx, start:start + tb, cols] * w_ref[j:j + 1, cols]
                y = term if y is None else y + term
            y = _silu(y)
            if idx == 2:
                vv_ref[h] = y
            else:
                yn = y * lax.rsqrt(jnp.sum(y * y, axis=-1, keepdims=True) + EPS)
                if idx == 0:
                    qn_ref[h] = (yn * (HEAD_DIM ** -0.5)).astype(BF16)
                else:
                    kn_ref[h] = yn.astype(BF16)
        ext_ref[idx, 0:HALO, :] = ext_ref[idx, tb:tb + HALO, :]

    ri = lax.broadcasted_iota(jnp.int32, (DN_GROUP * HEADS, CHUNK, CHUNK), 1)
    ci = lax.broadcasted_iota(jnp.int32, (DN_GROUP * HEADS, CHUNK, CHUNK), 2)
    incl = ri >= ci
    strict = ri > ci
    eye = jnp.where(ri == ci, 1.0, 0.0)
    blk16 = (ri // INV_BLOCK) == (ci // INV_BLOCK)
    blk32 = (ri // (2 * INV_BLOCK)) == (ci // (2 * INV_BLOCK))

    def group_body(c, _):
        rows = [pl.ds(pl.multiple_of((c * DN_GROUP + g) * CHUNK, CHUNK), CHUNK) for g in range(DN_GROUP)]
        slabs = [slab_ref[r, :] for r in rows]

        def col(off, width):
            return jnp.stack([jnp.broadcast_to(sl[:, off + h:off + h + 1], (CHUNK, width))
                              for sl in slabs for h in range(HEADS)])

        def grouped(ref):
            return jnp.concatenate([ref[:, r, :] for r in rows], axis=0)

        q = grouped(qn_ref)
        k = grouped(kn_ref)
        v = grouped(vv_ref)
        beta = col(L_BETA, HEAD_DIM)
        egc = col(L_EGC, HEAD_DIM)
        gc_row = jnp.concatenate([gct_ref[c * DN_GROUP + g] for g in range(DN_GROUP)], axis=0)

        decay = jnp.where(incl, jnp.exp(col(L_GC, CHUNK) - gc_row), 0.0)
        kk = _bdot_nt(k, k)
        qk = (_bdot_nt(q, k) * decay).astype(BF16)
        a = jnp.where(strict, kk * decay * beta[:, :, :CHUNK], 0.0)
        t = _inv_unit_lower(a, eye, blk16, blk32).astype(BF16)

        kf = k.astype(F32)
        vb = (v * beta).astype(BF16)
        kbg = (kf * (beta * egc)).astype(BF16)
        u = _bdot(t, vb)
        w = _bdot(t, kbg).astype(BF16)
        qd = (q.astype(F32) * egc).astype(BF16)
        kd = kf * col(L_EK, HEAD_DIM)
        kdt = jnp.stack([kd[n].T for n in range(DN_GROUP * HEADS)]).astype(BF16)

        for g in range(DN_GROUP):
            sel = slice(g * HEADS, (g + 1) * HEADS)
            s = s_ref[...]
            sb = s.astype(BF16)
            vnb = (u[sel] - _bdot(w[sel], sb)).astype(BF16)
            o = _bdot(qd[sel], sb) + _bdot(qk[sel], vnb)
            elast = jnp.stack([jnp.broadcast_to(slabs[g][CHUNK - 1:CHUNK, L_ELAST + h:L_ELAST + h + 1],
                                                (HEAD_DIM, HEAD_DIM)) for h in range(HEADS)])
            s_ref[...] = s * elast + _bdot(kdt[sel], vnb)

            r = lax.rsqrt(jnp.mean(o * o, axis=-1, keepdims=True) + EPS)
            on = (o * r) * og_ref[...]
            for h in range(HEADS):
                cols = slice(h * HEAD_DIM, (h + 1) * HEAD_DIM)
                o_ref[rows[g], cols] = (on[h] * _silu(z_ref[rows[g], cols].astype(F32))).astype(BF16)
        return 0

    lax.fori_loop(0, tb // (CHUNK * DN_GROUP), group_body, 0)


def _deltanet(proj3, conv_w, slab, gct, onorm_g):
    bsz, seq, _ = proj3.shape
    tb = DN_TB
    nct = tb // CHUNK

    def colblk(j):
        return pl.BlockSpec((None, tb, HEAD_W), lambda b, t: (b, t, j))

    def wblk(j):
        return pl.BlockSpec((CONV_K, HEAD_W), lambda b, t: (0, j))

    return pl.pallas_call(
        _deltanet_kernel,
        grid=(bsz, seq // tb),
        in_specs=[
            colblk(0), colblk(1), colblk(2), colblk(3),
            wblk(0), wblk(1), wblk(2),
            pl.BlockSpec((None, tb, LANES), lambda b, t: (b, t, 0)),
            pl.BlockSpec((None, nct, HEADS, 1, CHUNK), lambda b, t: (b, t, 0, 0, 0)),
            pl.BlockSpec((1, HEAD_DIM), lambda b, t: (0, 0)),
        ],
        out_specs=pl.BlockSpec((None, tb, HEAD_W), lambda b, t: (b, t, 0)),
        out_shape=jax.ShapeDtypeStruct((bsz, seq, HEAD_W), BF16),
        scratch_shapes=[
            pltpu.VMEM((3, tb + HALO, HEAD_W), F32),
            pltpu.VMEM((HEADS, tb, HEAD_DIM), BF16),
            pltpu.VMEM((HEADS, tb, HEAD_DIM), BF16),
            pltpu.VMEM((HEADS, tb, HEAD_DIM), F32),
            pltpu.VMEM((HEADS, HEAD_DIM, HEAD_DIM), F32),
        ],
        compiler_params=_cparams(("parallel", "arbitrary")),
        name="deltanet",
    )(proj3, proj3, proj3, proj3, conv_w, conv_w, conv_w, slab, gct, onorm_g)


FOX_T = 512
FOX_STRIP = 32
NEG_INF = float("-inf")
LOG2E = 1.4426950408889634


def _fox_kernel(q_ref, k_ref, v_ref, f_ref, o_ref, s_a_ref, s_b_ref, p_a_ref, p_b_ref, m_ref, l_ref, alpha_ref,
                acc_ref):
    tq = FOX_T
    i = pl.program_id(2)
    qs = (q_ref[...].astype(F32) * (HEAD_DIM ** -0.5 * LOG2E)).astype(BF16)
    f0 = f_ref[i][:, 0:1]
    m_ref[...] = jnp.full_like(m_ref, NEG_INF)
    l_ref[...] = jnp.zeros_like(l_ref)
    acc_ref[...] = jnp.zeros_like(acc_ref)

    def keys(j):
        return pl.ds(pl.multiple_of(j * tq, tq), tq)

    def scores(j):
        return _dot_nt(qs, k_ref[keys(j), :])

    def softmax(j, s_view, p_view, masked):
        bias = (f0 - f_ref[j]) * LOG2E

        def width(r):
            if not masked:
                return tq
            return min(tq, ((r + 1) * FOX_STRIP + LANES - 1) // LANES * LANES)

        def logits(r):
            rows = slice(r * FOX_STRIP, (r + 1) * FOX_STRIP)
            w = width(r)
            s = s_view[rows, :w] + bias[:, :w]
            if masked:
                ri = r * FOX_STRIP + lax.broadcasted_iota(jnp.int32, (FOX_STRIP, w), 0)
                ci = lax.broadcasted_iota(jnp.int32, (FOX_STRIP, w), 1)
                s = jnp.where(ci <= ri, s, NEG_INF)
            return rows, s

        for r in range(tq // FOX_STRIP):
            rows, s = logits(r)
            m_old = m_ref[rows, :]
            m_new = jnp.maximum(m_old, jnp.max(s, axis=-1, keepdims=True))
            alpha_ref[rows, :] = jnp.exp2(m_old - m_new)
            m_ref[rows, :] = m_new
        for r in range(tq // FOX_STRIP):
            rows, s = logits(r)
            w = width(r)
            p = jnp.exp2(s - jnp.concatenate([m_ref[rows, :]] * (w // LANES), axis=1))
            alpha = alpha_ref[rows, :]
            l_ref[rows, :] = alpha * l_ref[rows, :] + jnp.sum(p, axis=-1, keepdims=True)
            p_view[rows, :w] = p.astype(BF16)
            if w < tq:
                p_view[rows, w:] = jnp.zeros((FOX_STRIP, tq - w), BF16)

    s_a_ref[...] = scores(i)
    s_b_ref[...] = scores(jnp.maximum(i - 1, 0))
    softmax(i, s_a_ref, p_a_ref, True)

    def step(t, s_cur, p_cur, s_nxt, p_prv):
        j = i - t
        s_nxt[...] = scores(jnp.maximum(j - 1, 0))
        pv = _dot(p_prv[...], v_ref[keys(j + 1), :])
        softmax(j, s_cur, p_cur, False)
        acc_ref[...] = alpha_ref[...] * (acc_ref[...] + pv)

    def pair(u):
        step(2 * u + 1, s_b_ref, p_b_ref, s_a_ref, p_a_ref)
        step(2 * u + 2, s_a_ref, p_a_ref, s_b_ref, p_b_ref)

    def quad(w, _):
        pair(2 * w)
        pair(2 * w + 1)
        return 0

    quads = lax.shift_right_logical(i, 2)
    lax.fori_loop(0, quads, quad, 0)
    pl.when((i & 2) != 0)(functools.partial(pair, 2 * quads))

    def finish(p_last):
        acc = acc_ref[...] + _dot(p_last[...], v_ref[keys(0), :])
        o_ref[...] = (acc / l_ref[...]).astype(BF16)

    @pl.when((i & 1) == 1)
    def _():
        step(i, s_b_ref, p_b_ref, s_a_ref, p_a_ref)
        finish(p_b_ref)

    @pl.when((i & 1) == 0)
    def _():
        finish(p_a_ref)


def _fox(proj3, f_rows):
    bsz, seq, _ = proj3.shape
    tq = FOX_T
    qb, kb, vb = 4 * HEADS, 5 * HEADS, 6 * HEADS
    return pl.pallas_call(
        _fox_kernel,
        grid=(bsz, HEADS, seq // tq),
        in_specs=[
            pl.BlockSpec((None, tq, HEAD_DIM), lambda b, h, i: (b, i, qb + h)),
            pl.BlockSpec((None, seq, HEAD_DIM), lambda b, h, i: (b, 0, kb + h)),
            pl.BlockSpec((None, seq, HEAD_DIM), lambda b, h, i: (b, 0, vb + h)),
            pl.BlockSpec((None, None, seq // tq, 1, tq), lambda b, h, i: (b, h, 0, 0, 0)),
        ],
        out_specs=pl.BlockSpec((None, tq, HEAD_DIM), lambda b, h, i: (b, i, h)),
        out_shape=jax.ShapeDtypeStruct((bsz, seq, HEAD_W), BF16),
        scratch_shapes=[
            pltpu.VMEM((tq, tq), F32),
            pltpu.VMEM((tq, tq), F32),
            pltpu.VMEM((tq, tq), BF16),
            pltpu.VMEM((tq, tq), BF16),
            pltpu.VMEM((tq, LANES), F32),
            pltpu.VMEM((tq, LANES), F32),
            pltpu.VMEM((tq, LANES), F32),
            pltpu.VMEM((tq, HEAD_DIM), F32),
        ],
        compiler_params=_cparams(("parallel", "parallel", "arbitrary")),
        name="fox",
    )(proj3, proj3, proj3, f_rows)


OUT_TM = 512
R_E0, R_E1, R_W0, R_W1 = 0, 1, 2, 3


def _first_argmax(vals, lane):
    m = jnp.max(vals, axis=-1, keepdims=True)
    idx = jnp.min(jnp.where(vals == m, lane, LANES), axis=-1, keepdims=True)
    return m, idx


def _outproj_kernel(oa_ref, ob_ref, wa_ref, wb_ref, x_ref, g1_ref, sc_ref, sh_ref, g_ref,
                    wr_ref, br_ref, x1_ref, h2_ref, r_ref, cnt_ref):
    mix = _dot(oa_ref[...], wa_ref[...]) + _dot(ob_ref[...], wb_ref[...])
    x1 = x_ref[...] + g1_ref[...] * mix
    x1_ref[...] = x1
    h2 = _modulated_norm(x1, g_ref[...], sc_ref[...], sh_ref[...])
    h2_ref[...] = h2

    logits = _dot3_pre(h2, wr_ref[...]) + br_ref[...]
    tm = logits.shape[0]
    lane = lax.broadcasted_iota(jnp.int32, (tm, LANES), 1)
    gl = jnp.where(lane < N_GROUPS, logits, NEG_INF)
    gmax, gidx = _first_argmax(gl, lane)
    pg = 1.0 / jnp.sum(jnp.exp(gl - gmax), axis=-1, keepdims=True)
    e_lane = lane - N_GROUPS
    in_grp = (e_lane >= gidx * EXPERTS_PER_GROUP) & (e_lane < (gidx + 1) * EXPERTS_PER_GROUP)
    el = jnp.where(in_grp, logits, NEG_INF)
    v0, i0 = _first_argmax(el, lane)
    v1, i1 = _first_argmax(jnp.where(lane == i0, NEG_INF, el), lane)
    ex = jnp.exp(v1 - v0)
    w0 = pg / (1.0 + ex)
    w1 = pg * ex / (1.0 + ex)
    e0 = (i0 - N_GROUPS).astype(F32)
    e1 = (i1 - N_GROUPS).astype(F32)
    r_ref[...] = jnp.where(lane == R_E0, e0, jnp.where(lane == R_E1, e1,
                           jnp.where(lane == R_W0, w0, jnp.where(lane == R_W1, w1, 0.0))))

    @pl.when(pl.program_id(0) == 0)
    def _():
        cnt_ref[...] = jnp.zeros_like(cnt_ref)

    picked = (lane == i0 - N_GROUPS) | (lane == i1 - N_GROUPS)
    cnt_ref[...] += jnp.sum(jnp.where(picked, 1.0, 0.0), axis=0, keepdims=True)


def _outproj(o_a, o_b, wa, wb, x2, g1, sc2, sh2, g, wr_hi_lo, br, seq):
    n = x2.shape[0]
    tm = OUT_TM
    per_b = seq // tm
    modrow = pl.BlockSpec((None, 1, D_MODEL), lambda i: (i // per_b, 0, 0))
    const = lambda shape: pl.BlockSpec(shape, lambda i: (0, 0))
    return pl.pallas_call(
        _outproj_kernel,
        grid=(n // tm,),
        in_specs=[
            pl.BlockSpec((tm, HEAD_W), lambda i: (i, 0)),
            pl.BlockSpec((tm, HEAD_W), lambda i: (i, 0)),
            const((HEAD_W, D_MODEL)), const((HEAD_W, D_MODEL)),
            pl.BlockSpec((tm, D_MODEL), lambda i: (i, 0)),
            modrow, modrow, modrow,
            const((1, D_MODEL)),
            const((D_MODEL, 2 * LANES)), const((1, LANES)),
        ],
        out_specs=[
            pl.BlockSpec((tm, D_MODEL), lambda i: (i, 0)),
            pl.BlockSpec((tm, D_MODEL), lambda i: (i, 0)),
            pl.BlockSpec((tm, LANES), lambda i: (i, 0)),
            pl.BlockSpec((1, LANES), lambda i: (0, 0)),
        ],
        out_shape=[
            jax.ShapeDtypeStruct((n, D_MODEL), F32),
            jax.ShapeDtypeStruct((n, D_MODEL), F32),
            jax.ShapeDtypeStruct((n, LANES), F32),
            jax.ShapeDtypeStruct((1, LANES), F32),
        ],
        compiler_params=_cparams(("arbitrary",)),
        name="outproj",
    )(o_a, o_b, wa, wb, x2, g1, sc2, sh2, g, wr_hi_lo, br)


MOE_TM = 256
ROUTE_TB = 512


def _route_kernel(r_ref, cnt_ref, pos_ref, run_ref, base_ref):
    tb = ROUTE_TB
    t = pl.program_id(0)
    r = r_ref[...]
    lane = lax.broadcasted_iota(jnp.int32, (tb, LANES), 1)
    e0 = r[:, R_E0:R_E0 + 1].astype(jnp.int32)
    e1 = r[:, R_E1:R_E1 + 1].astype(jnp.int32)
    oh0 = lane == e0
    oh1 = lane == e1
    both = jnp.where(oh0 | oh1, 1.0, 0.0)

    @pl.when(t == 0)
    def _():
        counts = cnt_ref[...]
        padded = jnp.ceil(counts / MOE_TM) * MOE_TM
        li = lax.broadcasted_iota(jnp.int32, (LANES, LANES), 0)
        lj = lax.broadcasted_iota(jnp.int32, (LANES, LANES), 1)
        upper = jnp.where(li < lj, 1.0, 0.0).astype(BF16)
        hi = jnp.floor(padded / 256.0)
        lo = padded - hi * 256.0
        hi8 = jnp.broadcast_to(hi, (SUBLANES, LANES)).astype(BF16)
        lo8 = jnp.broadcast_to(lo, (SUBLANES, LANES)).astype(BF16)
        base = _dot(hi8, upper) * 256.0 + _dot(lo8, upper)
        base_ref[...] = base[0:1, :]
        run_ref[...] = jnp.zeros_like(run_ref)

    ri = lax.broadcasted_iota(jnp.int32, (tb, tb), 0)
    ci = lax.broadcasted_iota(jnp.int32, (tb, tb), 1)
    strict = jnp.where(ri > ci, 1.0, 0.0).astype(BF16)
    before = _dot(strict, both.astype(BF16)) + run_ref[...] + base_ref[...]
    p0 = jnp.sum(jnp.where(oh0, before, 0.0), axis=-1, keepdims=True)
    p1 = jnp.sum(jnp.where(oh1, before, 0.0), axis=-1, keepdims=True)
    pos_ref[...] = jnp.where(lane == 0, p0, jnp.where(lane == 1, p1, 0.0)).astype(jnp.int32)
    run_ref[...] = run_ref[...] + jnp.sum(both, axis=0, keepdims=True)


def _route(rslab, counts):
    n = rslab.shape[0]
    tb = ROUTE_TB
    return pl.pallas_call(
        _route_kernel,
        grid=(n // tb,),
        in_specs=[pl.BlockSpec((tb, LANES), lambda t: (t, 0)), pl.BlockSpec((1, LANES), lambda t: (0, 0))],
        out_specs=pl.BlockSpec((tb, LANES), lambda t: (t, 0)),
        out_shape=jax.ShapeDtypeStruct((n, LANES), jnp.int32),
        scratch_shapes=[pltpu.VMEM((1, LANES), F32), pltpu.VMEM((1, LANES), F32)],
        compiler_params=_cparams(("arbitrary",)),
        name="route",
    )(rslab, counts)


DISP_TB = 1024


def _dispatch_kernel(pad_start_ref, pad_len_ref, used_ref, pos_ref, h2_ref, xs_ref, zero_ref, sem, zsem):
    t = pl.program_id(0)
    tb = DISP_TB
    n_tiles = xs_ref.shape[0] // MOE_TM

    def row_copy(g, r, k):
        dst_row = pos_ref[0, 2 * SUBLANES * g + 2 * r + k]
        return pltpu.make_async_copy(h2_ref.at[g, pl.ds(r, 1)], xs_ref.at[pl.ds(dst_row, 1)], sem)

    def issue(g, _):
        for r in range(SUBLANES):
            row_copy(g, r, 0).start(priority=0)
            row_copy(g, r, 1).start(priority=1)
        return 0

    lax.fori_loop(0, tb // SUBLANES, issue, 0)

    @pl.when(t == 0)
    def _():
        zero_ref[...] = jnp.zeros_like(zero_ref)

        def zero_rows(wait, off, rows):
            cp = pltpu.make_async_copy(zero_ref.at[pl.ds(0, rows)], xs_ref.at[pl.ds(off, rows)], zsem)
            cp.wait() if wait else cp.start()

        def per_expert(wait, e, _):
            start = pad_start_ref[e]
            head = (-start) & (SUBLANES - 1)
            for r in range(SUBLANES - 1):
                pl.when(r < head)(functools.partial(zero_rows, wait, start + r, 1))
            off = start + head
            rest = pad_len_ref[e] - head
            piece = MOE_TM // 2
            while piece >= SUBLANES:
                take = (rest & piece) != 0
                pl.when(take)(functools.partial(zero_rows, wait, pl.multiple_of(off, SUBLANES), piece))
                off = off + jnp.where(take, piece, 0)
                piece //= 2
            return 0

        def per_tile(wait, i, _):
            zero_rows(wait, pl.multiple_of(i * MOE_TM, MOE_TM), MOE_TM)
            return 0

        for wait in (False, True):
            lax.fori_loop(0, N_EXPERTS, functools.partial(per_expert, wait), 0)
            lax.fori_loop(used_ref[0], n_tiles, functools.partial(per_tile, wait), 0)

    def drain(g, _):
        for r in range(SUBLANES):
            row_copy(g, r, 0).wait()
            row_copy(g, r, 1).wait()
        return 0

    lax.fori_loop(0, tb // SUBLANES, drain, 0)


def _dispatch(pad_start, pad_len, used, pos2, h2, p_rows):
    n = h2.shape[0]
    tb = DISP_TB
    return pl.pallas_call(
        _dispatch_kernel,
        grid_spec=pltpu.PrefetchScalarGridSpec(
            num_scalar_prefetch=3,
            grid=(n // tb,),
            in_specs=[
                pl.BlockSpec((None, 1, 2 * tb), lambda t, *_: (t, 0, 0), memory_space=pltpu.SMEM),
                pl.BlockSpec((tb // SUBLANES, SUBLANES, D_MODEL), lambda t, *_: (t, 0, 0)),
            ],
            out_specs=pl.BlockSpec(memory_space=pl.ANY),
            scratch_shapes=[
                pltpu.VMEM((MOE_TM, D_MODEL), F32),
                pltpu.SemaphoreType.DMA(()),
                pltpu.SemaphoreType.DMA(()),
            ],
        ),
        out_shape=jax.ShapeDtypeStruct((p_rows, D_MODEL), F32),
        compiler_params=_cparams(("arbitrary",)),
        name="dispatch",
    )(pad_start, pad_len, used, pos2, h2.reshape(n // SUBLANES, SUBLANES, D_MODEL))


def _experts_kernel(te_ref, tv_ref, tf_ref, ts_ref, tn_ref, xs_ref, w1_hbm, w3_hbm, w2_hbm, ys_ref,
                    w1f_ref, w3f_ref, w2f_ref, w1b_ref, w3b_ref, w2b_ref, sem):
    i = pl.program_id(0)

    def weight_copies(e, slot):
        return [pltpu.make_async_copy(src.at[e], dst.at[slot], sem.at[slot])
                for src, dst in ((w1_hbm, w1f_ref), (w3_hbm, w3f_ref), (w2_hbm, w2f_ref))]

    @pl.when(tv_ref[i] != 0)
    def _():
        @pl.when(tf_ref[i] != 0)
        def _():
            slot = ts_ref[i]

            @pl.when(i == 0)
            def _():
                for cp in weight_copies(te_ref[i], slot):
                    cp.start()

            for cp in weight_copies(te_ref[i], slot):
                cp.wait()

            @pl.when(tn_ref[i] >= 0)
            def _():
                for cp in weight_copies(tn_ref[i], 1 - slot):
                    cp.start()

            w1b_ref[...] = w1f_ref[slot].astype(BF16)
            w3b_ref[...] = w3f_ref[slot].astype(BF16)
            w2b_ref[...] = w2f_ref[slot].astype(BF16)

        x = xs_ref[...].astype(BF16)
        a = _dot(x, w1b_ref[...])
        b = _dot(x, w3b_ref[...])
        ys_ref[...] = _dot((_silu(a) * b).astype(BF16), w2b_ref[...])

    @pl.when(tv_ref[i] == 0)
    def _():
        ys_ref[...] = jnp.zeros_like(ys_ref)


def _experts(tile_expert, tile_valid, tile_first, tile_slot, tile_next, xs, w1, w3, w2):
    p_rows = xs.shape[0]
    tm = MOE_TM
    hbm = pl.BlockSpec(memory_space=pl.ANY)
    return pl.pallas_call(
        _experts_kernel,
        grid_spec=pltpu.PrefetchScalarGridSpec(
            num_scalar_prefetch=5,
            grid=(p_rows // tm,),
            in_specs=[pl.BlockSpec((tm, D_MODEL), lambda i, te, tv, *_: (jnp.where(tv[i] != 0, i, 0), 0)),
                      hbm, hbm, hbm],
            out_specs=pl.BlockSpec((tm, D_MODEL), lambda i, *_: (i, 0)),
            scratch_shapes=[
                pltpu.VMEM((2, D_MODEL, D_EXPERT), F32),
                pltpu.VMEM((2, D_MODEL, D_EXPERT), F32),
                pltpu.VMEM((2, D_EXPERT, D_MODEL), F32),
                pltpu.VMEM((D_MODEL, D_EXPERT), BF16),
                pltpu.VMEM((D_MODEL, D_EXPERT), BF16),
                pltpu.VMEM((D_EXPERT, D_MODEL), BF16),
                pltpu.SemaphoreType.DMA((2,)),
            ],
        ),
        out_shape=jax.ShapeDtypeStruct((p_rows, D_MODEL), F32),
        compiler_params=_cparams(("arbitrary",)),
        name="experts",
    )(tile_expert, tile_valid, tile_first, tile_slot, tile_next, xs, w1, w3, w2)


COMB_TB = 256


def _combine_kernel(pos_ref, pos_next_ref, ys_ref, r_ref, x1_ref, g2_ref, fg_ref, o_ref, buf_ref, sem, *, final):
    tb = COMB_TB
    t = pl.program_id(0)
    slot = t & 1

    def row_copy(p_ref, s, g, r, k):
        src_row = p_ref[0, 2 * SUBLANES * g + 2 * r + k]
        return pltpu.make_async_copy(ys_ref.at[pl.ds(src_row, 1)], buf_ref.at[s, k, g, pl.ds(r, 1)], sem.at[s])

    def issue(p_ref, s):
        def body(g, _):
            for r in range(SUBLANES):
                row_copy(p_ref, s, g, r, 0).start(priority=0)
                row_copy(p_ref, s, g, r, 1).start(priority=1)
            return 0

        lax.fori_loop(0, tb // SUBLANES, body, 0)

    def drain(g, _):
        for r in range(SUBLANES):
            row_copy(pos_ref, slot, g, r, 0).wait()
            row_copy(pos_ref, slot, g, r, 1).wait()
        return 0

    pl.when(t == 0)(functools.partial(issue, pos_ref, 0))
    pl.when(t + 1 < pl.num_programs(0))(functools.partial(issue, pos_next_ref, 1 - slot))
    lax.fori_loop(0, tb // SUBLANES, drain, 0)
    r = r_ref[...]
    y0 = buf_ref[slot, 0].reshape(tb, D_MODEL)
    y1 = buf_ref[slot, 1].reshape(tb, D_MODEL)
    y = r[:, R_W0:R_W0 + 1] * y0 + r[:, R_W1:R_W1 + 1] * y1
    x2 = x1_ref[...] + g2_ref[...] * y
    if final:
        x2 = (x2 * lax.rsqrt(jnp.mean(x2 * x2, axis=-1, keepdims=True) + EPS)) * fg_ref[...]
    o_ref[...] = x2


def _combine(pos2, ys, rslab, x1, g2, final_g, seq, final):
    n = x1.shape[0]
    tb = COMB_TB
    per_b = seq // tb
    return pl.pallas_call(
        functools.partial(_combine_kernel, final=final),
        grid=(n // tb,),
        in_specs=[
            pl.BlockSpec((None, 1, 2 * tb), lambda i: (i, 0, 0), memory_space=pltpu.SMEM),
            pl.BlockSpec((None, 1, 2 * tb), lambda i: (jnp.minimum(i + 1, n // tb - 1), 0, 0),
                         memory_space=pltpu.SMEM),
            pl.BlockSpec(memory_space=pl.ANY),
            pl.BlockSpec((tb, LANES), lambda i: (i, 0)),
            pl.BlockSpec((tb, D_MODEL), lambda i: (i, 0)),
            pl.BlockSpec((None, 1, D_MODEL), lambda i: (i // per_b, 0, 0)),
            pl.BlockSpec((1, D_MODEL), lambda i: (0, 0)),
        ],
        out_specs=pl.BlockSpec((tb, D_MODEL), lambda i: (i, 0)),
        out_shape=jax.ShapeDtypeStruct((n, D_MODEL), F32),
        scratch_shapes=[pltpu.VMEM((2, 2, tb // SUBLANES, SUBLANES, D_MODEL), F32), pltpu.SemaphoreType.DMA((2,))],
        compiler_params=_cparams(("arbitrary",)),
        name="combine",
    )(pos2, pos2, ys, rslab, x1, g2, final_g)


def _layer(x, c, w_ada, b_ada, norm1_g, w_in, conv_w, a_log, dt_bias, dn_onorm_g, fox_f_bias,
           w_out, norm2_g, w_rg, b_rg, w_re, b_re, w1, w3, w2, final_g, final):
    bsz, seq, d = x.shape
    n = bsz * seq
    x2 = x.reshape(n, d)

    mod = _adaln(c, w_ada, b_ada)
    sh1, sc1, g1, sh2, sc2, g2 = [m.reshape(bsz, 1, d) for m in jnp.split(mod, 6, axis=-1)]

    w_t = w_in.T

    def lane_row(vals, off):
        return jnp.zeros((1, LANES), F32).at[0, off:off + HEADS].set(vals)

    h, slab = _prenorm_gates(x, sc1, sh1, norm1_g.reshape(1, d), w_t, lane_row(a_log, 0), lane_row(dt_bias, 0),
                             lane_row(fox_f_bias, L_F))
    proj3 = _inproj(h.reshape(n, d), w_t).reshape(bsz, seq, MAIN_W)
    nc = seq // CHUNK
    gct = slab[:, :, L_GC:L_GC + HEADS].reshape(bsz, nc, CHUNK, HEADS).transpose(0, 1, 3, 2)
    gct = gct.reshape(bsz, nc, HEADS, 1, CHUNK)
    f_rows = slab[:, :, L_F:L_F + HEADS].transpose(0, 2, 1).reshape(bsz, HEADS, seq // FOX_T, 1, FOX_T)

    o_dn = _deltanet(proj3, conv_w, slab, gct, dn_onorm_g.reshape(1, HEAD_DIM))
    o_fx = _fox(proj3, f_rows)

    wr = jnp.zeros((d, LANES), F32).at[:, :N_GROUPS].set(w_rg).at[:, N_GROUPS:N_GROUPS + N_EXPERTS].set(w_re)
    br = jnp.zeros((1, LANES), F32).at[0, :N_GROUPS].set(b_rg).at[0, N_GROUPS:N_GROUPS + N_EXPERTS].set(b_re)
    wr_hi_lo = jnp.concatenate(_split_bf16(wr), axis=1)
    w_out_b = w_out.astype(BF16)
    x1, h2, rslab, counts = _outproj(o_dn.reshape(n, HEAD_W), o_fx.reshape(n, HEAD_W), w_out_b[:HEAD_W],
                                     w_out_b[HEAD_W:], x2, g1, sc2, sh2, norm2_g.reshape(1, d), wr_hi_lo, br, seq)

    pos = _route(rslab, counts)[:, 0:2]

    cnt = counts[0, :N_EXPERTS].astype(jnp.int32)
    tiles_per = (cnt + MOE_TM - 1) // MOE_TM
    tile_end = jnp.cumsum(tiles_per)
    base = (tile_end - tiles_per) * MOE_TM
    n_tiles = (2 * n) // MOE_TM + N_EXPERTS
    p_rows = n_tiles * MOE_TM
    tid = jnp.arange(n_tiles, dtype=jnp.int32)
    tile_valid = (tid < tile_end[-1]).astype(jnp.int32)
    te_raw = jnp.minimum(jnp.sum(tid[:, None] >= tile_end[None, :], axis=1), N_EXPERTS - 1).astype(jnp.int32)
    last_e = te_raw[jnp.maximum(tile_end[-1] - 1, 0)]
    tile_expert = jnp.where(tile_valid == 1, te_raw, last_e)
    tile_first = (jnp.concatenate([jnp.array([-1], jnp.int32), tile_expert[:-1]]) != tile_expert).astype(jnp.int32)
    pad_start = base + cnt
    pad_len = tiles_per * MOE_TM - cnt
    eid = jnp.arange(N_EXPERTS, dtype=jnp.int32)
    has = tiles_per > 0
    slot_e = (jnp.cumsum(has.astype(jnp.int32)) - 1) & 1
    later = jnp.where(has[None, :] & (eid[None, :] > eid[:, None]), eid[None, :], N_EXPERTS)
    next_e = jnp.min(later, axis=1)
    next_e = jnp.where(next_e < N_EXPERTS, next_e, -1).astype(jnp.int32)
    of_tile = tile_expert[:, None] == eid[None, :]
    tile_slot = jnp.sum(jnp.where(of_tile, slot_e[None, :], 0), axis=1).astype(jnp.int32)
    tile_next = jnp.sum(jnp.where(of_tile, next_e[None, :], 0), axis=1).astype(jnp.int32)

    xs = _dispatch(pad_start, pad_len, tile_end[-1:], pos.reshape(n // DISP_TB, 1, 2 * DISP_TB), h2, p_rows)
    ys = _experts(tile_expert, tile_valid, tile_first, tile_slot, tile_next, xs, w1.reshape(N_EXPERTS, d, D_EXPERT),
                  w3.reshape(N_EXPERTS, d, D_EXPERT), w2.reshape(N_EXPERTS, D_EXPERT, d))
    out = _combine(pos.reshape(n // COMB_TB, 1, 2 * COMB_TB), ys, rslab, x1, g2, final_g.reshape(1, d), seq, final)
    return out.reshape(bsz, seq, d)


def kernel(x, c, w_ada, b_ada, norm1_g, w_in, conv_w, a_log, dt_bias, dn_onorm_g, fox_f_bias, w_out, norm2_g,
           w_router_group, b_router_group, w_router_expert, b_router_expert, w1, w3, w2, final_g):
    depth = w_ada.shape[0]
    for l in range(depth):
        x = _layer(x, c, w_ada[l], b_ada[l], norm1_g[l], w_in[l], conv_w[l], a_log[l], dt_bias[l], dn_onorm_g[l],
                   fox_f_bias[l], w_out[l], norm2_g[l], w_router_group[l], b_router_group[l], w_router_expert[l],
                   b_router_expert[l], w1[l], w3[l], w2[l], final_g, l == depth - 1)
    return x
```

```python
import functools

import jax
import jax.numpy as jnp
from jax import lax
from jax.experimental import pallas as pl
from jax.experimental.pallas import tpu as pltpu

F32 = jnp.float32
BF16 = jnp.bfloat16

D_MODEL = 2048
EPS = 1e-6
CHUNK = 64
HEADS = 8
HEAD_DIM = 128
HEAD_W = HEADS * HEAD_DIM
CONV_K = 4
N_GROUPS = 4
EXPERTS_PER_GROUP = 8
N_EXPERTS = N_GROUPS * EXPERTS_PER_GROUP
D_EXPERT = 512
LANES = 128
SUBLANES = 8
MAIN_W = 7 * HEAD_W
VMEM_LIMIT = 56 * 1024 * 1024

L_GC, L_BETA, L_F, L_EGC, L_EK, L_ELAST = 0, 8, 16, 24, 32, 40


def _cparams(sem):
    return pltpu.CompilerParams(dimension_semantics=sem, vmem_limit_bytes=VMEM_LIMIT)


def _split_bf16(a):
    hi = a.astype(BF16)
    lo = (a - hi.astype(F32)).astype(BF16)
    return hi, lo


def _dot(a, b):
    return jnp.dot(a, b, preferred_element_type=F32)


def _dot_nt(a, b):
    return lax.dot_general(a, b, (((1,), (1,)), ((), ())), preferred_element_type=F32)


def _dot3_pre(a, b_hi_lo):
    ah, al = _split_bf16(a)
    n = b_hi_lo.shape[1] // 2
    both = _dot(ah, b_hi_lo)
    return both[:, :n] + (both[:, n:] + _dot(al, b_hi_lo[:, :n]))


def _softplus(x):
    return jnp.maximum(x, 0.0) + jnp.log1p(jnp.exp(-jnp.abs(x)))


def _silu(x):
    return x * jax.nn.sigmoid(x)


ADALN_TN = 1024

def _adaln_kernel(c_ref, w_ref, b_ref, o_ref):
    c = c_ref[...]
    o_ref[...] = _dot(_silu(c).astype(BF16), w_ref[...].astype(BF16)) + b_ref[...]


def _adaln(c, w, b):
    bsz = c.shape[0]
    n = w.shape[1]
    tn = ADALN_TN
    cp = jnp.zeros((SUBLANES, D_MODEL), F32).at[:bsz].set(c)
    out = pl.pallas_call(
        _adaln_kernel,
        grid=(n // tn,),
        in_specs=[
            pl.BlockSpec((SUBLANES, D_MODEL), lambda j: (0, 0)),
            pl.BlockSpec((D_MODEL, tn), lambda j: (0, j)),
            pl.BlockSpec((1, tn), lambda j: (0, j)),
        ],
        out_specs=pl.BlockSpec((SUBLANES, tn), lambda j: (0, j)),
        out_shape=jax.ShapeDtypeStruct((SUBLANES, n), F32),
        compiler_params=_cparams(("parallel",)),
        name="adaln",
    )(cp, w, b.reshape(1, n))
    return out[:bsz]


INPROJ_TM = 2048
INPROJ_TN = 1024
AB_ROW0 = 4 * HEAD_W
F_ROW0 = MAIN_W + 2 * HEADS


def _modulated_norm(x, g, sc, sh):
    r = lax.rsqrt(jnp.mean(x * x, axis=-1, keepdims=True) + EPS)
    return (x * r) * (g * (1.0 + sc)) + sh


def _inproj_kernel(h_ref, w_ref, o_ref):
    o_ref[...] = _dot_nt(h_ref[...], w_ref[...].astype(BF16)).astype(BF16)


def _inproj(h, w_t):
    n = h.shape[0]
    tm, tn = INPROJ_TM, INPROJ_TN
    first_part = AB_ROW0 // tn

    def w_rows(i, j):
        return (SUBLANES * (j * (tn // SUBLANES) + jnp.where(j >= first_part, 2 * HEADS // SUBLANES, 0)), 0)

    return pl.pallas_call(
        _inproj_kernel,
        grid=(n // tm, MAIN_W // tn),
        in_specs=[
            pl.BlockSpec((tm, D_MODEL), lambda i, j: (i, 0)),
            pl.BlockSpec((pl.Element(tn), pl.Element(D_MODEL)), w_rows),
        ],
        out_specs=pl.BlockSpec((tm, tn), lambda i, j: (i, j)),
        out_shape=jax.ShapeDtypeStruct((n, MAIN_W), BF16),
        compiler_params=_cparams(("parallel", "arbitrary")),
        name="inproj",
    )(h, w_t)


GATES_TB = 256


def _split3(a):
    p0 = a.astype(BF16)
    r1 = a - p0.astype(F32)
    p1 = r1.astype(BF16)
    p2 = (r1 - p1.astype(F32)).astype(BF16)
    return p0, p1, p2


def _prenorm_gates_kernel(x_ref, sc_ref, sh_ref, g_ref, wab_ref, wf_ref, alog_ref, dt_ref, fb_ref,
                          h_ref, o_ref, ws_ref, carry_ref):
    tb = GATES_TB

    @pl.when(pl.program_id(1) == 0)
    def _():
        carry_ref[...] = jnp.zeros_like(carry_ref)
        gate_rows = jnp.concatenate([wab_ref[...], wf_ref[...]], axis=0)
        pad = jnp.zeros((LANES - gate_rows.shape[0], D_MODEL), F32)
        hi, lo = _split_bf16(jnp.concatenate([gate_rows, pad], axis=0))
        ws_ref[0:LANES, :] = hi
        ws_ref[LANES:, :] = lo

    hb = _modulated_norm(x_ref[...], g_ref[...], sc_ref[...], sh_ref[...]).astype(BF16)
    h_ref[...] = hb
    both = _dot_nt(hb, ws_ref[...])
    pre = both[:, :LANES] + both[:, LANES:]
    lane = lax.broadcasted_iota(jnp.int32, (tb, LANES), 1)
    g = -jnp.exp(alog_ref[...]) * _softplus(pre + dt_ref[...])
    beta = jax.nn.sigmoid(pre)
    lf = -_softplus(-(pre + fb_ref[...]))

    ri = lax.broadcasted_iota(jnp.int32, (tb, tb), 0)
    ci = lax.broadcasted_iota(jnp.int32, (tb, tb), 1)
    same_chunk = (ri // CHUNK) == (ci // CHUNK)
    tri = (ri >= ci)
    m_all = jnp.where(tri, 1.0, 0.0).astype(BF16)
    m_chunk = jnp.where(tri & same_chunk, 1.0, 0.0).astype(BF16)
    m_tot = jnp.where(same_chunk, 1.0, 0.0).astype(BF16)

    in_a = lane < HEADS
    parts = jnp.concatenate(_split3(jnp.where(in_a, g, lf)), axis=1)
    sums = _dot(jnp.concatenate([m_chunk, m_tot, m_all], axis=0), parts)
    sums = sums[:, :LANES] + (sums[:, LANES:2 * LANES] + sums[:, 2 * LANES:])
    gc = sums[:tb]
    glast = sums[tb:2 * tb]
    fcum = sums[2 * tb:] + carry_ref[...]
    carry_ref[...] = fcum[tb - 1:tb, :]

    egc = jnp.where(in_a, jnp.exp(gc), 0.0)
    ek = jnp.where(in_a, jnp.exp(glast - gc), 0.0)
    elast = jnp.where(in_a, jnp.exp(glast), 0.0)
    out = jnp.where(in_a, gc, jnp.where(lane < 2 * HEADS, beta, jnp.where(lane < 3 * HEADS, fcum, 0.0)))
    out = out + pltpu.roll(egc, L_EGC, 1) + pltpu.roll(ek, L_EK, 1) + pltpu.roll(elast, L_ELAST, 1)
    o_ref[...] = out


def _prenorm_gates(x, sc1, sh1, g, w_t, alog_row, dt_row, fb_row):
    bsz, seq, _ = x.shape
    tb = GATES_TB
    row = pl.BlockSpec((1, LANES), lambda b, t: (0, 0))
    modrow = pl.BlockSpec((None, 1, D_MODEL), lambda b, t: (b, 0, 0))
    return pl.pallas_call(
        _prenorm_gates_kernel,
        grid=(bsz, seq // tb),
        in_specs=[
            pl.BlockSpec((None, tb, D_MODEL), lambda b, t: (b, t, 0)),
            modrow, modrow,
            pl.BlockSpec((1, D_MODEL), lambda b, t: (0, 0)),
            pl.BlockSpec((pl.Element(2 * HEADS), pl.Element(D_MODEL)), lambda b, t: (AB_ROW0, 0)),
            pl.BlockSpec((pl.Element(HEADS), pl.Element(D_MODEL)), lambda b, t: (F_ROW0, 0)),
            row, row, row,
        ],
        out_specs=[
            pl.BlockSpec((None, tb, D_MODEL), lambda b, t: (b, t, 0)),
            pl.BlockSpec((None, tb, LANES), lambda b, t: (b, t, 0)),
        ],
        out_shape=[
            jax.ShapeDtypeStruct((bsz, seq, D_MODEL), BF16),
            jax.ShapeDtypeStruct((bsz, seq, LANES), F32),
        ],
        scratch_shapes=[pltpu.VMEM((2 * LANES, D_MODEL), BF16), pltpu.VMEM((1, LANES), F32)],
        compiler_params=_cparams(("parallel", "arbitrary")),
        name="prenorm_gates",
    )(x, sc1, sh1, g, w_t, w_t, alog_row, dt_row, fb_row)


DN_TB = 256
DN_GROUP = 4
HALO = SUBLANES
INV_BLOCK = CHUNK // 4


def _bdot(a, b):
    return lax.dot_general(a, b, (((2,), (1,)), ((0,), (0,))), preferred_element_type=F32)


def _bdot_nt(a, b):
    return lax.dot_general(a, b, (((2,), (2,)), ((0,), (0,))), preferred_element_type=F32)


def _inv_unit_lower(a, eye, blk16, blk32):
    n = jnp.where(blk16, -a, 0.0)
    e1 = jnp.where(blk32 & jnp.logical_not(blk16), a, 0.0).astype(BF16)
    e2 = jnp.where(blk32, 0.0, a).astype(BF16)
    t = eye + n
    p = n.astype(BF16)
    for _ in range(INV_BLOCK.bit_length() - 2):
        p = _bdot(p, p).astype(BF16)
        t = t + _bdot(t.astype(BF16), p)
    for e in (e1, e2):
        tb = t.astype(BF16)
        t = t - _bdot(_bdot(tb, e).astype(BF16), tb)
    return t


def _deltanet_kernel(q_ref, k_ref, v_ref, z_ref, wq_ref, wk_ref, wv_ref, slab_ref, gct_ref, og_ref,
                     o_ref, ext_ref, qn_ref, kn_ref, vv_ref, s_ref):
    tb = DN_TB

    @pl.when(pl.program_id(1) == 0)
    def _():
        ext_ref[:, 0:HALO, :] = jnp.zeros((3, HALO, HEAD_W), F32)
        s_ref[...] = jnp.zeros_like(s_ref)

    for idx, (u_ref, w_ref) in enumerate(((q_ref, wq_ref), (k_ref, wk_ref), (v_ref, wv_ref))):
        for h in range(HEADS):
            cols = slice(h * HEAD_DIM, (h + 1) * HEAD_DIM)
            ext_ref[idx, HALO:HALO + tb, cols] = u_ref[:, cols].astype(F32)
            y = None
            for j in range(CONV_K):
                start = HALO - (CONV_K - 1) + j
                term = ext_ref[idx, start:start + tb, cols] * w_ref[j:j + 1, cols]
                y = term if y is None else y + term
            y = _silu(y)
            if idx == 2:
                vv_ref[h] = y
            else:
                yn = y * lax.rsqrt(jnp.sum(y * y, axis=-1, keepdims=True) + EPS)
                if idx == 0:
                    qn_ref[h] = (yn * (HEAD_DIM ** -0.5)).astype(BF16)
                else:
                    kn_ref[h] = yn.astype(BF16)
        ext_ref[idx, 0:HALO, :] = ext_ref[idx, tb:tb + HALO, :]

    ri = lax.broadcasted_iota(jnp.int32, (DN_GROUP * HEADS, CHUNK, CHUNK), 1)
    ci = lax.broadcasted_iota(jnp.int32, (DN_GROUP * HEADS, CHUNK, CHUNK), 2)
    incl = ri >= ci
    strict = ri > ci
    eye = jnp.where(ri == ci, 1.0, 0.0)
    blk16 = (ri // INV_BLOCK) == (ci // INV_BLOCK)
    blk32 = (ri // (2 * INV_BLOCK)) == (ci // (2 * INV_BLOCK))

    def group_body(c, _):
        rows = [pl.ds(pl.multiple_of((c * DN_GROUP + g) * CHUNK, CHUNK), CHUNK) for g in range(DN_GROUP)]
        slabs = [slab_ref[r, :] for r in rows]

        def col(off, width):
            return jnp.stack([jnp.broadcast_to(sl[:, off + h:off + h + 1], (CHUNK, width))
                              for sl in slabs for h in range(HEADS)])

        def grouped(ref):
            return jnp.concatenate([ref[:, r, :] for r in rows], axis=0)

        q = grouped(qn_ref)
        k = grouped(kn_ref)
        v = grouped(vv_ref)
        beta = col(L_BETA, HEAD_DIM)
        egc = col(L_EGC, HEAD_DIM)
        gc_row = jnp.concatenate([gct_ref[c * DN_GROUP + g] for g in range(DN_GROUP)], axis=0)

        decay = jnp.where(incl, jnp.exp(col(L_GC, CHUNK) - gc_row), 0.0)
        kk = _bdot_nt(k, k)
        qk = (_bdot_nt(q, k) * decay).astype(BF16)
        a = jnp.where(strict, kk * decay * beta[:, :, :CHUNK], 0.0)
        t = _inv_unit_lower(a, eye, blk16, blk32).astype(BF16)

        kf = k.astype(F32)
        vb = (v * beta).astype(BF16)
        kbg = (kf * (beta * egc)).astype(BF16)
        u = _bdot(t, vb)
        w = _bdot(t, kbg).astype(BF16)
        qd = (q.astype(F32) * egc).astype(BF16)
        kd = kf * col(L_EK, HEAD_DIM)
        kdt = jnp.stack([kd[n].T for n in range(DN_GROUP * HEADS)]).astype(BF16)

        for g in range(DN_GROUP):
            sel = slice(g * HEADS, (g + 1) * HEADS)
            s = s_ref[...]
            sb = s.astype(BF16)
            vnb = (u[sel] - _bdot(w[sel], sb)).astype(BF16)
            o = _bdot(qd[sel], sb) + _bdot(qk[sel], vnb)
            elast = jnp.stack([jnp.broadcast_to(slabs[g][CHUNK - 1:CHUNK, L_ELAST + h:L_ELAST + h + 1],
                                                (HEAD_DIM, HEAD_DIM)) for h in range(HEADS)])
            s_ref[...] = s * elast + _bdot(kdt[sel], vnb)

            r = lax.rsqrt(jnp.mean(o * o, axis=-1, keepdims=True) + EPS)
            on = (o * r) * og_ref[...]
            for h in range(HEADS):
                cols = slice(h * HEAD_DIM, (h + 1) * HEAD_DIM)
                o_ref[rows[g], cols] = (on[h] * _silu(z_ref[rows[g], cols].astype(F32))).astype(BF16)
        return 0

    lax.fori_loop(0, tb // (CHUNK * DN_GROUP), group_body, 0)


def _deltanet(proj3, conv_w, slab, gct, onorm_g):
    bsz, seq, _ = proj3.shape
    tb = DN_TB
    nct = tb // CHUNK

    def colblk(j):
        return pl.BlockSpec((None, tb, HEAD_W), lambda b, t: (b, t, j))

    def wblk(j):
        return pl.BlockSpec((CONV_K, HEAD_W), lambda b, t: (0, j))

    return pl.pallas_call(
        _deltanet_kernel,
        grid=(bsz, seq // tb),
        in_specs=[
            colblk(0), colblk(1), colblk(2), colblk(3),
            wblk(0), wblk(1), wblk(2),
            pl.BlockSpec((None, tb, LANES), lambda b, t: (b, t, 0)),
            pl.BlockSpec((None, nct, HEADS, 1, CHUNK), lambda b, t: (b, t, 0, 0, 0)),
            pl.BlockSpec((1, HEAD_DIM), lambda b, t: (0, 0)),
        ],
        out_specs=pl.BlockSpec((None, tb, HEAD_W), lambda b, t: (b, t, 0)),
        out_shape=jax.ShapeDtypeStruct((bsz, seq, HEAD_W), BF16),
        scratch_shapes=[
            pltpu.VMEM((3, tb + HALO, HEAD_W), F32),
            pltpu.VMEM((HEADS, tb, HEAD_DIM), BF16),
            pltpu.VMEM((HEADS, tb, HEAD_DIM), BF16),
            pltpu.VMEM((HEADS, tb, HEAD_DIM), F32),
            pltpu.VMEM((HEADS, HEAD_DIM, HEAD_DIM), F32),
        ],
        compiler_params=_cparams(("parallel", "arbitrary")),
        name="deltanet",
    )(proj3, proj3, proj3, proj3, conv_w, conv_w, conv_w, slab, gct, onorm_g)


FOX_T = 512
FOX_STRIP = 32
NEG_INF = float("-inf")
LOG2E = 1.4426950408889634


def _fox_kernel(q_ref, k_ref, v_ref, f_ref, o_ref, s_a_ref, s_b_ref, p_a_ref, p_b_ref, m_ref, l_ref, alpha_ref,
                acc_ref):
    tq = FOX_T
    i = pl.program_id(2)
    qs = (q_ref[...].astype(F32) * (HEAD_DIM ** -0.5 * LOG2E)).astype(BF16)
    f0 = f_ref[i][:, 0:1]
    m_ref[...] = jnp.full_like(m_ref, NEG_INF)
    l_ref[...] = jnp.zeros_like(l_ref)
    acc_ref[...] = jnp.zeros_like(acc_ref)

    def keys(j):
        return pl.ds(pl.multiple_of(j * tq, tq), tq)

    def scores(j):
        return _dot_nt(qs, k_ref[keys(j), :])

    def softmax(j, s_view, p_view, masked):
        bias = (f0 - f_ref[j]) * LOG2E

        def width(r):
            if not masked:
                return tq
            return min(tq, ((r + 1) * FOX_STRIP + LANES - 1) // LANES * LANES)

        def logits(r):
            rows = slice(r * FOX_STRIP, (r + 1) * FOX_STRIP)
            w = width(r)
            s = s_view[rows, :w] + bias[:, :w]
            if masked:
                ri = r * FOX_STRIP + lax.broadcasted_iota(jnp.int32, (FOX_STRIP, w), 0)
                ci = lax.broadcasted_iota(jnp.int32, (FOX_STRIP, w), 1)
                s = jnp.where(ci <= ri, s, NEG_INF)
            return rows, s

        for r in range(tq // FOX_STRIP):
            rows, s = logits(r)
            m_old = m_ref[rows, :]
            m_new = jnp.maximum(m_old, jnp.max(s, axis=-1, keepdims=True))
            alpha_ref[rows, :] = jnp.exp2(m_old - m_new)
            m_ref[rows, :] = m_new
        for r in range(tq // FOX_STRIP):
            rows, s = logits(r)
            w = width(r)
            p = jnp.exp2(s - jnp.concatenate([m_ref[rows, :]] * (w // LANES), axis=1))
            alpha = alpha_ref[rows, :]
            l_ref[rows, :] = alpha * l_ref[rows, :] + jnp.sum(p, axis=-1, keepdims=True)
            p_view[rows, :w] = p.astype(BF16)
            if w < tq:
                p_view[rows, w:] = jnp.zeros((FOX_STRIP, tq - w), BF16)

    s_a_ref[...] = scores(i)
    s_b_ref[...] = scores(jnp.maximum(i - 1, 0))
    softmax(i, s_a_ref, p_a_ref, True)

    def step(t, s_cur, p_cur, s_nxt, p_prv):
        j = i - t
        s_nxt[...] = scores(jnp.maximum(j - 1, 0))
        pv = _dot(p_prv[...], v_ref[keys(j + 1), :])
        softmax(j, s_cur, p_cur, False)
        acc_ref[...] = alpha_ref[...] * (acc_ref[...] + pv)

    def pair(u):
        step(2 * u + 1, s_b_ref, p_b_ref, s_a_ref, p_a_ref)
        step(2 * u + 2, s_a_ref, p_a_ref, s_b_ref, p_b_ref)

    def quad(w, _):
        pair(2 * w)
        pair(2 * w + 1)
        return 0

    quads = lax.shift_right_logical(i, 2)
    lax.fori_loop(0, quads, quad, 0)
    pl.when((i & 2) != 0)(functools.partial(pair, 2 * quads))

    def finish(p_last):
        acc = acc_ref[...] + _dot(p_last[...], v_ref[keys(0), :])
        o_ref[...] = (acc / l_ref[...]).astype(BF16)

    @pl.when((i & 1) == 1)
    def _():
        step(i, s_b_ref, p_b_ref, s_a_ref, p_a_ref)
        finish(p_b_ref)

    @pl.when((i & 1) == 0)
    def _():
        finish(p_a_ref)


def _fox(proj3, f_rows):
    bsz, seq, _ = proj3.shape
    tq = FOX_T
    qb, kb, vb = 4 * HEADS, 5 * HEADS, 6 * HEADS
    return pl.pallas_call(
        _fox_kernel,
        grid=(bsz, HEADS, seq // tq),
        in_specs=[
            pl.BlockSpec((None, tq, HEAD_DIM), lambda b, h, i: (b, i, qb + h)),
            pl.BlockSpec((None, seq, HEAD_DIM), lambda b, h, i: (b, 0, kb + h)),
            pl.BlockSpec((None, seq, HEAD_DIM), lambda b, h, i: (b, 0, vb + h)),
            pl.BlockSpec((None, None, seq // tq, 1, tq), lambda b, h, i: (b, h, 0, 0, 0)),
        ],
        out_specs=pl.BlockSpec((None, tq, HEAD_DIM), lambda b, h, i: (b, i, h)),
        out_shape=jax.ShapeDtypeStruct((bsz, seq, HEAD_W), BF16),
        scratch_shapes=[
            pltpu.VMEM((tq, tq), F32),
            pltpu.VMEM((tq, tq), F32),
            pltpu.VMEM((tq, tq), BF16),
            pltpu.VMEM((tq, tq), BF16),
            pltpu.VMEM((tq, LANES), F32),
            pltpu.VMEM((tq, LANES), F32),
            pltpu.VMEM((tq, LANES), F32),
            pltpu.VMEM((tq, HEAD_DIM), F32),
        ],
        compiler_params=_cparams(("parallel", "parallel", "arbitrary")),
        name="fox",
    )(proj3, proj3, proj3, f_rows)


OUT_TM = 512
R_E0, R_E1, R_W0, R_W1 = 0, 1, 2, 3


def _first_argmax(vals, lane):
    m = jnp.max(vals, axis=-1, keepdims=True)
    idx = jnp.min(jnp.where(vals == m, lane, LANES), axis=-1, keepdims=True)
    return m, idx


def _outproj_kernel(oa_ref, ob_ref, wa_ref, wb_ref, x_ref, g1_ref, sc_ref, sh_ref, g_ref,
                    wr_ref, br_ref, x1_ref, h2_ref, r_ref, cnt_ref):
    mix = _dot(oa_ref[...], wa_ref[...]) + _dot(ob_ref[...], wb_ref[...])
    x1 = x_ref[...] + g1_ref[...] * mix
    x1_ref[...] = x1
    h2 = _modulated_norm(x1, g_ref[...], sc_ref[...], sh_ref[...])
    h2_ref[...] = h2

    logits = _dot3_pre(h2, wr_ref[...]) + br_ref[...]
    tm = logits.shape[0]
    lane = lax.broadcasted_iota(jnp.int32, (tm, LANES), 1)
    gl = jnp.where(lane < N_GROUPS, logits, NEG_INF)
    gmax, gidx = _first_argmax(gl, lane)
    pg = 1.0 / jnp.sum(jnp.exp(gl - gmax), axis=-1, keepdims=True)
    e_lane = lane - N_GROUPS
    in_grp = (e_lane >= gidx * EXPERTS_PER_GROUP) & (e_lane < (gidx + 1) * EXPERTS_PER_GROUP)
    el = jnp.where(in_grp, logits, NEG_INF)
    v0, i0 = _first_argmax(el, lane)
    v1, i1 = _first_argmax(jnp.where(lane == i0, NEG_INF, el), lane)
    ex = jnp.exp(v1 - v0)
    w0 = pg / (1.0 + ex)
    w1 = pg * ex / (1.0 + ex)
    e0 = (i0 - N_GROUPS).astype(F32)
    e1 = (i1 - N_GROUPS).astype(F32)
    r_ref[...] = jnp.where(lane == R_E0, e0, jnp.where(lane == R_E1, e1,
                           jnp.where(lane == R_W0, w0, jnp.where(lane == R_W1, w1, 0.0))))

    @pl.when(pl.program_id(0) == 0)
    def _():
        cnt_ref[...] = jnp.zeros_like(cnt_ref)

    picked = (lane == i0 - N_GROUPS) | (lane == i1 - N_GROUPS)
    cnt_ref[...] += jnp.sum(jnp.where(picked, 1.0, 0.0), axis=0, keepdims=True)


def _outproj(o_a, o_b, wa, wb, x2, g1, sc2, sh2, g, wr_hi_lo, br, seq):
    n = x2.shape[0]
    tm = OUT_TM
    per_b = seq // tm
    modrow = pl.BlockSpec((None, 1, D_MODEL), lambda i: (i // per_b, 0, 0))
    const = lambda shape: pl.BlockSpec(shape, lambda i: (0, 0))
    return pl.pallas_call(
        _outproj_kernel,
        grid=(n // tm,),
        in_specs=[
            pl.BlockSpec((tm, HEAD_W), lambda i: (i, 0)),
            pl.BlockSpec((tm, HEAD_W), lambda i: (i, 0)),
            const((HEAD_W, D_MODEL)), const((HEAD_W, D_MODEL)),
            pl.BlockSpec((tm, D_MODEL), lambda i: (i, 0)),
            modrow, modrow, modrow,
            const((1, D_MODEL)),
            const((D_MODEL, 2 * LANES)), const((1, LANES)),
        ],
        out_specs=[
            pl.BlockSpec((tm, D_MODEL), lambda i: (i, 0)),
            pl.BlockSpec((tm, D_MODEL), lambda i: (i, 0)),
            pl.BlockSpec((tm, LANES), lambda i: (i, 0)),
            pl.BlockSpec((1, LANES), lambda i: (0, 0)),
        ],
        out_shape=[
            jax.ShapeDtypeStruct((n, D_MODEL), F32),
            jax.ShapeDtypeStruct((n, D_MODEL), F32),
            jax.ShapeDtypeStruct((n, LANES), F32),
            jax.ShapeDtypeStruct((1, LANES), F32),
        ],
        compiler_params=_cparams(("arbitrary",)),
        name="outproj",
    )(o_a, o_b, wa, wb, x2, g1, sc2, sh2, g, wr_hi_lo, br)


MOE_TM = 256
ROUTE_TB = 512


def _route_kernel(r_ref, cnt_ref, pos_ref, run_ref, base_ref):
    tb = ROUTE_TB
    t = pl.program_id(0)
    r = r_ref[...]
    lane = lax.broadcasted_iota(jnp.int32, (tb, LANES), 1)
    e0 = r[:, R_E0:R_E0 + 1].astype(jnp.int32)
    e1 = r[:, R_E1:R_E1 + 1].astype(jnp.int32)
    oh0 = lane == e0
    oh1 = lane == e1
    both = jnp.where(oh0 | oh1, 1.0, 0.0)

    @pl.when(t == 0)
    def _():
        counts = cnt_ref[...]
        padded = jnp.ceil(counts / MOE_TM) * MOE_TM
        li = lax.broadcasted_iota(jnp.int32, (LANES, LANES), 0)
        lj = lax.broadcasted_iota(jnp.int32, (LANES, LANES), 1)
        upper = jnp.where(li < lj, 1.0, 0.0).astype(BF16)
        hi = jnp.floor(padded / 256.0)
        lo = padded - hi * 256.0
        hi8 = jnp.broadcast_to(hi, (SUBLANES, LANES)).astype(BF16)
        lo8 = jnp.broadcast_to(lo, (SUBLANES, LANES)).astype(BF16)
        base = _dot(hi8, upper) * 256.0 + _dot(lo8, upper)
        base_ref[...] = base[0:1, :]
        run_ref[...] = jnp.zeros_like(run_ref)

    ri = lax.broadcasted_iota(jnp.int32, (tb, tb), 0)
    ci = lax.broadcasted_iota(jnp.int32, (tb, tb), 1)
    strict = jnp.where(ri > ci, 1.0, 0.0).astype(BF16)
    before = _dot(strict, both.astype(BF16)) + run_ref[...] + base_ref[...]
    p0 = jnp.sum(jnp.where(oh0, before, 0.0), axis=-1, keepdims=True)
    p1 = jnp.sum(jnp.where(oh1, before, 0.0), axis=-1, keepdims=True)
    pos_ref[...] = jnp.where(lane == 0, p0, jnp.where(lane == 1, p1, 0.0)).astype(jnp.int32)
    run_ref[...] = run_ref[...] + jnp.sum(both, axis=0, keepdims=True)


def _route(rslab, counts):
    n = rslab.shape[0]
    tb = ROUTE_TB
    return pl.pallas_call(
        _route_kernel,
        grid=(n // tb,),
        in_specs=[pl.BlockSpec((tb, LANES), lambda t: (t, 0)), pl.BlockSpec((1, LANES), lambda t: (0, 0))],
        out_specs=pl.BlockSpec((tb, LANES), lambda t: (t, 0)),
        out_shape=jax.ShapeDtypeStruct((n, LANES), jnp.int32),
        scratch_shapes=[pltpu.VMEM((1, LANES), F32), pltpu.VMEM((1, LANES), F32)],
        compiler_params=_cparams(("arbitrary",)),
        name="route",
    )(rslab, counts)


DISP_TB = 1024


def _dispatch_kernel(pad_start_ref, pad_len_ref, used_ref, pos_ref, h2_ref, xs_ref, zero_ref, sem, zsem):
    t = pl.program_id(0)
    tb = DISP_TB
    n_tiles = xs_ref.shape[0] // MOE_TM

    def row_copy(g, r, k):
        dst_row = pos_ref[0, 2 * SUBLANES * g + 2 * r + k]
        return pltpu.make_async_copy(h2_ref.at[g, pl.ds(r, 1)], xs_ref.at[pl.ds(dst_row, 1)], sem)

    def issue(g, _):
        for r in range(SUBLANES):
            row_copy(g, r, 0).start(priority=0)
            row_copy(g, r, 1).start(priority=1)
        return 0

    lax.fori_loop(0, tb // SUBLANES, issue, 0)

    @pl.when(t == 0)
    def _():
        zero_ref[...] = jnp.zeros_like(zero_ref)

        def zero_rows(wait, off, rows):
            cp = pltpu.make_async_copy(zero_ref.at[pl.ds(0, rows)], xs_ref.at[pl.ds(off, rows)], zsem)
            cp.wait() if wait else cp.start()

        def per_expert(wait, e, _):
            start = pad_start_ref[e]
            head = (-start) & (SUBLANES - 1)
            for r in range(SUBLANES - 1):
                pl.when(r < head)(functools.partial(zero_rows, wait, start + r, 1))
            off = start + head
            rest = pad_len_ref[e] - head
            piece = MOE_TM // 2
            while piece >= SUBLANES:
                take = (rest & piece) != 0
                pl.when(take)(functools.partial(zero_rows, wait, pl.multiple_of(off, SUBLANES), piece))
                off = off + jnp.where(take, piece, 0)
                piece //= 2
            return 0

        def per_tile(wait, i, _):
            zero_rows(wait, pl.multiple_of(i * MOE_TM, MOE_TM), MOE_TM)
            return 0

        for wait in (False, True):
            lax.fori_loop(0, N_EXPERTS, functools.partial(per_expert, wait), 0)
            lax.fori_loop(used_ref[0], n_tiles, functools.partial(per_tile, wait), 0)

    def drain(g, _):
        for r in range(SUBLANES):
            row_copy(g, r, 0).wait()
            row_copy(g, r, 1).wait()
        return 0

    lax.fori_loop(0, tb // SUBLANES, drain, 0)


def _dispatch(pad_start, pad_len, used, pos2, h2, p_rows):
    n = h2.shape[0]
    tb = DISP_TB
    return pl.pallas_call(
        _dispatch_kernel,
        grid_spec=pltpu.PrefetchScalarGridSpec(
            num_scalar_prefetch=3,
            grid=(n // tb,),
            in_specs=[
                pl.BlockSpec((None, 1, 2 * tb), lambda t, *_: (t, 0, 0), memory_space=pltpu.SMEM),
                pl.BlockSpec((tb // SUBLANES, SUBLANES, D_MODEL), lambda t, *_: (t, 0, 0)),
            ],
            out_specs=pl.BlockSpec(memory_space=pl.ANY),
            scratch_shapes=[
                pltpu.VMEM((MOE_TM, D_MODEL), F32),
                pltpu.SemaphoreType.DMA(()),
                pltpu.SemaphoreType.DMA(()),
            ],
        ),
        out_shape=jax.ShapeDtypeStruct((p_rows, D_MODEL), F32),
        compiler_params=_cparams(("arbitrary",)),
        name="dispatch",
    )(pad_start, pad_len, used, pos2, h2.reshape(n // SUBLANES, SUBLANES, D_MODEL))


def _experts_kernel(te_ref, tv_ref, tf_ref, ts_ref, tn_ref, xs_ref, w1_hbm, w3_hbm, w2_hbm, ys_ref,
                    w1f_ref, w3f_ref, w2f_ref, w1b_ref, w3b_ref, w2b_ref, sem):
    i = pl.program_id(0)

    def weight_copies(e, slot):
        return [pltpu.make_async_copy(src.at[e], dst.at[slot], sem.at[slot])
                for src, dst in ((w1_hbm, w1f_ref), (w3_hbm, w3f_ref), (w2_hbm, w2f_ref))]

    @pl.when(tv_ref[i] != 0)
    def _():
        @pl.when(tf_ref[i] != 0)
        def _():
            slot = ts_ref[i]

            @pl.when(i == 0)
            def _():
                for cp in weight_copies(te_ref[i], slot):
                    cp.start()

            for cp in weight_copies(te_ref[i], slot):
                cp.wait()

            @pl.when(tn_ref[i] >= 0)
            def _():
                for cp in weight_copies(tn_ref[i], 1 - slot):
                    cp.start()

            w1b_ref[...] = w1f_ref[slot].astype(BF16)
            w3b_ref[...] = w3f_ref[slot].astype(BF16)
            w2b_ref[...] = w2f_ref[slot].astype(BF16)

        x = xs_ref[...].astype(BF16)
        a = _dot(x, w1b_ref[...])
        b = _dot(x, w3b_ref[...])
        ys_ref[...] = _dot((_silu(a) * b).astype(BF16), w2b_ref[...])

    @pl.when(tv_ref[i] == 0)
    def _():
        ys_ref[...] = jnp.zeros_like(ys_ref)


def _experts(tile_expert, tile_valid, tile_first, tile_slot, tile_next, xs, w1, w3, w2):
    p_rows = xs.shape[0]
    tm = MOE_TM
    hbm = pl.BlockSpec(memory_space=pl.ANY)
    return pl.pallas_call(
        _experts_kernel,
        grid_spec=pltpu.PrefetchScalarGridSpec(
            num_scalar_prefetch=5,
            grid=(p_rows // tm,),
            in_specs=[pl.BlockSpec((tm, D_MODEL), lambda i, te, tv, *_: (jnp.where(tv[i] != 0, i, 0), 0)),
                      hbm, hbm, hbm],
            out_specs=pl.BlockSpec((tm, D_MODEL), lambda i, *_: (i, 0)),
            scratch_shapes=[
                pltpu.VMEM((2, D_MODEL, D_EXPERT), F32),
                pltpu.VMEM((2, D_MODEL, D_EXPERT), F32),
                pltpu.VMEM((2, D_EXPERT, D_MODEL), F32),
                pltpu.VMEM((D_MODEL, D_EXPERT), BF16),
                pltpu.VMEM((D_MODEL, D_EXPERT), BF16),
                pltpu.VMEM((D_EXPERT, D_MODEL), BF16),
                pltpu.SemaphoreType.DMA((2,)),
            ],
        ),
        out_shape=jax.ShapeDtypeStruct((p_rows, D_MODEL), F32),
        compiler_params=_cparams(("arbitrary",)),
        name="experts",
    )(tile_expert, tile_valid, tile_first, tile_slot, tile_next, xs, w1, w3, w2)


COMB_TB = 256
COMB_ROWS = 32


def _combine_kernel(pos_ref, pos_next_ref, ys_ref, r_ref, x1_ref, g2_ref, fg_ref, o_ref, buf_ref, sem, *, final):
    tb = COMB_TB
    t = pl.program_id(0)
    slot = t & 1

    def row_copy(p_ref, s, g, r, k):
        src_row = p_ref[0, 2 * SUBLANES * g + 2 * r + k]
        return pltpu.make_async_copy(ys_ref.at[pl.ds(src_row, 1)], buf_ref.at[s, k, g, pl.ds(r, 1)], sem.at[s])

    def issue_group(p_ref, s, g):
        for r in range(SUBLANES):
            row_copy(p_ref, s, g, r, 0).start(priority=0)
            row_copy(p_ref, s, g, r, 1).start(priority=1)

    def drain(g, _):
        for r in range(SUBLANES):
            row_copy(pos_ref, slot, g, r, 0).wait()
            row_copy(pos_ref, slot, g, r, 1).wait()
        return 0

    def combine_rows(q):
        rows = pl.ds(pl.multiple_of(q * COMB_ROWS, COMB_ROWS), COMB_ROWS)
        groups = pl.ds(q * (COMB_ROWS // SUBLANES), COMB_ROWS // SUBLANES)
        r = r_ref[rows, :]
        y0 = buf_ref[slot, 0, groups].reshape(COMB_ROWS, D_MODEL)
        y1 = buf_ref[slot, 1, groups].reshape(COMB_ROWS, D_MODEL)
        x2 = x1_ref[rows, :] + g2_ref[...] * (r[:, R_W0:R_W0 + 1] * y0 + r[:, R_W1:R_W1 + 1] * y1)
        if final:
            x2 = (x2 * lax.rsqrt(jnp.mean(x2 * x2, axis=-1, keepdims=True) + EPS)) * fg_ref[...]
        o_ref[rows, :] = x2

    @pl.when(t == 0)
    def _():
        lax.fori_loop(0, tb // SUBLANES, lambda g, c: (issue_group(pos_ref, 0, g), c)[1], 0)

    lax.fori_loop(0, tb // SUBLANES, drain, 0)

    @pl.when(t + 1 < pl.num_programs(0))
    def _():
        def body(q, c):
            combine_rows(q)
            for gg in range(COMB_ROWS // SUBLANES):
                issue_group(pos_next_ref, 1 - slot, q * (COMB_ROWS // SUBLANES) + gg)
            return c

        lax.fori_loop(0, tb // COMB_ROWS, body, 0)

    @pl.when(t + 1 == pl.num_programs(0))
    def _():
        lax.fori_loop(0, tb // COMB_ROWS, lambda q, c: (combine_rows(q), c)[1], 0)


def _combine(pos2, ys, rslab, x1, g2, final_g, seq, final):
    n = x1.shape[0]
    tb = COMB_TB
    per_b = seq // tb
    return pl.pallas_call(
        functools.partial(_combine_kernel, final=final),
        grid=(n // tb,),
        in_specs=[
            pl.BlockSpec((None, 1, 2 * tb), lambda i: (i, 0, 0), memory_space=pltpu.SMEM),
            pl.BlockSpec((None, 1, 2 * tb), lambda i: (jnp.minimum(i + 1, n // tb - 1), 0, 0),
                         memory_space=pltpu.SMEM),
            pl.BlockSpec(memory_space=pl.ANY),
            pl.BlockSpec((tb, LANES), lambda i: (i, 0)),
            pl.BlockSpec((tb, D_MODEL), lambda i: (i, 0)),
            pl.BlockSpec((None, 1, D_MODEL), lambda i: (i // per_b, 0, 0)),
            pl.BlockSpec((1, D_MODEL), lambda i: (0, 0)),
        ],
        out_specs=pl.BlockSpec((tb, D_MODEL), lambda i: (i, 0)),
        out_shape=jax.ShapeDtypeStruct((n, D_MODEL), F32),
        scratch_shapes=[pltpu.VMEM((2, 2, tb // SUBLANES, SUBLANES, D_MODEL), F32), pltpu.SemaphoreType.DMA((2,))],
        compiler_params=_cparams(("arbitrary",)),
        name="combine",
    )(pos2, pos2, ys, rslab, x1, g2, final_g)


def _layer(x, c, w_ada, b_ada, norm1_g, w_in, conv_w, a_log, dt_bias, dn_onorm_g, fox_f_bias,
           w_out, norm2_g, w_rg, b_rg, w_re, b_re, w1, w3, w2, final_g, final):
    bsz, seq, d = x.shape
    n = bsz * seq
    x2 = x.reshape(n, d)

    mod = _adaln(c, w_ada, b_ada)
    sh1, sc1, g1, sh2, sc2, g2 = [m.reshape(bsz, 1, d) for m in jnp.split(mod, 6, axis=-1)]

    w_t = w_in.T

    def lane_row(vals, off):
        return jnp.zeros((1, LANES), F32).at[0, off:off + HEADS].set(vals)

    h, slab = _prenorm_gates(x, sc1, sh1, norm1_g.reshape(1, d), w_t, lane_row(a_log, 0), lane_row(dt_bias, 0),
                             lane_row(fox_f_bias, L_F))
    proj3 = _inproj(h.reshape(n, d), w_t).reshape(bsz, seq, MAIN_W)
    nc = seq // CHUNK
    gct = slab[:, :, L_GC:L_GC + HEADS].reshape(bsz, nc, CHUNK, HEADS).transpose(0, 1, 3, 2)
    gct = gct.reshape(bsz, nc, HEADS, 1, CHUNK)
    f_rows = slab[:, :, L_F:L_F + HEADS].transpose(0, 2, 1).reshape(bsz, HEADS, seq // FOX_T, 1, FOX_T)

    o_dn = _deltanet(proj3, conv_w, slab, gct, dn_onorm_g.reshape(1, HEAD_DIM))
    o_fx = _fox(proj3, f_rows)

    wr = jnp.zeros((d, LANES), F32).at[:, :N_GROUPS].set(w_rg).at[:, N_GROUPS:N_GROUPS + N_EXPERTS].set(w_re)
    br = jnp.zeros((1, LANES), F32).at[0, :N_GROUPS].set(b_rg).at[0, N_GROUPS:N_GROUPS + N_EXPERTS].set(b_re)
    wr_hi_lo = jnp.concatenate(_split_bf16(wr), axis=1)
    w_out_b = w_out.astype(BF16)
    x1, h2, rslab, counts = _outproj(o_dn.reshape(n, HEAD_W), o_fx.reshape(n, HEAD_W), w_out_b[:HEAD_W],
                                     w_out_b[HEAD_W:], x2, g1, sc2, sh2, norm2_g.reshape(1, d), wr_hi_lo, br, seq)

    pos = _route(rslab, counts)[:, 0:2]

    cnt = counts[0, :N_EXPERTS].astype(jnp.int32)
    tiles_per = (cnt + MOE_TM - 1) // MOE_TM
    tile_end = jnp.cumsum(tiles_per)
    base = (tile_end - tiles_per) * MOE_TM
    n_tiles = (2 * n) // MOE_TM + N_EXPERTS
    p_rows = n_tiles * MOE_TM
    tid = jnp.arange(n_tiles, dtype=jnp.int32)
    tile_valid = (tid < tile_end[-1]).astype(jnp.int32)
    te_raw = jnp.minimum(jnp.sum(tid[:, None] >= tile_end[None, :], axis=1), N_EXPERTS - 1).astype(jnp.int32)
    last_e = te_raw[jnp.maximum(tile_end[-1] - 1, 0)]
    tile_expert = jnp.where(tile_valid == 1, te_raw, last_e)
    tile_first = (jnp.concatenate([jnp.array([-1], jnp.int32), tile_expert[:-1]]) != tile_expert).astype(jnp.int32)
    pad_start = base + cnt
    pad_len = tiles_per * MOE_TM - cnt
    eid = jnp.arange(N_EXPERTS, dtype=jnp.int32)
    has = tiles_per > 0
    slot_e = (jnp.cumsum(has.astype(jnp.int32)) - 1) & 1
    later = jnp.where(has[None, :] & (eid[None, :] > eid[:, None]), eid[None, :], N_EXPERTS)
    next_e = jnp.min(later, axis=1)
    next_e = jnp.where(next_e < N_EXPERTS, next_e, -1).astype(jnp.int32)
    of_tile = tile_expert[:, None] == eid[None, :]
    tile_slot = jnp.sum(jnp.where(of_tile, slot_e[None, :], 0), axis=1).astype(jnp.int32)
    tile_next = jnp.sum(jnp.where(of_tile, next_e[None, :], 0), axis=1).astype(jnp.int32)

    xs = _dispatch(pad_start, pad_len, tile_end[-1:], pos.reshape(n // DISP_TB, 1, 2 * DISP_TB), h2, p_rows)
    ys = _experts(tile_expert, tile_valid, tile_first, tile_slot, tile_next, xs, w1.reshape(N_EXPERTS, d, D_EXPERT),
                  w3.reshape(N_EXPERTS, d, D_EXPERT), w2.reshape(N_EXPERTS, D_EXPERT, d))
    out = _combine(pos.reshape(n // COMB_TB, 1, 2 * COMB_TB), ys, rslab, x1, g2, final_g.reshape(1, d), seq, final)
    return out.reshape(bsz, seq, d)


def kernel(x, c, w_ada, b_ada, norm1_g, w_in, conv_w, a_log, dt_bias, dn_onorm_g, fox_f_bias, w_out, norm2_g,
           w_router_group, b_router_group, w_router_expert, b_router_expert, w1, w3, w2, final_g):
    depth = w_ada.shape[0]
    for l in range(depth):
        x = _layer(x, c, w_ada[l], b_ada[l], norm1_g[l], w_in[l], conv_w[l], a_log[l], dt_bias[l], dn_onorm_g[l],
                   fox_f_bias[l], w_out[l], norm2_g[l], w_router_group[l], b_router_group[l], w_router_expert[l],
                   b_router_expert[l], w1[l], w3[l], w2[l], final_g, l == depth - 1)
    return x
```

```python
import functools

import jax
import jax.numpy as jnp
from jax import lax
from jax.experimental import pallas as pl
from jax.experimental.pallas import tpu as pltpu

F32 = jnp.float32
BF16 = jnp.bfloat16

D_MODEL = 2048
EPS = 1e-6
CHUNK = 64
HEADS = 8
HEAD_DIM = 128
HEAD_W = HEADS * HEAD_DIM
CONV_K = 4
N_GROUPS = 4
EXPERTS_PER_GROUP = 8
N_EXPERTS = N_GROUPS * EXPERTS_PER_GROUP
D_EXPERT = 512
LANES = 128
SUBLANES = 8
MAIN_W = 7 * HEAD_W
VMEM_LIMIT = 56 * 1024 * 1024

L_GC, L_BETA, L_F, L_EGC, L_EK, L_ELAST = 0, 8, 16, 24, 32, 40


def _cparams(sem):
    return pltpu.CompilerParams(dimension_semantics=sem, vmem_limit_bytes=VMEM_LIMIT)


def _split_bf16(a):
    hi = a.astype(BF16)
    lo = (a - hi.astype(F32)).astype(BF16)
    return hi, lo


def _dot(a, b):
    return jnp.dot(a, b, preferred_element_type=F32)


def _dot_nt(a, b):
    return lax.dot_general(a, b, (((1,), (1,)), ((), ())), preferred_element_type=F32)


def _dot3_pre(a, b_hi_lo):
    ah, al = _split_bf16(a)
    n = b_hi_lo.shape[1] // 2
    both = _dot(ah, b_hi_lo)
    return both[:, :n] + (both[:, n:] + _dot(al, b_hi_lo[:, :n]))


def _softplus(x):
    return jnp.maximum(x, 0.0) + jnp.log1p(jnp.exp(-jnp.abs(x)))


def _silu(x):
    return x * jax.nn.sigmoid(x)


ADALN_TN = 1024

def _adaln_kernel(c_ref, w_ref, b_ref, o_ref):
    c = c_ref[...]
    o_ref[...] = _dot(_silu(c).astype(BF16), w_ref[...].astype(BF16)) + b_ref[...]


def _adaln(c, w, b):
    bsz = c.shape[0]
    n = w.shape[1]
    tn = ADALN_TN
    cp = jnp.zeros((SUBLANES, D_MODEL), F32).at[:bsz].set(c)
    out = pl.pallas_call(
        _adaln_kernel,
        grid=(n // tn,),
        in_specs=[
            pl.BlockSpec((SUBLANES, D_MODEL), lambda j: (0, 0)),
            pl.BlockSpec((D_MODEL, tn), lambda j: (0, j)),
            pl.BlockSpec((1, tn), lambda j: (0, j)),
        ],
        out_specs=pl.BlockSpec((SUBLANES, tn), lambda j: (0, j)),
        out_shape=jax.ShapeDtypeStruct((SUBLANES, n), F32),
        compiler_params=_cparams(("parallel",)),
        name="adaln",
    )(cp, w, b.reshape(1, n))
    return out[:bsz]


INPROJ_TM = 2048
INPROJ_TN = 1024
AB_ROW0 = 4 * HEAD_W
F_ROW0 = MAIN_W + 2 * HEADS


def _modulated_norm(x, g, sc, sh):
    r = lax.rsqrt(jnp.mean(x * x, axis=-1, keepdims=True) + EPS)
    return (x * r) * (g * (1.0 + sc)) + sh


def _inproj_kernel(h_ref, w_ref, o_ref):
    o_ref[...] = _dot_nt(h_ref[...], w_ref[...].astype(BF16)).astype(BF16)


def _inproj(h, w_t):
    n = h.shape[0]
    tm, tn = INPROJ_TM, INPROJ_TN
    first_part = AB_ROW0 // tn

    def w_rows(i, j):
        return (SUBLANES * (j * (tn // SUBLANES) + jnp.where(j >= first_part, 2 * HEADS // SUBLANES, 0)), 0)

    return pl.pallas_call(
        _inproj_kernel,
        grid=(n // tm, MAIN_W // tn),
        in_specs=[
            pl.BlockSpec((tm, D_MODEL), lambda i, j: (i, 0)),
            pl.BlockSpec((pl.Element(tn), pl.Element(D_MODEL)), w_rows),
        ],
        out_specs=pl.BlockSpec((tm, tn), lambda i, j: (i, j)),
        out_shape=jax.ShapeDtypeStruct((n, MAIN_W), BF16),
        compiler_params=_cparams(("parallel", "arbitrary")),
        name="inproj",
    )(h, w_t)


GATES_TB = 256


def _split3(a):
    p0 = a.astype(BF16)
    r1 = a - p0.astype(F32)
    p1 = r1.astype(BF16)
    p2 = (r1 - p1.astype(F32)).astype(BF16)
    return p0, p1, p2


def _prenorm_gates_kernel(x_ref, sc_ref, sh_ref, g_ref, wab_ref, wf_ref, alog_ref, dt_ref, fb_ref,
                          h_ref, o_ref, ws_ref, carry_ref):
    tb = GATES_TB

    @pl.when(pl.program_id(1) == 0)
    def _():
        carry_ref[...] = jnp.zeros_like(carry_ref)
        gate_rows = jnp.concatenate([wab_ref[...], wf_ref[...]], axis=0)
        pad = jnp.zeros((LANES - gate_rows.shape[0], D_MODEL), F32)
        hi, lo = _split_bf16(jnp.concatenate([gate_rows, pad], axis=0))
        ws_ref[0:LANES, :] = hi
        ws_ref[LANES:, :] = lo

    hb = _modulated_norm(x_ref[...], g_ref[...], sc_ref[...], sh_ref[...]).astype(BF16)
    h_ref[...] = hb
    both = _dot_nt(hb, ws_ref[...])
    pre = both[:, :LANES] + both[:, LANES:]
    lane = lax.broadcasted_iota(jnp.int32, (tb, LANES), 1)
    g = -jnp.exp(alog_ref[...]) * _softplus(pre + dt_ref[...])
    beta = jax.nn.sigmoid(pre)
    lf = -_softplus(-(pre + fb_ref[...]))

    ri = lax.broadcasted_iota(jnp.int32, (tb, tb), 0)
    ci = lax.broadcasted_iota(jnp.int32, (tb, tb), 1)
    same_chunk = (ri // CHUNK) == (ci // CHUNK)
    tri = (ri >= ci)
    m_all = jnp.where(tri, 1.0, 0.0).astype(BF16)
    m_chunk = jnp.where(tri & same_chunk, 1.0, 0.0).astype(BF16)
    m_tot = jnp.where(same_chunk, 1.0, 0.0).astype(BF16)

    in_a = lane < HEADS
    parts = jnp.concatenate(_split3(jnp.where(in_a, g, lf)), axis=1)
    sums = _dot(jnp.concatenate([m_chunk, m_tot, m_all], axis=0), parts)
    sums = sums[:, :LANES] + (sums[:, LANES:2 * LANES] + sums[:, 2 * LANES:])
    gc = sums[:tb]
    glast = sums[tb:2 * tb]
    fcum = sums[2 * tb:] + carry_ref[...]
    carry_ref[...] = fcum[tb - 1:tb, :]

    egc = jnp.where(in_a, jnp.exp(gc), 0.0)
    ek = jnp.where(in_a, jnp.exp(glast - gc), 0.0)
    elast = jnp.where(in_a, jnp.exp(glast), 0.0)
    out = jnp.where(in_a, gc, jnp.where(lane < 2 * HEADS, beta, jnp.where(lane < 3 * HEADS, fcum, 0.0)))
    out = out + pltpu.roll(egc, L_EGC, 1) + pltpu.roll(ek, L_EK, 1) + pltpu.roll(elast, L_ELAST, 1)
    o_ref[...] = out


def _prenorm_gates(x, sc1, sh1, g, w_t, alog_row, dt_row, fb_row):
    bsz, seq, _ = x.shape
    tb = GATES_TB
    row = pl.BlockSpec((1, LANES), lambda b, t: (0, 0))
    modrow = pl.BlockSpec((None, 1, D_MODEL), lambda b, t: (b, 0, 0))
    return pl.pallas_call(
        _prenorm_gates_kernel,
        grid=(bsz, seq // tb),
        in_specs=[
            pl.BlockSpec((None, tb, D_MODEL), lambda b, t: (b, t, 0)),
            modrow, modrow,
            pl.BlockSpec((1, D_MODEL), lambda b, t: (0, 0)),
            pl.BlockSpec((pl.Element(2 * HEADS), pl.Element(D_MODEL)), lambda b, t: (AB_ROW0, 0)),
            pl.BlockSpec((pl.Element(HEADS), pl.Element(D_MODEL)), lambda b, t: (F_ROW0, 0)),
            row, row, row,
        ],
        out_specs=[
            pl.BlockSpec((None, tb, D_MODEL), lambda b, t: (b, t, 0)),
            pl.BlockSpec((None, tb, LANES), lambda b, t: (b, t, 0)),
        ],
        out_shape=[
            jax.ShapeDtypeStruct((bsz, seq, D_MODEL), BF16),
            jax.ShapeDtypeStruct((bsz, seq, LANES), F32),
        ],
        scratch_shapes=[pltpu.VMEM((2 * LANES, D_MODEL), BF16), pltpu.VMEM((1, LANES), F32)],
        compiler_params=_cparams(("parallel", "arbitrary")),
        name="prenorm_gates",
    )(x, sc1, sh1, g, w_t, w_t, alog_row, dt_row, fb_row)


DN_TB = 256
DN_GROUP = 4
HALO = SUBLANES
INV_BLOCK = CHUNK // 4


def _bdot(a, b):
    return lax.dot_general(a, b, (((2,), (1,)), ((0,), (0,))), preferred_element_type=F32)


def _bdot_nt(a, b):
    return lax.dot_general(a, b, (((2,), (2,)), ((0,), (0,))), preferred_element_type=F32)


def _inv_unit_lower(a, eye, blk16, blk32):
    n = jnp.where(blk16, -a, 0.0)
    e1 = jnp.where(blk32 & jnp.logical_not(blk16), a, 0.0).astype(BF16)
    e2 = jnp.where(blk32, 0.0, a).astype(BF16)
    t = eye + n
    p = n.astype(BF16)
    for _ in range(INV_BLOCK.bit_length() - 2):
        p = _bdot(p, p).astype(BF16)
        t = t + _bdot(t.astype(BF16), p)
    for e in (e1, e2):
        tb = t.astype(BF16)
        t = t - _bdot(_bdot(tb, e).astype(BF16), tb)
    return t


def _deltanet_kernel(q_ref, k_ref, v_ref, z_ref, wq_ref, wk_ref, wv_ref, slab_ref, gct_ref, og_ref,
                     o_ref, ext_ref, qn_ref, kn_ref, vv_ref, s_ref):
    tb = DN_TB

    @pl.when(pl.program_id(1) == 0)
    def _():
        ext_ref[:, 0:HALO, :] = jnp.zeros((3, HALO, HEAD_W), F32)
        s_ref[...] = jnp.zeros_like(s_ref)

    for idx, (u_ref, w_ref) in enumerate(((q_ref, wq_ref), (k_ref, wk_ref), (v_ref, wv_ref))):
        for h in range(HEADS):
            cols = slice(h * HEAD_DIM, (h + 1) * HEAD_DIM)
            ext_ref[idx, HALO:HALO + tb, cols] = u_ref[:, cols].astype(F32)
            y = None
            for j in range(CONV_K):
                start = HALO - (CONV_K - 1) + j
                term = ext_ref[idx, start:start + tb, cols] * w_ref[j:j + 1, cols]
                y = term if y is None else y + term
            y = _silu(y)
            if idx == 2:
                vv_ref[h] = y
            else:
                yn = y * lax.rsqrt(jnp.sum(y * y, axis=-1, keepdims=True) + EPS)
                if idx == 0:
                    qn_ref[h] = (yn * (HEAD_DIM ** -0.5)).astype(BF16)
                else:
                    kn_ref[h] = yn.astype(BF16)
        ext_ref[idx, 0:HALO, :] = ext_ref[idx, tb:tb + HALO, :]

    ri = lax.broadcasted_iota(jnp.int32, (DN_GROUP * HEADS, CHUNK, CHUNK), 1)
    ci = lax.broadcasted_iota(jnp.int32, (DN_GROUP * HEADS, CHUNK, CHUNK), 2)
    incl = ri >= ci
    strict = ri > ci
    eye = jnp.where(ri == ci, 1.0, 0.0)
    blk16 = (ri // INV_BLOCK) == (ci // INV_BLOCK)
    blk32 = (ri // (2 * INV_BLOCK)) == (ci // (2 * INV_BLOCK))

    def group_body(c, _):
        rows = [pl.ds(pl.multiple_of((c * DN_GROUP + g) * CHUNK, CHUNK), CHUNK) for g in range(DN_GROUP)]
        slabs = [slab_ref[r, :] for r in rows]

        def col(off, width):
            return jnp.stack([jnp.broadcast_to(sl[:, off + h:off + h + 1], (CHUNK, width))
                              for sl in slabs for h in range(HEADS)])

        def grouped(ref):
            return jnp.concatenate([ref[:, r, :] for r in rows], axis=0)

        q = grouped(qn_ref)
        k = grouped(kn_ref)
        v = grouped(vv_ref)
        beta = col(L_BETA, HEAD_DIM)
        egc = col(L_EGC, HEAD_DIM)
        gc_row = jnp.concatenate([gct_ref[c * DN_GROUP + g] for g in range(DN_GROUP)], axis=0)

        decay = jnp.where(incl, jnp.exp(col(L_GC, CHUNK) - gc_row), 0.0)
        kk = _bdot_nt(k, k)
        qk = (_bdot_nt(q, k) * decay).astype(BF16)
        a = jnp.where(strict, kk * decay * beta[:, :, :CHUNK], 0.0)
        t = _inv_unit_lower(a, eye, blk16, blk32).astype(BF16)

        kf = k.astype(F32)
        vb = (v * beta).astype(BF16)
        kbg = (kf * (beta * egc)).astype(BF16)
        u = _bdot(t, vb)
        w = _bdot(t, kbg).astype(BF16)
        qd = (q.astype(F32) * egc).astype(BF16)
        kd = kf * col(L_EK, HEAD_DIM)
        kdt = jnp.stack([kd[n].T for n in range(DN_GROUP * HEADS)]).astype(BF16)

        for g in range(DN_GROUP):
            sel = slice(g * HEADS, (g + 1) * HEADS)
            s = s_ref[...]
            sb = s.astype(BF16)
            vnb = (u[sel] - _bdot(w[sel], sb)).astype(BF16)
            o = _bdot(qd[sel], sb) + _bdot(qk[sel], vnb)
            elast = jnp.stack([jnp.broadcast_to(slabs[g][CHUNK - 1:CHUNK, L_ELAST + h:L_ELAST + h + 1],
                                                (HEAD_DIM, HEAD_DIM)) for h in range(HEADS)])
            s_ref[...] = s * elast + _bdot(kdt[sel], vnb)

            r = lax.rsqrt(jnp.mean(o * o, axis=-1, keepdims=True) + EPS)
            on = (o * r) * og_ref[...]
            for h in range(HEADS):
                cols = slice(h * HEAD_DIM, (h + 1) * HEAD_DIM)
                o_ref[rows[g], cols] = (on[h] * _silu(z_ref[rows[g], cols].astype(F32))).astype(BF16)
        return 0

    lax.fori_loop(0, tb // (CHUNK * DN_GROUP), group_body, 0)


def _deltanet(proj3, conv_w, slab, gct, onorm_g):
    bsz, seq, _ = proj3.shape
    tb = DN_TB
    nct = tb // CHUNK

    def colblk(j):
        return pl.BlockSpec((None, tb, HEAD_W), lambda b, t: (b, t, j))

    def wblk(j):
        return pl.BlockSpec((CONV_K, HEAD_W), lambda b, t: (0, j))

    return pl.pallas_call(
        _deltanet_kernel,
        grid=(bsz, seq // tb),
        in_specs=[
            colblk(0), colblk(1), colblk(2), colblk(3),
            wblk(0), wblk(1), wblk(2),
            pl.BlockSpec((None, tb, LANES), lambda b, t: (b, t, 0)),
            pl.BlockSpec((None, nct, HEADS, 1, CHUNK), lambda b, t: (b, t, 0, 0, 0)),
            pl.BlockSpec((1, HEAD_DIM), lambda b, t: (0, 0)),
        ],
        out_specs=pl.BlockSpec((None, tb, HEAD_W), lambda b, t: (b, t, 0)),
        out_shape=jax.ShapeDtypeStruct((bsz, seq, HEAD_W), BF16),
        scratch_shapes=[
            pltpu.VMEM((3, tb + HALO, HEAD_W), F32),
            pltpu.VMEM((HEADS, tb, HEAD_DIM), BF16),
            pltpu.VMEM((HEADS, tb, HEAD_DIM), BF16),
            pltpu.VMEM((HEADS, tb, HEAD_DIM), F32),
            pltpu.VMEM((HEADS, HEAD_DIM, HEAD_DIM), F32),
        ],
        compiler_params=_cparams(("parallel", "arbitrary")),
        name="deltanet",
    )(proj3, proj3, proj3, proj3, conv_w, conv_w, conv_w, slab, gct, onorm_g)


FOX_T = 512
FOX_STRIP = 32
NEG_INF = float("-inf")
LOG2E = 1.4426950408889634


def _fox_kernel(q_ref, k_ref, v_ref, f_ref, o_ref, s_a_ref, s_b_ref, p_a_ref, p_b_ref, m_ref, l_ref, alpha_ref,
                acc_ref):
    tq = FOX_T
    i = pl.program_id(2)
    qs = (q_ref[...].astype(F32) * (HEAD_DIM ** -0.5 * LOG2E)).astype(BF16)
    f0 = f_ref[i][:, 0:1]
    m_ref[...] = jnp.full_like(m_ref, NEG_INF)
    l_ref[...] = jnp.zeros_like(l_ref)
    acc_ref[...] = jnp.zeros_like(acc_ref)

    def keys(j):
        return pl.ds(pl.multiple_of(j * tq, tq), tq)

    def scores(j):
        return _dot_nt(qs, k_ref[keys(j), :])

    def softmax(j, s_view, p_view, masked):
        bias = (f0 - f_ref[j]) * LOG2E

        def width(r):
            if not masked:
                return tq
            return min(tq, ((r + 1) * FOX_STRIP + LANES - 1) // LANES * LANES)

        def logits(r):
            rows = slice(r * FOX_STRIP, (r + 1) * FOX_STRIP)
            w = width(r)
            s = s_view[rows, :w] + bias[:, :w]
            if masked:
                ri = r * FOX_STRIP + lax.broadcasted_iota(jnp.int32, (FOX_STRIP, w), 0)
                ci = lax.broadcasted_iota(jnp.int32, (FOX_STRIP, w), 1)
                s = jnp.where(ci <= ri, s, NEG_INF)
            return rows, s

        for r in range(tq // FOX_STRIP):
            rows, s = logits(r)
            m_old = m_ref[rows, :]
            m_new = jnp.maximum(m_old, jnp.max(s, axis=-1, keepdims=True))
            alpha_ref[rows, :] = jnp.exp2(m_old - m_new)
            m_ref[rows, :] = m_new
        for r in range(tq // FOX_STRIP):
            rows, s = logits(r)
            w = width(r)
            p = jnp.exp2(s - jnp.concatenate([m_ref[rows, :]] * (w // LANES), axis=1))
            alpha = alpha_ref[rows, :]
            l_ref[rows, :] = alpha * l_ref[rows, :] + jnp.sum(p, axis=-1, keepdims=True)
            p_view[rows, :w] = p.astype(BF16)
            if w < tq:
                p_view[rows, w:] = jnp.zeros((FOX_STRIP, tq - w), BF16)

    s_a_ref[...] = scores(i)
    s_b_ref[...] = scores(jnp.maximum(i - 1, 0))
    softmax(i, s_a_ref, p_a_ref, True)

    def step(t, s_cur, p_cur, s_nxt, p_prv):
        j = i - t
        s_nxt[...] = scores(jnp.maximum(j - 1, 0))
        pv = _dot(p_prv[...], v_ref[keys(j + 1), :])
        softmax(j, s_cur, p_cur, False)
        acc_ref[...] = alpha_ref[...] * (acc_ref[...] + pv)

    def pair(u):
        step(2 * u + 1, s_b_ref, p_b_ref, s_a_ref, p_a_ref)
        step(2 * u + 2, s_a_ref, p_a_ref, s_b_ref, p_b_ref)

    def quad(w, _):
        pair(2 * w)
        pair(2 * w + 1)
        return 0

    quads = lax.shift_right_logical(i, 2)
    lax.fori_loop(0, quads, quad, 0)

    def finish(p_last):
        acc = acc_ref[...] + _dot(p_last[...], v_ref[keys(0), :])
        o_ref[...] = (acc / l_ref[...]).astype(BF16)

    def tail(rest):
        if rest >= 2:
            pair(2 * quads)
        if rest % 2 == 1:
            step(i, s_b_ref, p_b_ref, s_a_ref, p_a_ref)
        finish(p_b_ref if rest % 2 == 1 else p_a_ref)

    for rest in range(4):
        pl.when((i & 3) == rest)(functools.partial(tail, rest))


def _fox(proj3, f_rows):
    bsz, seq, _ = proj3.shape
    tq = FOX_T
    qb, kb, vb = 4 * HEADS, 5 * HEADS, 6 * HEADS
    return pl.pallas_call(
        _fox_kernel,
        grid=(bsz, HEADS, seq // tq),
        in_specs=[
            pl.BlockSpec((None, tq, HEAD_DIM), lambda b, h, i: (b, i, qb + h)),
            pl.BlockSpec((None, seq, HEAD_DIM), lambda b, h, i: (b, 0, kb + h)),
            pl.BlockSpec((None, seq, HEAD_DIM), lambda b, h, i: (b, 0, vb + h)),
            pl.BlockSpec((None, None, seq // tq, 1, tq), lambda b, h, i: (b, h, 0, 0, 0)),
        ],
        out_specs=pl.BlockSpec((None, tq, HEAD_DIM), lambda b, h, i: (b, i, h)),
        out_shape=jax.ShapeDtypeStruct((bsz, seq, HEAD_W), BF16),
        scratch_shapes=[
            pltpu.VMEM((tq, tq), F32),
            pltpu.VMEM((tq, tq), F32),
            pltpu.VMEM((tq, tq), BF16),
            pltpu.VMEM((tq, tq), BF16),
            pltpu.VMEM((tq, LANES), F32),
            pltpu.VMEM((tq, LANES), F32),
            pltpu.VMEM((tq, LANES), F32),
            pltpu.VMEM((tq, HEAD_DIM), F32),
        ],
        compiler_params=_cparams(("parallel", "parallel", "arbitrary")),
        name="fox",
    )(proj3, proj3, proj3, f_rows)


OUT_TM = 512
R_E0, R_E1, R_W0, R_W1 = 0, 1, 2, 3


def _first_argmax(vals, lane):
    m = jnp.max(vals, axis=-1, keepdims=True)
    idx = jnp.min(jnp.where(vals == m, lane, LANES), axis=-1, keepdims=True)
    return m, idx


def _outproj_kernel(oa_ref, ob_ref, wa_ref, wb_ref, x_ref, g1_ref, sc_ref, sh_ref, g_ref,
                    wr_ref, br_ref, x1_ref, h2_ref, r_ref, cnt_ref):
    mix = _dot(oa_ref[...], wa_ref[...]) + _dot(ob_ref[...], wb_ref[...])
    x1 = x_ref[...] + g1_ref[...] * mix
    x1_ref[...] = x1
    h2 = _modulated_norm(x1, g_ref[...], sc_ref[...], sh_ref[...])
    h2_ref[...] = h2

    logits = _dot3_pre(h2, wr_ref[...]) + br_ref[...]
    tm = logits.shape[0]
    lane = lax.broadcasted_iota(jnp.int32, (tm, LANES), 1)
    gl = jnp.where(lane < N_GROUPS, logits, NEG_INF)
    gmax, gidx = _first_argmax(gl, lane)
    pg = 1.0 / jnp.sum(jnp.exp(gl - gmax), axis=-1, keepdims=True)
    e_lane = lane - N_GROUPS
    in_grp = (e_lane >= gidx * EXPERTS_PER_GROUP) & (e_lane < (gidx + 1) * EXPERTS_PER_GROUP)
    el = jnp.where(in_grp, logits, NEG_INF)
    v0, i0 = _first_argmax(el, lane)
    v1, i1 = _first_argmax(jnp.where(lane == i0, NEG_INF, el), lane)
    ex = jnp.exp(v1 - v0)
    w0 = pg / (1.0 + ex)
    w1 = pg * ex / (1.0 + ex)
    e0 = (i0 - N_GROUPS).astype(F32)
    e1 = (i1 - N_GROUPS).astype(F32)
    r_ref[...] = jnp.where(lane == R_E0, e0, jnp.where(lane == R_E1, e1,
                           jnp.where(lane == R_W0, w0, jnp.where(lane == R_W1, w1, 0.0))))

    @pl.when(pl.program_id(0) == 0)
    def _():
        cnt_ref[...] = jnp.zeros_like(cnt_ref)

    picked = (lane == i0 - N_GROUPS) | (lane == i1 - N_GROUPS)
    cnt_ref[...] += jnp.sum(jnp.where(picked, 1.0, 0.0), axis=0, keepdims=True)


def _outproj(o_a, o_b, wa, wb, x2, g1, sc2, sh2, g, wr_hi_lo, br, seq):
    n = x2.shape[0]
    tm = OUT_TM
    per_b = seq // tm
    modrow = pl.BlockSpec((None, 1, D_MODEL), lambda i: (i // per_b, 0, 0))
    const = lambda shape: pl.BlockSpec(shape, lambda i: (0, 0))
    return pl.pallas_call(
        _outproj_kernel,
        grid=(n // tm,),
        in_specs=[
            pl.BlockSpec((tm, HEAD_W), lambda i: (i, 0)),
            pl.BlockSpec((tm, HEAD_W), lambda i: (i, 0)),
            const((HEAD_W, D_MODEL)), const((HEAD_W, D_MODEL)),
            pl.BlockSpec((tm, D_MODEL), lambda i: (i, 0)),
            modrow, modrow, modrow,
            const((1, D_MODEL)),
            const((D_MODEL, 2 * LANES)), const((1, LANES)),
        ],
        out_specs=[
            pl.BlockSpec((tm, D_MODEL), lambda i: (i, 0)),
            pl.BlockSpec((tm, D_MODEL), lambda i: (i, 0)),
            pl.BlockSpec((tm, LANES), lambda i: (i, 0)),
            pl.BlockSpec((1, LANES), lambda i: (0, 0)),
        ],
        out_shape=[
            jax.ShapeDtypeStruct((n, D_MODEL), F32),
            jax.ShapeDtypeStruct((n, D_MODEL), F32),
            jax.ShapeDtypeStruct((n, LANES), F32),
            jax.ShapeDtypeStruct((1, LANES), F32),
        ],
        compiler_params=_cparams(("arbitrary",)),
        name="outproj",
    )(o_a, o_b, wa, wb, x2, g1, sc2, sh2, g, wr_hi_lo, br)


MOE_TM = 256
ROUTE_TB = 512


def _route_kernel(r_ref, cnt_ref, pos_ref, run_ref, base_ref):
    tb = ROUTE_TB
    t = pl.program_id(0)
    r = r_ref[...]
    lane = lax.broadcasted_iota(jnp.int32, (tb, LANES), 1)
    e0 = r[:, R_E0:R_E0 + 1].astype(jnp.int32)
    e1 = r[:, R_E1:R_E1 + 1].astype(jnp.int32)
    oh0 = lane == e0
    oh1 = lane == e1
    both = jnp.where(oh0 | oh1, 1.0, 0.0)

    @pl.when(t == 0)
    def _():
        counts = cnt_ref[...]
        padded = jnp.ceil(counts / MOE_TM) * MOE_TM
        li = lax.broadcasted_iota(jnp.int32, (LANES, LANES), 0)
        lj = lax.broadcasted_iota(jnp.int32, (LANES, LANES), 1)
        upper = jnp.where(li < lj, 1.0, 0.0).astype(BF16)
        hi = jnp.floor(padded / 256.0)
        lo = padded - hi * 256.0
        hi8 = jnp.broadcast_to(hi, (SUBLANES, LANES)).astype(BF16)
        lo8 = jnp.broadcast_to(lo, (SUBLANES, LANES)).astype(BF16)
        base = _dot(hi8, upper) * 256.0 + _dot(lo8, upper)
        base_ref[...] = base[0:1, :]
        run_ref[...] = jnp.zeros_like(run_ref)

    ri = lax.broadcasted_iota(jnp.int32, (tb, tb), 0)
    ci = lax.broadcasted_iota(jnp.int32, (tb, tb), 1)
    strict = jnp.where(ri > ci, 1.0, 0.0).astype(BF16)
    before = _dot(strict, both.astype(BF16)) + run_ref[...] + base_ref[...]
    p0 = jnp.sum(jnp.where(oh0, before, 0.0), axis=-1, keepdims=True)
    p1 = jnp.sum(jnp.where(oh1, before, 0.0), axis=-1, keepdims=True)
    pos_ref[...] = jnp.where(lane == 0, p0, jnp.where(lane == 1, p1, 0.0)).astype(jnp.int32)
    run_ref[...] = run_ref[...] + jnp.sum(both, axis=0, keepdims=True)


def _route(rslab, counts):
    n = rslab.shape[0]
    tb = ROUTE_TB
    return pl.pallas_call(
        _route_kernel,
        grid=(n // tb,),
        in_specs=[pl.BlockSpec((tb, LANES), lambda t: (t, 0)), pl.BlockSpec((1, LANES), lambda t: (0, 0))],
        out_specs=pl.BlockSpec((tb, LANES), lambda t: (t, 0)),
        out_shape=jax.ShapeDtypeStruct((n, LANES), jnp.int32),
        scratch_shapes=[pltpu.VMEM((1, LANES), F32), pltpu.VMEM((1, LANES), F32)],
        compiler_params=_cparams(("arbitrary",)),
        name="route",
    )(rslab, counts)


DISP_TB = 1024


def _dispatch_kernel(pad_start_ref, pad_len_ref, used_ref, pos_ref, h2_ref, xs_ref, zero_ref, sem, zsem):
    t = pl.program_id(0)
    tb = DISP_TB
    n_tiles = xs_ref.shape[0] // MOE_TM

    def row_copy(g, r, k):
        dst_row = pos_ref[0, 2 * SUBLANES * g + 2 * r + k]
        return pltpu.make_async_copy(h2_ref.at[g, pl.ds(r, 1)], xs_ref.at[pl.ds(dst_row, 1)], sem)

    def issue(g, _):
        for r in range(SUBLANES):
            row_copy(g, r, 0).start(priority=0)
            row_copy(g, r, 1).start(priority=1)
        return 0

    lax.fori_loop(0, tb // SUBLANES, issue, 0)

    @pl.when(t == 0)
    def _():
        zero_ref[...] = jnp.zeros_like(zero_ref)

        def zero_rows(wait, off, rows):
            cp = pltpu.make_async_copy(zero_ref.at[pl.ds(0, rows)], xs_ref.at[pl.ds(off, rows)], zsem)
            cp.wait() if wait else cp.start()

        def per_expert(wait, e, _):
            start = pad_start_ref[e]
            head = (-start) & (SUBLANES - 1)
            for r in range(SUBLANES - 1):
                pl.when(r < head)(functools.partial(zero_rows, wait, start + r, 1))
            off = start + head
            rest = pad_len_ref[e] - head
            piece = MOE_TM // 2
            while piece >= SUBLANES:
                take = (rest & piece) != 0
                pl.when(take)(functools.partial(zero_rows, wait, pl.multiple_of(off, SUBLANES), piece))
                off = off + jnp.where(take, piece, 0)
                piece //= 2
            return 0

        def per_tile(wait, i, _):
            zero_rows(wait, pl.multiple_of(i * MOE_TM, MOE_TM), MOE_TM)
            return 0

        for wait in (False, True):
            lax.fori_loop(0, N_EXPERTS, functools.partial(per_expert, wait), 0)
            lax.fori_loop(used_ref[0], n_tiles, functools.partial(per_tile, wait), 0)

    def drain(g, _):
        for r in range(SUBLANES):
            row_copy(g, r, 0).wait()
            row_copy(g, r, 1).wait()
        return 0

    lax.fori_loop(0, tb // SUBLANES, drain, 0)


def _dispatch(pad_start, pad_len, used, pos2, h2, p_rows):
    n = h2.shape[0]
    tb = DISP_TB
    return pl.pallas_call(
        _dispatch_kernel,
        grid_spec=pltpu.PrefetchScalarGridSpec(
            num_scalar_prefetch=3,
            grid=(n // tb,),
            in_specs=[
                pl.BlockSpec((None, 1, 2 * tb), lambda t, *_: (t, 0, 0), memory_space=pltpu.SMEM),
                pl.BlockSpec((tb // SUBLANES, SUBLANES, D_MODEL), lambda t, *_: (t, 0, 0)),
            ],
            out_specs=pl.BlockSpec(memory_space=pl.ANY),
            scratch_shapes=[
                pltpu.VMEM((MOE_TM, D_MODEL), F32),
                pltpu.SemaphoreType.DMA(()),
                pltpu.SemaphoreType.DMA(()),
            ],
        ),
        out_shape=jax.ShapeDtypeStruct((p_rows, D_MODEL), F32),
        compiler_params=_cparams(("arbitrary",)),
        name="dispatch",
    )(pad_start, pad_len, used, pos2, h2.reshape(n // SUBLANES, SUBLANES, D_MODEL))


def _experts_kernel(te_ref, tv_ref, tf_ref, ts_ref, tn_ref, xs_ref, w1_hbm, w3_hbm, w2_hbm, ys_ref,
                    w1f_ref, w3f_ref, w2f_ref, w1b_ref, w3b_ref, w2b_ref, sem):
    i = pl.program_id(0)

    def weight_copies(e, slot):
        return [pltpu.make_async_copy(src.at[e], dst.at[slot], sem.at[slot])
                for src, dst in ((w1_hbm, w1f_ref), (w3_hbm, w3f_ref), (w2_hbm, w2f_ref))]

    @pl.when(tv_ref[i] != 0)
    def _():
        @pl.when(tf_ref[i] != 0)
        def _():
            slot = ts_ref[i]

            @pl.when(i == 0)
            def _():
                for cp in weight_copies(te_ref[i], slot):
                    cp.start()

            for cp in weight_copies(te_ref[i], slot):
                cp.wait()

            @pl.when(tn_ref[i] >= 0)
            def _():
                for cp in weight_copies(tn_ref[i], 1 - slot):
                    cp.start()

            w1b_ref[...] = w1f_ref[slot].astype(BF16)
            w3b_ref[...] = w3f_ref[slot].astype(BF16)
            w2b_ref[...] = w2f_ref[slot].astype(BF16)

        x = xs_ref[...].astype(BF16)
        a = _dot(x, w1b_ref[...])
        b = _dot(x, w3b_ref[...])
        ys_ref[...] = _dot((_silu(a) * b).astype(BF16), w2b_ref[...])

    @pl.when(tv_ref[i] == 0)
    def _():
        ys_ref[...] = jnp.zeros_like(ys_ref)


def _experts(tile_expert, tile_valid, tile_first, tile_slot, tile_next, xs, w1, w3, w2):
    p_rows = xs.shape[0]
    tm = MOE_TM
    hbm = pl.BlockSpec(memory_space=pl.ANY)
    return pl.pallas_call(
        _experts_kernel,
        grid_spec=pltpu.PrefetchScalarGridSpec(
            num_scalar_prefetch=5,
            grid=(p_rows // tm,),
            in_specs=[pl.BlockSpec((tm, D_MODEL), lambda i, te, tv, *_: (jnp.where(tv[i] != 0, i, 0), 0)),
                      hbm, hbm, hbm],
            out_specs=pl.BlockSpec((tm, D_MODEL), lambda i, *_: (i, 0)),
            scratch_shapes=[
                pltpu.VMEM((2, D_MODEL, D_EXPERT), F32),
                pltpu.VMEM((2, D_MODEL, D_EXPERT), F32),
                pltpu.VMEM((2, D_EXPERT, D_MODEL), F32),
                pltpu.VMEM((D_MODEL, D_EXPERT), BF16),
                pltpu.VMEM((D_MODEL, D_EXPERT), BF16),
                pltpu.VMEM((D_EXPERT, D_MODEL), BF16),
                pltpu.SemaphoreType.DMA((2,)),
            ],
        ),
        out_shape=jax.ShapeDtypeStruct((p_rows, D_MODEL), F32),
        compiler_params=_cparams(("arbitrary",)),
        name="experts",
    )(tile_expert, tile_valid, tile_first, tile_slot, tile_next, xs, w1, w3, w2)


COMB_TB = 256


def _combine_kernel(pos_ref, pos_next_ref, ys_ref, r_ref, x1_ref, g2_ref, fg_ref, o_ref, buf_ref, sem, *, final):
    tb = COMB_TB
    t = pl.program_id(0)
    slot = t & 1

    def row_copy(p_ref, s, g, r, k):
        src_row = p_ref[0, 2 * SUBLANES * g + 2 * r + k]
        return pltpu.make_async_copy(ys_ref.at[pl.ds(src_row, 1)], buf_ref.at[s, k, g, pl.ds(r, 1)], sem.at[s])

    def issue(p_ref, s):
        def body(g, _):
            for r in range(SUBLANES):
                row_copy(p_ref, s, g, r, 0).start(priority=0)
                row_copy(p_ref, s, g, r, 1).start(priority=1)
            return 0

        lax.fori_loop(0, tb // SUBLANES, body, 0)

    def drain(g, _):
        for r in range(SUBLANES):
            row_copy(pos_ref, slot, g, r, 0).wait()
            row_copy(pos_ref, slot, g, r, 1).wait()
        return 0

    pl.when(t == 0)(functools.partial(issue, pos_ref, 0))
    pl.when(t + 1 < pl.num_programs(0))(functools.partial(issue, pos_next_ref, 1 - slot))
    lax.fori_loop(0, tb // SUBLANES, drain, 0)
    r = r_ref[...]
    y0 = buf_ref[slot, 0].reshape(tb, D_MODEL)
    y1 = buf_ref[slot, 1].reshape(tb, D_MODEL)
    y = r[:, R_W0:R_W0 + 1] * y0 + r[:, R_W1:R_W1 + 1] * y1
    x2 = x1_ref[...] + g2_ref[...] * y
    if final:
        x2 = (x2 * lax.rsqrt(jnp.mean(x2 * x2, axis=-1, keepdims=True) + EPS)) * fg_ref[...]
    o_ref[...] = x2


def _combine(pos2, ys, rslab, x1, g2, final_g, seq, final):
    n = x1.shape[0]
    tb = COMB_TB
    per_b = seq // tb
    return pl.pallas_call(
        functools.partial(_combine_kernel, final=final),
        grid=(n // tb,),
        in_specs=[
            pl.BlockSpec((None, 1, 2 * tb), lambda i: (i, 0, 0), memory_space=pltpu.SMEM),
            pl.BlockSpec((None, 1, 2 * tb), lambda i: (jnp.minimum(i + 1, n // tb - 1), 0, 0),
                         memory_space=pltpu.SMEM),
            pl.BlockSpec(memory_space=pl.ANY),
            pl.BlockSpec((tb, LANES), lambda i: (i, 0)),
            pl.BlockSpec((tb, D_MODEL), lambda i: (i, 0)),
            pl.BlockSpec((None, 1, D_MODEL), lambda i: (i // per_b, 0, 0)),
            pl.BlockSpec((1, D_MODEL), lambda i: (0, 0)),
        ],
        out_specs=pl.BlockSpec((tb, D_MODEL), lambda i: (i, 0)),
        out_shape=jax.ShapeDtypeStruct((n, D_MODEL), F32),
        scratch_shapes=[pltpu.VMEM((2, 2, tb // SUBLANES, SUBLANES, D_MODEL), F32), pltpu.SemaphoreType.DMA((2,))],
        compiler_params=_cparams(("arbitrary",)),
        name="combine",
    )(pos2, pos2, ys, rslab, x1, g2, final_g)


def _layer(x, c, w_ada, b_ada, norm1_g, w_in, conv_w, a_log, dt_bias, dn_onorm_g, fox_f_bias,
           w_out, norm2_g, w_rg, b_rg, w_re, b_re, w1, w3, w2, final_g, final):
    bsz, seq, d = x.shape
    n = bsz * seq
    x2 = x.reshape(n, d)

    mod = _adaln(c, w_ada, b_ada)
    sh1, sc1, g1, sh2, sc2, g2 = [m.reshape(bsz, 1, d) for m in jnp.split(mod, 6, axis=-1)]

    w_t = w_in.T

    def lane_row(vals, off):
        return jnp.zeros((1, LANES), F32).at[0, off:off + HEADS].set(vals)

    h, slab = _prenorm_gates(x, sc1, sh1, norm1_g.reshape(1, d), w_t, lane_row(a_log, 0), lane_row(dt_bias, 0),
                             lane_row(fox_f_bias, L_F))
    proj3 = _inproj(h.reshape(n, d), w_t).reshape(bsz, seq, MAIN_W)
    nc = seq // CHUNK
    gct = slab[:, :, L_GC:L_GC + HEADS].reshape(bsz, nc, CHUNK, HEADS).transpose(0, 1, 3, 2)
    gct = gct.reshape(bsz, nc, HEADS, 1, CHUNK)
    f_rows = slab[:, :, L_F:L_F + HEADS].transpose(0, 2, 1).reshape(bsz, HEADS, seq // FOX_T, 1, FOX_T)

    o_dn = _deltanet(proj3, conv_w, slab, gct, dn_onorm_g.reshape(1, HEAD_DIM))
    o_fx = _fox(proj3, f_rows)

    wr = jnp.zeros((d, LANES), F32).at[:, :N_GROUPS].set(w_rg).at[:, N_GROUPS:N_GROUPS + N_EXPERTS].set(w_re)
    br = jnp.zeros((1, LANES), F32).at[0, :N_GROUPS].set(b_rg).at[0, N_GROUPS:N_GROUPS + N_EXPERTS].set(b_re)
    wr_hi_lo = jnp.concatenate(_split_bf16(wr), axis=1)
    w_out_b = w_out.astype(BF16)
    x1, h2, rslab, counts = _outproj(o_dn.reshape(n, HEAD_W), o_fx.reshape(n, HEAD_W), w_out_b[:HEAD_W],
                                     w_out_b[HEAD_W:], x2, g1, sc2, sh2, norm2_g.reshape(1, d), wr_hi_lo, br, seq)

    pos = _route(rslab, counts)[:, 0:2]

    cnt = counts[0, :N_EXPERTS].astype(jnp.int32)
    tiles_per = (cnt + MOE_TM - 1) // MOE_TM
    tile_end = jnp.cumsum(tiles_per)
    base = (tile_end - tiles_per) * MOE_TM
    n_tiles = (2 * n) // MOE_TM + N_EXPERTS
    p_rows = n_tiles * MOE_TM
    tid = jnp.arange(n_tiles, dtype=jnp.int32)
    tile_valid = (tid < tile_end[-1]).astype(jnp.int32)
    te_raw = jnp.minimum(jnp.sum(tid[:, None] >= tile_end[None, :], axis=1), N_EXPERTS - 1).astype(jnp.int32)
    last_e = te_raw[jnp.maximum(tile_end[-1] - 1, 0)]
    tile_expert = jnp.where(tile_valid == 1, te_raw, last_e)
    tile_first = (jnp.concatenate([jnp.array([-1], jnp.int32), tile_expert[:-1]]) != tile_expert).astype(jnp.int32)
    pad_start = base + cnt
    pad_len = tiles_per * MOE_TM - cnt
    eid = jnp.arange(N_EXPERTS, dtype=jnp.int32)
    has = tiles_per > 0
    slot_e = (jnp.cumsum(has.astype(jnp.int32)) - 1) & 1
    later = jnp.where(has[None, :] & (eid[None, :] > eid[:, None]), eid[None, :], N_EXPERTS)
    next_e = jnp.min(later, axis=1)
    next_e = jnp.where(next_e < N_EXPERTS, next_e, -1).astype(jnp.int32)
    of_tile = tile_expert[:, None] == eid[None, :]
    tile_slot = jnp.sum(jnp.where(of_tile, slot_e[None, :], 0), axis=1).astype(jnp.int32)
    tile_next = jnp.sum(jnp.where(of_tile, next_e[None, :], 0), axis=1).astype(jnp.int32)

    xs = _dispatch(pad_start, pad_len, tile_end[-1:], pos.reshape(n // DISP_TB, 1, 2 * DISP_TB), h2, p_rows)
    ys = _experts(tile_expert, tile_valid, tile_first, tile_slot, tile_next, xs, w1.reshape(N_EXPERTS, d, D_EXPERT),
                  w3.reshape(N_EXPERTS, d, D_EXPERT), w2.reshape(N_EXPERTS, D_EXPERT, d))
    out = _combine(pos.reshape(n // COMB_TB, 1, 2 * COMB_TB), ys, rslab, x1, g2, final_g.reshape(1, d), seq, final)
    return out.reshape(bsz, seq, d)


def kernel(x, c, w_ada, b_ada, norm1_g, w_in, conv_w, a_log, dt_bias, dn_onorm_g, fox_f_bias, w_out, norm2_g,
           w_router_group, b_router_group, w_router_expert, b_router_expert, w1, w3, w2, final_g):
    depth = w_ada.shape[0]
    for l in range(depth):
        x = _layer(x, c, w_ada[l], b_ada[l], norm1_g[l], w_in[l], conv_w[l], a_log[l], dt_bias[l], dn_onorm_g[l],
                   fox_f_bias[l], w_out[l], norm2_g[l], w_router_group[l], b_router_group[l], w_router_expert[l],
                   b_router_expert[l], w1[l], w3[l], w2[l], final_g, l == depth - 1)
    return x
```

```python
import functools

import jax
import jax.numpy as jnp
from jax import lax
from jax.experimental import pallas as pl
from jax.experimental.pallas import tpu as pltpu

F32 = jnp.float32
BF16 = jnp.bfloat16

D_MODEL = 2048
EPS = 1e-6
CHUNK = 64
HEADS = 8
HEAD_DIM = 128
HEAD_W = HEADS * HEAD_DIM
CONV_K = 4
N_GROUPS = 4
EXPERTS_PER_GROUP = 8
N_EXPERTS = N_GROUPS * EXPERTS_PER_GROUP
D_EXPERT = 512
LANES = 128
SUBLANES = 8
MAIN_W = 7 * HEAD_W
VMEM_LIMIT = 56 * 1024 * 1024

L_GC, L_BETA, L_F, L_EGC, L_EK, L_ELAST = 0, 8, 16, 24, 32, 40


def _cparams(sem):
    return pltpu.CompilerParams(dimension_semantics=sem, vmem_limit_bytes=VMEM_LIMIT)


def _split_bf16(a):
    hi = a.astype(BF16)
    lo = (a - hi.astype(F32)).astype(BF16)
    return hi, lo


def _dot(a, b):
    return jnp.dot(a, b, preferred_element_type=F32)


def _dot_nt(a, b):
    return lax.dot_general(a, b, (((1,), (1,)), ((), ())), preferred_element_type=F32)


def _dot3_pre(a, b_hi_lo):
    ah, al = _split_bf16(a)
    n = b_hi_lo.shape[1] // 2
    both = _dot(ah, b_hi_lo)
    return both[:, :n] + (both[:, n:] + _dot(al, b_hi_lo[:, :n]))


def _softplus(x):
    return jnp.maximum(x, 0.0) + jnp.log1p(jnp.exp(-jnp.abs(x)))


def _silu(x):
    return x * jax.nn.sigmoid(x)


ADALN_TN = 1024

def _adaln_kernel(c_ref, w_ref, b_ref, o_ref):
    c = c_ref[...]
    o_ref[...] = _dot(_silu(c).astype(BF16), w_ref[...].astype(BF16)) + b_ref[...]


def _adaln(c, w, b):
    bsz = c.shape[0]
    n = w.shape[1]
    tn = ADALN_TN
    cp = jnp.zeros((SUBLANES, D_MODEL), F32).at[:bsz].set(c)
    out = pl.pallas_call(
        _adaln_kernel,
        grid=(n // tn,),
        in_specs=[
            pl.BlockSpec((SUBLANES, D_MODEL), lambda j: (0, 0)),
            pl.BlockSpec((D_MODEL, tn), lambda j: (0, j)),
            pl.BlockSpec((1, tn), lambda j: (0, j)),
        ],
        out_specs=pl.BlockSpec((SUBLANES, tn), lambda j: (0, j)),
        out_shape=jax.ShapeDtypeStruct((SUBLANES, n), F32),
        compiler_params=_cparams(("parallel",)),
        name="adaln",
    )(cp, w, b.reshape(1, n))
    return out[:bsz]


INPROJ_TM = 2048
INPROJ_TN = 1024
AB_ROW0 = 4 * HEAD_W
F_ROW0 = MAIN_W + 2 * HEADS


def _modulated_norm(x, g, sc, sh):
    r = lax.rsqrt(jnp.mean(x * x, axis=-1, keepdims=True) + EPS)
    return (x * r) * (g * (1.0 + sc)) + sh


def _inproj_kernel(h_ref, w_ref, o_ref):
    o_ref[...] = _dot_nt(h_ref[...], w_ref[...].astype(BF16)).astype(BF16)


def _inproj(h, w_t):
    n = h.shape[0]
    tm, tn = INPROJ_TM, INPROJ_TN
    first_part = AB_ROW0 // tn

    def w_rows(i, j):
        return (SUBLANES * (j * (tn // SUBLANES) + jnp.where(j >= first_part, 2 * HEADS // SUBLANES, 0)), 0)

    return pl.pallas_call(
        _inproj_kernel,
        grid=(n // tm, MAIN_W // tn),
        in_specs=[
            pl.BlockSpec((tm, D_MODEL), lambda i, j: (i, 0)),
            pl.BlockSpec((pl.Element(tn), pl.Element(D_MODEL)), w_rows),
        ],
        out_specs=pl.BlockSpec((tm, tn), lambda i, j: (i, j)),
        out_shape=jax.ShapeDtypeStruct((n, MAIN_W), BF16),
        compiler_params=_cparams(("parallel", "arbitrary")),
        name="inproj",
    )(h, w_t)


GATES_TB = 256


def _split3(a):
    p0 = a.astype(BF16)
    r1 = a - p0.astype(F32)
    p1 = r1.astype(BF16)
    p2 = (r1 - p1.astype(F32)).astype(BF16)
    return p0, p1, p2


def _prenorm_gates_kernel(x_ref, sc_ref, sh_ref, g_ref, wab_ref, wf_ref, alog_ref, dt_ref, fb_ref,
                          h_ref, o_ref, ws_ref, carry_ref):
    tb = GATES_TB

    @pl.when(pl.program_id(1) == 0)
    def _():
        carry_ref[...] = jnp.zeros_like(carry_ref)
        gate_rows = jnp.concatenate([wab_ref[...], wf_ref[...]], axis=0)
        pad = jnp.zeros((LANES - gate_rows.shape[0], D_MODEL), F32)
        hi, lo = _split_bf16(jnp.concatenate([gate_rows, pad], axis=0))
        ws_ref[0:LANES, :] = hi
        ws_ref[LANES:, :] = lo

    hb = _modulated_norm(x_ref[...], g_ref[...], sc_ref[...], sh_ref[...]).astype(BF16)
    h_ref[...] = hb
    both = _dot_nt(hb, ws_ref[...])
    pre = both[:, :LANES] + both[:, LANES:]
    lane = lax.broadcasted_iota(jnp.int32, (tb, LANES), 1)
    g = -jnp.exp(alog_ref[...]) * _softplus(pre + dt_ref[...])
    beta = jax.nn.sigmoid(pre)
    lf = -_softplus(-(pre + fb_ref[...]))

    ri = lax.broadcasted_iota(jnp.int32, (tb, tb), 0)
    ci = lax.broadcasted_iota(jnp.int32, (tb, tb), 1)
    same_chunk = (ri // CHUNK) == (ci // CHUNK)
    tri = (ri >= ci)
    m_all = jnp.where(tri, 1.0, 0.0).astype(BF16)
    m_chunk = jnp.where(tri & same_chunk, 1.0, 0.0).astype(BF16)
    m_tot = jnp.where(same_chunk, 1.0, 0.0).astype(BF16)

    in_a = lane < HEADS
    parts = jnp.concatenate(_split3(jnp.where(in_a, g, lf)), axis=1)
    sums = _dot(jnp.concatenate([m_chunk, m_tot, m_all], axis=0), parts)
    sums = sums[:, :LANES] + (sums[:, LANES:2 * LANES] + sums[:, 2 * LANES:])
    gc = sums[:tb]
    glast = sums[tb:2 * tb]
    fcum = sums[2 * tb:] + carry_ref[...]
    carry_ref[...] = fcum[tb - 1:tb, :]

    egc = jnp.where(in_a, jnp.exp(gc), 0.0)
    ek = jnp.where(in_a, jnp.exp(glast - gc), 0.0)
    elast = jnp.where(in_a, jnp.exp(glast), 0.0)
    out = jnp.where(in_a, gc, jnp.where(lane < 2 * HEADS, beta, jnp.where(lane < 3 * HEADS, fcum, 0.0)))
    out = out + pltpu.roll(egc, L_EGC, 1) + pltpu.roll(ek, L_EK, 1) + pltpu.roll(elast, L_ELAST, 1)
    o_ref[...] = out


def _prenorm_gates(x, sc1, sh1, g, w_t, alog_row, dt_row, fb_row):
    bsz, seq, _ = x.shape
    tb = GATES_TB
    row = pl.BlockSpec((1, LANES), lambda b, t: (0, 0))
    modrow = pl.BlockSpec((None, 1, D_MODEL), lambda b, t: (b, 0, 0))
    return pl.pallas_call(
        _prenorm_gates_kernel,
        grid=(bsz, seq // tb),
        in_specs=[
            pl.BlockSpec((None, tb, D_MODEL), lambda b, t: (b, t, 0)),
            modrow, modrow,
            pl.BlockSpec((1, D_MODEL), lambda b, t: (0, 0)),
            pl.BlockSpec((pl.Element(2 * HEADS), pl.Element(D_MODEL)), lambda b, t: (AB_ROW0, 0)),
            pl.BlockSpec((pl.Element(HEADS), pl.Element(D_MODEL)), lambda b, t: (F_ROW0, 0)),
            row, row, row,
        ],
        out_specs=[
            pl.BlockSpec((None, tb, D_MODEL), lambda b, t: (b, t, 0)),
            pl.BlockSpec((None, tb, LANES), lambda b, t: (b, t, 0)),
        ],
        out_shape=[
            jax.ShapeDtypeStruct((bsz, seq, D_MODEL), BF16),
            jax.ShapeDtypeStruct((bsz, seq, LANES), F32),
        ],
        scratch_shapes=[pltpu.VMEM((2 * LANES, D_MODEL), BF16), pltpu.VMEM((1, LANES), F32)],
        compiler_params=_cparams(("parallel", "arbitrary")),
        name="prenorm_gates",
    )(x, sc1, sh1, g, w_t, w_t, alog_row, dt_row, fb_row)


DN_TB = 256
DN_GROUP = 4
HALO = SUBLANES
INV_BLOCK = CHUNK // 4


def _bdot(a, b):
    return lax.dot_general(a, b, (((2,), (1,)), ((0,), (0,))), preferred_element_type=F32)


def _bdot_nt(a, b):
    return lax.dot_general(a, b, (((2,), (2,)), ((0,), (0,))), preferred_element_type=F32)


def _inv_unit_lower(a, eye, blk16, blk32):
    n = jnp.where(blk16, -a, 0.0)
    e1 = jnp.where(blk32 & jnp.logical_not(blk16), a, 0.0).astype(BF16)
    e2 = jnp.where(blk32, 0.0, a).astype(BF16)
    t = eye + n
    p = n.astype(BF16)
    for _ in range(INV_BLOCK.bit_length() - 2):
        p = _bdot(p, p).astype(BF16)
        t = t + _bdot(t.astype(BF16), p)
    for e in (e1, e2):
        tb = t.astype(BF16)
        t = t - _bdot(_bdot(tb, e).astype(BF16), tb)
    return t


def _deltanet_kernel(q_ref, k_ref, v_ref, z_ref, wq_ref, wk_ref, wv_ref, slab_ref, gct_ref, og_ref,
                     o_ref, ext_ref, qn_ref, kn_ref, vv_ref, s_ref):
    tb = DN_TB

    @pl.when(pl.program_id(1) == 0)
    def _():
        ext_ref[:, 0:HALO, :] = jnp.zeros((3, HALO, HEAD_W), F32)
        s_ref[...] = jnp.zeros_like(s_ref)

    for idx, (u_ref, w_ref) in enumerate(((q_ref, wq_ref), (k_ref, wk_ref), (v_ref, wv_ref))):
        for h in range(HEADS):
            cols = slice(h * HEAD_DIM, (h + 1) * HEAD_DIM)
            ext_ref[idx, HALO:HALO + tb, cols] = u_ref[:, cols].astype(F32)
            y = None
            for j in range(CONV_K):
                start = HALO - (CONV_K - 1) + j
                term = ext_ref[idx, start:start + tb, cols] * w_ref[j:j + 1, cols]
                y = term if y is None else y + term
            y = _silu(y)
            if idx == 2:
                vv_ref[h] = y
            else:
                yn = y * lax.rsqrt(jnp.sum(y * y, axis=-1, keepdims=True) + EPS)
                if idx == 0:
                    qn_ref[h] = (yn * (HEAD_DIM ** -0.5)).astype(BF16)
                else:
                    kn_ref[h] = yn.astype(BF16)
        ext_ref[idx, 0:HALO, :] = ext_ref[idx, tb:tb + HALO, :]

    ri = lax.broadcasted_iota(jnp.int32, (DN_GROUP * HEADS, CHUNK, CHUNK), 1)
    ci = lax.broadcasted_iota(jnp.int32, (DN_GROUP * HEADS, CHUNK, CHUNK), 2)
    incl = ri >= ci
    strict = ri > ci
    eye = jnp.where(ri == ci, 1.0, 0.0)
    blk16 = (ri // INV_BLOCK) == (ci // INV_BLOCK)
    blk32 = (ri // (2 * INV_BLOCK)) == (ci // (2 * INV_BLOCK))

    def group_body(c, _):
        rows = [pl.ds(pl.multiple_of((c * DN_GROUP + g) * CHUNK, CHUNK), CHUNK) for g in range(DN_GROUP)]
        slabs = [slab_ref[r, :] for r in rows]

        def col(off, width):
            return jnp.stack([jnp.broadcast_to(sl[:, off + h:off + h + 1], (CHUNK, width))
                              for sl in slabs for h in range(HEADS)])

        def grouped(ref):
            return jnp.concatenate([ref[:, r, :] for r in rows], axis=0)

        q = grouped(qn_ref)
        k = grouped(kn_ref)
        v = grouped(vv_ref)
        beta = col(L_BETA, HEAD_DIM)
        egc = col(L_EGC, HEAD_DIM)
        gc_row = jnp.concatenate([gct_ref[c * DN_GROUP + g] for g in range(DN_GROUP)], axis=0)

        decay = jnp.where(incl, jnp.exp(col(L_GC, CHUNK) - gc_row), 0.0)
        kk = _bdot_nt(k, k)
        qk = (_bdot_nt(q, k) * decay).astype(BF16)
        a = jnp.where(strict, kk * decay * beta[:, :, :CHUNK], 0.0)
        t = _inv_unit_lower(a, eye, blk16, blk32).astype(BF16)

        kf = k.astype(F32)
        vb = (v * beta).astype(BF16)
        kbg = (kf * (beta * egc)).astype(BF16)
        u = _bdot(t, vb)
        w = _bdot(t, kbg).astype(BF16)
        qd = (q.astype(F32) * egc).astype(BF16)
        kd = kf * col(L_EK, HEAD_DIM)
        kdt = jnp.stack([kd[n].T for n in range(DN_GROUP * HEADS)]).astype(BF16)

        for g in range(DN_GROUP):
            sel = slice(g * HEADS, (g + 1) * HEADS)
            s = s_ref[...]
            sb = s.astype(BF16)
            vnb = (u[sel] - _bdot(w[sel], sb)).astype(BF16)
            o = _bdot(qd[sel], sb) + _bdot(qk[sel], vnb)
            elast = jnp.stack([jnp.broadcast_to(slabs[g][CHUNK - 1:CHUNK, L_ELAST + h:L_ELAST + h + 1],
                                                (HEAD_DIM, HEAD_DIM)) for h in range(HEADS)])
            s_ref[...] = s * elast + _bdot(kdt[sel], vnb)

            r = lax.rsqrt(jnp.mean(o * o, axis=-1, keepdims=True) + EPS)
            on = (o * r) * og_ref[...]
            for h in range(HEADS):
                cols = slice(h * HEAD_DIM, (h + 1) * HEAD_DIM)
                o_ref[rows[g], cols] = (on[h] * _silu(z_ref[rows[g], cols].astype(F32))).astype(BF16)
        return 0

    lax.fori_loop(0, tb // (CHUNK * DN_GROUP), group_body, 0)


def _deltanet(proj3, conv_w, slab, gct, onorm_g):
    bsz, seq, _ = proj3.shape
    tb = DN_TB
    nct = tb // CHUNK

    def colblk(j):
        return pl.BlockSpec((None, tb, HEAD_W), lambda b, t: (b, t, j))

    def wblk(j):
        return pl.BlockSpec((CONV_K, HEAD_W), lambda b, t: (0, j))

    return pl.pallas_call(
        _deltanet_kernel,
        grid=(bsz, seq // tb),
        in_specs=[
            colblk(0), colblk(1), colblk(2), colblk(3),
            wblk(0), wblk(1), wblk(2),
            pl.BlockSpec((None, tb, LANES), lambda b, t: (b, t, 0)),
            pl.BlockSpec((None, nct, HEADS, 1, CHUNK), lambda b, t: (b, t, 0, 0, 0)),
            pl.BlockSpec((1, HEAD_DIM), lambda b, t: (0, 0)),
        ],
        out_specs=pl.BlockSpec((None, tb, HEAD_W), lambda b, t: (b, t, 0)),
        out_shape=jax.ShapeDtypeStruct((bsz, seq, HEAD_W), BF16),
        scratch_shapes=[
            pltpu.VMEM((3, tb + HALO, HEAD_W), F32),
            pltpu.VMEM((HEADS, tb, HEAD_DIM), BF16),
            pltpu.VMEM((HEADS, tb, HEAD_DIM), BF16),
            pltpu.VMEM((HEADS, tb, HEAD_DIM), F32),
            pltpu.VMEM((HEADS, HEAD_DIM, HEAD_DIM), F32),
        ],
        compiler_params=_cparams(("parallel", "arbitrary")),
        name="deltanet",
    )(proj3, proj3, proj3, proj3, conv_w, conv_w, conv_w, slab, gct, onorm_g)


FOX_T = 512
FOX_STRIP = 32
NEG_INF = float("-inf")
LOG2E = 1.4426950408889634


def _fox_kernel(q_ref, k_ref, v_ref, f_ref, o_ref, s_a_ref, s_b_ref, p_a_ref, p_b_ref, m_ref, l_ref, alpha_ref,
                acc_ref):
    tq = FOX_T
    i = pl.program_id(2)
    qs = (q_ref[...].astype(F32) * (HEAD_DIM ** -0.5 * LOG2E)).astype(BF16)
    f0 = f_ref[i][:, 0:1]
    m_ref[...] = jnp.full_like(m_ref, NEG_INF)
    l_ref[...] = jnp.zeros_like(l_ref)
    acc_ref[...] = jnp.zeros_like(acc_ref)

    def keys(j):
        return pl.ds(pl.multiple_of(j * tq, tq), tq)

    def scores(j):
        return _dot_nt(qs, k_ref[keys(j), :])

    def softmax(j, s_view, p_view, masked):
        bias = (f0 - f_ref[j]) * LOG2E

        def width(r):
            if not masked:
                return tq
            return min(tq, ((r + 1) * FOX_STRIP + LANES - 1) // LANES * LANES)

        def logits(r):
            rows = slice(r * FOX_STRIP, (r + 1) * FOX_STRIP)
            w = width(r)
            s = s_view[rows, :w] + bias[:, :w]
            if masked:
                ri = r * FOX_STRIP + lax.broadcasted_iota(jnp.int32, (FOX_STRIP, w), 0)
                ci = lax.broadcasted_iota(jnp.int32, (FOX_STRIP, w), 1)
                s = jnp.where(ci <= ri, s, NEG_INF)
            return rows, s

        for r in range(tq // FOX_STRIP):
            rows, s = logits(r)
            m_old = m_ref[rows, :]
            m_new = jnp.maximum(m_old, jnp.max(s, axis=-1, keepdims=True))
            alpha_ref[rows, :] = jnp.exp2(m_old - m_new)
            m_ref[rows, :] = m_new
        for r in range(tq // FOX_STRIP):
            rows, s = logits(r)
            w = width(r)
            p = jnp.exp2(s - jnp.concatenate([m_ref[rows, :]] * (w // LANES), axis=1))
            alpha = alpha_ref[rows, :]
            l_ref[rows, :] = alpha * l_ref[rows, :] + jnp.sum(p, axis=-1, keepdims=True)
            p_view[rows, :w] = p.astype(BF16)
            if w < tq:
                p_view[rows, w:] = jnp.zeros((FOX_STRIP, tq - w), BF16)

    def diagonal():
        s_a_ref[...] = scores(i)
        s_b_ref[...] = scores(jnp.maximum(i - 1, 0))
        softmax(i, s_a_ref, p_a_ref, True)

    def step(t, s_cur, p_cur, s_nxt, p_prv):
        j = i - t
        s_nxt[...] = scores(jnp.maximum(j - 1, 0))
        pv = _dot(p_prv[...], v_ref[keys(j + 1), :])
        softmax(j, s_cur, p_cur, False)
        acc_ref[...] = alpha_ref[...] * (acc_ref[...] + pv)

    def pair(u):
        step(2 * u + 1, s_b_ref, p_b_ref, s_a_ref, p_a_ref)
        step(2 * u + 2, s_a_ref, p_a_ref, s_b_ref, p_b_ref)

    def quad(w, _):
        pair(2 * w)
        pair(2 * w + 1)
        return 0

    quads = lax.shift_right_logical(i, 2)

    def finish(p_last):
        acc = acc_ref[...] + _dot(p_last[...], v_ref[keys(0), :])
        o_ref[...] = (acc / l_ref[...]).astype(BF16)

    def tail(rest):
        if rest >= 2:
            pair(2 * quads)
        if rest % 2 == 1:
            step(i, s_b_ref, p_b_ref, s_a_ref, p_a_ref)
        finish(p_b_ref if rest % 2 == 1 else p_a_ref)

    def short(rest):
        diagonal()
        tail(rest)

    for rest in range(4):
        pl.when(i == rest)(functools.partial(short, rest))
    pl.when(quads > 0)(diagonal)
    lax.fori_loop(0, quads, quad, 0)
    for rest in range(4):
        pl.when((quads > 0) & ((i & 3) == rest))(functools.partial(tail, rest))


def _fox(proj3, f_rows):
    bsz, seq, _ = proj3.shape
    tq = FOX_T
    qb, kb, vb = 4 * HEADS, 5 * HEADS, 6 * HEADS
    return pl.pallas_call(
        _fox_kernel,
        grid=(bsz, HEADS, seq // tq),
        in_specs=[
            pl.BlockSpec((None, tq, HEAD_DIM), lambda b, h, i: (b, i, qb + h)),
            pl.BlockSpec((None, seq, HEAD_DIM), lambda b, h, i: (b, 0, kb + h)),
            pl.BlockSpec((None, seq, HEAD_DIM), lambda b, h, i: (b, 0, vb + h)),
            pl.BlockSpec((None, None, seq // tq, 1, tq), lambda b, h, i: (b, h, 0, 0, 0)),
        ],
        out_specs=pl.BlockSpec((None, tq, HEAD_DIM), lambda b, h, i: (b, i, h)),
        out_shape=jax.ShapeDtypeStruct((bsz, seq, HEAD_W), BF16),
        scratch_shapes=[
            pltpu.VMEM((tq, tq), F32),
            pltpu.VMEM((tq, tq), F32),
            pltpu.VMEM((tq, tq), BF16),
            pltpu.VMEM((tq, tq), BF16),
            pltpu.VMEM((tq, LANES), F32),
            pltpu.VMEM((tq, LANES), F32),
            pltpu.VMEM((tq, LANES), F32),
            pltpu.VMEM((tq, HEAD_DIM), F32),
        ],
        compiler_params=_cparams(("parallel", "parallel", "arbitrary")),
        name="fox",
    )(proj3, proj3, proj3, f_rows)


OUT_TM = 512
R_E0, R_E1, R_W0, R_W1 = 0, 1, 2, 3


def _first_argmax(vals, lane):
    m = jnp.max(vals, axis=-1, keepdims=True)
    idx = jnp.min(jnp.where(vals == m, lane, LANES), axis=-1, keepdims=True)
    return m, idx


def _outproj_kernel(oa_ref, ob_ref, wa_ref, wb_ref, x_ref, g1_ref, sc_ref, sh_ref, g_ref,
                    wr_ref, br_ref, x1_ref, h2_ref, r_ref, cnt_ref):
    mix = _dot(oa_ref[...], wa_ref[...]) + _dot(ob_ref[...], wb_ref[...])
    x1 = x_ref[...] + g1_ref[...] * mix
    x1_ref[...] = x1
    h2 = _modulated_norm(x1, g_ref[...], sc_ref[...], sh_ref[...])
    h2_ref[...] = h2

    logits = _dot3_pre(h2, wr_ref[...]) + br_ref[...]
    tm = logits.shape[0]
    lane = lax.broadcasted_iota(jnp.int32, (tm, LANES), 1)
    gl = jnp.where(lane < N_GROUPS, logits, NEG_INF)
    gmax, gidx = _first_argmax(gl, lane)
    pg = 1.0 / jnp.sum(jnp.exp(gl - gmax), axis=-1, keepdims=True)
    e_lane = lane - N_GROUPS
    in_grp = (e_lane >= gidx * EXPERTS_PER_GROUP) & (e_lane < (gidx + 1) * EXPERTS_PER_GROUP)
    el = jnp.where(in_grp, logits, NEG_INF)
    v0, i0 = _first_argmax(el, lane)
    v1, i1 = _first_argmax(jnp.where(lane == i0, NEG_INF, el), lane)
    ex = jnp.exp(v1 - v0)
    w0 = pg / (1.0 + ex)
    w1 = pg * ex / (1.0 + ex)
    e0 = (i0 - N_GROUPS).astype(F32)
    e1 = (i1 - N_GROUPS).astype(F32)
    r_ref[...] = jnp.where(lane == R_E0, e0, jnp.where(lane == R_E1, e1,
                           jnp.where(lane == R_W0, w0, jnp.where(lane == R_W1, w1, 0.0))))

    @pl.when(pl.program_id(0) == 0)
    def _():
        cnt_ref[...] = jnp.zeros_like(cnt_ref)

    picked = (lane == i0 - N_GROUPS) | (lane == i1 - N_GROUPS)
    cnt_ref[...] += jnp.sum(jnp.where(picked, 1.0, 0.0), axis=0, keepdims=True)


def _outproj(o_a, o_b, wa, wb, x2, g1, sc2, sh2, g, wr_hi_lo, br, seq):
    n = x2.shape[0]
    tm = OUT_TM
    per_b = seq // tm
    modrow = pl.BlockSpec((None, 1, D_MODEL), lambda i: (i // per_b, 0, 0))
    const = lambda shape: pl.BlockSpec(shape, lambda i: (0, 0))
    return pl.pallas_call(
        _outproj_kernel,
        grid=(n // tm,),
        in_specs=[
            pl.BlockSpec((tm, HEAD_W), lambda i: (i, 0)),
            pl.BlockSpec((tm, HEAD_W), lambda i: (i, 0)),
            const((HEAD_W, D_MODEL)), const((HEAD_W, D_MODEL)),
            pl.BlockSpec((tm, D_MODEL), lambda i: (i, 0)),
            modrow, modrow, modrow,
            const((1, D_MODEL)),
            const((D_MODEL, 2 * LANES)), const((1, LANES)),
        ],
        out_specs=[
            pl.BlockSpec((tm, D_MODEL), lambda i: (i, 0)),
            pl.BlockSpec((tm, D_MODEL), lambda i: (i, 0)),
            pl.BlockSpec((tm, LANES), lambda i: (i, 0)),
            pl.BlockSpec((1, LANES), lambda i: (0, 0)),
        ],
        out_shape=[
            jax.ShapeDtypeStruct((n, D_MODEL), F32),
            jax.ShapeDtypeStruct((n, D_MODEL), F32),
            jax.ShapeDtypeStruct((n, LANES), F32),
            jax.ShapeDtypeStruct((1, LANES), F32),
        ],
        compiler_params=_cparams(("arbitrary",)),
        name="outproj",
    )(o_a, o_b, wa, wb, x2, g1, sc2, sh2, g, wr_hi_lo, br)


MOE_TM = 256
ROUTE_TB = 512


def _route_kernel(r_ref, cnt_ref, pos_ref, run_ref, base_ref):
    tb = ROUTE_TB
    t = pl.program_id(0)
    r = r_ref[...]
    lane = lax.broadcasted_iota(jnp.int32, (tb, LANES), 1)
    e0 = r[:, R_E0:R_E0 + 1].astype(jnp.int32)
    e1 = r[:, R_E1:R_E1 + 1].astype(jnp.int32)
    oh0 = lane == e0
    oh1 = lane == e1
    both = jnp.where(oh0 | oh1, 1.0, 0.0)

    @pl.when(t == 0)
    def _():
        counts = cnt_ref[...]
        padded = jnp.ceil(counts / MOE_TM) * MOE_TM
        li = lax.broadcasted_iota(jnp.int32, (LANES, LANES), 0)
        lj = lax.broadcasted_iota(jnp.int32, (LANES, LANES), 1)
        upper = jnp.where(li < lj, 1.0, 0.0).astype(BF16)
        hi = jnp.floor(padded / 256.0)
        lo = padded - hi * 256.0
        hi8 = jnp.broadcast_to(hi, (SUBLANES, LANES)).astype(BF16)
        lo8 = jnp.broadcast_to(lo, (SUBLANES, LANES)).astype(BF16)
        base = _dot(hi8, upper) * 256.0 + _dot(lo8, upper)
        base_ref[...] = base[0:1, :]
        run_ref[...] = jnp.zeros_like(run_ref)

    ri = lax.broadcasted_iota(jnp.int32, (tb, tb), 0)
    ci = lax.broadcasted_iota(jnp.int32, (tb, tb), 1)
    strict = jnp.where(ri > ci, 1.0, 0.0).astype(BF16)
    before = _dot(strict, both.astype(BF16)) + run_ref[...] + base_ref[...]
    p0 = jnp.sum(jnp.where(oh0, before, 0.0), axis=-1, keepdims=True)
    p1 = jnp.sum(jnp.where(oh1, before, 0.0), axis=-1, keepdims=True)
    pos_ref[...] = jnp.where(lane == 0, p0, jnp.where(lane == 1, p1, 0.0)).astype(jnp.int32)
    run_ref[...] = run_ref[...] + jnp.sum(both, axis=0, keepdims=True)


def _route(rslab, counts):
    n = rslab.shape[0]
    tb = ROUTE_TB
    return pl.pallas_call(
        _route_kernel,
        grid=(n // tb,),
        in_specs=[pl.BlockSpec((tb, LANES), lambda t: (t, 0)), pl.BlockSpec((1, LANES), lambda t: (0, 0))],
        out_specs=pl.BlockSpec((tb, LANES), lambda t: (t, 0)),
        out_shape=jax.ShapeDtypeStruct((n, LANES), jnp.int32),
        scratch_shapes=[pltpu.VMEM((1, LANES), F32), pltpu.VMEM((1, LANES), F32)],
        compiler_params=_cparams(("arbitrary",)),
        name="route",
    )(rslab, counts)


DISP_TB = 1024


def _dispatch_kernel(pad_start_ref, pad_len_ref, used_ref, pos_ref, h2_ref, xs_ref, zero_ref, sem, zsem):
    t = pl.program_id(0)
    tb = DISP_TB
    n_tiles = xs_ref.shape[0] // MOE_TM

    def row_copy(g, r, k):
        dst_row = pos_ref[0, 2 * SUBLANES * g + 2 * r + k]
        return pltpu.make_async_copy(h2_ref.at[g, pl.ds(r, 1)], xs_ref.at[pl.ds(dst_row, 1)], sem)

    def issue(g, _):
        for r in range(SUBLANES):
            row_copy(g, r, 0).start(priority=0)
            row_copy(g, r, 1).start(priority=1)
        return 0

    lax.fori_loop(0, tb // SUBLANES, issue, 0)

    @pl.when(t == 0)
    def _():
        zero_ref[...] = jnp.zeros_like(zero_ref)

        def zero_rows(wait, off, rows):
            cp = pltpu.make_async_copy(zero_ref.at[pl.ds(0, rows)], xs_ref.at[pl.ds(off, rows)], zsem)
            cp.wait() if wait else cp.start()

        def per_expert(wait, e, _):
            start = pad_start_ref[e]
            head = (-start) & (SUBLANES - 1)
            for r in range(SUBLANES - 1):
                pl.when(r < head)(functools.partial(zero_rows, wait, start + r, 1))
            off = start + head
            rest = pad_len_ref[e] - head
            piece = MOE_TM // 2
            while piece >= SUBLANES:
                take = (rest & piece) != 0
                pl.when(take)(functools.partial(zero_rows, wait, pl.multiple_of(off, SUBLANES), piece))
                off = off + jnp.where(take, piece, 0)
                piece //= 2
            return 0

        def per_tile(wait, i, _):
            zero_rows(wait, pl.multiple_of(i * MOE_TM, MOE_TM), MOE_TM)
            return 0

        for wait in (False, True):
            lax.fori_loop(0, N_EXPERTS, functools.partial(per_expert, wait), 0)
            lax.fori_loop(used_ref[0], n_tiles, functools.partial(per_tile, wait), 0)

    def drain(g, _):
        for r in range(SUBLANES):
            row_copy(g, r, 0).wait()
            row_copy(g, r, 1).wait()
        return 0

    lax.fori_loop(0, tb // SUBLANES, drain, 0)


def _dispatch(pad_start, pad_len, used, pos2, h2, p_rows):
    n = h2.shape[0]
    tb = DISP_TB
    return pl.pallas_call(
        _dispatch_kernel,
        grid_spec=pltpu.PrefetchScalarGridSpec(
            num_scalar_prefetch=3,
            grid=(n // tb,),
            in_specs=[
                pl.BlockSpec((None, 1, 2 * tb), lambda t, *_: (t, 0, 0), memory_space=pltpu.SMEM),
                pl.BlockSpec((tb // SUBLANES, SUBLANES, D_MODEL), lambda t, *_: (t, 0, 0)),
            ],
            out_specs=pl.BlockSpec(memory_space=pl.ANY),
            scratch_shapes=[
                pltpu.VMEM((MOE_TM, D_MODEL), F32),
                pltpu.SemaphoreType.DMA(()),
                pltpu.SemaphoreType.DMA(()),
            ],
        ),
        out_shape=jax.ShapeDtypeStruct((p_rows, D_MODEL), F32),
        compiler_params=_cparams(("arbitrary",)),
        name="dispatch",
    )(pad_start, pad_len, used, pos2, h2.reshape(n // SUBLANES, SUBLANES, D_MODEL))


def _experts_kernel(te_ref, tv_ref, tf_ref, ts_ref, tn_ref, xs_ref, w1_hbm, w3_hbm, w2_hbm, ys_ref,
                    w1f_ref, w3f_ref, w2f_ref, w1b_ref, w3b_ref, w2b_ref, sem):
    i = pl.program_id(0)

    def weight_copies(e, slot):
        return [pltpu.make_async_copy(src.at[e], dst.at[slot], sem.at[slot])
                for src, dst in ((w1_hbm, w1f_ref), (w3_hbm, w3f_ref), (w2_hbm, w2f_ref))]

    @pl.when(tv_ref[i] != 0)
    def _():
        @pl.when(tf_ref[i] != 0)
        def _():
            slot = ts_ref[i]

            @pl.when(i == 0)
            def _():
                for cp in weight_copies(te_ref[i], slot):
                    cp.start()

            for cp in weight_copies(te_ref[i], slot):
                cp.wait()

            @pl.when(tn_ref[i] >= 0)
            def _():
                for cp in weight_copies(tn_ref[i], 1 - slot):
                    cp.start()

            w1b_ref[...] = w1f_ref[slot].astype(BF16)
            w3b_ref[...] = w3f_ref[slot].astype(BF16)
            w2b_ref[...] = w2f_ref[slot].astype(BF16)

        x = xs_ref[...].astype(BF16)
        a = _dot(x, w1b_ref[...])
        b = _dot(x, w3b_ref[...])
        ys_ref[...] = _dot((_silu(a) * b).astype(BF16), w2b_ref[...])

    @pl.when(tv_ref[i] == 0)
    def _():
        ys_ref[...] = jnp.zeros_like(ys_ref)


def _experts(tile_expert, tile_valid, tile_first, tile_slot, tile_next, xs, w1, w3, w2):
    p_rows = xs.shape[0]
    tm = MOE_TM
    hbm = pl.BlockSpec(memory_space=pl.ANY)
    return pl.pallas_call(
        _experts_kernel,
        grid_spec=pltpu.PrefetchScalarGridSpec(
            num_scalar_prefetch=5,
            grid=(p_rows // tm,),
            in_specs=[pl.BlockSpec((tm, D_MODEL), lambda i, te, tv, *_: (jnp.where(tv[i] != 0, i, 0), 0)),
                      hbm, hbm, hbm],
            out_specs=pl.BlockSpec((tm, D_MODEL), lambda i, *_: (i, 0)),
            scratch_shapes=[
                pltpu.VMEM((2, D_MODEL, D_EXPERT), F32),
                pltpu.VMEM((2, D_MODEL, D_EXPERT), F32),
                pltpu.VMEM((2, D_EXPERT, D_MODEL), F32),
                pltpu.VMEM((D_MODEL, D_EXPERT), BF16),
                pltpu.VMEM((D_MODEL, D_EXPERT), BF16),
                pltpu.VMEM((D_EXPERT, D_MODEL), BF16),
                pltpu.SemaphoreType.DMA((2,)),
            ],
        ),
        out_shape=jax.ShapeDtypeStruct((p_rows, D_MODEL), F32),
        compiler_params=_cparams(("arbitrary",)),
        name="experts",
    )(tile_expert, tile_valid, tile_first, tile_slot, tile_next, xs, w1, w3, w2)


COMB_TB = 256


def _combine_kernel(pos_ref, pos_next_ref, ys_ref, r_ref, x1_ref, g2_ref, fg_ref, o_ref, buf_ref, sem, *, final):
    tb = COMB_TB
    t = pl.program_id(0)
    slot = t & 1

    def row_copy(p_ref, s, g, r, k):
        src_row = p_ref[0, 2 * SUBLANES * g + 2 * r + k]
        return pltpu.make_async_copy(ys_ref.at[pl.ds(src_row, 1)], buf_ref.at[s, k, g, pl.ds(r, 1)], sem.at[s])

    def issue(p_ref, s):
        def body(g, _):
            for r in range(SUBLANES):
                row_copy(p_ref, s, g, r, 0).start(priority=0)
                row_copy(p_ref, s, g, r, 1).start(priority=1)
            return 0

        lax.fori_loop(0, tb // SUBLANES, body, 0)

    def drain(g, _):
        for r in range(SUBLANES):
            row_copy(pos_ref, slot, g, r, 0).wait()
            row_copy(pos_ref, slot, g, r, 1).wait()
        return 0

    pl.when(t == 0)(functools.partial(issue, pos_ref, 0))
    pl.when(t + 1 < pl.num_programs(0))(functools.partial(issue, pos_next_ref, 1 - slot))
    lax.fori_loop(0, tb // SUBLANES, drain, 0)
    r = r_ref[...]
    y0 = buf_ref[slot, 0].reshape(tb, D_MODEL)
    y1 = buf_ref[slot, 1].reshape(tb, D_MODEL)
    y = r[:, R_W0:R_W0 + 1] * y0 + r[:, R_W1:R_W1 + 1] * y1
    x2 = x1_ref[...] + g2_ref[...] * y
    if final:
        x2 = (x2 * lax.rsqrt(jnp.mean(x2 * x2, axis=-1, keepdims=True) + EPS)) * fg_ref[...]
    o_ref[...] = x2


def _combine(pos2, ys, rslab, x1, g2, final_g, seq, final):
    n = x1.shape[0]
    tb = COMB_TB
    per_b = seq // tb
    return pl.pallas_call(
        functools.partial(_combine_kernel, final=final),
        grid=(n // tb,),
        in_specs=[
            pl.BlockSpec((None, 1, 2 * tb), lambda i: (i, 0, 0), memory_space=pltpu.SMEM),
            pl.BlockSpec((None, 1, 2 * tb), lambda i: (jnp.minimum(i + 1, n // tb - 1), 0, 0),
                         memory_space=pltpu.SMEM),
            pl.BlockSpec(memory_space=pl.ANY),
            pl.BlockSpec((tb, LANES), lambda i: (i, 0)),
            pl.BlockSpec((tb, D_MODEL), lambda i: (i, 0)),
            pl.BlockSpec((None, 1, D_MODEL), lambda i: (i // per_b, 0, 0)),
            pl.BlockSpec((1, D_MODEL), lambda i: (0, 0)),
        ],
        out_specs=pl.BlockSpec((tb, D_MODEL), lambda i: (i, 0)),
        out_shape=jax.ShapeDtypeStruct((n, D_MODEL), F32),
        scratch_shapes=[pltpu.VMEM((2, 2, tb // SUBLANES, SUBLANES, D_MODEL), F32), pltpu.SemaphoreType.DMA((2,))],
        compiler_params=_cparams(("arbitrary",)),
        name="combine",
    )(pos2, pos2, ys, rslab, x1, g2, final_g)


def _layer(x, c, w_ada, b_ada, norm1_g, w_in, conv_w, a_log, dt_bias, dn_onorm_g, fox_f_bias,
           w_out, norm2_g, w_rg, b_rg, w_re, b_re, w1, w3, w2, final_g, final):
    bsz, seq, d = x.shape
    n = bsz * seq
    x2 = x.reshape(n, d)

    mod = _adaln(c, w_ada, b_ada)
    sh1, sc1, g1, sh2, sc2, g2 = [m.reshape(bsz, 1, d) for m in jnp.split(mod, 6, axis=-1)]

    w_t = w_in.T

    def lane_row(vals, off):
        return jnp.zeros((1, LANES), F32).at[0, off:off + HEADS].set(vals)

    h, slab = _prenorm_gates(x, sc1, sh1, norm1_g.reshape(1, d), w_t, lane_row(a_log, 0), lane_row(dt_bias, 0),
                             lane_row(fox_f_bias, L_F))
    proj3 = _inproj(h.reshape(n, d), w_t).reshape(bsz, seq, MAIN_W)
    nc = seq // CHUNK
    gct = slab[:, :, L_GC:L_GC + HEADS].reshape(bsz, nc, CHUNK, HEADS).transpose(0, 1, 3, 2)
    gct = gct.reshape(bsz, nc, HEADS, 1, CHUNK)
    f_rows = slab[:, :, L_F:L_F + HEADS].transpose(0, 2, 1).reshape(bsz, HEADS, seq // FOX_T, 1, FOX_T)

    o_dn = _deltanet(proj3, conv_w, slab, gct, dn_onorm_g.reshape(1, HEAD_DIM))
    o_fx = _fox(proj3, f_rows)

    wr = jnp.zeros((d, LANES), F32).at[:, :N_GROUPS].set(w_rg).at[:, N_GROUPS:N_GROUPS + N_EXPERTS].set(w_re)
    br = jnp.zeros((1, LANES), F32).at[0, :N_GROUPS].set(b_rg).at[0, N_GROUPS:N_GROUPS + N_EXPERTS].set(b_re)
    wr_hi_lo = jnp.concatenate(_split_bf16(wr), axis=1)
    w_out_b = w_out.astype(BF16)
    x1, h2, rslab, counts = _outproj(o_dn.reshape(n, HEAD_W), o_fx.reshape(n, HEAD_W), w_out_b[:HEAD_W],
                                     w_out_b[HEAD_W:], x2, g1, sc2, sh2, norm2_g.reshape(1, d), wr_hi_lo, br, seq)

    pos = _route(rslab, counts)[:, 0:2]

    cnt = counts[0, :N_EXPERTS].astype(jnp.int32)
    tiles_per = (cnt + MOE_TM - 1) // MOE_TM
    tile_end = jnp.cumsum(tiles_per)
    base = (tile_end - tiles_per) * MOE_TM
    n_tiles = (2 * n) // MOE_TM + N_EXPERTS
    p_rows = n_tiles * MOE_TM
    tid = jnp.arange(n_tiles, dtype=jnp.int32)
    tile_valid = (tid < tile_end[-1]).astype(jnp.int32)
    te_raw = jnp.minimum(jnp.sum(tid[:, None] >= tile_end[None, :], axis=1), N_EXPERTS - 1).astype(jnp.int32)
    last_e = te_raw[jnp.maximum(tile_end[-1] - 1, 0)]
    tile_expert = jnp.where(tile_valid == 1, te_raw, last_e)
    tile_first = (jnp.concatenate([jnp.array([-1], jnp.int32), tile_expert[:-1]]) != tile_expert).astype(jnp.int32)
    pad_start = base + cnt
    pad_len = tiles_per * MOE_TM - cnt
    eid = jnp.arange(N_EXPERTS, dtype=jnp.int32)
    has = tiles_per > 0
    slot_e = (jnp.cumsum(has.astype(jnp.int32)) - 1) & 1
    later = jnp.where(has[None, :] & (eid[None, :] > eid[:, None]), eid[None, :], N_EXPERTS)
    next_e = jnp.min(later, axis=1)
    next_e = jnp.where(next_e < N_EXPERTS, next_e, -1).astype(jnp.int32)
    of_tile = tile_expert[:, None] == eid[None, :]
    tile_slot = jnp.sum(jnp.where(of_tile, slot_e[None, :], 0), axis=1).astype(jnp.int32)
    tile_next = jnp.sum(jnp.where(of_tile, next_e[None, :], 0), axis=1).astype(jnp.int32)

    xs = _dispatch(pad_start, pad_len, tile_end[-1:], pos.reshape(n // DISP_TB, 1, 2 * DISP_TB), h2, p_rows)
    ys = _experts(tile_expert, tile_valid, tile_first, tile_slot, tile_next, xs, w1.reshape(N_EXPERTS, d, D_EXPERT),
                  w3.reshape(N_EXPERTS, d, D_EXPERT), w2.reshape(N_EXPERTS, D_EXPERT, d))
    out = _combine(pos.reshape(n // COMB_TB, 1, 2 * COMB_TB), ys, rslab, x1, g2, final_g.reshape(1, d), seq, final)
    return out.reshape(bsz, seq, d)


def kernel(x, c, w_ada, b_ada, norm1_g, w_in, conv_w, a_log, dt_bias, dn_onorm_g, fox_f_bias, w_out, norm2_g,
           w_router_group, b_router_group, w_router_expert, b_router_expert, w1, w3, w2, final_g):
    depth = w_ada.shape[0]
    for l in range(depth):
        x = _layer(x, c, w_ada[l], b_ada[l], norm1_g[l], w_in[l], conv_w[l], a_log[l], dt_bias[l], dn_onorm_g[l],
                   fox_f_bias[l], w_out[l], norm2_g[l], w_router_group[l], b_router_group[l], w_router_expert[l],
                   b_router_expert[l], w1[l], w3[l], w2[l], final_g, l == depth - 1)
    return x
```
